```python
import math
import jax, jax.numpy as jnp
from jax import lax
import numpy as np

D_MODEL = 1024
BATCH = 16
SEQ = 2048
DEPTH = 2

HEAD_DIM = 64
H_SB = 4
H_DSA = 4
H_IDX = 4
D_IDX = 64
TOPK_MAX = 256
H_SW = 8
KV_SW = 2
WINDOW = 128
BLOCK = 128
N_BRANCH = 3
N_BUCKETS = 32
MAX_DISTANCE = 128
N_MEM = 256
H_X = 4
XHEAD_DIM = 128
D_FF_DENSE = (11 * D_MODEL) // 4
N_EXPERTS = 8
TOP_K_EXPERTS = 2
D_FF_EXPERT = (7 * D_MODEL) // 2
N_DENSE_LAYERS = (DEPTH + 1) // 2
N_MOE_LAYERS = DEPTH // 2
EPS = 1e-6

IN_SIZES = (
    H_SB * HEAD_DIM, H_SB * HEAD_DIM, H_SB * HEAD_DIM,
    H_DSA * HEAD_DIM, HEAD_DIM, HEAD_DIM,
    H_IDX * D_IDX, D_IDX, H_IDX,
    H_SW * HEAD_DIM, KV_SW * HEAD_DIM, KV_SW * HEAD_DIM,
    N_BRANCH * D_MODEL,
)
N_IN = sum(IN_SIZES)

kernel_name = 'hybrid_gated_sb_dsa_swa_moe'


def rmsnorm(x, g):
    xf = x.astype(jnp.float32)
    y = xf * lax.rsqrt(jnp.mean(xf * xf, axis=-1, keepdims=True) + EPS)
    return (y * g.astype(jnp.float32)).astype(x.dtype)


def _split(t, sizes):
    return jnp.split(t, np.cumsum(sizes)[:-1].tolist(), axis=-1)


def t5_bucket(rel):
    n = jnp.maximum(rel, 0)
    max_exact = N_BUCKETS // 2
    nf = jnp.maximum(n, 1).astype(jnp.float32)
    large = max_exact + (jnp.log(nf / max_exact) / math.log(MAX_DISTANCE / max_exact)
                         * (N_BUCKETS - max_exact)).astype(jnp.int32)
    large = jnp.minimum(large, N_BUCKETS - 1)
    return jnp.where(n < max_exact, n, large)


def stick_breaking_attention(q, k, v):
    B_, S_, H, d = q.shape
    nblk = S_ // BLOCK
    scale = d ** -0.5
    kpos = jnp.arange(S_)

    def block(i):
        q0 = i * BLOCK
        qpos = q0 + jnp.arange(BLOCK)
        q_blk = lax.dynamic_slice_in_dim(q, q0, BLOCK, axis=1)
        z = jnp.einsum('bthd,bshd->bhts', q_blk, k, preferred_element_type=jnp.float32) * scale
        causal = kpos[None, :] < qpos[:, None]
        log_beta = jax.nn.log_sigmoid(z)
        log_keep = jnp.where(causal, jax.nn.log_sigmoid(-z), 0.0)
        rev = lax.cumsum(log_keep, axis=3, reverse=True) - log_keep
        att = jnp.where(causal, jnp.exp(log_beta + rev), 0.0)
        return jnp.einsum('bhts,bshd->bthd', att.astype(v.dtype), v)

    out = lax.map(block, jnp.arange(nblk))
    return out.transpose(1, 0, 2, 3, 4).reshape(B_, S_, H, d)


def dsa_attention(q, k, v, q_idx, k_idx, w_idx, bias_tab):
    B_, S_, H, d = q.shape
    topk = min(TOPK_MAX, S_ // 4)
    nblk = S_ // BLOCK
    kpos = jnp.arange(S_)
    gather = jax.vmap(lambda t, i: t[i])

    def block(i):
        q0 = i * BLOCK
        qpos = q0 + jnp.arange(BLOCK)
        q_blk = lax.dynamic_slice_in_dim(q, q0, BLOCK, axis=1)
        qi_blk = lax.dynamic_slice_in_dim(q_idx, q0, BLOCK, axis=1)
        wi_blk = lax.dynamic_slice_in_dim(w_idx, q0, BLOCK, axis=1)
        act = jax.nn.relu(jnp.einsum('bthe,bse->bths', qi_blk, k_idx,
                                     preferred_element_type=jnp.float32) * D_IDX ** -0.5)
        score = jnp.einsum('bth,bths->bts', wi_blk.astype(jnp.float32), act)
        score = jnp.where(kpos[None, None, :] <= qpos[None, :, None], score, -jnp.inf)
        _, idx = lax.top_k(score, topk)
        valid = idx <= qpos[None, :, None]
        k_sel = gather(k, idx)
        v_sel = gather(v, idx)
        s = jnp.einsum('bthd,btkd->bhtk', q_blk, k_sel, preferred_element_type=jnp.float32) * d ** -0.5
        bias = bias_tab[t5_bucket(qpos[None, :, None] - idx)]
        s = s + bias.transpose(0, 3, 1, 2).astype(jnp.float32)
        s = jnp.where(valid[:, None], s, -jnp.inf)
        p = jax.nn.softmax(s, axis=-1)
        return jnp.einsum('bhtk,btkd->bthd', p.astype(v.dtype), v_sel)

    out = lax.map(block, jnp.arange(nblk))
    return out.transpose(1, 0, 2, 3, 4).reshape(B_, S_, H, d)


def sliding_window_attention(q, k, v, sinks, bias_tab):
    B_, S_, H, d = q.shape
    G = H // KV_SW
    nblk = S_ // BLOCK
    qb = q.reshape(B_, nblk, BLOCK, KV_SW, G, d)
    kb = k.reshape(B_, nblk, BLOCK, KV_SW, d)
    vb = v.reshape(B_, nblk, BLOCK, KV_SW, d)
    pad = jnp.zeros_like(kb[:, :1])
    k2 = jnp.concatenate([jnp.concatenate([pad, kb[:, :-1]], axis=1), kb], axis=2)
    v2 = jnp.concatenate([jnp.concatenate([pad, vb[:, :-1]], axis=1), vb], axis=2)
    s = jnp.einsum('bnqkgd,bnjkd->bnkgqj', qb, k2, preferred_element_type=jnp.float32) * d ** -0.5
    kj = jnp.arange(2 * BLOCK)[None, :]
    rel = (jnp.arange(BLOCK)[:, None] + BLOCK) - kj
    in_win = (rel >= 0) & (rel < WINDOW)
    has_prev = (jnp.arange(nblk)[:, None, None] > 0) | (kj[None] >= BLOCK)
    mask = in_win[None] & has_prev
    bias = bias_tab[t5_bucket(rel)].transpose(2, 0, 1).reshape(KV_SW, G, BLOCK, 2 * BLOCK)
    s = s + bias[None, None].astype(jnp.float32)
    s = jnp.where(mask[None, :, None, None], s, -jnp.inf)
    sink = sinks.astype(jnp.float32).reshape(KV_SW, G)[None, None, :, :, None, None]
    m = jnp.maximum(jnp.max(s, axis=-1, keepdims=True), sink)
    e = jnp.exp(s - m)
    p = e / (jnp.sum(e, axis=-1, keepdims=True) + jnp.exp(sink - m))
    o = jnp.einsum('bnkgqj,bnjkd->bnqkgd', p.astype(v.dtype), v2)
    return o.reshape(B_, S_, H, d)


def cross_attention(xn, memn, w_q, w_kv, w_o, q_gain, k_gain):
    B_, S_, _ = xn.shape
    M = memn.shape[1]
    q = rmsnorm((xn @ w_q).reshape(B_, S_, H_X, XHEAD_DIM), q_gain)
    k, v = _split(memn @ w_kv, (H_X * XHEAD_DIM, H_X * XHEAD_DIM))
    k = rmsnorm(k.reshape(B_, M, H_X, XHEAD_DIM), k_gain)
    v = v.reshape(B_, M, H_X, XHEAD_DIM)
    s = jnp.einsum('bshd,bmhd->bhsm', q, k, preferred_element_type=jnp.float32) * XHEAD_DIM ** -0.5
    p = jax.nn.softmax(s, axis=-1)
    o = jnp.einsum('bhsm,bmhd->bshd', p.astype(v.dtype), v)
    return o.reshape(B_, S_, H_X * XHEAD_DIM) @ w_o


def swiglu(h, w_gu, w_down):
    g, u = jnp.split(h @ w_gu, 2, axis=-1)
    return (jax.nn.silu(g) * u) @ w_down


def moe_swiglu(h, w_router, w_gu, w_down):
    B_, S_, D = h.shape
    t = h.reshape(-1, D)
    logits = jnp.dot(t, w_router, preferred_element_type=jnp.float32)
    top_val, top_idx = lax.top_k(logits, TOP_K_EXPERTS)
    top_gate = jax.nn.softmax(top_val, axis=-1)
    gate = jnp.sum(jax.nn.one_hot(top_idx, N_EXPERTS, dtype=jnp.float32) * top_gate[..., None], axis=1)
    out = jnp.zeros(t.shape, jnp.float32)
    for e in range(N_EXPERTS):
        out = out + gate[:, e:e + 1] * swiglu(t, w_gu[e], w_down[e]).astype(jnp.float32)
    return out.astype(h.dtype).reshape(B_, S_, D)


def setup_inputs(seed: int = 0) -> dict:
    key = jax.random.key(seed)
    ks = iter(jax.random.split(key, 32))
    D = D_MODEL

    def nrm(shape, scale):
        return jax.random.normal(next(ks), shape, jnp.float32) * scale

    def gain(shape):
        return 1.0 + nrm(shape, 0.02)

    w_a = H_SB * HEAD_DIM
    w_b = H_DSA * HEAD_DIM
    w_c = H_SW * HEAD_DIM
    w_x = H_X * XHEAD_DIM
    return {
        'x': nrm((BATCH, SEQ, D), 1.0),
        'mem': nrm((BATCH, N_MEM, D), 1.0),
        'rel_bias': nrm((N_BUCKETS, H_DSA + H_SW), 0.2),
        'norm_mix': gain((DEPTH, D)),
        'w_in': nrm((DEPTH, D, N_IN), D ** -0.5),
        'b_gate': nrm((DEPTH, N_BRANCH, D), 0.02),
        'qn_dsa': gain((DEPTH, HEAD_DIM)),
        'kn_dsa': gain((DEPTH, HEAD_DIM)),
        'qn_swa': gain((DEPTH, HEAD_DIM)),
        'kn_swa': gain((DEPTH, HEAD_DIM)),
        'sinks': nrm((DEPTH, H_SW), 0.5),
        'w_pa': nrm((DEPTH, w_a, D), w_a ** -0.5),
        'w_pb': nrm((DEPTH, w_b, D), w_b ** -0.5),
        'w_pc': nrm((DEPTH, w_c, D), w_c ** -0.5),
        'w_out': nrm((DEPTH, D, D), D ** -0.5),
        'norm_x': gain((DEPTH, D)),
        'norm_mem': gain((DEPTH, D)),
        'w_xq': nrm((DEPTH, D, w_x), D ** -0.5),
        'w_xkv': nrm((DEPTH, D, 2 * w_x), D ** -0.5),
        'w_xo': nrm((DEPTH, w_x, D), w_x ** -0.5),
        'qn_x': gain((DEPTH, XHEAD_DIM)),
        'kn_x': gain((DEPTH, XHEAD_DIM)),
        'norm_ffn': gain((DEPTH, D)),
        'w_gu_dense': nrm((N_DENSE_LAYERS, D, 2 * D_FF_DENSE), D ** -0.5),
        'w_down_dense': nrm((N_DENSE_LAYERS, D_FF_DENSE, D), D_FF_DENSE ** -0.5),
        'w_router': nrm((N_MOE_LAYERS, D, N_EXPERTS), D ** -0.5),
        'w_gu_moe': nrm((N_MOE_LAYERS, N_EXPERTS, D, 2 * D_FF_EXPERT), D ** -0.5),
        'w_down_moe': nrm((N_MOE_LAYERS, N_EXPERTS, D_FF_EXPERT, D), D_FF_EXPERT ** -0.5),
    }


def reference(x, mem, rel_bias, norm_mix, w_in, b_gate, qn_dsa, kn_dsa, qn_swa, kn_swa, sinks,
              w_pa, w_pb, w_pc, w_out, norm_x, norm_mem, w_xq, w_xkv, w_xo, qn_x, kn_x,
              norm_ffn, w_gu_dense, w_down_dense, w_router, w_gu_moe, w_down_moe):
    B_, S_, D = x.shape
    bias_dsa = rel_bias[:, :H_DSA]
    bias_swa = rel_bias[:, H_DSA:]

    def heads(t, n, d):
        return t.reshape(B_, S_, n, d)

    for l in range(DEPTH):
        h = rmsnorm(x, norm_mix[l])
        qa, ka, va, qb, kb, vb, qi, ki, wi, qc, kc, vc, g = _split(h @ w_in[l], IN_SIZES)
        o_a = stick_breaking_attention(heads(qa, H_SB, HEAD_DIM), heads(ka, H_SB, HEAD_DIM),
                                       heads(va, H_SB, HEAD_DIM))
        o_b = dsa_attention(rmsnorm(heads(qb, H_DSA, HEAD_DIM), qn_dsa[l]), rmsnorm(kb, kn_dsa[l]), vb,
                            heads(qi, H_IDX, D_IDX), ki, wi, bias_dsa)
        o_c = sliding_window_attention(rmsnorm(heads(qc, H_SW, HEAD_DIM), qn_swa[l]),
                                       rmsnorm(heads(kc, KV_SW, HEAD_DIM), kn_swa[l]),
                                       heads(vc, KV_SW, HEAD_DIM), sinks[l], bias_swa)
        gates = jax.nn.sigmoid((g.reshape(B_, S_, N_BRANCH, D) + b_gate[l]).astype(jnp.float32)).astype(x.dtype)
        merged = (gates[:, :, 0] * (o_a.reshape(B_, S_, -1) @ w_pa[l])
                  + gates[:, :, 1] * (o_b.reshape(B_, S_, -1) @ w_pb[l])
                  + gates[:, :, 2] * (o_c.reshape(B_, S_, -1) @ w_pc[l]))
        x = x + merged @ w_out[l]
        x = x + cross_attention(rmsnorm(x, norm_x[l]), rmsnorm(mem, norm_mem[l]),
                                w_xq[l], w_xkv[l], w_xo[l], qn_x[l], kn_x[l])
        hf = rmsnorm(x, norm_ffn[l])
        if l % 2 == 0:
            x = x + swiglu(hf, w_gu_dense[l // 2], w_down_dense[l // 2])
        else:
            x = x + moe_swiglu(hf, w_router[l // 2], w_gu_moe[l // 2], w_down_moe[l // 2])
    return x
```

```python
import functools
import math

import numpy as np
import jax
import jax.numpy as jnp
from jax import lax
from jax.experimental import pallas as pl
from jax.experimental.pallas import tpu as pltpu

HEAD_DIM = 64
H_SB = 4
H_DSA = 4
H_IDX = 4
D_IDX = 64
TOPK_MAX = 256
H_SW = 8
KV_SW = 2
WINDOW = 128
BLOCK = 128
N_BRANCH = 3
N_BUCKETS = 32
MAX_DISTANCE = 128
H_X = 4
XHEAD_DIM = 128
N_EXPERTS = 8
EPS = 1e-6

LANES = 128
VMEM_LIMIT = 56 * 1024 * 1024
NEG = -1e30
INT_MIN = -(2 ** 31)

F32 = jnp.float32
BF16 = jnp.bfloat16


def _nt(a, b):
    return lax.dot_general(a, b, (((1,), (1,)), ((), ())), preferred_element_type=F32)


def _mm(a, b):
    return jnp.dot(a, b, preferred_element_type=F32)


def _rms_rows(x, g):
    ms = jnp.mean(x * x, axis=-1, keepdims=True)
    return x * lax.rsqrt(ms + EPS) * g


def _params(sem):
    return pltpu.CompilerParams(dimension_semantics=sem, vmem_limit_bytes=VMEM_LIMIT)


def _sb_kernel(a_ref, o_ref, *, tq):
    i = pl.program_id(1)
    q0 = pl.multiple_of(i * tq, tq)
    hd = HEAD_DIM
    row = lax.broadcasted_iota(jnp.int32, (tq, tq), 0)
    col = lax.broadcasted_iota(jnp.int32, (tq, tq), 1)
    strict = col < row
    u_r = lax.broadcasted_iota(jnp.int32, (tq, 2 * tq), 0)
    u_c = lax.broadcasted_iota(jnp.int32, (tq, 2 * tq), 1)
    u1 = jnp.where((u_r > u_c) | (u_c >= tq), 1.0, 0.0).astype(BF16)

    outs = []
    for h in range(H_SB):
        q = a_ref[0, pl.ds(q0, tq), h * hd:(h + 1) * hd]

        def scores(k0):
            k = a_ref[0, pl.ds(k0, tq), 256 + h * hd:256 + (h + 1) * hd]
            v = a_ref[0, pl.ds(k0, tq), 512 + h * hd:512 + (h + 1) * hd]
            z = _nt(q, k)
            sp = jnp.maximum(z, 0.0) + jnp.log1p(jnp.exp(-jnp.abs(z)))
            return z, sp, v

        z, sp, v = scores(q0)
        lk = jnp.where(strict, -sp, 0.0)
        r = _mm(lk.astype(BF16), u1)
        att = jnp.where(strict, jnp.exp(z - sp + r[:, :tq]), 0.0)
        acc = _mm(att.astype(BF16), v)
        carry = r[:, tq:]

        def body(jj, st):
            acc, carry = st
            k0 = pl.multiple_of((i - 1 - jj) * tq, tq)
            z, sp, v = scores(k0)
            r = _mm((-sp).astype(BF16), u1)
            att = jnp.exp(z - sp + r[:, :tq] + carry)
            acc = acc + _mm(att.astype(BF16), v)
            return acc, carry + r[:, tq:]

        acc, _ = lax.fori_loop(0, i, body, (acc, carry))
        outs.append(acc)
    o_ref[0] = jnp.concatenate(outs, axis=-1).astype(o_ref.dtype)


def sb_attention(proj, col_block, *, tq=BLOCK):
    b, s, _ = proj.shape
    return pl.pallas_call(
        functools.partial(_sb_kernel, tq=tq),
        grid=(b, s // tq),
        in_specs=[pl.BlockSpec((1, s, 768), lambda bi, i: (bi, 0, col_block))],
        out_specs=pl.BlockSpec((1, tq, H_SB * HEAD_DIM), lambda bi, i: (bi, i, 0)),
        out_shape=jax.ShapeDtypeStruct((b, s, H_SB * HEAD_DIM), BF16),
        compiler_params=_params(("parallel", "arbitrary")),
        name="sb_attention",
    )(proj)


def _inproj_kernel(x_ref, g_ref, w_ref, o_ref, xn_ref):
    @pl.when(pl.program_id(1) == 0)
    def _():
        xn_ref[...] = _rms_rows(x_ref[...], g_ref[...]).astype(BF16)

    o_ref[...] = _mm(xn_ref[...], w_ref[...]).astype(o_ref.dtype)


def in_projection(x2d, gain, w, *, tm=512, tn=768):
    n, d = x2d.shape
    c = w.shape[1]
    return pl.pallas_call(
        _inproj_kernel,
        grid=(n // tm, c // tn),
        in_specs=[pl.BlockSpec((tm, d), lambda i, j: (i, 0)),
                  pl.BlockSpec((1, d), lambda i, j: (0, 0)),
                  pl.BlockSpec((d, tn), lambda i, j: (0, j))],
        out_specs=pl.BlockSpec((tm, tn), lambda i, j: (i, j)),
        out_shape=jax.ShapeDtypeStruct((n, c), BF16),
        scratch_shapes=[pltpu.VMEM((tm, d), BF16)],
        compiler_params=_params(("parallel", "arbitrary")),
        name="in_projection",
    )(x2d, gain, w)


def _head_sumsq(x, head_dim):
    r_i = lax.broadcasted_iota(jnp.int32, (LANES, LANES), 0) // head_dim
    c_i = lax.broadcasted_iota(jnp.int32, (LANES, LANES), 1) // head_dim
    bd = jnp.where(r_i == c_i, 1.0, 0.0).astype(F32)
    x2 = x * x
    parts = [_mm(x2[:, b * LANES:(b + 1) * LANES], bd) for b in range(x.shape[1] // LANES)]
    return parts[0] if len(parts) == 1 else jnp.concatenate(parts, axis=-1)


def _head_rms(x, g, head_dim):
    ss = _head_sumsq(x, head_dim)
    return x * lax.rsqrt(ss * (1.0 / head_dim) + EPS) * g


def _eye(n, dtype):
    r = lax.broadcasted_iota(jnp.int32, (n, n), 0)
    c = lax.broadcasted_iota(jnp.int32, (n, n), 1)
    return jnp.where(r == c, 1.0, 0.0).astype(dtype)


def _dsa_kernel(a_ref, gq_ref, gk_ref, bias_ref, o_ref, kbn_ref, vt_ref, keys_ref, *, tq, ck, topk):
    s_len = a_ref.shape[1]
    i = pl.program_id(1)
    q0 = pl.multiple_of(i * tq, tq)
    hd = HEAD_DIM
    nblk = s_len // tq

    @pl.when(i == 0)
    def _():
        kv = a_ref[0, :, 512:640]
        kvf = kv.astype(F32)
        lane = lax.broadcasted_iota(jnp.int32, (1, LANES), 1)
        ss = _head_sumsq(jnp.where(lane < hd, kvf, 0.0), LANES)
        kn = kvf * lax.rsqrt(ss * (1.0 / hd) + EPS) * gk_ref[...]
        kbn_ref[...] = kn[:, :hd].astype(BF16)
        kvt = _nt(_eye(LANES, BF16), kv)
        for jb in range(nblk):
            vt_ref[jb] = kvt[hd:, jb * tq:(jb + 1) * tq].astype(BF16)

    qi = a_ref[0, pl.ds(q0, tq), 256:512]
    wblk = a_ref[0, pl.ds(q0, tq), 640:768]
    sel_r = lax.broadcasted_iota(jnp.int32, (8, LANES), 0)
    sel_c = lax.broadcasted_iota(jnp.int32, (8, LANES), 1)
    w_t = _nt(jnp.where(sel_c == sel_r + D_IDX, 1.0, 0.0).astype(BF16), wblk)
    qpos = q0 + lax.broadcasted_iota(jnp.int32, (1, tq), 1)
    nck = (q0 + tq + ck - 1) // ck
    row_ck = lax.broadcasted_iota(jnp.int32, (ck, tq), 0)

    def p1(c, carry):
        k0 = pl.multiple_of(c * ck, ck)
        kic = a_ref[0, pl.ds(k0, ck), 640:704]
        sc = jnp.zeros((ck, tq), F32)
        for h in range(H_IDX):
            act = jnp.maximum(_nt(kic, qi[:, h * D_IDX:(h + 1) * D_IDX]), 0.0)
            sc = sc + w_t[h:h + 1, :] * act
        bits = pltpu.bitcast(sc, jnp.int32)
        key = jnp.where(bits < 0, -(bits & 0x7FFFFFFF), bits)
        key = jnp.where(k0 + row_ck <= qpos, key, INT_MIN)
        keys_ref[pl.ds(k0, ck), :] = key
        return carry

    lax.fori_loop(0, nck, p1, 0)

    def count(pred):
        def body(c, acc):
            k0 = pl.multiple_of(c * ck, ck)
            m = jnp.where(pred(keys_ref[pl.ds(k0, ck), :], k0 + row_ck), 1, 0)
            return acc + jnp.sum(m.reshape(ck // 8, 8, tq), axis=0)
        acc = lax.fori_loop(0, nck, body, jnp.zeros((8, tq), jnp.int32))
        return jnp.sum(acc, axis=0, keepdims=True)

    def search():
        c0 = count(lambda k, p: k >= 0)
        t = jnp.where(c0 >= topk, 0, INT_MIN).astype(jnp.int32)

        def vstep(b, t):
            cand = t | lax.shift_left(jnp.int32(1), 30 - b)
            return jnp.where(count(lambda k, p: k >= cand) >= topk, cand, t)

        t = lax.fori_loop(0, 31, vstep, t)
        need = topk - count(lambda k, p: k > t)
        nbits = max(1, (s_len - 1).bit_length())

        def istep(b, y):
            cand = y | lax.shift_left(jnp.int32(1), nbits - 1 - b)
            return jnp.where(count(lambda k, p: (k == t) & (p < cand)) < need, cand, y)

        y = lax.fori_loop(0, nbits, istep, jnp.zeros((1, tq), jnp.int32))
        y = jnp.where(t == INT_MIN, -1, y)
        return t, y

    def no_search():
        return (jnp.full((1, tq), INT_MIN, jnp.int32), jnp.full((1, tq), -1, jnp.int32))

    thr, ymax = lax.cond(q0 + tq > topk, search, no_search)

    qb = a_ref[0, pl.ds(q0, tq), 0:256].astype(F32)
    qn = _head_rms(qb, gq_ref[...], hd).astype(BF16)
    qh = [qn[:, h * hd:(h + 1) * hd] for h in range(H_DSA)]
    row_b = lax.broadcasted_iota(jnp.int32, (tq, tq), 0)

    def p3(j, st):
        ms, ls, accs = st
        k0 = pl.multiple_of(j * tq, tq)
        kj = kbn_ref[pl.ds(k0, tq), :]
        vtj = vt_ref[j]
        key = keys_ref[pl.ds(k0, tq), :]
        sel = (key > thr) | ((key == thr) & (k0 + row_b <= ymax))
        bidx = jnp.minimum(i - j, 2)
        nm, nl, na = [], [], []
        for h in range(H_DSA):
            st_ = _nt(kj, qh[h]) + bias_ref[bidx, h]
            mb = jnp.max(jnp.where(sel, st_, NEG), axis=0, keepdims=True)
            m_new = jnp.maximum(ms[h], mb)
            p = jnp.where(sel, jnp.exp(st_ - m_new), 0.0)
            alpha = jnp.exp(ms[h] - m_new)
            nm.append(m_new)
            nl.append(ls[h] * alpha + jnp.sum(p, axis=0, keepdims=True))
            na.append(accs[h] * alpha + _mm(vtj, p.astype(BF16)))
        return tuple(nm), tuple(nl), tuple(na)

    init = (tuple(jnp.full((1, tq), NEG, F32) for _ in range(H_DSA)),
            tuple(jnp.zeros((1, tq), F32) for _ in range(H_DSA)),
            tuple(jnp.zeros((hd, tq), F32) for _ in range(H_DSA)))
    _, ls, accs = lax.fori_loop(0, i + 1, p3, init)
    o_t = jnp.concatenate([accs[h] / ls[h] for h in range(H_DSA)], axis=0)
    o_ref[0] = _nt(_eye(tq, BF16), o_t.astype(BF16)).astype(o_ref.dtype)


def dsa_attention(proj, col_block, gq, gk, bias_t, *, tq=BLOCK, ck=512):
    b, s, _ = proj.shape
    topk = min(TOPK_MAX, s // 4)
    ck = min(ck, s)
    return pl.pallas_call(
        functools.partial(_dsa_kernel, tq=tq, ck=ck, topk=topk),
        grid=(b, s // tq),
        in_specs=[pl.BlockSpec((1, s, 768), lambda bi, i: (bi, 0, col_block)),
                  pl.BlockSpec((1, 256), lambda bi, i: (0, 0)),
                  pl.BlockSpec((1, LANES), lambda bi, i: (0, 0)),
                  pl.BlockSpec((3, H_DSA, tq, tq), lambda bi, i: (0, 0, 0, 0))],
        out_specs=pl.BlockSpec((1, tq, H_DSA * HEAD_DIM), lambda bi, i: (bi, i, 0)),
        out_shape=jax.ShapeDtypeStruct((b, s, H_DSA * HEAD_DIM), BF16),
        scratch_shapes=[pltpu.VMEM((s, HEAD_DIM), BF16),
                        pltpu.VMEM((s // tq, HEAD_DIM, tq), BF16),
                        pltpu.VMEM((s, tq), jnp.int32)],
        compiler_params=_params(("parallel", "arbitrary")),
        name="dsa_attention",
    )(proj, gq, gk, bias_t)


def _t5_bucket(rel):
    n = jnp.maximum(rel, 0)
    max_exact = N_BUCKETS // 2
    nf = jnp.maximum(n, 1).astype(F32)
    large = max_exact + (jnp.log(nf / max_exact) / math.log(MAX_DISTANCE / max_exact)
                         * (N_BUCKETS - max_exact)).astype(jnp.int32)
    large = jnp.minimum(large, N_BUCKETS - 1)
    return jnp.where(n < max_exact, n, large)


def dsa_bias_tiles(tab, tq=BLOCK):
    ks = jnp.arange(tq)[:, None]
    tl = jnp.arange(tq)[None, :]
    rel = jnp.stack([tl - ks, tq + tl - ks, jnp.full((tq, tq), 2 * tq + MAX_DISTANCE)])
    return tab[_t5_bucket(rel)].transpose(0, 3, 1, 2).astype(F32)


def _swa_kernel(sink_ref, cur_ref, prev_ref, gq_ref, gk_ref, bias_ref, o_ref, *, tq):
    i = pl.program_id(1)
    hd = HEAD_DIM
    g = H_SW // KV_SW
    qn = _head_rms(cur_ref[0, :, 0:512].astype(F32), gq_ref[...], hd).astype(BF16)
    k_cur = _head_rms(cur_ref[0, :, 512:640].astype(F32), gk_ref[...], hd).astype(BF16)
    k_prev = _head_rms(prev_ref[0, :, 512:640].astype(F32), gk_ref[...], hd).astype(BF16)
    k2 = jnp.concatenate([k_prev, k_cur], axis=0)
    v2 = jnp.concatenate([prev_ref[0, :, 640:768], cur_ref[0, :, 640:768]], axis=0)
    col = lax.broadcasted_iota(jnp.int32, (tq, 2 * tq), 1)
    has_prev = (col >= tq) | (i > 0)
    outs = []
    for h in range(H_SW):
        kv = h // g
        s = _nt(qn[:, h * hd:(h + 1) * hd], k2[:, kv * hd:(kv + 1) * hd]) + bias_ref[h]
        s = jnp.where(has_prev, s, NEG)
        sink = sink_ref[h]
        m = jnp.maximum(jnp.max(s, axis=-1, keepdims=True), sink)
        e = jnp.exp(s - m)
        den = jnp.sum(e, axis=-1, keepdims=True) + jnp.exp(sink - m)
        outs.append(_mm(e.astype(BF16), v2[:, kv * hd:(kv + 1) * hd]) / den)
    o_ref[0] = jnp.concatenate(outs, axis=-1).astype(o_ref.dtype)


def swa_bias_tiles(tab, tq=BLOCK):
    rel = (jnp.arange(tq)[:, None] + tq) - jnp.arange(2 * tq)[None, :]
    in_win = (rel >= 0) & (rel < WINDOW)
    bias = tab[_t5_bucket(rel)].transpose(2, 0, 1).astype(F32)
    return jnp.where(in_win[None], bias, NEG)


def swa_attention(proj, col_block, sinks, gq, gk, bias, *, tq=BLOCK):
    b, s, _ = proj.shape
    return pl.pallas_call(
        functools.partial(_swa_kernel, tq=tq),
        grid_spec=pltpu.PrefetchScalarGridSpec(
            num_scalar_prefetch=0,
            grid=(b, s // tq),
            in_specs=[pl.BlockSpec(memory_space=pltpu.SMEM),
                      pl.BlockSpec((1, tq, 768), lambda bi, i: (bi, i, col_block)),
                      pl.BlockSpec((1, tq, 768), lambda bi, i: (bi, jnp.maximum(i - 1, 0), col_block)),
                      pl.BlockSpec((1, 512), lambda bi, i: (0, 0)),
                      pl.BlockSpec((1, LANES), lambda bi, i: (0, 0)),
                      pl.BlockSpec((H_SW, tq, 2 * tq), lambda bi, i: (0, 0, 0))],
            out_specs=pl.BlockSpec((1, tq, H_SW * HEAD_DIM), lambda bi, i: (bi, i, 0))),
        out_shape=jax.ShapeDtypeStruct((b, s, H_SW * HEAD_DIM), BF16),
        compiler_params=_params(("parallel", "arbitrary")),
        name="swa_attention",
    )(sinks, proj, proj, gq, gk, bias)


def _merge_kernel(x_ref, g_ref, oa_ref, ob_ref, oc_ref, bg_ref, wa_ref, wb_ref, wc_ref, wo_ref, o_ref):
    d = x_ref.shape[1]
    merged = None
    for k, (o_k, w_k) in enumerate(((oa_ref, wa_ref), (ob_ref, wb_ref), (oc_ref, wc_ref))):
        logit = g_ref[:, k * d:(k + 1) * d].astype(F32) + bg_ref[k:k + 1, :]
        gate = 1.0 / (1.0 + jnp.exp(-logit))
        term = gate * _mm(o_k[...], w_k[...])
        merged = term if merged is None else merged + term
    o_ref[...] = x_ref[...] + _mm(merged.astype(BF16), wo_ref[...])


def merge_project(x2d, proj2d, o_a, o_b, o_c, b_gate, w_pa, w_pb, w_pc, w_out, *, tm=512):
    n, d = x2d.shape
    full = lambda a: pl.BlockSpec(a.shape, lambda i: (0,) * a.ndim)
    row = lambda a: pl.BlockSpec((tm, a.shape[1]), lambda i: (i, 0))
    return pl.pallas_call(
        _merge_kernel,
        grid=(n // tm,),
        in_specs=[row(x2d), pl.BlockSpec((tm, N_BRANCH * d), lambda i: (i, 0)),
                  row(o_a), row(o_b), row(o_c),
                  full(b_gate), full(w_pa), full(w_pb), full(w_pc), full(w_out)],
        out_specs=pl.BlockSpec((tm, d), lambda i: (i, 0)),
        out_shape=jax.ShapeDtypeStruct((n, d), F32),
        compiler_params=_params(("parallel",)),
        name="merge_project",
    )(x2d, proj2d, o_a, o_b, o_c, b_gate, w_pa, w_pb, w_pc, w_out)


def _memkv_kernel(m_ref, g_ref, w_ref, gk_ref, k_ref, v_ref):
    w_x = k_ref.shape[2]
    mn = _rms_rows(m_ref[0], g_ref[...]).astype(BF16)
    kv = _mm(mn, w_ref[...])
    k_ref[0] = _head_rms(kv[:, :w_x], gk_ref[...], XHEAD_DIM).astype(k_ref.dtype)
    v_ref[0] = kv[:, w_x:].astype(v_ref.dtype)


def memory_kv(mem, gain, w_kv, gk):
    b, m, d = mem.shape
    w_x = w_kv.shape[1] // 2
    return pl.pallas_call(
        _memkv_kernel,
        grid=(b,),
        in_specs=[pl.BlockSpec((1, m, d), lambda i: (i, 0, 0)),
                  pl.BlockSpec((1, d), lambda i: (0, 0)),
                  pl.BlockSpec(w_kv.shape, lambda i: (0, 0)),
                  pl.BlockSpec((1, w_x), lambda i: (0, 0))],
        out_specs=[pl.BlockSpec((1, m, w_x), lambda i: (i, 0, 0)),
                   pl.BlockSpec((1, m, w_x), lambda i: (i, 0, 0))],
        out_shape=[jax.ShapeDtypeStruct((b, m, w_x), BF16)] * 2,
        compiler_params=_params(("parallel",)),
        name="memory_kv",
    )(mem, gain, w_kv, gk)


def _xattn_kernel(x_ref, g_ref, wq_ref, gq_ref, k_ref, v_ref, wo_ref, o_ref):
    x = x_ref[0]
    xn = _rms_rows(x, g_ref[...]).astype(BF16)
    q = _head_rms(_mm(xn, wq_ref[...]), gq_ref[...], XHEAD_DIM).astype(BF16)
    outs = []
    for h in range(H_X):
        sl = slice(h * XHEAD_DIM, (h + 1) * XHEAD_DIM)
        s = _nt(q[:, sl], k_ref[0, :, sl])
        e = jnp.exp(s - jnp.max(s, axis=-1, keepdims=True))
        den = jnp.sum(e, axis=-1, keepdims=True)
        outs.append(_mm(e.astype(BF16), v_ref[0, :, sl]) / den)
    o = jnp.concatenate(outs, axis=-1).astype(BF16)
    o_ref[0] = x + _mm(o, wo_ref[...])


def cross_attention(x, gain, w_q, gq, k_mem, v_mem, w_o, *, tq=256):
    b, s, d = x.shape
    m, w_x = k_mem.shape[1:]
    full = lambda a: pl.BlockSpec(a.shape, lambda bi, i: (0,) * a.ndim)
    return pl.pallas_call(
        _xattn_kernel,
        grid=(b, s // tq),
        in_specs=[pl.BlockSpec((1, tq, d), lambda bi, i: (bi, i, 0)),
                  full(gain), full(w_q), full(gq),
                  pl.BlockSpec((1, m, w_x), lambda bi, i: (bi, 0, 0)),
                  pl.BlockSpec((1, m, w_x), lambda bi, i: (bi, 0, 0)),
                  full(w_o)],
        out_specs=pl.BlockSpec((1, tq, d), lambda bi, i: (bi, i, 0)),
        out_shape=jax.ShapeDtypeStruct((b, s, d), F32),
        compiler_params=_params(("parallel", "parallel")),
        name="cross_attention",
    )(x, gain, w_q, gq, k_mem, v_mem, w_o)


def _router_kernel(x_ref, g_ref, whi_ref, wlo_ref, o_ref):
    hf = _rms_rows(x_ref[...], g_ref[...])
    hi = hf.astype(BF16)
    lo = (hf - hi.astype(F32)).astype(BF16)
    logits = _mm(hi, whi_ref[...]) + (_mm(hi, wlo_ref[...]) + _mm(lo, whi_ref[...]))
    lane = lax.broadcasted_iota(jnp.int32, logits.shape, 1).astype(F32)
    logits = jnp.where(lane < N_EXPERTS, logits, NEG)
    m1 = jnp.max(logits, axis=-1, keepdims=True)
    i1 = jnp.min(jnp.where(logits == m1, lane, float(LANES)), axis=-1, keepdims=True)
    rest = jnp.where(lane == i1, NEG, logits)
    m2 = jnp.max(rest, axis=-1, keepdims=True)
    i2 = jnp.min(jnp.where(rest == m2, lane, float(LANES)), axis=-1, keepdims=True)
    e2 = jnp.exp(m2 - m1)
    den = 1.0 + e2
    o_ref[...] = jnp.where(lane == i1, 1.0 / den, 0.0) + jnp.where(lane == i2, e2 / den, 0.0)


def router_gates(x2d, gain, w_hi, w_lo, *, tm=512):
    n, d = x2d.shape
    return pl.pallas_call(
        _router_kernel,
        grid=(n // tm,),
        in_specs=[pl.BlockSpec((tm, d), lambda i: (i, 0)),
                  pl.BlockSpec((1, d), lambda i: (0, 0)),
                  pl.BlockSpec((d, LANES), lambda i: (0, 0)),
                  pl.BlockSpec((d, LANES), lambda i: (0, 0))],
        out_specs=pl.BlockSpec((tm, LANES), lambda i: (i, 0)),
        out_shape=jax.ShapeDtypeStruct((n, LANES), F32),
        compiler_params=_params(("parallel",)),
        name="router_gates",
    )(x2d, gain, w_hi, w_lo)


def _ffn_kernel(x_ref, g_ref, gate_ref, wg_ref, wu_ref, wd_ref, o_ref, xn_ref, acc_ref, gcol_ref):
    e = pl.program_id(1)
    j = pl.program_id(2)

    @pl.when((e == 0) & (j == 0))
    def _():
        xn_ref[...] = _rms_rows(x_ref[...], g_ref[...]).astype(BF16)
        acc_ref[...] = jnp.zeros_like(acc_ref)

    @pl.when(j == 0)
    def _():
        lane = lax.broadcasted_iota(jnp.int32, gate_ref.shape, 1)
        gcol_ref[...] = jnp.sum(jnp.where(lane == e, gate_ref[...], 0.0), axis=-1, keepdims=True)

    xn = xn_ref[...]
    gg = _mm(xn, wg_ref[0])
    uu = _mm(xn, wu_ref[0])
    act = gg * (1.0 / (1.0 + jnp.exp(-gg))) * uu * gcol_ref[...]
    acc_ref[...] += _mm(act.astype(BF16), wd_ref[0])

    @pl.when((e == pl.num_programs(1) - 1) & (j == pl.num_programs(2) - 1))
    def _():
        o_ref[...] = x_ref[...] + acc_ref[...]


def gated_ffn(x2d, gain, gates, w_gu, w_down, *, tm=512, tf=256):
    n, d = x2d.shape
    ne, f, _ = w_down.shape
    nf = f // tf
    return pl.pallas_call(
        _ffn_kernel,
        grid=(n // tm, ne, nf),
        in_specs=[pl.BlockSpec((tm, d), lambda i, e, j: (i, 0)),
                  pl.BlockSpec((1, d), lambda i, e, j: (0, 0)),
                  pl.BlockSpec((tm, LANES), lambda i, e, j: (i, 0)),
                  pl.BlockSpec((1, d, tf), lambda i, e, j: (e, 0, j)),
                  pl.BlockSpec((1, d, tf), lambda i, e, j: (e, 0, j + nf)),
                  pl.BlockSpec((1, tf, d), lambda i, e, j: (e, j, 0))],
        out_specs=pl.BlockSpec((tm, d), lambda i, e, j: (i, 0)),
        out_shape=jax.ShapeDtypeStruct((n, d), F32),
        scratch_shapes=[pltpu.VMEM((tm, d), BF16), pltpu.VMEM((tm, d), F32), pltpu.VMEM((tm, 1), F32)],
        compiler_params=_params(("parallel", "arbitrary", "arbitrary")),
        name="gated_ffn",
    )(x2d, gain, gates, w_gu, w_gu, w_down)


def _pack_w_in(w, d):
    sizes = (256, 256, 256, 256, 64, 64, 256, 64, 4, 512, 128, 128, N_BRANCH * d)
    qa, ka, va, qb, kb, vb, qi, ki, wi, qc, kc, vc, g = jnp.split(w, np.cumsum(sizes)[:-1].tolist(), axis=-1)
    pad = jnp.zeros((w.shape[0], 60), w.dtype)
    return jnp.concatenate([g, qa * HEAD_DIM ** -0.5, ka, va,
                            qb, qi * D_IDX ** -0.5, kb, vb, ki, wi, pad,
                            qc, kc, vc], axis=-1).astype(BF16)


def kernel(x, mem, rel_bias, norm_mix, w_in, b_gate, qn_dsa, kn_dsa, qn_swa, kn_swa, sinks, w_pa, w_pb, w_pc, w_out, norm_x, norm_mem, w_xq, w_xkv, w_xo, qn_x, kn_x, norm_ffn, w_gu_dense, w_down_dense, w_router, w_gu_moe, w_down_moe):
    b, s, d = x.shape
    depth = w_in.shape[0]
    n = b * s
    row = lambda v: v.reshape(1, -1).astype(F32)
    bias_dsa = dsa_bias_tiles(rel_bias[:, :H_DSA])
    bias_swa = swa_bias_tiles(rel_bias[:, H_DSA:])
    qscale = HEAD_DIM ** -0.5
    ones_gate = jnp.ones((n, LANES), F32)
    gcol = (N_BRANCH * d) // 768

    x2 = x.reshape(n, d)
    for l in range(depth):
        proj = in_projection(x2, row(norm_mix[l]), _pack_w_in(w_in[l], d))
        proj3 = proj.reshape(b, s, -1)
        o_a = sb_attention(proj3, gcol)
        o_b = dsa_attention(proj3, gcol + 1, row(jnp.tile(qn_dsa[l] * qscale, H_DSA)),
                            row(jnp.concatenate([kn_dsa[l], jnp.zeros_like(kn_dsa[l])])), bias_dsa)
        o_c = swa_attention(proj3, gcol + 2, sinks[l].astype(F32),
                            row(jnp.tile(qn_swa[l] * qscale, H_SW)), row(jnp.tile(kn_swa[l], KV_SW)), bias_swa)
        x2 = merge_project(x2, proj, o_a.reshape(n, -1), o_b.reshape(n, -1), o_c.reshape(n, -1),
                           b_gate[l].astype(F32), w_pa[l].astype(BF16), w_pb[l].astype(BF16),
                           w_pc[l].astype(BF16), w_out[l].astype(BF16))
        k_mem, v_mem = memory_kv(mem, row(norm_mem[l]), w_xkv[l].astype(BF16), row(jnp.tile(kn_x[l], H_X)))
        x2 = cross_attention(x2.reshape(b, s, d), row(norm_x[l]), w_xq[l].astype(BF16),
                             row(jnp.tile(qn_x[l] * XHEAD_DIM ** -0.5, H_X)), k_mem, v_mem,
                             w_xo[l].astype(BF16)).reshape(n, d)
        if l % 2 == 0:
            x2 = gated_ffn(x2, row(norm_ffn[l]), ones_gate, w_gu_dense[l // 2][None].astype(BF16),
                           w_down_dense[l // 2][None].astype(BF16), tf=256)
        else:
            wr = jnp.pad(w_router[l // 2].astype(F32), ((0, 0), (0, LANES - N_EXPERTS)))
            wr_hi = wr.astype(BF16)
            wr_lo = (wr - wr_hi.astype(F32)).astype(BF16)
            gates = router_gates(x2, row(norm_ffn[l]), wr_hi, wr_lo)
            x2 = gated_ffn(x2, row(norm_ffn[l]), gates, w_gu_moe[l // 2].astype(BF16),
                           w_down_moe[l // 2].astype(BF16), tf=512)
    return x2.reshape(b, s, d)
```

```python
import functools
import math

import numpy as np
import jax
import jax.numpy as jnp
from jax import lax
from jax.experimental import pallas as pl
from jax.experimental.pallas import tpu as pltpu

HEAD_DIM = 64
H_SB = 4
H_DSA = 4
H_IDX = 4
D_IDX = 64
TOPK_MAX = 256
H_SW = 8
KV_SW = 2
WINDOW = 128
BLOCK = 128
N_BRANCH = 3
N_BUCKETS = 32
MAX_DISTANCE = 128
H_X = 4
XHEAD_DIM = 128
N_EXPERTS = 8
EPS = 1e-6

LANES = 128
VMEM_LIMIT = 56 * 1024 * 1024
NEG = -1e30
INT_MIN = -(2 ** 31)

F32 = jnp.float32
BF16 = jnp.bfloat16


def _nt(a, b):
    return lax.dot_general(a, b, (((1,), (1,)), ((), ())), preferred_element_type=F32)


def _mm(a, b):
    return jnp.dot(a, b, preferred_element_type=F32)


def _rms_rows(x, g):
    ms = jnp.mean(x * x, axis=-1, keepdims=True)
    return x * lax.rsqrt(ms + EPS) * g


def _params(sem):
    return pltpu.CompilerParams(dimension_semantics=sem, vmem_limit_bytes=VMEM_LIMIT)


def _sb_kernel(a_ref, o_ref, *, tq):
    i = pl.program_id(1)
    q0 = pl.multiple_of(i * tq, tq)
    hd = HEAD_DIM
    row = lax.broadcasted_iota(jnp.int32, (tq, tq), 0)
    col = lax.broadcasted_iota(jnp.int32, (tq, tq), 1)
    strict = col < row
    u_inc = jnp.where(row >= col, 1.0, 0.0).astype(BF16)
    qs = [a_ref[0, pl.ds(q0, tq), h * hd:(h + 1) * hd] for h in range(H_SB)]

    def block(k0, accs, carries, diag):
        new_acc, new_carry = [], []
        for h in range(H_SB):
            k = a_ref[0, pl.ds(k0, tq), 256 + h * hd:256 + (h + 1) * hd]
            v = a_ref[0, pl.ds(k0, tq), 512 + h * hd:512 + (h + 1) * hd]
            z = _nt(qs[h], k)
            lk = -(jnp.maximum(z, 0.0) + jnp.log(1.0 + jnp.exp(-jnp.abs(z))))
            if diag:
                lk = jnp.where(strict, lk, 0.0)
            r = _mm(lk.astype(BF16), u_inc)
            att = jnp.exp(z + r + carries[h]) if not diag else jnp.where(strict, jnp.exp(z + r), 0.0)
            pv = _mm(att.astype(BF16), v)
            new_acc.append(pv if diag else accs[h] + pv)
            new_carry.append(r[:, 0:1] if diag else carries[h] + r[:, 0:1])
        return tuple(new_acc), tuple(new_carry)

    accs, carries = block(q0, None, None, True)

    def body(jj, st):
        k0 = pl.multiple_of((i - 1 - jj) * tq, tq)
        return block(k0, st[0], st[1], False)

    accs, _ = lax.fori_loop(0, i, body, (accs, carries))
    o_ref[0] = jnp.concatenate(accs, axis=-1).astype(o_ref.dtype)


def sb_attention(proj, col_block, *, tq=256):
    b, s, _ = proj.shape
    return pl.pallas_call(
        functools.partial(_sb_kernel, tq=tq),
        grid=(b, s // tq),
        in_specs=[pl.BlockSpec((1, s, 768), lambda bi, i: (bi, 0, col_block))],
        out_specs=pl.BlockSpec((1, tq, H_SB * HEAD_DIM), lambda bi, i: (bi, i, 0)),
        out_shape=jax.ShapeDtypeStruct((b, s, H_SB * HEAD_DIM), BF16),
        compiler_params=_params(("parallel", "arbitrary")),
        name="sb_attention",
    )(proj)


def _inproj_kernel(x_ref, g_ref, w_ref, o_ref, xn_ref):
    @pl.when(pl.program_id(1) == 0)
    def _():
        xn_ref[...] = _rms_rows(x_ref[...], g_ref[...]).astype(BF16)

    o_ref[...] = _mm(xn_ref[...], w_ref[...]).astype(o_ref.dtype)


def in_projection(x2d, gain, w, *, tm=512, tn=768):
    n, d = x2d.shape
    c = w.shape[1]
    return pl.pallas_call(
        _inproj_kernel,
        grid=(n // tm, c // tn),
        in_specs=[pl.BlockSpec((tm, d), lambda i, j: (i, 0)),
                  pl.BlockSpec((1, d), lambda i, j: (0, 0)),
                  pl.BlockSpec((d, tn), lambda i, j: (0, j))],
        out_specs=pl.BlockSpec((tm, tn), lambda i, j: (i, j)),
        out_shape=jax.ShapeDtypeStruct((n, c), BF16),
        scratch_shapes=[pltpu.VMEM((tm, d), BF16)],
        compiler_params=_params(("parallel", "arbitrary")),
        name="in_projection",
    )(x2d, gain, w)


def _head_sumsq(x, head_dim):
    r_i = lax.broadcasted_iota(jnp.int32, (LANES, LANES), 0) // head_dim
    c_i = lax.broadcasted_iota(jnp.int32, (LANES, LANES), 1) // head_dim
    bd = jnp.where(r_i == c_i, 1.0, 0.0).astype(F32)
    x2 = x * x
    parts = [_mm(x2[:, b * LANES:(b + 1) * LANES], bd) for b in range(x.shape[1] // LANES)]
    return parts[0] if len(parts) == 1 else jnp.concatenate(parts, axis=-1)


def _head_rms(x, g, head_dim):
    ss = _head_sumsq(x, head_dim)
    return x * lax.rsqrt(ss * (1.0 / head_dim) + EPS) * g


def _eye(n, dtype):
    r = lax.broadcasted_iota(jnp.int32, (n, n), 0)
    c = lax.broadcasted_iota(jnp.int32, (n, n), 1)
    return jnp.where(r == c, 1.0, 0.0).astype(dtype)


def _dsa_kernel(a_ref, gq_ref, gk_ref, bias_ref, o_ref, kbn_ref, vt_ref, keys_ref, *, tq, ck, topk):
    s_len = a_ref.shape[1]
    i = pl.program_id(1)
    q0 = pl.multiple_of(i * tq, tq)
    hd = HEAD_DIM
    nblk = s_len // tq

    @pl.when(i == 0)
    def _():
        kv = a_ref[0, :, 512:640]
        kvf = kv.astype(F32)
        lane = lax.broadcasted_iota(jnp.int32, (1, LANES), 1)
        ss = _head_sumsq(jnp.where(lane < hd, kvf, 0.0), LANES)
        kn = kvf * lax.rsqrt(ss * (1.0 / hd) + EPS) * gk_ref[...]
        kbn_ref[...] = kn[:, :hd].astype(BF16)
        kvt = _nt(_eye(LANES, BF16), kv)
        for jb in range(nblk):
            vt_ref[jb] = kvt[hd:, jb * tq:(jb + 1) * tq].astype(BF16)

    qi = a_ref[0, pl.ds(q0, tq), 256:512]
    wblk = a_ref[0, pl.ds(q0, tq), 640:768]
    sel_r = lax.broadcasted_iota(jnp.int32, (8, LANES), 0)
    sel_c = lax.broadcasted_iota(jnp.int32, (8, LANES), 1)
    w_t = _nt(jnp.where(sel_c == sel_r + D_IDX, 1.0, 0.0).astype(BF16), wblk)
    qpos = q0 + lax.broadcasted_iota(jnp.int32, (1, tq), 1)
    nck = (q0 + tq + ck - 1) // ck
    row_ck = lax.broadcasted_iota(jnp.int32, (ck, tq), 0)

    def p1(c, carry):
        k0 = pl.multiple_of(c * ck, ck)
        kic = a_ref[0, pl.ds(k0, ck), 640:704]
        sc = jnp.zeros((ck, tq), F32)
        for h in range(H_IDX):
            act = jnp.maximum(_nt(kic, qi[:, h * D_IDX:(h + 1) * D_IDX]), 0.0)
            sc = sc + w_t[h:h + 1, :] * act
        bits = pltpu.bitcast(sc, jnp.int32)
        key = jnp.where(bits < 0, -(bits & 0x7FFFFFFF), bits)
        key = jnp.where(k0 + row_ck <= qpos, key, INT_MIN)
        keys_ref[pl.ds(k0, ck), :] = key
        return carry

    lax.fori_loop(0, nck, p1, 0)

    def count(pred):
        def body(c, acc):
            k0 = pl.multiple_of(c * ck, ck)
            m = jnp.where(pred(keys_ref[pl.ds(k0, ck), :], k0 + row_ck), 1, 0)
            return acc + jnp.sum(m.reshape(ck // 8, 8, tq), axis=0)
        acc = lax.fori_loop(0, nck, body, jnp.zeros((8, tq), jnp.int32))
        return jnp.sum(acc, axis=0, keepdims=True)

    def search():
        c0 = count(lambda k, p: k >= 0)
        t = jnp.where(c0 >= topk, 0, INT_MIN).astype(jnp.int32)

        def vstep(b, t):
            cand = t | lax.shift_left(jnp.int32(1), 30 - b)
            return jnp.where(count(lambda k, p: k >= cand) >= topk, cand, t)

        t = lax.fori_loop(0, 31, vstep, t)
        need = topk - count(lambda k, p: k > t)
        nbits = max(1, (s_len - 1).bit_length())

        def istep(b, y):
            cand = y | lax.shift_left(jnp.int32(1), nbits - 1 - b)
            return jnp.where(count(lambda k, p: (k == t) & (p < cand)) < need, cand, y)

        y = lax.fori_loop(0, nbits, istep, jnp.zeros((1, tq), jnp.int32))
        y = jnp.where(t == INT_MIN, -1, y)
        return t, y

    def no_search():
        return (jnp.full((1, tq), INT_MIN, jnp.int32), jnp.full((1, tq), -1, jnp.int32))

    thr, ymax = lax.cond(q0 + tq > topk, search, no_search)

    qb = a_ref[0, pl.ds(q0, tq), 0:256].astype(F32)
    qn = _head_rms(qb, gq_ref[...], hd).astype(BF16)
    qh = [qn[:, h * hd:(h + 1) * hd] for h in range(H_DSA)]
    row_b = lax.broadcasted_iota(jnp.int32, (tq, tq), 0)

    def p3(j, st):
        ms, ls, accs = st
        k0 = pl.multiple_of(j * tq, tq)
        kj = kbn_ref[pl.ds(k0, tq), :]
        vtj = vt_ref[j]
        key = keys_ref[pl.ds(k0, tq), :]
        sel = (key > thr) | ((key == thr) & (k0 + row_b <= ymax))
        bidx = jnp.minimum(i - j, 2)
        nm, nl, na = [], [], []
        for h in range(H_DSA):
            st_ = _nt(kj, qh[h]) + bias_ref[bidx, h]
            mb = jnp.max(jnp.where(sel, st_, NEG), axis=0, keepdims=True)
            m_new = jnp.maximum(ms[h], mb)
            p = jnp.where(sel, jnp.exp(st_ - m_new), 0.0)
            alpha = jnp.exp(ms[h] - m_new)
            nm.append(m_new)
            nl.append(ls[h] * alpha + jnp.sum(p, axis=0, keepdims=True))
            na.append(accs[h] * alpha + _mm(vtj, p.astype(BF16)))
        return tuple(nm), tuple(nl), tuple(na)

    init = (tuple(jnp.full((1, tq), NEG, F32) for _ in range(H_DSA)),
            tuple(jnp.zeros((1, tq), F32) for _ in range(H_DSA)),
            tuple(jnp.zeros((hd, tq), F32) for _ in range(H_DSA)))
    _, ls, accs = lax.fori_loop(0, i + 1, p3, init)
    o_t = jnp.concatenate([accs[h] / ls[h] for h in range(H_DSA)], axis=0)
    o_ref[0] = _nt(_eye(tq, BF16), o_t.astype(BF16)).astype(o_ref.dtype)


def dsa_attention(proj, col_block, gq, gk, bias_t, *, tq=BLOCK, ck=512):
    b, s, _ = proj.shape
    topk = min(TOPK_MAX, s // 4)
    ck = min(ck, s)
    return pl.pallas_call(
        functools.partial(_dsa_kernel, tq=tq, ck=ck, topk=topk),
        grid=(b, s // tq),
        in_specs=[pl.BlockSpec((1, s, 768), lambda bi, i: (bi, 0, col_block)),
                  pl.BlockSpec((1, 256), lambda bi, i: (0, 0)),
                  pl.BlockSpec((1, LANES), lambda bi, i: (0, 0)),
                  pl.BlockSpec((3, H_DSA, tq, tq), lambda bi, i: (0, 0, 0, 0))],
        out_specs=pl.BlockSpec((1, tq, H_DSA * HEAD_DIM), lambda bi, i: (bi, i, 0)),
        out_shape=jax.ShapeDtypeStruct((b, s, H_DSA * HEAD_DIM), BF16),
        scratch_shapes=[pltpu.VMEM((s, HEAD_DIM), BF16),
                        pltpu.VMEM((s // tq, HEAD_DIM, tq), BF16),
                        pltpu.VMEM((s, tq), jnp.int32)],
        compiler_params=_params(("parallel", "arbitrary")),
        name="dsa_attention",
    )(proj, gq, gk, bias_t)


def _t5_bucket(rel):
    n = jnp.maximum(rel, 0)
    max_exact = N_BUCKETS // 2
    nf = jnp.maximum(n, 1).astype(F32)
    large = max_exact + (jnp.log(nf / max_exact) / math.log(MAX_DISTANCE / max_exact)
                         * (N_BUCKETS - max_exact)).astype(jnp.int32)
    large = jnp.minimum(large, N_BUCKETS - 1)
    return jnp.where(n < max_exact, n, large)


def dsa_bias_tiles(tab, tq=BLOCK):
    ks = jnp.arange(tq)[:, None]
    tl = jnp.arange(tq)[None, :]
    rel = jnp.stack([tl - ks, tq + tl - ks, jnp.full((tq, tq), 2 * tq + MAX_DISTANCE)])
    return tab[_t5_bucket(rel)].transpose(0, 3, 1, 2).astype(F32)


def _swa_kernel(sink_ref, cur_ref, prev_ref, gq_ref, gk_ref, bias_ref, o_ref, *, tq):
    i = pl.program_id(1)
    hd = HEAD_DIM
    g = H_SW // KV_SW
    qn = _head_rms(cur_ref[0, :, 0:512].astype(F32), gq_ref[...], hd).astype(BF16)
    k_cur = _head_rms(cur_ref[0, :, 512:640].astype(F32), gk_ref[...], hd).astype(BF16)
    k_prev = _head_rms(prev_ref[0, :, 512:640].astype(F32), gk_ref[...], hd).astype(BF16)
    k2 = jnp.concatenate([k_prev, k_cur], axis=0)
    v2 = jnp.concatenate([prev_ref[0, :, 640:768], cur_ref[0, :, 640:768]], axis=0)
    col = lax.broadcasted_iota(jnp.int32, (tq, 2 * tq), 1)
    has_prev = (col >= tq) | (i > 0)
    outs = []
    for h in range(H_SW):
        kv = h // g
        s = _nt(qn[:, h * hd:(h + 1) * hd], k2[:, kv * hd:(kv + 1) * hd]) + bias_ref[h]
        s = jnp.where(has_prev, s, NEG)
        sink = sink_ref[h]
        m = jnp.maximum(jnp.max(s, axis=-1, keepdims=True), sink)
        e = jnp.exp(s - m)
        den = jnp.sum(e, axis=-1, keepdims=True) + jnp.exp(sink - m)
        outs.append(_mm(e.astype(BF16), v2[:, kv * hd:(kv + 1) * hd]) / den)
    o_ref[0] = jnp.concatenate(outs, axis=-1).astype(o_ref.dtype)


def swa_bias_tiles(tab, tq=BLOCK):
    rel = (jnp.arange(tq)[:, None] + tq) - jnp.arange(2 * tq)[None, :]
    in_win = (rel >= 0) & (rel < WINDOW)
    bias = tab[_t5_bucket(rel)].transpose(2, 0, 1).astype(F32)
    return jnp.where(in_win[None], bias, NEG)


def swa_attention(proj, col_block, sinks, gq, gk, bias, *, tq=BLOCK):
    b, s, _ = proj.shape
    return pl.pallas_call(
        functools.partial(_swa_kernel, tq=tq),
        grid_spec=pltpu.PrefetchScalarGridSpec(
            num_scalar_prefetch=0,
            grid=(b, s // tq),
            in_specs=[pl.BlockSpec(memory_space=pltpu.SMEM),
                      pl.BlockSpec((1, tq, 768), lambda bi, i: (bi, i, col_block)),
                      pl.BlockSpec((1, tq, 768), lambda bi, i: (bi, jnp.maximum(i - 1, 0), col_block)),
                      pl.BlockSpec((1, 512), lambda bi, i: (0, 0)),
                      pl.BlockSpec((1, LANES), lambda bi, i: (0, 0)),
                      pl.BlockSpec((H_SW, tq, 2 * tq), lambda bi, i: (0, 0, 0))],
            out_specs=pl.BlockSpec((1, tq, H_SW * HEAD_DIM), lambda bi, i: (bi, i, 0))),
        out_shape=jax.ShapeDtypeStruct((b, s, H_SW * HEAD_DIM), BF16),
        compiler_params=_params(("parallel", "arbitrary")),
        name="swa_attention",
    )(sinks, proj, proj, gq, gk, bias)


def _merge_kernel(x_ref, g_ref, oa_ref, ob_ref, oc_ref, bg_ref, wa_ref, wb_ref, wc_ref, wo_ref, o_ref):
    d = x_ref.shape[1]
    merged = None
    for k, (o_k, w_k) in enumerate(((oa_ref, wa_ref), (ob_ref, wb_ref), (oc_ref, wc_ref))):
        logit = g_ref[:, k * d:(k + 1) * d].astype(F32) + bg_ref[k:k + 1, :]
        gate = 1.0 / (1.0 + jnp.exp(-logit))
        term = gate * _mm(o_k[...], w_k[...])
        merged = term if merged is None else merged + term
    o_ref[...] = x_ref[...] + _mm(merged.astype(BF16), wo_ref[...])


def merge_project(x2d, proj2d, o_a, o_b, o_c, b_gate, w_pa, w_pb, w_pc, w_out, *, tm=512):
    n, d = x2d.shape
    full = lambda a: pl.BlockSpec(a.shape, lambda i: (0,) * a.ndim)
    row = lambda a: pl.BlockSpec((tm, a.shape[1]), lambda i: (i, 0))
    return pl.pallas_call(
        _merge_kernel,
        grid=(n // tm,),
        in_specs=[row(x2d), pl.BlockSpec((tm, N_BRANCH * d), lambda i: (i, 0)),
                  row(o_a), row(o_b), row(o_c),
                  full(b_gate), full(w_pa), full(w_pb), full(w_pc), full(w_out)],
        out_specs=pl.BlockSpec((tm, d), lambda i: (i, 0)),
        out_shape=jax.ShapeDtypeStruct((n, d), F32),
        compiler_params=_params(("parallel",)),
        name="merge_project",
    )(x2d, proj2d, o_a, o_b, o_c, b_gate, w_pa, w_pb, w_pc, w_out)


def _memkv_kernel(m_ref, g_ref, w_ref, gk_ref, k_ref, v_ref):
    w_x = k_ref.shape[2]
    mn = _rms_rows(m_ref[0], g_ref[...]).astype(BF16)
    kv = _mm(mn, w_ref[...])
    k_ref[0] = _head_rms(kv[:, :w_x], gk_ref[...], XHEAD_DIM).astype(k_ref.dtype)
    v_ref[0] = kv[:, w_x:].astype(v_ref.dtype)


def memory_kv(mem, gain, w_kv, gk):
    b, m, d = mem.shape
    w_x = w_kv.shape[1] // 2
    return pl.pallas_call(
        _memkv_kernel,
        grid=(b,),
        in_specs=[pl.BlockSpec((1, m, d), lambda i: (i, 0, 0)),
                  pl.BlockSpec((1, d), lambda i: (0, 0)),
                  pl.BlockSpec(w_kv.shape, lambda i: (0, 0)),
                  pl.BlockSpec((1, w_x), lambda i: (0, 0))],
        out_specs=[pl.BlockSpec((1, m, w_x), lambda i: (i, 0, 0)),
                   pl.BlockSpec((1, m, w_x), lambda i: (i, 0, 0))],
        out_shape=[jax.ShapeDtypeStruct((b, m, w_x), BF16)] * 2,
        compiler_params=_params(("parallel",)),
        name="memory_kv",
    )(mem, gain, w_kv, gk)


def _xattn_kernel(x_ref, g_ref, wq_ref, gq_ref, k_ref, v_ref, wo_ref, o_ref):
    x = x_ref[0]
    xn = _rms_rows(x, g_ref[...]).astype(BF16)
    q = _head_rms(_mm(xn, wq_ref[...]), gq_ref[...], XHEAD_DIM).astype(BF16)
    outs = []
    for h in range(H_X):
        sl = slice(h * XHEAD_DIM, (h + 1) * XHEAD_DIM)
        s = _nt(q[:, sl], k_ref[0, :, sl])
        e = jnp.exp(s - jnp.max(s, axis=-1, keepdims=True))
        den = jnp.sum(e, axis=-1, keepdims=True)
        outs.append(_mm(e.astype(BF16), v_ref[0, :, sl]) / den)
    o = jnp.concatenate(outs, axis=-1).astype(BF16)
    o_ref[0] = x + _mm(o, wo_ref[...])


def cross_attention(x, gain, w_q, gq, k_mem, v_mem, w_o, *, tq=256):
    b, s, d = x.shape
    m, w_x = k_mem.shape[1:]
    full = lambda a: pl.BlockSpec(a.shape, lambda bi, i: (0,) * a.ndim)
    return pl.pallas_call(
        _xattn_kernel,
        grid=(b, s // tq),
        in_specs=[pl.BlockSpec((1, tq, d), lambda bi, i: (bi, i, 0)),
                  full(gain), full(w_q), full(gq),
                  pl.BlockSpec((1, m, w_x), lambda bi, i: (bi, 0, 0)),
                  pl.BlockSpec((1, m, w_x), lambda bi, i: (bi, 0, 0)),
                  full(w_o)],
        out_specs=pl.BlockSpec((1, tq, d), lambda bi, i: (bi, i, 0)),
        out_shape=jax.ShapeDtypeStruct((b, s, d), F32),
        compiler_params=_params(("parallel", "parallel")),
        name="cross_attention",
    )(x, gain, w_q, gq, k_mem, v_mem, w_o)


def _router_kernel(x_ref, g_ref, whi_ref, wlo_ref, o_ref):
    hf = _rms_rows(x_ref[...], g_ref[...])
    hi = hf.astype(BF16)
    lo = (hf - hi.astype(F32)).astype(BF16)
    logits = _mm(hi, whi_ref[...]) + (_mm(hi, wlo_ref[...]) + _mm(lo, whi_ref[...]))
    lane = lax.broadcasted_iota(jnp.int32, logits.shape, 1).astype(F32)
    logits = jnp.where(lane < N_EXPERTS, logits, NEG)
    m1 = jnp.max(logits, axis=-1, keepdims=True)
    i1 = jnp.min(jnp.where(logits == m1, lane, float(LANES)), axis=-1, keepdims=True)
    rest = jnp.where(lane == i1, NEG, logits)
    m2 = jnp.max(rest, axis=-1, keepdims=True)
    i2 = jnp.min(jnp.where(rest == m2, lane, float(LANES)), axis=-1, keepdims=True)
    e2 = jnp.exp(m2 - m1)
    den = 1.0 + e2
    o_ref[...] = (jnp.where(lane == 0.0, i1, 0.0) + jnp.where(lane == 1.0, i2, 0.0)
                  + jnp.where(lane == 2.0, 1.0 / den, 0.0) + jnp.where(lane == 3.0, e2 / den, 0.0))


def router_gates(x2d, gain, w_hi, w_lo, *, tm=512):
    n, d = x2d.shape
    return pl.pallas_call(
        _router_kernel,
        grid=(n // tm,),
        in_specs=[pl.BlockSpec((tm, d), lambda i: (i, 0)),
                  pl.BlockSpec((1, d), lambda i: (0, 0)),
                  pl.BlockSpec((d, LANES), lambda i: (0, 0)),
                  pl.BlockSpec((d, LANES), lambda i: (0, 0))],
        out_specs=pl.BlockSpec((tm, LANES), lambda i: (i, 0)),
        out_shape=jax.ShapeDtypeStruct((n, LANES), F32),
        compiler_params=_params(("parallel",)),
        name="router_gates",
    )(x2d, gain, w_hi, w_lo)


def _swiglu_tile(xn, wg, wu, wd):
    gg = _mm(xn, wg)
    uu = _mm(xn, wu)
    act = gg * (1.0 / (1.0 + jnp.exp(-gg))) * uu
    return _mm(act.astype(BF16), wd)


def _ffn_kernel(x_ref, g_ref, wg_ref, wu_ref, wd_ref, o_ref, xn_ref, acc_ref):
    j = pl.program_id(1)

    @pl.when(j == 0)
    def _():
        xn_ref[...] = _rms_rows(x_ref[...], g_ref[...]).astype(BF16)
        acc_ref[...] = jnp.zeros_like(acc_ref)

    acc_ref[...] += _swiglu_tile(xn_ref[...], wg_ref[...], wu_ref[...], wd_ref[...])

    @pl.when(j == pl.num_programs(1) - 1)
    def _():
        o_ref[...] = x_ref[...] + acc_ref[...]


def dense_ffn(x2d, gain, w_gu, w_down, *, tm=512, tf=256):
    n, d = x2d.shape
    f = w_down.shape[0]
    nf = f // tf
    return pl.pallas_call(
        _ffn_kernel,
        grid=(n // tm, nf),
        in_specs=[pl.BlockSpec((tm, d), lambda i, j: (i, 0)),
                  pl.BlockSpec((1, d), lambda i, j: (0, 0)),
                  pl.BlockSpec((d, tf), lambda i, j: (0, j)),
                  pl.BlockSpec((d, tf), lambda i, j: (0, j + nf)),
                  pl.BlockSpec((tf, d), lambda i, j: (j, 0))],
        out_specs=pl.BlockSpec((tm, d), lambda i, j: (i, 0)),
        out_shape=jax.ShapeDtypeStruct((n, d), F32),
        scratch_shapes=[pltpu.VMEM((tm, d), BF16), pltpu.VMEM((tm, d), F32)],
        compiler_params=_params(("parallel", "arbitrary")),
        name="dense_ffn",
    )(x2d, gain, w_gu, w_gu, w_down)


def _moe_kernel(te_ref, tn_ref, tok_ref, tok_next_ref, dst_ref, x_hbm, g_ref, gate_ref, wg_ref, wu_ref, wd_ref,
                y_hbm, xg_ref, xn_ref, acc_ref, yb_ref, gsem, ssem, *, tm):
    i = pl.program_id(0)
    j = pl.program_id(1)
    nt = pl.num_programs(0)
    nf = pl.num_programs(1)
    slot = i % 2
    n_rows = tn_ref[i]

    def gather_copy(tok, r, s):
        return pltpu.make_async_copy(x_hbm.at[pl.ds(tok, 1), :], xg_ref.at[s, pl.ds(r, 1), :], gsem.at[s])

    def scatter_copy(dst, r, s):
        return pltpu.make_async_copy(yb_ref.at[s, pl.ds(r, 1), :], y_hbm.at[pl.ds(dst, 1), :], ssem.at[s])

    def start_gather(ids_ref, s):
        def body(r, c):
            gather_copy(ids_ref[0, 0, r], r, s).start()
            return c
        lax.fori_loop(0, tm, body, 0)

    def wait_rows(make_copy, count, s):
        def body(r, c):
            make_copy(0, r, s).wait()
            return c
        lax.fori_loop(0, count, body, 0)

    @pl.when(n_rows > 0)
    def _():
        @pl.when(j == 0)
        def _():
            @pl.when(i == 0)
            def _():
                start_gather(tok_ref, slot)

            wait_rows(gather_copy, tm, slot)
            xn_ref[...] = _rms_rows(xg_ref[slot], g_ref[...]).astype(BF16)
            acc_ref[...] = jnp.zeros_like(acc_ref)
            nxt = jnp.minimum(i + 1, nt - 1)

            @pl.when((i + 1 < nt) & (tn_ref[nxt] > 0))
            def _():
                start_gather(tok_next_ref, 1 - slot)

        acc_ref[...] += _swiglu_tile(xn_ref[...], wg_ref[0], wu_ref[0], wd_ref[0])

        @pl.when(j == nf - 1)
        def _():
            yb_ref[slot] = acc_ref[...] * gate_ref[...]

            def body(r, c):
                scatter_copy(dst_ref[0, 0, r], r, slot).start()
                return c
            lax.fori_loop(0, n_rows, body, 0)

            @pl.when(i > 0)
            def _():
                wait_rows(scatter_copy, tn_ref[jnp.maximum(i - 1, 0)], 1 - slot)

            nxt = jnp.minimum(i + 1, nt - 1)

            @pl.when((i == nt - 1) | (tn_ref[nxt] == 0))
            def _():
                wait_rows(scatter_copy, n_rows, slot)


def moe_experts(x2d, gain, plan, w_gu, w_down, *, tm, tf=512):
    n, d = x2d.shape
    ne, f, _ = w_down.shape
    nf = f // tf
    tile_e, tile_n, row_tok, row_dst, row_gate = plan
    nt = tile_e.shape[0]
    smem_rows = lambda imap: pl.BlockSpec((1, 1, tm), imap, memory_space=pltpu.SMEM)
    live = lambda j, tn, i: j * jnp.minimum(tn[i], 1)
    return pl.pallas_call(
        functools.partial(_moe_kernel, tm=tm),
        grid_spec=pltpu.PrefetchScalarGridSpec(
            num_scalar_prefetch=2,
            grid=(nt, nf),
            in_specs=[smem_rows(lambda i, j, te, tn: (i, 0, 0)),
                      smem_rows(lambda i, j, te, tn: (jnp.minimum(i + 1, nt - 1), 0, 0)),
                      smem_rows(lambda i, j, te, tn: (i, 0, 0)),
                      pl.BlockSpec(memory_space=pl.ANY),
                      pl.BlockSpec((1, d), lambda i, j, te, tn: (0, 0)),
                      pl.BlockSpec((tm, 1), lambda i, j, te, tn: (i, 0)),
                      pl.BlockSpec((1, d, tf), lambda i, j, te, tn: (te[i], 0, live(j, tn, i))),
                      pl.BlockSpec((1, d, tf), lambda i, j, te, tn: (te[i], 0, live(j, tn, i) + nf)),
                      pl.BlockSpec((1, tf, d), lambda i, j, te, tn: (te[i], live(j, tn, i), 0))],
            out_specs=pl.BlockSpec(memory_space=pl.ANY),
            scratch_shapes=[pltpu.VMEM((2, tm, d), F32), pltpu.VMEM((tm, d), BF16), pltpu.VMEM((tm, d), F32),
                            pltpu.VMEM((2, tm, d), F32),
                            pltpu.SemaphoreType.DMA((2,)), pltpu.SemaphoreType.DMA((2,))]),
        out_shape=jax.ShapeDtypeStruct((2 * n, d), F32),
        compiler_params=_params(("arbitrary", "arbitrary")),
        name="moe_experts",
    )(tile_e, tile_n, row_tok, row_tok, row_dst, x2d, gain, row_gate, w_gu, w_gu, w_down)


def moe_plan(route, *, tm):
    n = route.shape[0]
    flat_e = route[:, :2].astype(jnp.int32).reshape(-1)
    flat_g = route[:, 2:4].reshape(-1)
    nt = (2 * n) // tm + N_EXPERTS
    order = jnp.argsort(flat_e, stable=True).astype(jnp.int32)
    counts = jnp.sum(flat_e[:, None] == jnp.arange(N_EXPERTS)[None, :], axis=0).astype(jnp.int32)
    off = jnp.cumsum(counts) - counts
    tiles = (counts + tm - 1) // tm
    tile_off = jnp.cumsum(tiles) - tiles
    tile_id = jnp.arange(nt, dtype=jnp.int32)
    used = tile_id < jnp.sum(tiles)
    tile_e = jnp.clip(jnp.sum(tile_id[:, None] >= tile_off[None, :], axis=1) - 1, 0, N_EXPERTS - 1)
    tile_e = jnp.where(used, tile_e, tile_e[jnp.maximum(jnp.sum(tiles) - 1, 0)]).astype(jnp.int32)
    first_row = (tile_id - tile_off[tile_e]) * tm
    tile_n = jnp.where(used, jnp.clip(counts[tile_e] - first_row, 0, tm), 0).astype(jnp.int32)
    r = jnp.arange(tm, dtype=jnp.int32)[None, :]
    valid = r < tile_n[:, None]
    a = order[jnp.clip(off[tile_e][:, None] + first_row[:, None] + r, 0, 2 * n - 1)]
    row_tok = jnp.where(valid, a // 2, 0).astype(jnp.int32).reshape(nt, 1, tm)
    row_dst = jnp.where(valid, a, 0).astype(jnp.int32).reshape(nt, 1, tm)
    row_gate = jnp.where(valid, flat_g[a], 0.0).astype(F32).reshape(nt * tm, 1)
    return tile_e, tile_n, row_tok, row_dst, row_gate


def _combine_kernel(x_ref, y_ref, o_ref):
    d = x_ref.shape[1]
    o_ref[...] = x_ref[...] + (y_ref[:, :d] + y_ref[:, d:])


def moe_combine(x2d, y, *, tm=512):
    n, d = x2d.shape
    return pl.pallas_call(
        _combine_kernel,
        grid=(n // tm,),
        in_specs=[pl.BlockSpec((tm, d), lambda i: (i, 0)), pl.BlockSpec((tm, 2 * d), lambda i: (i, 0))],
        out_specs=pl.BlockSpec((tm, d), lambda i: (i, 0)),
        out_shape=jax.ShapeDtypeStruct((n, d), F32),
        compiler_params=_params(("parallel",)),
        name="moe_combine",
    )(x2d, y.reshape(n, 2 * d))


def _pack_w_in(w, d):
    sizes = (256, 256, 256, 256, 64, 64, 256, 64, 4, 512, 128, 128, N_BRANCH * d)
    qa, ka, va, qb, kb, vb, qi, ki, wi, qc, kc, vc, g = jnp.split(w, np.cumsum(sizes)[:-1].tolist(), axis=-1)
    pad = jnp.zeros((w.shape[0], 60), w.dtype)
    return jnp.concatenate([g, qa * HEAD_DIM ** -0.5, ka, va,
                            qb, qi * D_IDX ** -0.5, kb, vb, ki, wi, pad,
                            qc, kc, vc], axis=-1).astype(BF16)


def kernel(x, mem, rel_bias, norm_mix, w_in, b_gate, qn_dsa, kn_dsa, qn_swa, kn_swa, sinks, w_pa, w_pb, w_pc, w_out, norm_x, norm_mem, w_xq, w_xkv, w_xo, qn_x, kn_x, norm_ffn, w_gu_dense, w_down_dense, w_router, w_gu_moe, w_down_moe):
    b, s, d = x.shape
    depth = w_in.shape[0]
    n = b * s
    row = lambda v: v.reshape(1, -1).astype(F32)
    bias_dsa = dsa_bias_tiles(rel_bias[:, :H_DSA])
    bias_swa = swa_bias_tiles(rel_bias[:, H_DSA:])
    qscale = HEAD_DIM ** -0.5
    gcol = (N_BRANCH * d) // 768

    x2 = x.reshape(n, d)
    for l in range(depth):
        proj = in_projection(x2, row(norm_mix[l]), _pack_w_in(w_in[l], d))
        proj3 = proj.reshape(b, s, -1)
        o_a = sb_attention(proj3, gcol)
        o_b = dsa_attention(proj3, gcol + 1, row(jnp.tile(qn_dsa[l] * qscale, H_DSA)),
                            row(jnp.concatenate([kn_dsa[l], jnp.zeros_like(kn_dsa[l])])), bias_dsa)
        o_c = swa_attention(proj3, gcol + 2, sinks[l].astype(F32),
                            row(jnp.tile(qn_swa[l] * qscale, H_SW)), row(jnp.tile(kn_swa[l], KV_SW)), bias_swa)
        x2 = merge_project(x2, proj, o_a.reshape(n, -1), o_b.reshape(n, -1), o_c.reshape(n, -1),
                           b_gate[l].astype(F32), w_pa[l].astype(BF16), w_pb[l].astype(BF16),
                           w_pc[l].astype(BF16), w_out[l].astype(BF16))
        k_mem, v_mem = memory_kv(mem, row(norm_mem[l]), w_xkv[l].astype(BF16), row(jnp.tile(kn_x[l], H_X)))
        x2 = cross_attention(x2.reshape(b, s, d), row(norm_x[l]), w_xq[l].astype(BF16),
                             row(jnp.tile(qn_x[l] * XHEAD_DIM ** -0.5, H_X)), k_mem, v_mem,
                             w_xo[l].astype(BF16)).reshape(n, d)
        if l % 2 == 0:
            x2 = dense_ffn(x2, row(norm_ffn[l]), w_gu_dense[l // 2].astype(BF16),
                           w_down_dense[l // 2].astype(BF16), tf=256)
        else:
            wr = jnp.pad(w_router[l // 2].astype(F32), ((0, 0), (0, LANES - N_EXPERTS)))
            wr_hi = wr.astype(BF16)
            wr_lo = (wr - wr_hi.astype(F32)).astype(BF16)
            route = router_gates(x2, row(norm_ffn[l]), wr_hi, wr_lo)
            tm_moe = min(512, n)
            y = moe_experts(x2, row(norm_ffn[l]), moe_plan(route, tm=tm_moe), w_gu_moe[l // 2].astype(BF16),
                            w_down_moe[l // 2].astype(BF16), tm=tm_moe)
            x2 = moe_combine(x2, y)
    return x2.reshape(b, s, d)
```

```python
import functools
import math

import numpy as np
import jax
import jax.numpy as jnp
from jax import lax
from jax.experimental import pallas as pl
from jax.experimental.pallas import tpu as pltpu

HEAD_DIM = 64
H_SB = 4
H_DSA = 4
H_IDX = 4
D_IDX = 64
TOPK_MAX = 256
H_SW = 8
KV_SW = 2
WINDOW = 128
BLOCK = 128
N_BRANCH = 3
N_BUCKETS = 32
MAX_DISTANCE = 128
H_X = 4
XHEAD_DIM = 128
N_EXPERTS = 8
EPS = 1e-6

LANES = 128
VMEM_LIMIT = 56 * 1024 * 1024
NEG = -1e30
INT_MIN = -(2 ** 31)

F32 = jnp.float32
BF16 = jnp.bfloat16


def _nt(a, b):
    return lax.dot_general(a, b, (((1,), (1,)), ((), ())), preferred_element_type=F32)


def _mm(a, b):
    return jnp.dot(a, b, preferred_element_type=F32)


def _rms_rows(x, g):
    ms = jnp.mean(x * x, axis=-1, keepdims=True)
    return x * lax.rsqrt(ms + EPS) * g


def _params(sem):
    return pltpu.CompilerParams(dimension_semantics=sem, vmem_limit_bytes=VMEM_LIMIT)


def _sb_kernel(a_ref, o_ref, *, tq):
    i = pl.program_id(1)
    q0 = pl.multiple_of(i * tq, tq)
    hd = HEAD_DIM
    row = lax.broadcasted_iota(jnp.int32, (tq, tq), 0)
    col = lax.broadcasted_iota(jnp.int32, (tq, tq), 1)
    strict = col < row
    u_inc = jnp.where(row >= col, 1.0, 0.0).astype(BF16)
    qs = [a_ref[0, pl.ds(q0, tq), h * hd:(h + 1) * hd] for h in range(H_SB)]

    def block(k0, accs, carries, diag):
        new_acc, new_carry = [], []
        for h in range(H_SB):
            k = a_ref[0, pl.ds(k0, tq), 256 + h * hd:256 + (h + 1) * hd]
            v = a_ref[0, pl.ds(k0, tq), 512 + h * hd:512 + (h + 1) * hd]
            z = _nt(qs[h], k)
            lk = -(jnp.maximum(z, 0.0) + jnp.log(1.0 + jnp.exp(-jnp.abs(z))))
            if diag:
                lk = jnp.where(strict, lk, 0.0)
            r = _mm(lk.astype(BF16), u_inc)
            att = jnp.exp(z + r + carries[h]) if not diag else jnp.where(strict, jnp.exp(z + r), 0.0)
            pv = _mm(att.astype(BF16), v)
            new_acc.append(pv if diag else accs[h] + pv)
            new_carry.append(r[:, 0:1] if diag else carries[h] + r[:, 0:1])
        return tuple(new_acc), tuple(new_carry)

    accs, carries = block(q0, None, None, True)

    def body(jj, st):
        k0 = pl.multiple_of((i - 1 - jj) * tq, tq)
        return block(k0, st[0], st[1], False)

    accs, _ = lax.fori_loop(0, i, body, (accs, carries))
    o_ref[0] = jnp.concatenate(accs, axis=-1).astype(o_ref.dtype)


def sb_attention(proj, col_block, *, tq=256):
    b, s, _ = proj.shape
    return pl.pallas_call(
        functools.partial(_sb_kernel, tq=tq),
        grid=(b, s // tq),
        in_specs=[pl.BlockSpec((1, s, 768), lambda bi, i: (bi, 0, col_block))],
        out_specs=pl.BlockSpec((1, tq, H_SB * HEAD_DIM), lambda bi, i: (bi, i, 0)),
        out_shape=jax.ShapeDtypeStruct((b, s, H_SB * HEAD_DIM), BF16),
        compiler_params=_params(("parallel", "arbitrary")),
        name="sb_attention",
    )(proj)


def _inproj_kernel(x_ref, g_ref, w_ref, o_ref, xn_ref):
    @pl.when(pl.program_id(1) == 0)
    def _():
        xn_ref[...] = _rms_rows(x_ref[...], g_ref[...]).astype(BF16)

    o_ref[...] = _mm(xn_ref[...], w_ref[...]).astype(o_ref.dtype)


def in_projection(x2d, gain, w, *, tm=512, tn=768):
    n, d = x2d.shape
    c = w.shape[1]
    return pl.pallas_call(
        _inproj_kernel,
        grid=(n // tm, c // tn),
        in_specs=[pl.BlockSpec((tm, d), lambda i, j: (i, 0)),
                  pl.BlockSpec((1, d), lambda i, j: (0, 0)),
                  pl.BlockSpec((d, tn), lambda i, j: (0, j))],
        out_specs=pl.BlockSpec((tm, tn), lambda i, j: (i, j)),
        out_shape=jax.ShapeDtypeStruct((n, c), BF16),
        scratch_shapes=[pltpu.VMEM((tm, d), BF16)],
        compiler_params=_params(("parallel", "arbitrary")),
        name="in_projection",
    )(x2d, gain, w)


def _head_sumsq(x, head_dim):
    r_i = lax.broadcasted_iota(jnp.int32, (LANES, LANES), 0) // head_dim
    c_i = lax.broadcasted_iota(jnp.int32, (LANES, LANES), 1) // head_dim
    bd = jnp.where(r_i == c_i, 1.0, 0.0).astype(F32)
    x2 = x * x
    parts = [_mm(x2[:, b * LANES:(b + 1) * LANES], bd) for b in range(x.shape[1] // LANES)]
    return parts[0] if len(parts) == 1 else jnp.concatenate(parts, axis=-1)


def _head_rms(x, g, head_dim):
    ss = _head_sumsq(x, head_dim)
    return x * lax.rsqrt(ss * (1.0 / head_dim) + EPS) * g


def _eye(n, dtype):
    r = lax.broadcasted_iota(jnp.int32, (n, n), 0)
    c = lax.broadcasted_iota(jnp.int32, (n, n), 1)
    return jnp.where(r == c, 1.0, 0.0).astype(dtype)


def _dsa_kernel(a_ref, gq_ref, gk_ref, bias_ref, o_ref, kbn_ref, vt_ref, keys_ref, *, tq, ck, topk):
    s_len = a_ref.shape[1]
    i = pl.program_id(1)
    q0 = pl.multiple_of(i * tq, tq)
    hd = HEAD_DIM
    sub = ck // tq

    @pl.when(i == 0)
    def _():
        kv = a_ref[0, :, 512:640]
        kvf = kv.astype(F32)
        lane = lax.broadcasted_iota(jnp.int32, (1, LANES), 1)
        ss = _head_sumsq(jnp.where(lane < hd, kvf, 0.0), LANES)
        kn = kvf * lax.rsqrt(ss * (1.0 / hd) + EPS) * gk_ref[...]
        kbn_ref[...] = kn[:, :hd].astype(BF16)
        kvt = _nt(_eye(LANES, BF16), kv)
        for cb in range(s_len // ck):
            vt_ref[cb] = kvt[hd:, cb * ck:(cb + 1) * ck].astype(BF16)

    def heads_on_rows(x):
        return jnp.concatenate([x[:, h * hd:(h + 1) * hd] for h in range(x.shape[1] // hd)], axis=0)

    qi_all = heads_on_rows(a_ref[0, pl.ds(q0, tq), 256:512])
    wblk = a_ref[0, pl.ds(q0, tq), 640:768]
    sel_r = lax.broadcasted_iota(jnp.int32, (8, LANES), 0)
    sel_c = lax.broadcasted_iota(jnp.int32, (8, LANES), 1)
    w_t = _nt(jnp.where(sel_c == sel_r + D_IDX, 1.0, 0.0).astype(BF16), wblk)
    qpos = q0 + lax.broadcasted_iota(jnp.int32, (1, tq), 1)
    nck = (q0 + tq + ck - 1) // ck
    row_ck = lax.broadcasted_iota(jnp.int32, (ck, tq), 0)

    def p1(c, carry):
        k0 = pl.multiple_of(c * ck, ck)
        act = jnp.maximum(_nt(a_ref[0, pl.ds(k0, ck), 640:704], qi_all), 0.0)
        sc = w_t[0:1, :] * act[:, 0:tq]
        for h in range(1, H_IDX):
            sc = sc + w_t[h:h + 1, :] * act[:, h * tq:(h + 1) * tq]
        bits = pltpu.bitcast(sc, jnp.int32)
        key = jnp.where(bits < 0, -(bits & 0x7FFFFFFF), bits)
        key = jnp.where(k0 + row_ck <= qpos, key, INT_MIN)
        keys_ref[pl.ds(k0, ck), :] = key
        return carry

    lax.fori_loop(0, nck, p1, 0)

    def count(pred):
        def body(c, acc):
            k0 = pl.multiple_of(c * ck, ck)
            m = jnp.where(pred(keys_ref[pl.ds(k0, ck), :], k0 + row_ck), 1, 0)
            return acc + jnp.sum(m.reshape(ck // 8, 8, tq), axis=0)
        acc = lax.fori_loop(0, nck, body, jnp.zeros((8, tq), jnp.int32))
        return jnp.sum(acc, axis=0, keepdims=True)

    def search():
        c0 = count(lambda k, p: k >= 0)
        t = jnp.where(c0 >= topk, 0, INT_MIN).astype(jnp.int32)

        def vstep(b, t):
            cand = t | lax.shift_left(jnp.int32(1), 30 - b)
            return jnp.where(count(lambda k, p: k >= cand) >= topk, cand, t)

        t = lax.fori_loop(0, 31, vstep, t)
        need = topk - count(lambda k, p: k > t)

        def break_ties():
            nbits = max(1, (s_len - 1).bit_length())

            def istep(b, y):
                cand = y | lax.shift_left(jnp.int32(1), nbits - 1 - b)
                return jnp.where(count(lambda k, p: (k == t) & (p < cand)) < need, cand, y)

            y = lax.fori_loop(0, nbits, istep, jnp.zeros((1, tq), jnp.int32))
            return jnp.where(t == INT_MIN, -1, y)

        def keep_all_ties():
            return jnp.where(t == INT_MIN, -1, s_len).astype(jnp.int32)

        surplus = jnp.max(count(lambda k, p: k == t) - need)
        return t, lax.cond(surplus > 0, break_ties, keep_all_ties)

    def no_search():
        return (jnp.full((1, tq), INT_MIN, jnp.int32), jnp.full((1, tq), -1, jnp.int32))

    thr, ymax = lax.cond(q0 + tq > topk, search, no_search)

    qb = a_ref[0, pl.ds(q0, tq), 0:256].astype(F32)
    qn_all = heads_on_rows(_head_rms(qb, gq_ref[...], hd).astype(BF16))

    def p3(c, st):
        ms, ls, accs = st
        k0 = pl.multiple_of(c * ck, ck)
        s_all = _nt(kbn_ref[pl.ds(k0, ck), :], qn_all)
        vtc = vt_ref[c]
        key = keys_ref[pl.ds(k0, ck), :]
        sel = (key > thr) | ((key == thr) & (k0 + row_ck <= ymax))
        bidx = [jnp.clip(i - (c * sub + r), 0, 2) for r in range(sub)]
        nm, nl, na = [], [], []
        for h in range(H_DSA):
            bias = jnp.concatenate([bias_ref[bidx[r], h] for r in range(sub)], axis=0)
            s_h = s_all[:, h * tq:(h + 1) * tq] + bias
            mb = jnp.max(jnp.where(sel, s_h, NEG), axis=0, keepdims=True)
            m_new = jnp.maximum(ms[h], mb)
            p = jnp.where(sel, jnp.exp(s_h - m_new), 0.0)
            alpha = jnp.exp(ms[h] - m_new)
            nm.append(m_new)
            nl.append(ls[h] * alpha + jnp.sum(p, axis=0, keepdims=True))
            na.append(accs[h] * alpha + _mm(vtc, p.astype(BF16)))
        return tuple(nm), tuple(nl), tuple(na)

    init = (tuple(jnp.full((1, tq), NEG, F32) for _ in range(H_DSA)),
            tuple(jnp.zeros((1, tq), F32) for _ in range(H_DSA)),
            tuple(jnp.zeros((hd, tq), F32) for _ in range(H_DSA)))
    _, ls, accs = lax.fori_loop(0, nck, p3, init)
    o_t = jnp.concatenate([accs[h] / ls[h] for h in range(H_DSA)], axis=0)
    o_ref[0] = _nt(_eye(tq, BF16), o_t.astype(BF16)).astype(o_ref.dtype)


def dsa_attention(proj, col_block, gq, gk, bias_t, *, tq=BLOCK, ck=512):
    b, s, _ = proj.shape
    topk = min(TOPK_MAX, s // 4)
    ck = min(ck, s)
    return pl.pallas_call(
        functools.partial(_dsa_kernel, tq=tq, ck=ck, topk=topk),
        grid=(b, s // tq),
        in_specs=[pl.BlockSpec((1, s, 768), lambda bi, i: (bi, 0, col_block)),
                  pl.BlockSpec((1, 256), lambda bi, i: (0, 0)),
                  pl.BlockSpec((1, LANES), lambda bi, i: (0, 0)),
                  pl.BlockSpec((3, H_DSA, tq, tq), lambda bi, i: (0, 0, 0, 0))],
        out_specs=pl.BlockSpec((1, tq, H_DSA * HEAD_DIM), lambda bi, i: (bi, i, 0)),
        out_shape=jax.ShapeDtypeStruct((b, s, H_DSA * HEAD_DIM), BF16),
        scratch_shapes=[pltpu.VMEM((s, HEAD_DIM), BF16),
                        pltpu.VMEM((s // ck, HEAD_DIM, ck), BF16),
                        pltpu.VMEM((s, tq), jnp.int32)],
        compiler_params=_params(("parallel", "arbitrary")),
        name="dsa_attention",
    )(proj, gq, gk, bias_t)


def _t5_bucket(rel):
    n = jnp.maximum(rel, 0)
    max_exact = N_BUCKETS // 2
    nf = jnp.maximum(n, 1).astype(F32)
    large = max_exact + (jnp.log(nf / max_exact) / math.log(MAX_DISTANCE / max_exact)
                         * (N_BUCKETS - max_exact)).astype(jnp.int32)
    large = jnp.minimum(large, N_BUCKETS - 1)
    return jnp.where(n < max_exact, n, large)


def _bucket_lookup(tab, rel):
    hit = _t5_bucket(rel)[..., None, None] == jnp.arange(N_BUCKETS)[:, None]
    return jnp.sum(jnp.where(hit, tab.astype(F32), 0.0), axis=-2)


def dsa_bias_tiles(tab, tq=BLOCK):
    ks = jnp.arange(tq)[:, None]
    tl = jnp.arange(tq)[None, :]
    rel = jnp.stack([tl - ks, tq + tl - ks, jnp.full((tq, tq), 2 * tq + MAX_DISTANCE)])
    return _bucket_lookup(tab, rel).transpose(0, 3, 1, 2)


def _swa_kernel(sink_ref, cur_ref, prev_ref, gq_ref, gk_ref, bias_ref, o_ref, *, tq):
    i = pl.program_id(1)
    hd = HEAD_DIM
    g = H_SW // KV_SW
    qn = _head_rms(cur_ref[0, :, 0:512].astype(F32), gq_ref[...], hd).astype(BF16)
    k_cur = _head_rms(cur_ref[0, :, 512:640].astype(F32), gk_ref[...], hd).astype(BF16)
    k_prev = _head_rms(prev_ref[0, :, 512:640].astype(F32), gk_ref[...], hd).astype(BF16)
    k2 = jnp.concatenate([k_prev, k_cur], axis=0)
    v2 = jnp.concatenate([prev_ref[0, :, 640:768], cur_ref[0, :, 640:768]], axis=0)
    col = lax.broadcasted_iota(jnp.int32, (tq, 2 * tq), 1)
    has_prev = (col >= tq) | (i > 0)
    outs = []
    for h in range(H_SW):
        kv = h // g
        s = _nt(qn[:, h * hd:(h + 1) * hd], k2[:, kv * hd:(kv + 1) * hd]) + bias_ref[h]
        s = jnp.where(has_prev, s, NEG)
        sink = sink_ref[h]
        m = jnp.maximum(jnp.max(s, axis=-1, keepdims=True), sink)
        e = jnp.exp(s - m)
        den = jnp.sum(e, axis=-1, keepdims=True) + jnp.exp(sink - m)
        outs.append(_mm(e.astype(BF16), v2[:, kv * hd:(kv + 1) * hd]) / den)
    o_ref[0] = jnp.concatenate(outs, axis=-1).astype(o_ref.dtype)


def swa_bias_tiles(tab, tq=BLOCK):
    rel = (jnp.arange(tq)[:, None] + tq) - jnp.arange(2 * tq)[None, :]
    in_win = (rel >= 0) & (rel < WINDOW)
    bias = _bucket_lookup(tab, rel).transpose(2, 0, 1)
    return jnp.where(in_win[None], bias, NEG)


def swa_attention(proj, col_block, sinks, gq, gk, bias, *, tq=BLOCK):
    b, s, _ = proj.shape
    return pl.pallas_call(
        functools.partial(_swa_kernel, tq=tq),
        grid_spec=pltpu.PrefetchScalarGridSpec(
            num_scalar_prefetch=0,
            grid=(b, s // tq),
            in_specs=[pl.BlockSpec(memory_space=pltpu.SMEM),
                      pl.BlockSpec((1, tq, 768), lambda bi, i: (bi, i, col_block)),
                      pl.BlockSpec((1, tq, 768), lambda bi, i: (bi, jnp.maximum(i - 1, 0), col_block)),
                      pl.BlockSpec((1, 512), lambda bi, i: (0, 0)),
                      pl.BlockSpec((1, LANES), lambda bi, i: (0, 0)),
                      pl.BlockSpec((H_SW, tq, 2 * tq), lambda bi, i: (0, 0, 0))],
            out_specs=pl.BlockSpec((1, tq, H_SW * HEAD_DIM), lambda bi, i: (bi, i, 0))),
        out_shape=jax.ShapeDtypeStruct((b, s, H_SW * HEAD_DIM), BF16),
        compiler_params=_params(("parallel", "arbitrary")),
        name="swa_attention",
    )(sinks, proj, proj, gq, gk, bias)


def _merge_kernel(x_ref, g_ref, oa_ref, ob_ref, oc_ref, bg_ref, wa_ref, wb_ref, wc_ref, wo_ref, o_ref):
    d = x_ref.shape[1]
    merged = None
    for k, (o_k, w_k) in enumerate(((oa_ref, wa_ref), (ob_ref, wb_ref), (oc_ref, wc_ref))):
        logit = g_ref[:, k * d:(k + 1) * d].astype(F32) + bg_ref[k:k + 1, :]
        gate = 1.0 / (1.0 + jnp.exp(-logit))
        term = gate * _mm(o_k[...], w_k[...])
        merged = term if merged is None else merged + term
    o_ref[...] = x_ref[...] + _mm(merged.astype(BF16), wo_ref[...])


def merge_project(x2d, proj2d, o_a, o_b, o_c, b_gate, w_pa, w_pb, w_pc, w_out, *, tm=512):
    n, d = x2d.shape
    full = lambda a: pl.BlockSpec(a.shape, lambda i: (0,) * a.ndim)
    row = lambda a: pl.BlockSpec((tm, a.shape[1]), lambda i: (i, 0))
    return pl.pallas_call(
        _merge_kernel,
        grid=(n // tm,),
        in_specs=[row(x2d), pl.BlockSpec((tm, N_BRANCH * d), lambda i: (i, 0)),
                  row(o_a), row(o_b), row(o_c),
                  full(b_gate), full(w_pa), full(w_pb), full(w_pc), full(w_out)],
        out_specs=pl.BlockSpec((tm, d), lambda i: (i, 0)),
        out_shape=jax.ShapeDtypeStruct((n, d), F32),
        compiler_params=_params(("parallel",)),
        name="merge_project",
    )(x2d, proj2d, o_a, o_b, o_c, b_gate, w_pa, w_pb, w_pc, w_out)


def _memkv_kernel(m_ref, g_ref, w_ref, gk_ref, k_ref, v_ref):
    w_x = k_ref.shape[2]
    mn = _rms_rows(m_ref[0], g_ref[...]).astype(BF16)
    kv = _mm(mn, w_ref[...])
    k_ref[0] = _head_rms(kv[:, :w_x], gk_ref[...], XHEAD_DIM).astype(k_ref.dtype)
    v_ref[0] = kv[:, w_x:].astype(v_ref.dtype)


def memory_kv(mem, gain, w_kv, gk):
    b, m, d = mem.shape
    w_x = w_kv.shape[1] // 2
    return pl.pallas_call(
        _memkv_kernel,
        grid=(b,),
        in_specs=[pl.BlockSpec((1, m, d), lambda i: (i, 0, 0)),
                  pl.BlockSpec((1, d), lambda i: (0, 0)),
                  pl.BlockSpec(w_kv.shape, lambda i: (0, 0)),
                  pl.BlockSpec((1, w_x), lambda i: (0, 0))],
        out_specs=[pl.BlockSpec((1, m, w_x), lambda i: (i, 0, 0)),
                   pl.BlockSpec((1, m, w_x), lambda i: (i, 0, 0))],
        out_shape=[jax.ShapeDtypeStruct((b, m, w_x), BF16)] * 2,
        compiler_params=_params(("parallel",)),
        name="memory_kv",
    )(mem, gain, w_kv, gk)


def _xattn_kernel(x_ref, g_ref, wq_ref, gq_ref, k_ref, v_ref, wo_ref, o_ref):
    x = x_ref[0]
    xn = _rms_rows(x, g_ref[...]).astype(BF16)
    q = _head_rms(_mm(xn, wq_ref[...]), gq_ref[...], XHEAD_DIM).astype(BF16)
    outs = []
    for h in range(H_X):
        sl = slice(h * XHEAD_DIM, (h + 1) * XHEAD_DIM)
        s = _nt(q[:, sl], k_ref[0, :, sl])
        e = jnp.exp(s - jnp.max(s, axis=-1, keepdims=True))
        den = jnp.sum(e, axis=-1, keepdims=True)
        outs.append(_mm(e.astype(BF16), v_ref[0, :, sl]) / den)
    o = jnp.concatenate(outs, axis=-1).astype(BF16)
    o_ref[0] = x + _mm(o, wo_ref[...])


def cross_attention(x, gain, w_q, gq, k_mem, v_mem, w_o, *, tq=256):
    b, s, d = x.shape
    m, w_x = k_mem.shape[1:]
    full = lambda a: pl.BlockSpec(a.shape, lambda bi, i: (0,) * a.ndim)
    return pl.pallas_call(
        _xattn_kernel,
        grid=(b, s // tq),
        in_specs=[pl.BlockSpec((1, tq, d), lambda bi, i: (bi, i, 0)),
                  full(gain), full(w_q), full(gq),
                  pl.BlockSpec((1, m, w_x), lambda bi, i: (bi, 0, 0)),
                  pl.BlockSpec((1, m, w_x), lambda bi, i: (bi, 0, 0)),
                  full(w_o)],
        out_specs=pl.BlockSpec((1, tq, d), lambda bi, i: (bi, i, 0)),
        out_shape=jax.ShapeDtypeStruct((b, s, d), F32),
        compiler_params=_params(("parallel", "parallel")),
        name="cross_attention",
    )(x, gain, w_q, gq, k_mem, v_mem, w_o)


def _router_kernel(x_ref, g_ref, whi_ref, wlo_ref, o_ref):
    hf = _rms_rows(x_ref[...], g_ref[...])
    hi = hf.astype(BF16)
    lo = (hf - hi.astype(F32)).astype(BF16)
    logits = _mm(hi, whi_ref[...]) + (_mm(hi, wlo_ref[...]) + _mm(lo, whi_ref[...]))
    lane = lax.broadcasted_iota(jnp.int32, logits.shape, 1).astype(F32)
    logits = jnp.where(lane < N_EXPERTS, logits, NEG)
    m1 = jnp.max(logits, axis=-1, keepdims=True)
    i1 = jnp.min(jnp.where(logits == m1, lane, float(LANES)), axis=-1, keepdims=True)
    rest = jnp.where(lane == i1, NEG, logits)
    m2 = jnp.max(rest, axis=-1, keepdims=True)
    i2 = jnp.min(jnp.where(rest == m2, lane, float(LANES)), axis=-1, keepdims=True)
    e2 = jnp.exp(m2 - m1)
    den = 1.0 + e2
    o_ref[...] = (jnp.where(lane == 0.0, i1, 0.0) + jnp.where(lane == 1.0, i2, 0.0)
                  + jnp.where(lane == 2.0, 1.0 / den, 0.0) + jnp.where(lane == 3.0, e2 / den, 0.0))


def router_gates(x2d, gain, w_hi, w_lo, *, tm=512):
    n, d = x2d.shape
    return pl.pallas_call(
        _router_kernel,
        grid=(n // tm,),
        in_specs=[pl.BlockSpec((tm, d), lambda i: (i, 0)),
                  pl.BlockSpec((1, d), lambda i: (0, 0)),
                  pl.BlockSpec((d, LANES), lambda i: (0, 0)),
                  pl.BlockSpec((d, LANES), lambda i: (0, 0))],
        out_specs=pl.BlockSpec((tm, LANES), lambda i: (i, 0)),
        out_shape=jax.ShapeDtypeStruct((n, LANES), F32),
        compiler_params=_params(("parallel",)),
        name="router_gates",
    )(x2d, gain, w_hi, w_lo)


def _swiglu_tile(xn, wg, wu, wd):
    gg = _mm(xn, wg)
    uu = _mm(xn, wu)
    act = gg * (1.0 / (1.0 + jnp.exp(-gg))) * uu
    return _mm(act.astype(BF16), wd)


def _ffn_kernel(x_ref, g_ref, wg_ref, wu_ref, wd_ref, o_ref, xn_ref, acc_ref):
    j = pl.program_id(1)

    @pl.when(j == 0)
    def _():
        xn_ref[...] = _rms_rows(x_ref[...], g_ref[...]).astype(BF16)
        acc_ref[...] = jnp.zeros_like(acc_ref)

    acc_ref[...] += _swiglu_tile(xn_ref[...], wg_ref[...], wu_ref[...], wd_ref[...])

    @pl.when(j == pl.num_programs(1) - 1)
    def _():
        o_ref[...] = x_ref[...] + acc_ref[...]


def dense_ffn(x2d, gain, w_gu, w_down, *, tm=512, tf=256):
    n, d = x2d.shape
    f = w_down.shape[0]
    nf = f // tf
    return pl.pallas_call(
        _ffn_kernel,
        grid=(n // tm, nf),
        in_specs=[pl.BlockSpec((tm, d), lambda i, j: (i, 0)),
                  pl.BlockSpec((1, d), lambda i, j: (0, 0)),
                  pl.BlockSpec((d, tf), lambda i, j: (0, j)),
                  pl.BlockSpec((d, tf), lambda i, j: (0, j + nf)),
                  pl.BlockSpec((tf, d), lambda i, j: (j, 0))],
        out_specs=pl.BlockSpec((tm, d), lambda i, j: (i, 0)),
        out_shape=jax.ShapeDtypeStruct((n, d), F32),
        scratch_shapes=[pltpu.VMEM((tm, d), BF16), pltpu.VMEM((tm, d), F32)],
        compiler_params=_params(("parallel", "arbitrary")),
        name="dense_ffn",
    )(x2d, gain, w_gu, w_gu, w_down)


def _moe_kernel(te_ref, tn_ref, tok_ref, tok_next_ref, dst_ref, x_hbm, g_ref, gate_ref, wg_ref, wu_ref, wd_ref,
                y_hbm, xg_ref, xn_ref, acc_ref, yb_ref, gsem, ssem, *, tm):
    i = pl.program_id(0)
    j = pl.program_id(1)
    nt = pl.num_programs(0)
    nf = pl.num_programs(1)
    slot = i % 2
    n_rows = tn_ref[i]

    def gather_copy(tok, r, s):
        return pltpu.make_async_copy(x_hbm.at[pl.ds(tok, 1), :], xg_ref.at[s, pl.ds(r, 1), :], gsem.at[s])

    def scatter_copy(dst, r, s):
        return pltpu.make_async_copy(yb_ref.at[s, pl.ds(r, 1), :], y_hbm.at[pl.ds(dst, 1), :], ssem.at[s])

    def start_gather(ids_ref, s):
        def body(r, c):
            gather_copy(ids_ref[0, 0, r], r, s).start()
            return c
        lax.fori_loop(0, tm, body, 0)

    def wait_rows(make_copy, count, s):
        def body(r, c):
            make_copy(0, r, s).wait()
            return c
        lax.fori_loop(0, count, body, 0)

    @pl.when(n_rows > 0)
    def _():
        @pl.when(j == 0)
        def _():
            @pl.when(i == 0)
            def _():
                start_gather(tok_ref, slot)

            wait_rows(gather_copy, tm, slot)
            xn_ref[...] = _rms_rows(xg_ref[slot], g_ref[...]).astype(BF16)
            acc_ref[...] = jnp.zeros_like(acc_ref)
            nxt = jnp.minimum(i + 1, nt - 1)

            @pl.when((i + 1 < nt) & (tn_ref[nxt] > 0))
            def _():
                start_gather(tok_next_ref, 1 - slot)

        acc_ref[...] += _swiglu_tile(xn_ref[...], wg_ref[0], wu_ref[0], wd_ref[0])

        @pl.when(j == nf - 1)
        def _():
            yb_ref[slot] = acc_ref[...] * gate_ref[...]

            def body(r, c):
                scatter_copy(dst_ref[0, 0, r], r, slot).start()
                return c
            lax.fori_loop(0, n_rows, body, 0)

            @pl.when(i > 0)
            def _():
                wait_rows(scatter_copy, tn_ref[jnp.maximum(i - 1, 0)], 1 - slot)

            nxt = jnp.minimum(i + 1, nt - 1)

            @pl.when((i == nt - 1) | (tn_ref[nxt] == 0))
            def _():
                wait_rows(scatter_copy, n_rows, slot)


def moe_experts(x2d, gain, plan, w_gu, w_down, *, tm, tf=512):
    n, d = x2d.shape
    ne, f, _ = w_down.shape
    nf = f // tf
    tile_e, tile_n, row_tok, row_dst, row_gate = plan
    nt = tile_e.shape[0]
    smem_rows = lambda imap: pl.BlockSpec((1, 1, tm), imap, memory_space=pltpu.SMEM)
    live = lambda j, tn, i: j * jnp.minimum(tn[i], 1)
    return pl.pallas_call(
        functools.partial(_moe_kernel, tm=tm),
        grid_spec=pltpu.PrefetchScalarGridSpec(
            num_scalar_prefetch=2,
            grid=(nt, nf),
            in_specs=[smem_rows(lambda i, j, te, tn: (i, 0, 0)),
                      smem_rows(lambda i, j, te, tn: (jnp.minimum(i + 1, nt - 1), 0, 0)),
                      smem_rows(lambda i, j, te, tn: (i, 0, 0)),
                      pl.BlockSpec(memory_space=pl.ANY),
                      pl.BlockSpec((1, d), lambda i, j, te, tn: (0, 0)),
                      pl.BlockSpec((tm, 1), lambda i, j, te, tn: (i, 0)),
                      pl.BlockSpec((1, d, tf), lambda i, j, te, tn: (te[i], 0, live(j, tn, i))),
                      pl.BlockSpec((1, d, tf), lambda i, j, te, tn: (te[i], 0, live(j, tn, i) + nf)),
                      pl.BlockSpec((1, tf, d), lambda i, j, te, tn: (te[i], live(j, tn, i), 0))],
            out_specs=pl.BlockSpec(memory_space=pl.ANY),
            scratch_shapes=[pltpu.VMEM((2, tm, d), F32), pltpu.VMEM((tm, d), BF16), pltpu.VMEM((tm, d), F32),
                            pltpu.VMEM((2, tm, d), F32),
                            pltpu.SemaphoreType.DMA((2,)), pltpu.SemaphoreType.DMA((2,))]),
        out_shape=jax.ShapeDtypeStruct((2 * n, d), F32),
        compiler_params=_params(("arbitrary", "arbitrary")),
        name="moe_experts",
    )(tile_e, tile_n, row_tok, row_tok, row_dst, x2d, gain, row_gate, w_gu, w_gu, w_down)


def moe_plan(route, *, tm):
    n = route.shape[0]
    flat_e = route[:, :2].astype(jnp.int32).reshape(-1)
    flat_g = route[:, 2:4].reshape(-1)
    nt = (2 * n) // tm + N_EXPERTS
    order = jnp.argsort(flat_e, stable=True).astype(jnp.int32)
    counts = jnp.sum(flat_e[:, None] == jnp.arange(N_EXPERTS)[None, :], axis=0).astype(jnp.int32)
    off = jnp.cumsum(counts) - counts
    tiles = (counts + tm - 1) // tm
    tile_off = jnp.cumsum(tiles) - tiles
    tile_id = jnp.arange(nt, dtype=jnp.int32)
    used = tile_id < jnp.sum(tiles)
    tile_e = jnp.clip(jnp.sum(tile_id[:, None] >= tile_off[None, :], axis=1) - 1, 0, N_EXPERTS - 1)
    tile_e = jnp.where(used, tile_e, tile_e[jnp.maximum(jnp.sum(tiles) - 1, 0)]).astype(jnp.int32)
    first_row = (tile_id - tile_off[tile_e]) * tm
    tile_n = jnp.where(used, jnp.clip(counts[tile_e] - first_row, 0, tm), 0).astype(jnp.int32)
    r = jnp.arange(tm, dtype=jnp.int32)[None, :]
    valid = r < tile_n[:, None]
    a = order[jnp.clip(off[tile_e][:, None] + first_row[:, None] + r, 0, 2 * n - 1)]
    row_tok = jnp.where(valid, a // 2, 0).astype(jnp.int32).reshape(nt, 1, tm)
    row_dst = jnp.where(valid, (a % 2) * n + a // 2, 0).astype(jnp.int32).reshape(nt, 1, tm)
    row_gate = jnp.where(valid, flat_g[a], 0.0).astype(F32).reshape(nt * tm, 1)
    return tile_e, tile_n, row_tok, row_dst, row_gate


def _combine_kernel(x_ref, y0_ref, y1_ref, o_ref):
    o_ref[...] = x_ref[...] + (y0_ref[...] + y1_ref[...])


def moe_combine(x2d, y, *, tm=512):
    n, d = x2d.shape
    nb = n // tm
    return pl.pallas_call(
        _combine_kernel,
        grid=(nb,),
        in_specs=[pl.BlockSpec((tm, d), lambda i: (i, 0)),
                  pl.BlockSpec((tm, d), lambda i: (i, 0)),
                  pl.BlockSpec((tm, d), lambda i: (i + nb, 0))],
        out_specs=pl.BlockSpec((tm, d), lambda i: (i, 0)),
        out_shape=jax.ShapeDtypeStruct((n, d), F32),
        compiler_params=_params(("parallel",)),
        name="moe_combine",
    )(x2d, y, y)


def _pack_w_in(w, d):
    sizes = (256, 256, 256, 256, 64, 64, 256, 64, 4, 512, 128, 128, N_BRANCH * d)
    qa, ka, va, qb, kb, vb, qi, ki, wi, qc, kc, vc, g = jnp.split(w, np.cumsum(sizes)[:-1].tolist(), axis=-1)
    pad = jnp.zeros((w.shape[0], 60), w.dtype)
    return jnp.concatenate([g, qa * HEAD_DIM ** -0.5, ka, va,
                            qb, qi * D_IDX ** -0.5, kb, vb, ki, wi, pad,
                            qc, kc, vc], axis=-1).astype(BF16)


def kernel(x, mem, rel_bias, norm_mix, w_in, b_gate, qn_dsa, kn_dsa, qn_swa, kn_swa, sinks, w_pa, w_pb, w_pc, w_out, norm_x, norm_mem, w_xq, w_xkv, w_xo, qn_x, kn_x, norm_ffn, w_gu_dense, w_down_dense, w_router, w_gu_moe, w_down_moe):
    b, s, d = x.shape
    depth = w_in.shape[0]
    n = b * s
    row = lambda v: v.reshape(1, -1).astype(F32)
    bias_dsa = dsa_bias_tiles(rel_bias[:, :H_DSA])
    bias_swa = swa_bias_tiles(rel_bias[:, H_DSA:])
    qscale = HEAD_DIM ** -0.5
    gcol = (N_BRANCH * d) // 768

    x2 = x.reshape(n, d)
    for l in range(depth):
        proj = in_projection(x2, row(norm_mix[l]), _pack_w_in(w_in[l], d))
        proj3 = proj.reshape(b, s, -1)
        o_a = sb_attention(proj3, gcol)
        o_b = dsa_attention(proj3, gcol + 1, row(jnp.tile(qn_dsa[l] * qscale, H_DSA)),
                            row(jnp.concatenate([kn_dsa[l], jnp.zeros_like(kn_dsa[l])])), bias_dsa)
        o_c = swa_attention(proj3, gcol + 2, sinks[l].astype(F32),
                            row(jnp.tile(qn_swa[l] * qscale, H_SW)), row(jnp.tile(kn_swa[l], KV_SW)), bias_swa)
        x2 = merge_project(x2, proj, o_a.reshape(n, -1), o_b.reshape(n, -1), o_c.reshape(n, -1),
                           b_gate[l].astype(F32), w_pa[l].astype(BF16), w_pb[l].astype(BF16),
                           w_pc[l].astype(BF16), w_out[l].astype(BF16))
        k_mem, v_mem = memory_kv(mem, row(norm_mem[l]), w_xkv[l].astype(BF16), row(jnp.tile(kn_x[l], H_X)))
        x2 = cross_attention(x2.reshape(b, s, d), row(norm_x[l]), w_xq[l].astype(BF16),
                             row(jnp.tile(qn_x[l] * XHEAD_DIM ** -0.5, H_X)), k_mem, v_mem,
                             w_xo[l].astype(BF16)).reshape(n, d)
        if l % 2 == 0:
            x2 = dense_ffn(x2, row(norm_ffn[l]), w_gu_dense[l // 2].astype(BF16),
                           w_down_dense[l // 2].astype(BF16), tf=256)
        else:
            wr = jnp.pad(w_router[l // 2].astype(F32), ((0, 0), (0, LANES - N_EXPERTS)))
            wr_hi = wr.astype(BF16)
            wr_lo = (wr - wr_hi.astype(F32)).astype(BF16)
            route = router_gates(x2, row(norm_ffn[l]), wr_hi, wr_lo)
            tm_moe = min(512, n)
            y = moe_experts(x2, row(norm_ffn[l]), moe_plan(route, tm=tm_moe), w_gu_moe[l // 2].astype(BF16),
                            w_down_moe[l // 2].astype(BF16), tm=tm_moe)
            x2 = moe_combine(x2, y)
    return x2.reshape(b, s, d)
```

```python
import functools
import math

import numpy as np
import jax
import jax.numpy as jnp
from jax import lax
from jax.experimental import pallas as pl
from jax.experimental.pallas import tpu as pltpu

HEAD_DIM = 64
H_SB = 4
H_DSA = 4
H_IDX = 4
D_IDX = 64
TOPK_MAX = 256
H_SW = 8
KV_SW = 2
WINDOW = 128
BLOCK = 128
N_BRANCH = 3
N_BUCKETS = 32
MAX_DISTANCE = 128
H_X = 4
XHEAD_DIM = 128
N_EXPERTS = 8
EPS = 1e-6

LANES = 128
VMEM_LIMIT = 56 * 1024 * 1024
NEG = -1e30
INT_MIN = -(2 ** 31)

F32 = jnp.float32
BF16 = jnp.bfloat16


def _nt(a, b):
    return lax.dot_general(a, b, (((1,), (1,)), ((), ())), preferred_element_type=F32)


def _mm(a, b):
    return jnp.dot(a, b, preferred_element_type=F32)


def _rms_rows(x, g):
    ms = jnp.mean(x * x, axis=-1, keepdims=True)
    return x * lax.rsqrt(ms + EPS) * g


def _params(sem):
    return pltpu.CompilerParams(dimension_semantics=sem, vmem_limit_bytes=VMEM_LIMIT)


def _sb_kernel(a_ref, o_ref, *, tq):
    i = pl.program_id(1)
    q0 = pl.multiple_of(i * tq, tq)
    hd = HEAD_DIM
    row = lax.broadcasted_iota(jnp.int32, (tq, tq), 0)
    col = lax.broadcasted_iota(jnp.int32, (tq, tq), 1)
    strict = col < row
    u_inc = jnp.where(row >= col, 1.0, 0.0).astype(BF16)
    qs = [a_ref[0, pl.ds(q0, tq), h * hd:(h + 1) * hd] for h in range(H_SB)]

    def block(k0, accs, carries, diag):
        new_acc, new_carry = [], []
        for h in range(H_SB):
            k = a_ref[0, pl.ds(k0, tq), 256 + h * hd:256 + (h + 1) * hd]
            v = a_ref[0, pl.ds(k0, tq), 512 + h * hd:512 + (h + 1) * hd]
            z = _nt(qs[h], k)
            lk = -(jnp.maximum(z, 0.0) + jnp.log(1.0 + jnp.exp(-jnp.abs(z))))
            if diag:
                lk = jnp.where(strict, lk, 0.0)
            r = _mm(lk.astype(BF16), u_inc)
            att = jnp.exp(z + r + carries[h]) if not diag else jnp.where(strict, jnp.exp(z + r), 0.0)
            pv = _mm(att.astype(BF16), v)
            new_acc.append(pv if diag else accs[h] + pv)
            new_carry.append(r[:, 0:1] if diag else carries[h] + r[:, 0:1])
        return tuple(new_acc), tuple(new_carry)

    accs, carries = block(q0, None, None, True)

    def body(jj, st):
        k0 = pl.multiple_of((i - 1 - jj) * tq, tq)
        return block(k0, st[0], st[1], False)

    accs, _ = lax.fori_loop(0, i, body, (accs, carries))
    o_ref[0] = jnp.concatenate(accs, axis=-1).astype(o_ref.dtype)


def sb_attention(proj, col_block, *, tq=256):
    b, s, _ = proj.shape
    return pl.pallas_call(
        functools.partial(_sb_kernel, tq=tq),
        grid=(b, s // tq),
        in_specs=[pl.BlockSpec((1, s, 768), lambda bi, i: (bi, 0, col_block))],
        out_specs=pl.BlockSpec((1, tq, H_SB * HEAD_DIM), lambda bi, i: (bi, i, 0)),
        out_shape=jax.ShapeDtypeStruct((b, s, H_SB * HEAD_DIM), BF16),
        compiler_params=_params(("parallel", "arbitrary")),
        name="sb_attention",
    )(proj)


def _inproj_kernel(x_ref, g_ref, w_ref, o_ref, xn_ref):
    @pl.when(pl.program_id(1) == 0)
    def _():
        xn_ref[...] = _rms_rows(x_ref[...], g_ref[...]).astype(BF16)

    o_ref[...] = _mm(xn_ref[...], w_ref[...]).astype(o_ref.dtype)


def in_projection(x2d, gain, w, *, tm=1024, tn=768):
    n, d = x2d.shape
    tm = min(tm, n)
    c = w.shape[1]
    return pl.pallas_call(
        _inproj_kernel,
        grid=(n // tm, c // tn),
        in_specs=[pl.BlockSpec((tm, d), lambda i, j: (i, 0)),
                  pl.BlockSpec((1, d), lambda i, j: (0, 0)),
                  pl.BlockSpec((d, tn), lambda i, j: (0, j))],
        out_specs=pl.BlockSpec((tm, tn), lambda i, j: (i, j)),
        out_shape=jax.ShapeDtypeStruct((n, c), BF16),
        scratch_shapes=[pltpu.VMEM((tm, d), BF16)],
        compiler_params=_params(("parallel", "arbitrary")),
        name="in_projection",
    )(x2d, gain, w)


def _head_sumsq(x, head_dim):
    r_i = lax.broadcasted_iota(jnp.int32, (LANES, LANES), 0) // head_dim
    c_i = lax.broadcasted_iota(jnp.int32, (LANES, LANES), 1) // head_dim
    bd = jnp.where(r_i == c_i, 1.0, 0.0).astype(F32)
    x2 = x * x
    parts = [_mm(x2[:, b * LANES:(b + 1) * LANES], bd) for b in range(x.shape[1] // LANES)]
    return parts[0] if len(parts) == 1 else jnp.concatenate(parts, axis=-1)


def _head_rms(x, g, head_dim):
    ss = _head_sumsq(x, head_dim)
    return x * lax.rsqrt(ss * (1.0 / head_dim) + EPS) * g


def _eye(n, dtype):
    r = lax.broadcasted_iota(jnp.int32, (n, n), 0)
    c = lax.broadcasted_iota(jnp.int32, (n, n), 1)
    return jnp.where(r == c, 1.0, 0.0).astype(dtype)


def _dsa_kernel(a_ref, gq_ref, gk_ref, bias_ref, o_ref, kbn_ref, vt_ref, keys_ref, *, tq, ck, topk):
    s_len = a_ref.shape[1]
    i = pl.program_id(1)
    q0 = pl.multiple_of(i * tq, tq)
    hd = HEAD_DIM
    sub = ck // tq

    @pl.when(i == 0)
    def _():
        kv = a_ref[0, :, 512:640]
        kvf = kv.astype(F32)
        lane = lax.broadcasted_iota(jnp.int32, (1, LANES), 1)
        ss = _head_sumsq(jnp.where(lane < hd, kvf, 0.0), LANES)
        kn = kvf * lax.rsqrt(ss * (1.0 / hd) + EPS) * gk_ref[...]
        kbn_ref[...] = kn[:, :hd].astype(BF16)
        kvt = _nt(_eye(LANES, BF16), kv)
        for cb in range(s_len // ck):
            vt_ref[cb] = kvt[hd:, cb * ck:(cb + 1) * ck].astype(BF16)

    def heads_on_rows(x):
        return jnp.concatenate([x[:, h * hd:(h + 1) * hd] for h in range(x.shape[1] // hd)], axis=0)

    qi_all = heads_on_rows(a_ref[0, pl.ds(q0, tq), 256:512])
    wblk = a_ref[0, pl.ds(q0, tq), 640:768]
    sel_r = lax.broadcasted_iota(jnp.int32, (8, LANES), 0)
    sel_c = lax.broadcasted_iota(jnp.int32, (8, LANES), 1)
    w_t = _nt(jnp.where(sel_c == sel_r + D_IDX, 1.0, 0.0).astype(BF16), wblk)
    qpos = q0 + lax.broadcasted_iota(jnp.int32, (1, tq), 1)
    nck = (q0 + tq + ck - 1) // ck
    row_ck = lax.broadcasted_iota(jnp.int32, (ck, tq), 0)

    def p1(c, carry):
        k0 = pl.multiple_of(c * ck, ck)
        act = jnp.maximum(_nt(a_ref[0, pl.ds(k0, ck), 640:704], qi_all), 0.0)
        sc = w_t[0:1, :] * act[:, 0:tq]
        for h in range(1, H_IDX):
            sc = sc + w_t[h:h + 1, :] * act[:, h * tq:(h + 1) * tq]
        bits = pltpu.bitcast(sc, jnp.int32)
        key = jnp.where(bits < 0, -(bits & 0x7FFFFFFF), bits)
        key = jnp.where(k0 + row_ck <= qpos, key, INT_MIN)
        keys_ref[pl.ds(k0, ck), :] = key
        return carry

    lax.fori_loop(0, nck, p1, 0)

    def count(pred):
        def body(c, acc):
            k0 = pl.multiple_of(c * ck, ck)
            m = jnp.where(pred(keys_ref[pl.ds(k0, ck), :], k0 + row_ck), 1, 0)
            return acc + jnp.sum(m.reshape(ck // 8, 8, tq), axis=0)
        acc = lax.fori_loop(0, nck, body, jnp.zeros((8, tq), jnp.int32))
        return jnp.sum(acc, axis=0, keepdims=True)

    def search():
        c0 = count(lambda k, p: k >= 0)
        t = jnp.where(c0 >= topk, 0, INT_MIN).astype(jnp.int32)

        def vstep(b, t):
            cand = t | lax.shift_left(jnp.int32(1), 30 - b)
            return jnp.where(count(lambda k, p: k >= cand) >= topk, cand, t)

        t = lax.fori_loop(0, 31, vstep, t)
        need = topk - count(lambda k, p: k > t)

        def break_ties():
            nbits = max(1, (s_len - 1).bit_length())

            def istep(b, y):
                cand = y | lax.shift_left(jnp.int32(1), nbits - 1 - b)
                return jnp.where(count(lambda k, p: (k == t) & (p < cand)) < need, cand, y)

            y = lax.fori_loop(0, nbits, istep, jnp.zeros((1, tq), jnp.int32))
            return jnp.where(t == INT_MIN, -1, y)

        def keep_all_ties():
            return jnp.where(t == INT_MIN, -1, s_len).astype(jnp.int32)

        surplus = jnp.max(count(lambda k, p: k == t) - need)
        return t, lax.cond(surplus > 0, break_ties, keep_all_ties)

    def no_search():
        return (jnp.full((1, tq), INT_MIN, jnp.int32), jnp.full((1, tq), -1, jnp.int32))

    thr, ymax = lax.cond(q0 + tq > topk, search, no_search)

    qb = a_ref[0, pl.ds(q0, tq), 0:256].astype(F32)
    qn_all = heads_on_rows(_head_rms(qb, gq_ref[...], hd).astype(BF16))

    def p3(c, st):
        ms, ls, accs = st
        k0 = pl.multiple_of(c * ck, ck)
        s_all = _nt(kbn_ref[pl.ds(k0, ck), :], qn_all)
        vtc = vt_ref[c]
        key = keys_ref[pl.ds(k0, ck), :]
        sel = (key > thr) | ((key == thr) & (k0 + row_ck <= ymax))
        bidx = [jnp.clip(i - (c * sub + r), 0, 2) for r in range(sub)]
        nm, nl, na = [], [], []
        for h in range(H_DSA):
            bias = jnp.concatenate([bias_ref[bidx[r], h] for r in range(sub)], axis=0)
            s_h = s_all[:, h * tq:(h + 1) * tq] + bias
            mb = jnp.max(jnp.where(sel, s_h, NEG), axis=0, keepdims=True)
            m_new = jnp.maximum(ms[h], mb)
            p = jnp.where(sel, jnp.exp(s_h - m_new), 0.0)
            alpha = jnp.exp(ms[h] - m_new)
            nm.append(m_new)
            nl.append(ls[h] * alpha + jnp.sum(p, axis=0, keepdims=True))
            na.append(accs[h] * alpha + _mm(vtc, p.astype(BF16)))
        return tuple(nm), tuple(nl), tuple(na)

    init = (tuple(jnp.full((1, tq), NEG, F32) for _ in range(H_DSA)),
            tuple(jnp.zeros((1, tq), F32) for _ in range(H_DSA)),
            tuple(jnp.zeros((hd, tq), F32) for _ in range(H_DSA)))
    _, ls, accs = lax.fori_loop(0, nck, p3, init)
    o_t = jnp.concatenate([accs[h] / ls[h] for h in range(H_DSA)], axis=0)
    o_ref[0] = _nt(_eye(tq, BF16), o_t.astype(BF16)).astype(o_ref.dtype)


def dsa_attention(proj, col_block, gq, gk, bias_t, *, tq=BLOCK, ck=512):
    b, s, _ = proj.shape
    topk = min(TOPK_MAX, s // 4)
    ck = min(ck, s)
    return pl.pallas_call(
        functools.partial(_dsa_kernel, tq=tq, ck=ck, topk=topk),
        grid=(b, s // tq),
        in_specs=[pl.BlockSpec((1, s, 768), lambda bi, i: (bi, 0, col_block)),
                  pl.BlockSpec((1, 256), lambda bi, i: (0, 0)),
                  pl.BlockSpec((1, LANES), lambda bi, i: (0, 0)),
                  pl.BlockSpec((3, H_DSA, tq, tq), lambda bi, i: (0, 0, 0, 0))],
        out_specs=pl.BlockSpec((1, tq, H_DSA * HEAD_DIM), lambda bi, i: (bi, i, 0)),
        out_shape=jax.ShapeDtypeStruct((b, s, H_DSA * HEAD_DIM), BF16),
        scratch_shapes=[pltpu.VMEM((s, HEAD_DIM), BF16),
                        pltpu.VMEM((s // ck, HEAD_DIM, ck), BF16),
                        pltpu.VMEM((s, tq), jnp.int32)],
        compiler_params=_params(("parallel", "arbitrary")),
        name="dsa_attention",
    )(proj, gq, gk, bias_t)


def _t5_bucket(rel):
    n = jnp.maximum(rel, 0)
    max_exact = N_BUCKETS // 2
    nf = jnp.maximum(n, 1).astype(F32)
    large = max_exact + (jnp.log(nf / max_exact) / math.log(MAX_DISTANCE / max_exact)
                         * (N_BUCKETS - max_exact)).astype(jnp.int32)
    large = jnp.minimum(large, N_BUCKETS - 1)
    return jnp.where(n < max_exact, n, large)


def _bucket_lookup(tab, rel):
    hit = _t5_bucket(rel)[..., None, None] == jnp.arange(N_BUCKETS)[:, None]
    return jnp.sum(jnp.where(hit, tab.astype(F32), 0.0), axis=-2)


def dsa_bias_tiles(tab, tq=BLOCK):
    ks = jnp.arange(tq)[:, None]
    tl = jnp.arange(tq)[None, :]
    rel = jnp.stack([tl - ks, tq + tl - ks, jnp.full((tq, tq), 2 * tq + MAX_DISTANCE)])
    return _bucket_lookup(tab, rel).transpose(0, 3, 1, 2)


def _swa_kernel(sink_ref, cur_ref, prev_ref, gq_ref, gk_ref, bias_ref, o_ref, *, tq):
    i = pl.program_id(1)
    hd = HEAD_DIM
    g = H_SW // KV_SW
    qn = _head_rms(cur_ref[0, :, 0:512].astype(F32), gq_ref[...], hd).astype(BF16)
    k_cur = _head_rms(cur_ref[0, :, 512:640].astype(F32), gk_ref[...], hd).astype(BF16)
    k_prev = _head_rms(prev_ref[0, :, 512:640].astype(F32), gk_ref[...], hd).astype(BF16)
    k2 = jnp.concatenate([k_prev, k_cur], axis=0)
    v2 = jnp.concatenate([prev_ref[0, :, 640:768], cur_ref[0, :, 640:768]], axis=0)
    col = lax.broadcasted_iota(jnp.int32, (tq, 2 * tq), 1)
    has_prev = (col >= tq) | (i > 0)
    outs = []
    for h in range(H_SW):
        kv = h // g
        s = _nt(qn[:, h * hd:(h + 1) * hd], k2[:, kv * hd:(kv + 1) * hd]) + bias_ref[h]
        s = jnp.where(has_prev, s, NEG)
        sink = sink_ref[h]
        m = jnp.maximum(jnp.max(s, axis=-1, keepdims=True), sink)
        e = jnp.exp(s - m)
        den = jnp.sum(e, axis=-1, keepdims=True) + jnp.exp(sink - m)
        outs.append(_mm(e.astype(BF16), v2[:, kv * hd:(kv + 1) * hd]) / den)
    o_ref[0] = jnp.concatenate(outs, axis=-1).astype(o_ref.dtype)


def swa_bias_tiles(tab, tq=BLOCK):
    rel = (jnp.arange(tq)[:, None] + tq) - jnp.arange(2 * tq)[None, :]
    in_win = (rel >= 0) & (rel < WINDOW)
    bias = _bucket_lookup(tab, rel).transpose(2, 0, 1)
    return jnp.where(in_win[None], bias, NEG)


def swa_attention(proj, col_block, sinks, gq, gk, bias, *, tq=BLOCK):
    b, s, _ = proj.shape
    return pl.pallas_call(
        functools.partial(_swa_kernel, tq=tq),
        grid_spec=pltpu.PrefetchScalarGridSpec(
            num_scalar_prefetch=0,
            grid=(b, s // tq),
            in_specs=[pl.BlockSpec(memory_space=pltpu.SMEM),
                      pl.BlockSpec((1, tq, 768), lambda bi, i: (bi, i, col_block)),
                      pl.BlockSpec((1, tq, 768), lambda bi, i: (bi, jnp.maximum(i - 1, 0), col_block)),
                      pl.BlockSpec((1, 512), lambda bi, i: (0, 0)),
                      pl.BlockSpec((1, LANES), lambda bi, i: (0, 0)),
                      pl.BlockSpec((H_SW, tq, 2 * tq), lambda bi, i: (0, 0, 0))],
            out_specs=pl.BlockSpec((1, tq, H_SW * HEAD_DIM), lambda bi, i: (bi, i, 0))),
        out_shape=jax.ShapeDtypeStruct((b, s, H_SW * HEAD_DIM), BF16),
        compiler_params=_params(("parallel", "arbitrary")),
        name="swa_attention",
    )(sinks, proj, proj, gq, gk, bias)


def _merge_kernel(x_ref, g_ref, oa_ref, ob_ref, oc_ref, bg_ref, wa_ref, wb_ref, wc_ref, wo_ref, o_ref):
    d = x_ref.shape[1]
    merged = None
    for k, (o_k, w_k) in enumerate(((oa_ref, wa_ref), (ob_ref, wb_ref), (oc_ref, wc_ref))):
        logit = g_ref[:, k * d:(k + 1) * d].astype(F32) + bg_ref[k:k + 1, :]
        gate = 1.0 / (1.0 + jnp.exp(-logit))
        term = gate * _mm(o_k[...], w_k[...])
        merged = term if merged is None else merged + term
    o_ref[...] = x_ref[...] + _mm(merged.astype(BF16), wo_ref[...])


def merge_project(x2d, proj2d, o_a, o_b, o_c, b_gate, w_pa, w_pb, w_pc, w_out, *, tm=512):
    n, d = x2d.shape
    full = lambda a: pl.BlockSpec(a.shape, lambda i: (0,) * a.ndim)
    row = lambda a: pl.BlockSpec((tm, a.shape[1]), lambda i: (i, 0))
    return pl.pallas_call(
        _merge_kernel,
        grid=(n // tm,),
        in_specs=[row(x2d), pl.BlockSpec((tm, N_BRANCH * d), lambda i: (i, 0)),
                  row(o_a), row(o_b), row(o_c),
                  full(b_gate), full(w_pa), full(w_pb), full(w_pc), full(w_out)],
        out_specs=pl.BlockSpec((tm, d), lambda i: (i, 0)),
        out_shape=jax.ShapeDtypeStruct((n, d), F32),
        compiler_params=_params(("parallel",)),
        name="merge_project",
    )(x2d, proj2d, o_a, o_b, o_c, b_gate, w_pa, w_pb, w_pc, w_out)


def _memkv_kernel(m_ref, g_ref, w_ref, gk_ref, k_ref, v_ref):
    w_x = k_ref.shape[2]
    mn = _rms_rows(m_ref[0], g_ref[...]).astype(BF16)
    kv = _mm(mn, w_ref[...])
    k_ref[0] = _head_rms(kv[:, :w_x], gk_ref[...], XHEAD_DIM).astype(k_ref.dtype)
    v_ref[0] = kv[:, w_x:].astype(v_ref.dtype)


def memory_kv(mem, gain, w_kv, gk):
    b, m, d = mem.shape
    w_x = w_kv.shape[1] // 2
    return pl.pallas_call(
        _memkv_kernel,
        grid=(b,),
        in_specs=[pl.BlockSpec((1, m, d), lambda i: (i, 0, 0)),
                  pl.BlockSpec((1, d), lambda i: (0, 0)),
                  pl.BlockSpec(w_kv.shape, lambda i: (0, 0)),
                  pl.BlockSpec((1, w_x), lambda i: (0, 0))],
        out_specs=[pl.BlockSpec((1, m, w_x), lambda i: (i, 0, 0)),
                   pl.BlockSpec((1, m, w_x), lambda i: (i, 0, 0))],
        out_shape=[jax.ShapeDtypeStruct((b, m, w_x), BF16)] * 2,
        compiler_params=_params(("parallel",)),
        name="memory_kv",
    )(mem, gain, w_kv, gk)


def _xattn_kernel(x_ref, g_ref, wq_ref, gq_ref, k_ref, v_ref, wo_ref, o_ref):
    x = x_ref[0]
    xn = _rms_rows(x, g_ref[...]).astype(BF16)
    q = _head_rms(_mm(xn, wq_ref[...]), gq_ref[...], XHEAD_DIM).astype(BF16)
    outs = []
    for h in range(H_X):
        sl = slice(h * XHEAD_DIM, (h + 1) * XHEAD_DIM)
        s = _nt(q[:, sl], k_ref[0, :, sl])
        e = jnp.exp(s - jnp.max(s, axis=-1, keepdims=True))
        den = jnp.sum(e, axis=-1, keepdims=True)
        outs.append(_mm(e.astype(BF16), v_ref[0, :, sl]) / den)
    o = jnp.concatenate(outs, axis=-1).astype(BF16)
    o_ref[0] = x + _mm(o, wo_ref[...])


def cross_attention(x, gain, w_q, gq, k_mem, v_mem, w_o, *, tq=512):
    b, s, d = x.shape
    m, w_x = k_mem.shape[1:]
    full = lambda a: pl.BlockSpec(a.shape, lambda bi, i: (0,) * a.ndim)
    return pl.pallas_call(
        _xattn_kernel,
        grid=(b, s // tq),
        in_specs=[pl.BlockSpec((1, tq, d), lambda bi, i: (bi, i, 0)),
                  full(gain), full(w_q), full(gq),
                  pl.BlockSpec((1, m, w_x), lambda bi, i: (bi, 0, 0)),
                  pl.BlockSpec((1, m, w_x), lambda bi, i: (bi, 0, 0)),
                  full(w_o)],
        out_specs=pl.BlockSpec((1, tq, d), lambda bi, i: (bi, i, 0)),
        out_shape=jax.ShapeDtypeStruct((b, s, d), F32),
        compiler_params=_params(("parallel", "parallel")),
        name="cross_attention",
    )(x, gain, w_q, gq, k_mem, v_mem, w_o)


def _router_kernel(x_ref, g_ref, whi_ref, wlo_ref, o_ref):
    hf = _rms_rows(x_ref[...], g_ref[...])
    hi = hf.astype(BF16)
    lo = (hf - hi.astype(F32)).astype(BF16)
    logits = _mm(hi, whi_ref[...]) + (_mm(hi, wlo_ref[...]) + _mm(lo, whi_ref[...]))
    lane = lax.broadcasted_iota(jnp.int32, logits.shape, 1).astype(F32)
    logits = jnp.where(lane < N_EXPERTS, logits, NEG)
    m1 = jnp.max(logits, axis=-1, keepdims=True)
    i1 = jnp.min(jnp.where(logits == m1, lane, float(LANES)), axis=-1, keepdims=True)
    rest = jnp.where(lane == i1, NEG, logits)
    m2 = jnp.max(rest, axis=-1, keepdims=True)
    i2 = jnp.min(jnp.where(rest == m2, lane, float(LANES)), axis=-1, keepdims=True)
    e2 = jnp.exp(m2 - m1)
    den = 1.0 + e2
    o_ref[...] = (jnp.where(lane == 0.0, i1, 0.0) + jnp.where(lane == 1.0, i2, 0.0)
                  + jnp.where(lane == 2.0, 1.0 / den, 0.0) + jnp.where(lane == 3.0, e2 / den, 0.0))


def router_gates(x2d, gain, w_hi, w_lo, *, tm=512):
    n, d = x2d.shape
    return pl.pallas_call(
        _router_kernel,
        grid=(n // tm,),
        in_specs=[pl.BlockSpec((tm, d), lambda i: (i, 0)),
                  pl.BlockSpec((1, d), lambda i: (0, 0)),
                  pl.BlockSpec((d, LANES), lambda i: (0, 0)),
                  pl.BlockSpec((d, LANES), lambda i: (0, 0))],
        out_specs=pl.BlockSpec((tm, LANES), lambda i: (i, 0)),
        out_shape=jax.ShapeDtypeStruct((n, LANES), F32),
        compiler_params=_params(("parallel",)),
        name="router_gates",
    )(x2d, gain, w_hi, w_lo)


def _swiglu_tile(xn, wg, wu, wd):
    gg = _mm(xn, wg)
    uu = _mm(xn, wu)
    act = gg * (1.0 / (1.0 + jnp.exp(-gg))) * uu
    return _mm(act.astype(BF16), wd)


def _ffn_kernel(x_ref, g_ref, wg_ref, wu_ref, wd_ref, o_ref, xn_ref, acc_ref):
    j = pl.program_id(1)

    @pl.when(j == 0)
    def _():
        xn_ref[...] = _rms_rows(x_ref[...], g_ref[...]).astype(BF16)
        acc_ref[...] = jnp.zeros_like(acc_ref)

    acc_ref[...] += _swiglu_tile(xn_ref[...], wg_ref[...], wu_ref[...], wd_ref[...])

    @pl.when(j == pl.num_programs(1) - 1)
    def _():
        o_ref[...] = x_ref[...] + acc_ref[...]


def dense_ffn(x2d, gain, w_gu, w_down, *, tm=1024, tf=256):
    n, d = x2d.shape
    tm = min(tm, n)
    f = w_down.shape[0]
    nf = f // tf
    return pl.pallas_call(
        _ffn_kernel,
        grid=(n // tm, nf),
        in_specs=[pl.BlockSpec((tm, d), lambda i, j: (i, 0)),
                  pl.BlockSpec((1, d), lambda i, j: (0, 0)),
                  pl.BlockSpec((d, tf), lambda i, j: (0, j)),
                  pl.BlockSpec((d, tf), lambda i, j: (0, j + nf)),
                  pl.BlockSpec((tf, d), lambda i, j: (j, 0))],
        out_specs=pl.BlockSpec((tm, d), lambda i, j: (i, 0)),
        out_shape=jax.ShapeDtypeStruct((n, d), F32),
        scratch_shapes=[pltpu.VMEM((tm, d), BF16), pltpu.VMEM((tm, d), F32)],
        compiler_params=_params(("parallel", "arbitrary")),
        name="dense_ffn",
    )(x2d, gain, w_gu, w_gu, w_down)


def _moe_kernel(te_ref, tn_ref, tok_ref, tok_next_ref, dst_ref, x_hbm, g_ref, gate_ref, wg_ref, wu_ref, wd_ref,
                y_hbm, xg_ref, xn_ref, acc_ref, yb_ref, gsem, ssem, *, tm):
    i = pl.program_id(0)
    j = pl.program_id(1)
    nt = pl.num_programs(0)
    nf = pl.num_programs(1)
    slot = i % 2
    unroll = 8

    def start_rows(ids_ref, s, make_copy):
        def body(r8, c):
            for u in range(unroll):
                r = r8 * unroll + u
                make_copy(ids_ref[0, 0, r], r, s).start()
            return c
        lax.fori_loop(0, tm // unroll, body, 0)

    def gather_copy(tok, r, s):
        return pltpu.make_async_copy(x_hbm.at[pl.ds(tok, 1), :], xg_ref.at[s, pl.ds(r, 1), :], gsem.at[s])

    def scatter_copy(dst, r, s):
        return pltpu.make_async_copy(yb_ref.at[s, pl.ds(r, 1), :], y_hbm.at[pl.ds(dst, 1), :], ssem.at[s])

    def wait_gather(s):
        pltpu.make_async_copy(x_hbm.at[pl.ds(0, tm), :], xg_ref.at[s], gsem.at[s]).wait()

    def wait_scatter(s):
        pltpu.make_async_copy(yb_ref.at[s], y_hbm.at[pl.ds(0, tm), :], ssem.at[s]).wait()

    @pl.when(tn_ref[i] > 0)
    def _():
        nxt = jnp.minimum(i + 1, nt - 1)

        @pl.when(j == 0)
        def _():
            @pl.when(i == 0)
            def _():
                start_rows(tok_ref, slot, gather_copy)
                yb_ref[1] = jnp.zeros(yb_ref.shape[1:], yb_ref.dtype)
                n_real = y_hbm.shape[0] - 2 * tm
                for half in range(2):
                    spare = pltpu.make_async_copy(yb_ref.at[1], y_hbm.at[pl.ds(n_real + half * tm, tm), :],
                                                  ssem.at[1])
                    spare.start()
                    spare.wait()

            wait_gather(slot)
            xn_ref[...] = _rms_rows(xg_ref[slot], g_ref[...]).astype(BF16)
            acc_ref[...] = jnp.zeros_like(acc_ref)

            @pl.when((i + 1 < nt) & (tn_ref[nxt] > 0))
            def _():
                start_rows(tok_next_ref, 1 - slot, gather_copy)

        acc_ref[...] += _swiglu_tile(xn_ref[...], wg_ref[0], wu_ref[0], wd_ref[0])

        @pl.when(j == nf - 1)
        def _():
            yb_ref[slot] = acc_ref[...] * gate_ref[...]
            start_rows(dst_ref, slot, scatter_copy)

            @pl.when(i > 0)
            def _():
                wait_scatter(1 - slot)

            @pl.when((i == nt - 1) | (tn_ref[nxt] == 0))
            def _():
                wait_scatter(slot)


def moe_experts(x2d, gain, plan, w_gu, w_down, *, tm, tf=512):
    n, d = x2d.shape
    ne, f, _ = w_down.shape
    nf = f // tf
    tile_e, tile_n, row_tok, row_dst, row_gate = plan
    nt = tile_e.shape[0]
    smem_rows = lambda imap: pl.BlockSpec((1, 1, tm), imap, memory_space=pltpu.SMEM)
    live = lambda j, tn, i: j * jnp.minimum(tn[i], 1)
    return pl.pallas_call(
        functools.partial(_moe_kernel, tm=tm),
        grid_spec=pltpu.PrefetchScalarGridSpec(
            num_scalar_prefetch=2,
            grid=(nt, nf),
            in_specs=[smem_rows(lambda i, j, te, tn: (i, 0, 0)),
                      smem_rows(lambda i, j, te, tn: (jnp.minimum(i + 1, nt - 1), 0, 0)),
                      smem_rows(lambda i, j, te, tn: (i, 0, 0)),
                      pl.BlockSpec(memory_space=pl.ANY),
                      pl.BlockSpec((1, d), lambda i, j, te, tn: (0, 0)),
                      pl.BlockSpec((tm, 1), lambda i, j, te, tn: (i, 0)),
                      pl.BlockSpec((1, d, tf), lambda i, j, te, tn: (te[i], 0, live(j, tn, i))),
                      pl.BlockSpec((1, d, tf), lambda i, j, te, tn: (te[i], 0, live(j, tn, i) + nf)),
                      pl.BlockSpec((1, tf, d), lambda i, j, te, tn: (te[i], live(j, tn, i), 0))],
            out_specs=pl.BlockSpec(memory_space=pl.ANY),
            scratch_shapes=[pltpu.VMEM((2, tm, d), F32), pltpu.VMEM((tm, d), BF16), pltpu.VMEM((tm, d), F32),
                            pltpu.VMEM((2, tm, d), F32),
                            pltpu.SemaphoreType.DMA((2,)), pltpu.SemaphoreType.DMA((2,))]),
        out_shape=jax.ShapeDtypeStruct((2 * n + 2 * tm, d), F32),
        compiler_params=_params(("arbitrary", "arbitrary")),
        name="moe_experts",
    )(tile_e, tile_n, row_tok, row_tok, row_dst, x2d, gain, row_gate, w_gu, w_gu, w_down)


def moe_plan(route, *, tm):
    n = route.shape[0]
    flat_e = route[:, :2].astype(jnp.int32).reshape(-1)
    flat_g = route[:, 2:4].reshape(-1)
    nt = (2 * n) // tm + N_EXPERTS
    order = jnp.argsort(flat_e, stable=True).astype(jnp.int32)
    counts = jnp.sum(flat_e[:, None] == jnp.arange(N_EXPERTS)[None, :], axis=0).astype(jnp.int32)
    off = jnp.cumsum(counts) - counts
    tiles = (counts + tm - 1) // tm
    tile_off = jnp.cumsum(tiles) - tiles
    tile_id = jnp.arange(nt, dtype=jnp.int32)
    used = tile_id < jnp.sum(tiles)
    tile_e = jnp.clip(jnp.sum(tile_id[:, None] >= tile_off[None, :], axis=1) - 1, 0, N_EXPERTS - 1)
    tile_e = jnp.where(used, tile_e, tile_e[jnp.maximum(jnp.sum(tiles) - 1, 0)]).astype(jnp.int32)
    first_row = (tile_id - tile_off[tile_e]) * tm
    tile_n = jnp.where(used, jnp.clip(counts[tile_e] - first_row, 0, tm), 0).astype(jnp.int32)
    r = jnp.arange(tm, dtype=jnp.int32)[None, :]
    valid = r < tile_n[:, None]
    a = order[jnp.clip(off[tile_e][:, None] + first_row[:, None] + r, 0, 2 * n - 1)]
    row_tok = jnp.where(valid, a // 2, 0).astype(jnp.int32).reshape(nt, 1, tm)
    spare = 2 * n + (tile_id[:, None] % 2) * tm + r
    row_dst = jnp.where(valid, (a % 2) * n + a // 2, spare).astype(jnp.int32).reshape(nt, 1, tm)
    row_gate = jnp.where(valid, flat_g[a], 0.0).astype(F32).reshape(nt * tm, 1)
    return tile_e, tile_n, row_tok, row_dst, row_gate


def _combine_kernel(x_ref, y0_ref, y1_ref, o_ref):
    o_ref[...] = x_ref[...] + (y0_ref[...] + y1_ref[...])


def moe_combine(x2d, y, *, tm=512):
    n, d = x2d.shape
    nb = n // tm
    return pl.pallas_call(
        _combine_kernel,
        grid=(nb,),
        in_specs=[pl.BlockSpec((tm, d), lambda i: (i, 0)),
                  pl.BlockSpec((tm, d), lambda i: (i, 0)),
                  pl.BlockSpec((tm, d), lambda i: (i + nb, 0))],
        out_specs=pl.BlockSpec((tm, d), lambda i: (i, 0)),
        out_shape=jax.ShapeDtypeStruct((n, d), F32),
        compiler_params=_params(("parallel",)),
        name="moe_combine",
    )(x2d, y, y)


def _pack_w_in(w, d):
    sizes = (256, 256, 256, 256, 64, 64, 256, 64, 4, 512, 128, 128, N_BRANCH * d)
    qa, ka, va, qb, kb, vb, qi, ki, wi, qc, kc, vc, g = jnp.split(w, np.cumsum(sizes)[:-1].tolist(), axis=-1)
    pad = jnp.zeros((w.shape[0], 60), w.dtype)
    return jnp.concatenate([g, qa * HEAD_DIM ** -0.5, ka, va,
                            qb, qi * D_IDX ** -0.5, kb, vb, ki, wi, pad,
                            qc, kc, vc], axis=-1).astype(BF16)


def kernel(x, mem, rel_bias, norm_mix, w_in, b_gate, qn_dsa, kn_dsa, qn_swa, kn_swa, sinks, w_pa, w_pb, w_pc, w_out, norm_x, norm_mem, w_xq, w_xkv, w_xo, qn_x, kn_x, norm_ffn, w_gu_dense, w_down_dense, w_router, w_gu_moe, w_down_moe):
    b, s, d = x.shape
    depth = w_in.shape[0]
    n = b * s
    row = lambda v: v.reshape(1, -1).astype(F32)
    bias_dsa = dsa_bias_tiles(rel_bias[:, :H_DSA])
    bias_swa = swa_bias_tiles(rel_bias[:, H_DSA:])
    qscale = HEAD_DIM ** -0.5
    gcol = (N_BRANCH * d) // 768

    x2 = x.reshape(n, d)
    for l in range(depth):
        proj = in_projection(x2, row(norm_mix[l]), _pack_w_in(w_in[l], d))
        proj3 = proj.reshape(b, s, -1)
        o_a = sb_attention(proj3, gcol)
        o_b = dsa_attention(proj3, gcol + 1, row(jnp.tile(qn_dsa[l] * qscale, H_DSA)),
                            row(jnp.concatenate([kn_dsa[l], jnp.zeros_like(kn_dsa[l])])), bias_dsa)
        o_c = swa_attention(proj3, gcol + 2, sinks[l].astype(F32),
                            row(jnp.tile(qn_swa[l] * qscale, H_SW)), row(jnp.tile(kn_swa[l], KV_SW)), bias_swa)
        x2 = merge_project(x2, proj, o_a.reshape(n, -1), o_b.reshape(n, -1), o_c.reshape(n, -1),
                           b_gate[l].astype(F32), w_pa[l].astype(BF16), w_pb[l].astype(BF16),
                           w_pc[l].astype(BF16), w_out[l].astype(BF16))
        k_mem, v_mem = memory_kv(mem, row(norm_mem[l]), w_xkv[l].astype(BF16), row(jnp.tile(kn_x[l], H_X)))
        x2 = cross_attention(x2.reshape(b, s, d), row(norm_x[l]), w_xq[l].astype(BF16),
                             row(jnp.tile(qn_x[l] * XHEAD_DIM ** -0.5, H_X)), k_mem, v_mem,
                             w_xo[l].astype(BF16)).reshape(n, d)
        if l % 2 == 0:
            x2 = dense_ffn(x2, row(norm_ffn[l]), w_gu_dense[l // 2].astype(BF16),
                           w_down_dense[l // 2].astype(BF16), tf=256)
        else:
            wr = jnp.pad(w_router[l // 2].astype(F32), ((0, 0), (0, LANES - N_EXPERTS)))
            wr_hi = wr.astype(BF16)
            wr_lo = (wr - wr_hi.astype(F32)).astype(BF16)
            route = router_gates(x2, row(norm_ffn[l]), wr_hi, wr_lo)
            tm_moe = min(512, n)
            y = moe_experts(x2, row(norm_ffn[l]), moe_plan(route, tm=tm_moe), w_gu_moe[l // 2].astype(BF16),
                            w_down_moe[l // 2].astype(BF16), tm=tm_moe)
            x2 = moe_combine(x2, y)
    return x2.reshape(b, s, d)
```

```python
import functools
import math

import numpy as np
import jax
import jax.numpy as jnp
from jax import lax
from jax.experimental import pallas as pl
from jax.experimental.pallas import tpu as pltpu

HEAD_DIM = 64
H_SB = 4
H_DSA = 4
H_IDX = 4
D_IDX = 64
TOPK_MAX = 256
H_SW = 8
KV_SW = 2
WINDOW = 128
BLOCK = 128
N_BRANCH = 3
N_BUCKETS = 32
MAX_DISTANCE = 128
H_X = 4
XHEAD_DIM = 128
N_EXPERTS = 8
EPS = 1e-6

LANES = 128
VMEM_LIMIT = 56 * 1024 * 1024
NEG = -1e30
INT_MIN = -(2 ** 31)
INT_MAX = 2 ** 31 - 1
LOG2E = math.log2(math.e)

F32 = jnp.float32
BF16 = jnp.bfloat16


def _nt(a, b):
    return lax.dot_general(a, b, (((1,), (1,)), ((), ())), preferred_element_type=F32)


def _mm(a, b):
    return jnp.dot(a, b, preferred_element_type=F32)


def _rms_rows(x, g):
    ms = jnp.mean(x * x, axis=-1, keepdims=True)
    return x * lax.rsqrt(ms + EPS) * g


def _params(sem):
    return pltpu.CompilerParams(dimension_semantics=sem, vmem_limit_bytes=VMEM_LIMIT)


def _sb_kernel(a_ref, o_ref, *, tq):
    i = pl.program_id(1)
    q0 = pl.multiple_of(i * tq, tq)
    hd = HEAD_DIM
    row = lax.broadcasted_iota(jnp.int32, (tq, tq), 0)
    col = lax.broadcasted_iota(jnp.int32, (tq, tq), 1)
    strict = col < row
    u_inc = jnp.where(row >= col, 1.0, 0.0).astype(BF16)
    qs = [a_ref[0, pl.ds(q0, tq), h * hd:(h + 1) * hd] for h in range(H_SB)]

    def block(k0, accs, carries, diag):
        new_acc, new_carry = [], []
        for h in range(H_SB):
            k = a_ref[0, pl.ds(k0, tq), 256 + h * hd:256 + (h + 1) * hd]
            v = a_ref[0, pl.ds(k0, tq), 512 + h * hd:512 + (h + 1) * hd]
            z = _nt(qs[h], k)
            lk = -(jnp.maximum(z, 0.0) + jnp.log2(1.0 + jnp.exp2(-jnp.abs(z))))
            if diag:
                lk = jnp.where(strict, lk, 0.0)
            r = _mm(lk.astype(BF16), u_inc)
            att = jnp.exp2(z + r + carries[h]) if not diag else jnp.where(strict, jnp.exp2(z + r), 0.0)
            pv = _mm(att.astype(BF16), v)
            new_acc.append(pv if diag else accs[h] + pv)
            new_carry.append(r[:, 0:1] if diag else carries[h] + r[:, 0:1])
        return tuple(new_acc), tuple(new_carry)

    accs, carries = block(q0, None, None, True)

    def body(jj, st):
        k0 = pl.multiple_of((i - 1 - jj) * tq, tq)
        return block(k0, st[0], st[1], False)

    accs, _ = lax.fori_loop(0, i, body, (accs, carries))
    o_ref[0] = jnp.concatenate(accs, axis=-1).astype(o_ref.dtype)


def sb_attention(proj, col_block, *, tq=256):
    b, s, _ = proj.shape
    return pl.pallas_call(
        functools.partial(_sb_kernel, tq=tq),
        grid=(b, s // tq),
        in_specs=[pl.BlockSpec((1, s, 768), lambda bi, i: (bi, 0, col_block))],
        out_specs=pl.BlockSpec((1, tq, H_SB * HEAD_DIM), lambda bi, i: (bi, i, 0)),
        out_shape=jax.ShapeDtypeStruct((b, s, H_SB * HEAD_DIM), BF16),
        compiler_params=_params(("parallel", "arbitrary")),
        name="sb_attention",
    )(proj)


def _inproj_kernel(x_ref, g_ref, w_ref, o_ref, xn_ref):
    @pl.when(pl.program_id(1) == 0)
    def _():
        xn_ref[...] = _rms_rows(x_ref[...], g_ref[...]).astype(BF16)

    o_ref[...] = _mm(xn_ref[...], w_ref[...]).astype(o_ref.dtype)


def in_projection(x2d, gain, w, *, tm=1024, tn=768):
    n, d = x2d.shape
    tm = min(tm, n)
    c = w.shape[1]
    return pl.pallas_call(
        _inproj_kernel,
        grid=(n // tm, c // tn),
        in_specs=[pl.BlockSpec((tm, d), lambda i, j: (i, 0)),
                  pl.BlockSpec((1, d), lambda i, j: (0, 0)),
                  pl.BlockSpec((d, tn), lambda i, j: (0, j))],
        out_specs=pl.BlockSpec((tm, tn), lambda i, j: (i, j)),
        out_shape=jax.ShapeDtypeStruct((n, c), BF16),
        scratch_shapes=[pltpu.VMEM((tm, d), BF16)],
        compiler_params=_params(("parallel", "arbitrary")),
        name="in_projection",
    )(x2d, gain, w)


def _head_sumsq(x, head_dim):
    r_i = lax.broadcasted_iota(jnp.int32, (LANES, LANES), 0) // head_dim
    c_i = lax.broadcasted_iota(jnp.int32, (LANES, LANES), 1) // head_dim
    bd = jnp.where(r_i == c_i, 1.0, 0.0).astype(F32)
    x2 = x * x
    parts = [_mm(x2[:, b * LANES:(b + 1) * LANES], bd) for b in range(x.shape[1] // LANES)]
    return parts[0] if len(parts) == 1 else jnp.concatenate(parts, axis=-1)


def _head_rms(x, g, head_dim):
    ss = _head_sumsq(x, head_dim)
    return x * lax.rsqrt(ss * (1.0 / head_dim) + EPS) * g


def _eye(n, dtype):
    r = lax.broadcasted_iota(jnp.int32, (n, n), 0)
    c = lax.broadcasted_iota(jnp.int32, (n, n), 1)
    return jnp.where(r == c, 1.0, 0.0).astype(dtype)


def _dsa_kernel(a_ref, gq_ref, gk_ref, bias_ref, o_ref, kbn_ref, vt_ref, keys_ref, tie_ref, *, tq, ck, topk):
    s_len = a_ref.shape[1]
    i = pl.program_id(1)
    q0 = pl.multiple_of(i * tq, tq)
    hd = HEAD_DIM
    sub = ck // tq

    @pl.when(i == 0)
    def _():
        kv = a_ref[0, :, 512:640]
        kvf = kv.astype(F32)
        lane = lax.broadcasted_iota(jnp.int32, (1, LANES), 1)
        ss = _head_sumsq(jnp.where(lane < hd, kvf, 0.0), LANES)
        kn = kvf * lax.rsqrt(ss * (1.0 / hd) + EPS) * gk_ref[...]
        kbn_ref[...] = kn[:, :hd].astype(BF16)
        kvt = _nt(_eye(LANES, BF16), kv)
        for cb in range(s_len // ck):
            vt_ref[cb] = kvt[hd:, cb * ck:(cb + 1) * ck].astype(BF16)

    def heads_on_rows(x):
        return jnp.concatenate([x[:, h * hd:(h + 1) * hd] for h in range(x.shape[1] // hd)], axis=0)

    qi_all = heads_on_rows(a_ref[0, pl.ds(q0, tq), 256:512])
    wblk = a_ref[0, pl.ds(q0, tq), 640:768]
    sel_r = lax.broadcasted_iota(jnp.int32, (8, LANES), 0)
    sel_c = lax.broadcasted_iota(jnp.int32, (8, LANES), 1)
    w_t = _nt(jnp.where(sel_c == sel_r + D_IDX, 1.0, 0.0).astype(BF16), wblk)
    qpos = q0 + lax.broadcasted_iota(jnp.int32, (1, tq), 1)
    nck = (q0 + tq + ck - 1) // ck
    row_ck = lax.broadcasted_iota(jnp.int32, (ck, tq), 0)

    def p1(c, carry):
        k0 = pl.multiple_of(c * ck, ck)
        act = jnp.maximum(_nt(a_ref[0, pl.ds(k0, ck), 640:704], qi_all), 0.0)
        sc = w_t[0:1, :] * act[:, 0:tq]
        for h in range(1, H_IDX):
            sc = sc + w_t[h:h + 1, :] * act[:, h * tq:(h + 1) * tq]
        bits = pltpu.bitcast(sc, jnp.int32)
        key = jnp.where(bits < 0, -(bits & 0x7FFFFFFF), bits)
        key = jnp.where(k0 + row_ck <= qpos, key, INT_MIN)
        keys_ref[pl.ds(k0, ck), :] = key
        return carry

    lax.fori_loop(0, nck, p1, 0)

    def count(pred, src_ref=keys_ref):
        def body(c, acc):
            k0 = pl.multiple_of(c * ck, ck)
            m = jnp.where(pred(src_ref[pl.ds(k0, ck), :]), 1, 0)
            return acc + jnp.sum(m.reshape(ck // 8, 8, tq), axis=0)
        acc = lax.fori_loop(0, nck, body, jnp.zeros((8, tq), jnp.int32))
        return jnp.sum(acc, axis=0, keepdims=True)

    def search():
        c0 = count(lambda k: k >= 0)
        t = jnp.where(c0 >= topk, 0, INT_MIN).astype(jnp.int32)

        def vstep(b, t):
            cand = t | lax.shift_left(jnp.int32(1), 30 - b)
            return jnp.where(count(lambda k: k >= cand) >= topk, cand, t)

        t = lax.fori_loop(0, 31, vstep, t)
        need = topk - count(lambda k: k > t)

        def break_ties():
            nbits = max(1, (s_len - 1).bit_length())

            def tie_positions(c, carry):
                k0 = pl.multiple_of(c * ck, ck)
                tie_ref[pl.ds(k0, ck), :] = jnp.where(keys_ref[pl.ds(k0, ck), :] == t, k0 + row_ck, INT_MAX)
                return carry

            lax.fori_loop(0, nck, tie_positions, 0)

            def istep(b, y):
                cand = y | lax.shift_left(jnp.int32(1), nbits - 1 - b)
                return jnp.where(count(lambda p: p < cand, tie_ref) < need, cand, y)

            y = lax.fori_loop(0, nbits, istep, jnp.zeros((1, tq), jnp.int32))
            return jnp.where(t == INT_MIN, -1, y)

        def keep_all_ties():
            return jnp.where(t == INT_MIN, -1, s_len).astype(jnp.int32)

        surplus = jnp.max(count(lambda k: k == t) - need)
        return t, lax.cond(surplus > 0, break_ties, keep_all_ties)

    def no_search():
        return (jnp.full((1, tq), INT_MIN, jnp.int32), jnp.full((1, tq), -1, jnp.int32))

    thr, ymax = lax.cond(q0 + tq > topk, search, no_search)

    qb = a_ref[0, pl.ds(q0, tq), 0:256].astype(F32)
    qn_all = heads_on_rows(_head_rms(qb, gq_ref[...], hd).astype(BF16))

    def p3(c, st):
        ms, ls, accs = st
        k0 = pl.multiple_of(c * ck, ck)
        s_all = _nt(kbn_ref[pl.ds(k0, ck), :], qn_all)
        vtc = vt_ref[c]
        key = keys_ref[pl.ds(k0, ck), :]
        sel = (key > thr) | ((key == thr) & (k0 + row_ck <= ymax))
        bidx = [jnp.clip(i - (c * sub + r), 0, 2) for r in range(sub)]
        nm, nl, na = [], [], []
        for h in range(H_DSA):
            bias = jnp.concatenate([bias_ref[bidx[r], h] for r in range(sub)], axis=0)
            s_h = s_all[:, h * tq:(h + 1) * tq] + bias
            mb = jnp.max(jnp.where(sel, s_h, NEG), axis=0, keepdims=True)
            m_new = jnp.maximum(ms[h], mb)
            p = jnp.where(sel, jnp.exp2(s_h - m_new), 0.0)
            alpha = jnp.exp2(ms[h] - m_new)
            nm.append(m_new)
            nl.append(ls[h] * alpha + jnp.sum(p, axis=0, keepdims=True))
            na.append(accs[h] * alpha + _mm(vtc, p.astype(BF16)))
        return tuple(nm), tuple(nl), tuple(na)

    init = (tuple(jnp.full((1, tq), NEG, F32) for _ in range(H_DSA)),
            tuple(jnp.zeros((1, tq), F32) for _ in range(H_DSA)),
            tuple(jnp.zeros((hd, tq), F32) for _ in range(H_DSA)))
    _, ls, accs = lax.fori_loop(0, nck, p3, init)
    o_t = jnp.concatenate([accs[h] / ls[h] for h in range(H_DSA)], axis=0)
    o_ref[0] = _nt(_eye(tq, BF16), o_t.astype(BF16)).astype(o_ref.dtype)


def dsa_attention(proj, col_block, gq, gk, bias_t, *, tq=BLOCK, ck=512):
    b, s, _ = proj.shape
    topk = min(TOPK_MAX, s // 4)
    ck = min(ck, s)
    return pl.pallas_call(
        functools.partial(_dsa_kernel, tq=tq, ck=ck, topk=topk),
        grid=(b, s // tq),
        in_specs=[pl.BlockSpec((1, s, 768), lambda bi, i: (bi, 0, col_block)),
                  pl.BlockSpec((1, 256), lambda bi, i: (0, 0)),
                  pl.BlockSpec((1, LANES), lambda bi, i: (0, 0)),
                  pl.BlockSpec((3, H_DSA, tq, tq), lambda bi, i: (0, 0, 0, 0))],
        out_specs=pl.BlockSpec((1, tq, H_DSA * HEAD_DIM), lambda bi, i: (bi, i, 0)),
        out_shape=jax.ShapeDtypeStruct((b, s, H_DSA * HEAD_DIM), BF16),
        scratch_shapes=[pltpu.VMEM((s, HEAD_DIM), BF16),
                        pltpu.VMEM((s // ck, HEAD_DIM, ck), BF16),
                        pltpu.VMEM((s, tq), jnp.int32),
                        pltpu.VMEM((s, tq), jnp.int32)],
        compiler_params=_params(("parallel", "arbitrary")),
        name="dsa_attention",
    )(proj, gq, gk, bias_t)


def _t5_bucket(rel):
    n = jnp.maximum(rel, 0)
    max_exact = N_BUCKETS // 2
    nf = jnp.maximum(n, 1).astype(F32)
    large = max_exact + (jnp.log(nf / max_exact) / math.log(MAX_DISTANCE / max_exact)
                         * (N_BUCKETS - max_exact)).astype(jnp.int32)
    large = jnp.minimum(large, N_BUCKETS - 1)
    return jnp.where(n < max_exact, n, large)


def _bucket_lookup(tab, rel):
    hit = _t5_bucket(rel)[..., None, None] == jnp.arange(N_BUCKETS)[:, None]
    return jnp.sum(jnp.where(hit, tab.astype(F32), 0.0), axis=-2)


def dsa_bias_tiles(tab, tq=BLOCK):
    ks = jnp.arange(tq)[:, None]
    tl = jnp.arange(tq)[None, :]
    rel = jnp.stack([tl - ks, tq + tl - ks, jnp.full((tq, tq), 2 * tq + MAX_DISTANCE)])
    return _bucket_lookup(tab, rel).transpose(0, 3, 1, 2)


def _swa_kernel(sink_ref, cur_ref, prev_ref, gq_ref, gk_ref, bias_ref, o_ref, *, tq):
    i = pl.program_id(1)
    hd = HEAD_DIM
    g = H_SW // KV_SW
    qn = _head_rms(cur_ref[0, :, 0:512].astype(F32), gq_ref[...], hd).astype(BF16)
    k_cur = _head_rms(cur_ref[0, :, 512:640].astype(F32), gk_ref[...], hd).astype(BF16)
    k_prev = _head_rms(prev_ref[0, :, 512:640].astype(F32), gk_ref[...], hd).astype(BF16)
    k2 = jnp.concatenate([k_prev, k_cur], axis=0)
    v2 = jnp.concatenate([prev_ref[0, :, 640:768], cur_ref[0, :, 640:768]], axis=0)
    col = lax.broadcasted_iota(jnp.int32, (tq, 2 * tq), 1)
    has_prev = (col >= tq) | (i > 0)
    outs = []
    for h in range(H_SW):
        kv = h // g
        s = _nt(qn[:, h * hd:(h + 1) * hd], k2[:, kv * hd:(kv + 1) * hd]) + bias_ref[h]
        s = jnp.where(has_prev, s, NEG)
        sink = sink_ref[h]
        m = jnp.maximum(jnp.max(s, axis=-1, keepdims=True), sink)
        e = jnp.exp2(s - m)
        den = jnp.sum(e, axis=-1, keepdims=True) + jnp.exp2(sink - m)
        outs.append(_mm(e.astype(BF16), v2[:, kv * hd:(kv + 1) * hd]) / den)
    o_ref[0] = jnp.concatenate(outs, axis=-1).astype(o_ref.dtype)


def swa_bias_tiles(tab, tq=BLOCK):
    rel = (jnp.arange(tq)[:, None] + tq) - jnp.arange(2 * tq)[None, :]
    in_win = (rel >= 0) & (rel < WINDOW)
    bias = _bucket_lookup(tab, rel).transpose(2, 0, 1)
    return jnp.where(in_win[None], bias, NEG)


def swa_attention(proj, col_block, sinks, gq, gk, bias, *, tq=BLOCK):
    b, s, _ = proj.shape
    return pl.pallas_call(
        functools.partial(_swa_kernel, tq=tq),
        grid_spec=pltpu.PrefetchScalarGridSpec(
            num_scalar_prefetch=0,
            grid=(b, s // tq),
            in_specs=[pl.BlockSpec(memory_space=pltpu.SMEM),
                      pl.BlockSpec((1, tq, 768), lambda bi, i: (bi, i, col_block)),
                      pl.BlockSpec((1, tq, 768), lambda bi, i: (bi, jnp.maximum(i - 1, 0), col_block)),
                      pl.BlockSpec((1, 512), lambda bi, i: (0, 0)),
                      pl.BlockSpec((1, LANES), lambda bi, i: (0, 0)),
                      pl.BlockSpec((H_SW, tq, 2 * tq), lambda bi, i: (0, 0, 0))],
            out_specs=pl.BlockSpec((1, tq, H_SW * HEAD_DIM), lambda bi, i: (bi, i, 0))),
        out_shape=jax.ShapeDtypeStruct((b, s, H_SW * HEAD_DIM), BF16),
        compiler_params=_params(("parallel", "arbitrary")),
        name="swa_attention",
    )(sinks, proj, proj, gq, gk, bias)


def _merge_kernel(x_ref, g_ref, oa_ref, ob_ref, oc_ref, bg_ref, wa_ref, wb_ref, wc_ref, wo_ref, o_ref):
    d = x_ref.shape[1]
    merged = None
    for k, (o_k, w_k) in enumerate(((oa_ref, wa_ref), (ob_ref, wb_ref), (oc_ref, wc_ref))):
        logit = g_ref[:, k * d:(k + 1) * d].astype(F32) + bg_ref[k:k + 1, :]
        gate = 1.0 / (1.0 + jnp.exp(-logit))
        term = gate * _mm(o_k[...], w_k[...])
        merged = term if merged is None else merged + term
    o_ref[...] = x_ref[...] + _mm(merged.astype(BF16), wo_ref[...])


def merge_project(x2d, proj2d, o_a, o_b, o_c, b_gate, w_pa, w_pb, w_pc, w_out, *, tm=512):
    n, d = x2d.shape
    full = lambda a: pl.BlockSpec(a.shape, lambda i: (0,) * a.ndim)
    row = lambda a: pl.BlockSpec((tm, a.shape[1]), lambda i: (i, 0))
    return pl.pallas_call(
        _merge_kernel,
        grid=(n // tm,),
        in_specs=[row(x2d), pl.BlockSpec((tm, N_BRANCH * d), lambda i: (i, 0)),
                  row(o_a), row(o_b), row(o_c),
                  full(b_gate), full(w_pa), full(w_pb), full(w_pc), full(w_out)],
        out_specs=pl.BlockSpec((tm, d), lambda i: (i, 0)),
        out_shape=jax.ShapeDtypeStruct((n, d), F32),
        compiler_params=_params(("parallel",)),
        name="merge_project",
    )(x2d, proj2d, o_a, o_b, o_c, b_gate, w_pa, w_pb, w_pc, w_out)


def _memkv_kernel(m_ref, g_ref, w_ref, gk_ref, k_ref, v_ref):
    w_x = k_ref.shape[2]
    mn = _rms_rows(m_ref[0], g_ref[...]).astype(BF16)
    kv = _mm(mn, w_ref[...])
    k_ref[0] = _head_rms(kv[:, :w_x], gk_ref[...], XHEAD_DIM).astype(k_ref.dtype)
    v_ref[0] = kv[:, w_x:].astype(v_ref.dtype)


def memory_kv(mem, gain, w_kv, gk):
    b, m, d = mem.shape
    w_x = w_kv.shape[1] // 2
    return pl.pallas_call(
        _memkv_kernel,
        grid=(b,),
        in_specs=[pl.BlockSpec((1, m, d), lambda i: (i, 0, 0)),
                  pl.BlockSpec((1, d), lambda i: (0, 0)),
                  pl.BlockSpec(w_kv.shape, lambda i: (0, 0)),
                  pl.BlockSpec((1, w_x), lambda i: (0, 0))],
        out_specs=[pl.BlockSpec((1, m, w_x), lambda i: (i, 0, 0)),
                   pl.BlockSpec((1, m, w_x), lambda i: (i, 0, 0))],
        out_shape=[jax.ShapeDtypeStruct((b, m, w_x), BF16)] * 2,
        compiler_params=_params(("parallel",)),
        name="memory_kv",
    )(mem, gain, w_kv, gk)


def _xattn_kernel(x_ref, g_ref, wq_ref, gq_ref, k_ref, v_ref, wo_ref, o_ref):
    x = x_ref[0]
    xn = _rms_rows(x, g_ref[...]).astype(BF16)
    q = _head_rms(_mm(xn, wq_ref[...]), gq_ref[...], XHEAD_DIM).astype(BF16)
    outs = []
    for h in range(H_X):
        sl = slice(h * XHEAD_DIM, (h + 1) * XHEAD_DIM)
        s = _nt(q[:, sl], k_ref[0, :, sl])
        e = jnp.exp2(s - jnp.max(s, axis=-1, keepdims=True))
        den = jnp.sum(e, axis=-1, keepdims=True)
        outs.append(_mm(e.astype(BF16), v_ref[0, :, sl]) / den)
    o = jnp.concatenate(outs, axis=-1).astype(BF16)
    o_ref[0] = x + _mm(o, wo_ref[...])


def cross_attention(x, gain, w_q, gq, k_mem, v_mem, w_o, *, tq=512):
    b, s, d = x.shape
    m, w_x = k_mem.shape[1:]
    full = lambda a: pl.BlockSpec(a.shape, lambda bi, i: (0,) * a.ndim)
    return pl.pallas_call(
        _xattn_kernel,
        grid=(b, s // tq),
        in_specs=[pl.BlockSpec((1, tq, d), lambda bi, i: (bi, i, 0)),
                  full(gain), full(w_q), full(gq),
                  pl.BlockSpec((1, m, w_x), lambda bi, i: (bi, 0, 0)),
                  pl.BlockSpec((1, m, w_x), lambda bi, i: (bi, 0, 0)),
                  full(w_o)],
        out_specs=pl.BlockSpec((1, tq, d), lambda bi, i: (bi, i, 0)),
        out_shape=jax.ShapeDtypeStruct((b, s, d), F32),
        compiler_params=_params(("parallel", "parallel")),
        name="cross_attention",
    )(x, gain, w_q, gq, k_mem, v_mem, w_o)


def _router_kernel(x_ref, g_ref, whi_ref, wlo_ref, o_ref):
    hf = _rms_rows(x_ref[...], g_ref[...])
    hi = hf.astype(BF16)
    lo = (hf - hi.astype(F32)).astype(BF16)
    logits = _mm(hi, whi_ref[...]) + (_mm(hi, wlo_ref[...]) + _mm(lo, whi_ref[...]))
    lane = lax.broadcasted_iota(jnp.int32, logits.shape, 1).astype(F32)
    logits = jnp.where(lane < N_EXPERTS, logits, NEG)
    m1 = jnp.max(logits, axis=-1, keepdims=True)
    i1 = jnp.min(jnp.where(logits == m1, lane, float(LANES)), axis=-1, keepdims=True)
    rest = jnp.where(lane == i1, NEG, logits)
    m2 = jnp.max(rest, axis=-1, keepdims=True)
    i2 = jnp.min(jnp.where(rest == m2, lane, float(LANES)), axis=-1, keepdims=True)
    e2 = jnp.exp(m2 - m1)
    den = 1.0 + e2
    o_ref[...] = (jnp.where(lane == 0.0, i1, 0.0) + jnp.where(lane == 1.0, i2, 0.0)
                  + jnp.where(lane == 2.0, 1.0 / den, 0.0) + jnp.where(lane == 3.0, e2 / den, 0.0))


def router_gates(x2d, gain, w_hi, w_lo, *, tm=512):
    n, d = x2d.shape
    return pl.pallas_call(
        _router_kernel,
        grid=(n // tm,),
        in_specs=[pl.BlockSpec((tm, d), lambda i: (i, 0)),
                  pl.BlockSpec((1, d), lambda i: (0, 0)),
                  pl.BlockSpec((d, LANES), lambda i: (0, 0)),
                  pl.BlockSpec((d, LANES), lambda i: (0, 0))],
        out_specs=pl.BlockSpec((tm, LANES), lambda i: (i, 0)),
        out_shape=jax.ShapeDtypeStruct((n, LANES), F32),
        compiler_params=_params(("parallel",)),
        name="router_gates",
    )(x2d, gain, w_hi, w_lo)


def _swiglu_tile(xn, wg, wu, wd):
    gg = _mm(xn, wg)
    uu = _mm(xn, wu)
    act = gg * (1.0 / (1.0 + jnp.exp(-gg))) * uu
    return _mm(act.astype(BF16), wd)


def _ffn_kernel(x_ref, g_ref, wg_ref, wu_ref, wd_ref, o_ref, xn_ref, acc_ref):
    j = pl.program_id(1)

    @pl.when(j == 0)
    def _():
        xn_ref[...] = _rms_rows(x_ref[...], g_ref[...]).astype(BF16)
        acc_ref[...] = jnp.zeros_like(acc_ref)

    acc_ref[...] += _swiglu_tile(xn_ref[...], wg_ref[...], wu_ref[...], wd_ref[...])

    @pl.when(j == pl.num_programs(1) - 1)
    def _():
        o_ref[...] = x_ref[...] + acc_ref[...]


def dense_ffn(x2d, gain, w_gu, w_down, *, tm=1024, tf=256):
    n, d = x2d.shape
    tm = min(tm, n)
    f = w_down.shape[0]
    nf = f // tf
    return pl.pallas_call(
        _ffn_kernel,
        grid=(n // tm, nf),
        in_specs=[pl.BlockSpec((tm, d), lambda i, j: (i, 0)),
                  pl.BlockSpec((1, d), lambda i, j: (0, 0)),
                  pl.BlockSpec((d, tf), lambda i, j: (0, j)),
                  pl.BlockSpec((d, tf), lambda i, j: (0, j + nf)),
                  pl.BlockSpec((tf, d), lambda i, j: (j, 0))],
        out_specs=pl.BlockSpec((tm, d), lambda i, j: (i, 0)),
        out_shape=jax.ShapeDtypeStruct((n, d), F32),
        scratch_shapes=[pltpu.VMEM((tm, d), BF16), pltpu.VMEM((tm, d), F32)],
        compiler_params=_params(("parallel", "arbitrary")),
        name="dense_ffn",
    )(x2d, gain, w_gu, w_gu, w_down)


def _moe_kernel(te_ref, tn_ref, tok_ref, tok_next_ref, dst_prev_ref, x_hbm, g_ref, gate_ref, wg_ref, wu_ref,
                wd_ref, y_hbm, xg_ref, xn_ref, acc_ref, yb_ref, gsem, ssem, *, tm, nf):
    i = pl.program_id(0)
    j = pl.program_id(1)
    nt = pl.num_programs(0)
    slot = i % 2
    per_step = tm // nf
    active = tn_ref[i] > 0
    prev_active = (i > 0) & (tn_ref[jnp.maximum(i - 1, 0)] > 0)
    prev_issued = (i > 0) & (tn_ref[jnp.maximum(i - 2, 0)] > 0)

    def gather_copy(tok, r, s):
        return pltpu.make_async_copy(x_hbm.at[pl.ds(tok, 1), :], xg_ref.at[s, pl.ds(r, 1), :], gsem.at[s])

    def scatter_copy(dst, r, s):
        return pltpu.make_async_copy(yb_ref.at[s, pl.ds(r, 1), :], y_hbm.at[pl.ds(dst, 1), :], ssem.at[s])

    def start_all_rows(ids_ref, s, make_copy):
        def body(r8, c):
            for u in range(8):
                r = r8 * 8 + u
                make_copy(ids_ref[0, 0, r], r, s).start()
            return c
        lax.fori_loop(0, tm // 8, body, 0)

    def wait_gather(s):
        pltpu.make_async_copy(x_hbm.at[pl.ds(0, tm), :], xg_ref.at[s], gsem.at[s]).wait()

    def wait_scatter(s):
        pltpu.make_async_copy(yb_ref.at[s], y_hbm.at[pl.ds(0, tm), :], ssem.at[s]).wait()

    @pl.when(j == 0)
    def _():
        @pl.when(i == 0)
        def _():
            start_all_rows(tok_ref, slot, gather_copy)
            yb_ref[1] = jnp.zeros(yb_ref.shape[1:], yb_ref.dtype)
            n_real = y_hbm.shape[0] - 2 * tm
            for half in range(2):
                spare = pltpu.make_async_copy(yb_ref.at[1], y_hbm.at[pl.ds(n_real + half * tm, tm), :], ssem.at[1])
                spare.start()
                spare.wait()

        @pl.when(active | prev_active)
        def _():
            wait_gather(slot)

        @pl.when(prev_issued)
        def _():
            wait_scatter(slot)

        @pl.when(active)
        def _():
            xn_ref[...] = _rms_rows(xg_ref[slot], g_ref[...]).astype(BF16)
            acc_ref[...] = jnp.zeros_like(acc_ref)

        @pl.when(prev_active & jnp.logical_not(active))
        def _():
            start_all_rows(dst_prev_ref, 1 - slot, scatter_copy)

    @pl.when(active)
    def _():
        acc_ref[...] += _swiglu_tile(xn_ref[...], wg_ref[0], wu_ref[0], wd_ref[0])
        for u in range(per_step):
            r = j * per_step + u
            gather_copy(tok_next_ref[0, 0, r], r, 1 - slot).start()
            scatter_copy(dst_prev_ref[0, 0, r], r, 1 - slot).start()

        @pl.when(j == nf - 1)
        def _():
            yb_ref[slot] = acc_ref[...] * gate_ref[...]

    @pl.when((j == nf - 1) & (i == nt - 1) & (active | prev_active))
    def _():
        wait_scatter(1 - slot)


def moe_experts(x2d, gain, plan, w_gu, w_down, *, tm, tf=512):
    n, d = x2d.shape
    ne, f, _ = w_down.shape
    nf = f // tf
    assert tm % nf == 0 and tm % 16 == 0
    tile_e, tile_n, row_tok, row_dst_prev, row_gate = plan
    nt = tile_e.shape[0]
    smem_rows = lambda imap: pl.BlockSpec((1, 1, tm), imap, memory_space=pltpu.SMEM)
    live = lambda j, tn, i: j * jnp.minimum(tn[i], 1)
    return pl.pallas_call(
        functools.partial(_moe_kernel, tm=tm, nf=nf),
        grid_spec=pltpu.PrefetchScalarGridSpec(
            num_scalar_prefetch=2,
            grid=(nt, nf),
            in_specs=[smem_rows(lambda i, j, te, tn: (i, 0, 0)),
                      smem_rows(lambda i, j, te, tn: (jnp.minimum(i + 1, nt - 1), 0, 0)),
                      smem_rows(lambda i, j, te, tn: (i, 0, 0)),
                      pl.BlockSpec(memory_space=pl.ANY),
                      pl.BlockSpec((1, d), lambda i, j, te, tn: (0, 0)),
                      pl.BlockSpec((tm, 1), lambda i, j, te, tn: (i, 0)),
                      pl.BlockSpec((1, d, tf), lambda i, j, te, tn: (te[i], 0, live(j, tn, i))),
                      pl.BlockSpec((1, d, tf), lambda i, j, te, tn: (te[i], 0, live(j, tn, i) + nf)),
                      pl.BlockSpec((1, tf, d), lambda i, j, te, tn: (te[i], live(j, tn, i), 0))],
            out_specs=pl.BlockSpec(memory_space=pl.ANY),
            scratch_shapes=[pltpu.VMEM((2, tm, d), F32), pltpu.VMEM((tm, d), BF16), pltpu.VMEM((tm, d), F32),
                            pltpu.VMEM((2, tm, d), F32),
                            pltpu.SemaphoreType.DMA((2,)), pltpu.SemaphoreType.DMA((2,))]),
        out_shape=jax.ShapeDtypeStruct((2 * n + 2 * tm, d), F32),
        compiler_params=_params(("arbitrary", "arbitrary")),
        name="moe_experts",
    )(tile_e, tile_n, row_tok, row_tok, row_dst_prev, x2d, gain, row_gate, w_gu, w_gu, w_down)


def moe_plan(route, *, tm):
    n = route.shape[0]
    flat_e = route[:, :2].astype(jnp.int32).reshape(-1)
    flat_g = route[:, 2:4].reshape(-1)
    nt = (2 * n) // tm + N_EXPERTS + 1
    order = jnp.argsort(flat_e, stable=True).astype(jnp.int32)
    counts = jnp.sum(flat_e[:, None] == jnp.arange(N_EXPERTS)[None, :], axis=0).astype(jnp.int32)
    off = jnp.cumsum(counts) - counts
    tiles = (counts + tm - 1) // tm
    tile_off = jnp.cumsum(tiles) - tiles
    tile_id = jnp.arange(nt, dtype=jnp.int32)
    used = tile_id < jnp.sum(tiles)
    tile_e = jnp.clip(jnp.sum(tile_id[:, None] >= tile_off[None, :], axis=1) - 1, 0, N_EXPERTS - 1)
    tile_e = jnp.where(used, tile_e, tile_e[jnp.maximum(jnp.sum(tiles) - 1, 0)]).astype(jnp.int32)
    first_row = (tile_id - tile_off[tile_e]) * tm
    tile_n = jnp.where(used, jnp.clip(counts[tile_e] - first_row, 0, tm), 0).astype(jnp.int32)
    r = jnp.arange(tm, dtype=jnp.int32)[None, :]
    valid = r < tile_n[:, None]
    a = order[jnp.clip(off[tile_e][:, None] + first_row[:, None] + r, 0, 2 * n - 1)]
    row_tok = jnp.where(valid, a // 2, 0).astype(jnp.int32).reshape(nt, 1, tm)
    spare = 2 * n + (tile_id[:, None] % 2) * tm + r
    row_dst = jnp.where(valid, (a % 2) * n + a // 2, spare).astype(jnp.int32)
    row_dst_prev = jnp.concatenate([2 * n + tm + r, row_dst[:-1]], axis=0).reshape(nt, 1, tm)
    row_gate = jnp.where(valid, flat_g[a], 0.0).astype(F32).reshape(nt * tm, 1)
    return tile_e, tile_n, row_tok, row_dst_prev, row_gate


def _combine_kernel(x_ref, y0_ref, y1_ref, o_ref):
    o_ref[...] = x_ref[...] + (y0_ref[...] + y1_ref[...])


def moe_combine(x2d, y, *, tm=512):
    n, d = x2d.shape
    nb = n // tm
    return pl.pallas_call(
        _combine_kernel,
        grid=(nb,),
        in_specs=[pl.BlockSpec((tm, d), lambda i: (i, 0)),
                  pl.BlockSpec((tm, d), lambda i: (i, 0)),
                  pl.BlockSpec((tm, d), lambda i: (i + nb, 0))],
        out_specs=pl.BlockSpec((tm, d), lambda i: (i, 0)),
        out_shape=jax.ShapeDtypeStruct((n, d), F32),
        compiler_params=_params(("parallel",)),
        name="moe_combine",
    )(x2d, y, y)


def _pack_w_in(w, d):
    sizes = (256, 256, 256, 256, 64, 64, 256, 64, 4, 512, 128, 128, N_BRANCH * d)
    qa, ka, va, qb, kb, vb, qi, ki, wi, qc, kc, vc, g = jnp.split(w, np.cumsum(sizes)[:-1].tolist(), axis=-1)
    pad = jnp.zeros((w.shape[0], 60), w.dtype)
    return jnp.concatenate([g, qa * (HEAD_DIM ** -0.5 * LOG2E), ka, va,
                            qb, qi * D_IDX ** -0.5, kb, vb, ki, wi, pad,
                            qc, kc, vc], axis=-1).astype(BF16)


def kernel(x, mem, rel_bias, norm_mix, w_in, b_gate, qn_dsa, kn_dsa, qn_swa, kn_swa, sinks, w_pa, w_pb, w_pc, w_out, norm_x, norm_mem, w_xq, w_xkv, w_xo, qn_x, kn_x, norm_ffn, w_gu_dense, w_down_dense, w_router, w_gu_moe, w_down_moe):
    b, s, d = x.shape
    depth = w_in.shape[0]
    n = b * s
    row = lambda v: v.reshape(1, -1).astype(F32)
    bias_dsa = dsa_bias_tiles(rel_bias[:, :H_DSA] * LOG2E)
    bias_swa = swa_bias_tiles(rel_bias[:, H_DSA:] * LOG2E)
    qscale = HEAD_DIM ** -0.5 * LOG2E
    gcol = (N_BRANCH * d) // 768

    x2 = x.reshape(n, d)
    for l in range(depth):
        proj = in_projection(x2, row(norm_mix[l]), _pack_w_in(w_in[l], d))
        proj3 = proj.reshape(b, s, -1)
        o_a = sb_attention(proj3, gcol)
        o_b = dsa_attention(proj3, gcol + 1, row(jnp.tile(qn_dsa[l] * qscale, H_DSA)),
                            row(jnp.concatenate([kn_dsa[l], jnp.zeros_like(kn_dsa[l])])), bias_dsa)
        o_c = swa_attention(proj3, gcol + 2, sinks[l].astype(F32) * LOG2E,
                            row(jnp.tile(qn_swa[l] * qscale, H_SW)), row(jnp.tile(kn_swa[l], KV_SW)), bias_swa)
        x2 = merge_project(x2, proj, o_a.reshape(n, -1), o_b.reshape(n, -1), o_c.reshape(n, -1),
                           b_gate[l].astype(F32), w_pa[l].astype(BF16), w_pb[l].astype(BF16),
                           w_pc[l].astype(BF16), w_out[l].astype(BF16))
        k_mem, v_mem = memory_kv(mem, row(norm_mem[l]), w_xkv[l].astype(BF16), row(jnp.tile(kn_x[l], H_X)))
        x2 = cross_attention(x2.reshape(b, s, d), row(norm_x[l]), w_xq[l].astype(BF16),
                             row(jnp.tile(qn_x[l] * (XHEAD_DIM ** -0.5 * LOG2E), H_X)), k_mem, v_mem,
                             w_xo[l].astype(BF16)).reshape(n, d)
        if l % 2 == 0:
            x2 = dense_ffn(x2, row(norm_ffn[l]), w_gu_dense[l // 2].astype(BF16),
                           w_down_dense[l // 2].astype(BF16), tf=256)
        else:
            wr = jnp.pad(w_router[l // 2].astype(F32), ((0, 0), (0, LANES - N_EXPERTS)))
            wr_hi = wr.astype(BF16)
            wr_lo = (wr - wr_hi.astype(F32)).astype(BF16)
            route = router_gates(x2, row(norm_ffn[l]), wr_hi, wr_lo)
            tm_moe = 672
            y = moe_experts(x2, row(norm_ffn[l]), moe_plan(route, tm=tm_moe), w_gu_moe[l // 2].astype(BF16),
                            w_down_moe[l // 2].astype(BF16), tm=tm_moe)
            x2 = moe_combine(x2, y)
    return x2.reshape(b, s, d)
```

```python
import functools
import math

import numpy as np
import jax
import jax.numpy as jnp
from jax import lax
from jax.experimental import pallas as pl
from jax.experimental.pallas import tpu as pltpu

HEAD_DIM = 64
H_SB = 4
H_DSA = 4
H_IDX = 4
D_IDX = 64
TOPK_MAX = 256
H_SW = 8
KV_SW = 2
WINDOW = 128
BLOCK = 128
N_BRANCH = 3
N_BUCKETS = 32
MAX_DISTANCE = 128
H_X = 4
XHEAD_DIM = 128
N_EXPERTS = 8
EPS = 1e-6

LANES = 128
VMEM_LIMIT = 56 * 1024 * 1024
NEG = -1e30
INT_MIN = -(2 ** 31)
INT_MAX = 2 ** 31 - 1
LOG2E = math.log2(math.e)

F32 = jnp.float32
BF16 = jnp.bfloat16


def _nt(a, b):
    return lax.dot_general(a, b, (((1,), (1,)), ((), ())), preferred_element_type=F32)


def _mm(a, b):
    return jnp.dot(a, b, preferred_element_type=F32)


def _rms_rows(x, g):
    ms = jnp.mean(x * x, axis=-1, keepdims=True)
    return x * lax.rsqrt(ms + EPS) * g


def _params(sem):
    return pltpu.CompilerParams(dimension_semantics=sem, vmem_limit_bytes=VMEM_LIMIT)


def _sb_kernel(a_ref, o_ref, *, tq):
    i = pl.program_id(1)
    q0 = pl.multiple_of(i * tq, tq)
    hd = HEAD_DIM
    row = lax.broadcasted_iota(jnp.int32, (tq, tq), 0)
    col = lax.broadcasted_iota(jnp.int32, (tq, tq), 1)
    strict = col < row
    u_inc = jnp.where(row >= col, 1.0, 0.0).astype(BF16)
    qs = [a_ref[0, pl.ds(q0, tq), h * hd:(h + 1) * hd] for h in range(H_SB)]

    heads = range(H_SB)

    def block(k0s, accs, carries, diag):
        chains = [(b, h) for b in range(len(k0s)) for h in heads]
        ks = {(b, h): a_ref[0, pl.ds(k0s[b], tq), 256 + h * hd:256 + (h + 1) * hd] for b, h in chains}
        vs = {(b, h): a_ref[0, pl.ds(k0s[b], tq), 512 + h * hd:512 + (h + 1) * hd] for b, h in chains}
        zs = {c: _nt(qs[c[1]], ks[c]) for c in chains}
        lks = {c: -(jnp.maximum(zs[c], 0.0) + jnp.log2(1.0 + jnp.exp2(-jnp.abs(zs[c])))) for c in chains}
        if diag:
            lks = {c: jnp.where(strict, lks[c], 0.0) for c in chains}
        rs = {c: _mm(lks[c].astype(BF16), u_inc) for c in chains}
        carry = {h: (None if diag else carries[h]) for h in heads}
        atts = {}
        for b, h in chains:
            if diag:
                atts[(b, h)] = jnp.where(strict, jnp.exp2(zs[(b, h)] + rs[(b, h)]), 0.0)
                carry[h] = rs[(b, h)][:, 0:1]
            else:
                atts[(b, h)] = jnp.exp2(zs[(b, h)] + rs[(b, h)] + carry[h])
                carry[h] = carry[h] + rs[(b, h)][:, 0:1]
        pvs = {c: _mm(atts[c].astype(BF16), vs[c]) for c in chains}
        new_acc = []
        for h in heads:
            tot = pvs[(0, h)] if diag else accs[h] + pvs[(0, h)]
            for b in range(1, len(k0s)):
                tot = tot + pvs[(b, h)]
            new_acc.append(tot)
        return tuple(new_acc), tuple(carry[h] for h in heads)

    state = block([q0], None, None, True)

    def two_blocks(jj, st):
        near = pl.multiple_of((i - 1 - 2 * jj) * tq, tq)
        far = pl.multiple_of((i - 2 - 2 * jj) * tq, tq)
        return block([near, far], st[0], st[1], False)

    state = lax.fori_loop(0, i // 2, two_blocks, state)
    accs, _ = lax.fori_loop(0, i % 2, lambda jj, st: block([0], st[0], st[1], False), state)
    o_ref[0] = jnp.concatenate(accs, axis=-1).astype(o_ref.dtype)


def sb_attention(proj, col_block, *, tq=256):
    b, s, _ = proj.shape
    return pl.pallas_call(
        functools.partial(_sb_kernel, tq=tq),
        grid=(b, s // tq),
        in_specs=[pl.BlockSpec((1, s, 768), lambda bi, i: (bi, 0, col_block))],
        out_specs=pl.BlockSpec((1, tq, H_SB * HEAD_DIM), lambda bi, i: (bi, i, 0)),
        out_shape=jax.ShapeDtypeStruct((b, s, H_SB * HEAD_DIM), BF16),
        compiler_params=_params(("parallel", "arbitrary")),
        name="sb_attention",
    )(proj)


def _inproj_kernel(x_ref, g_ref, w_ref, o_ref, xn_ref):
    @pl.when(pl.program_id(1) == 0)
    def _():
        xn_ref[...] = _rms_rows(x_ref[...], g_ref[...]).astype(BF16)

    o_ref[...] = _mm(xn_ref[...], w_ref[...]).astype(o_ref.dtype)


def in_projection(x2d, gain, w, *, tm=1024, tn=768):
    n, d = x2d.shape
    tm = min(tm, n)
    c = w.shape[1]
    return pl.pallas_call(
        _inproj_kernel,
        grid=(n // tm, c // tn),
        in_specs=[pl.BlockSpec((tm, d), lambda i, j: (i, 0)),
                  pl.BlockSpec((1, d), lambda i, j: (0, 0)),
                  pl.BlockSpec((d, tn), lambda i, j: (0, j))],
        out_specs=pl.BlockSpec((tm, tn), lambda i, j: (i, j)),
        out_shape=jax.ShapeDtypeStruct((n, c), BF16),
        scratch_shapes=[pltpu.VMEM((tm, d), BF16)],
        compiler_params=_params(("parallel", "arbitrary")),
        name="in_projection",
    )(x2d, gain, w)


def _head_sumsq(x, head_dim):
    r_i = lax.broadcasted_iota(jnp.int32, (LANES, LANES), 0) // head_dim
    c_i = lax.broadcasted_iota(jnp.int32, (LANES, LANES), 1) // head_dim
    bd = jnp.where(r_i == c_i, 1.0, 0.0).astype(F32)
    x2 = x * x
    parts = [_mm(x2[:, b * LANES:(b + 1) * LANES], bd) for b in range(x.shape[1] // LANES)]
    return parts[0] if len(parts) == 1 else jnp.concatenate(parts, axis=-1)


def _head_rms(x, g, head_dim):
    ss = _head_sumsq(x, head_dim)
    return x * lax.rsqrt(ss * (1.0 / head_dim) + EPS) * g


def _eye(n, dtype):
    r = lax.broadcasted_iota(jnp.int32, (n, n), 0)
    c = lax.broadcasted_iota(jnp.int32, (n, n), 1)
    return jnp.where(r == c, 1.0, 0.0).astype(dtype)


def _dsa_kernel(a_ref, gq_ref, gk_ref, bias_ref, o_ref, kbn_ref, vt_ref, keys_ref, tie_ref, *, tq, ck, topk):
    s_len = a_ref.shape[1]
    i = pl.program_id(1)
    q0 = pl.multiple_of(i * tq, tq)
    hd = HEAD_DIM
    sub = ck // tq

    @pl.when(i == 0)
    def _():
        kv = a_ref[0, :, 512:640]
        kvf = kv.astype(F32)
        lane = lax.broadcasted_iota(jnp.int32, (1, LANES), 1)
        ss = _head_sumsq(jnp.where(lane < hd, kvf, 0.0), LANES)
        kn = kvf * lax.rsqrt(ss * (1.0 / hd) + EPS) * gk_ref[...]
        kbn_ref[...] = kn[:, :hd].astype(BF16)
        kvt = _nt(_eye(LANES, BF16), kv)
        for cb in range(s_len // ck):
            vt_ref[cb] = kvt[hd:, cb * ck:(cb + 1) * ck].astype(BF16)

    def heads_on_rows(x):
        return jnp.concatenate([x[:, h * hd:(h + 1) * hd] for h in range(x.shape[1] // hd)], axis=0)

    qi_all = heads_on_rows(a_ref[0, pl.ds(q0, tq), 256:512])
    wblk = a_ref[0, pl.ds(q0, tq), 640:768]
    sel_r = lax.broadcasted_iota(jnp.int32, (8, LANES), 0)
    sel_c = lax.broadcasted_iota(jnp.int32, (8, LANES), 1)
    w_t = _nt(jnp.where(sel_c == sel_r + D_IDX, 1.0, 0.0).astype(BF16), wblk)
    qpos = q0 + lax.broadcasted_iota(jnp.int32, (1, tq), 1)
    nck = (q0 + tq + ck - 1) // ck
    row_ck = lax.broadcasted_iota(jnp.int32, (ck, tq), 0)

    def p1(c, carry):
        k0 = pl.multiple_of(c * ck, ck)
        act = jnp.maximum(_nt(a_ref[0, pl.ds(k0, ck), 640:704], qi_all), 0.0)
        sc = w_t[0:1, :] * act[:, 0:tq]
        for h in range(1, H_IDX):
            sc = sc + w_t[h:h + 1, :] * act[:, h * tq:(h + 1) * tq]
        bits = pltpu.bitcast(sc, jnp.int32)
        key = jnp.where(bits < 0, -(bits & 0x7FFFFFFF), bits)
        key = jnp.where(k0 + row_ck <= qpos, key, INT_MIN)
        keys_ref[pl.ds(k0, ck), :] = key
        return carry

    lax.fori_loop(0, nck, p1, 0)

    def count(pred, src_ref=keys_ref):
        def body(c, acc):
            k0 = pl.multiple_of(c * ck, ck)
            m = jnp.where(pred(src_ref[pl.ds(k0, ck), :]), 1, 0)
            return acc + jnp.sum(m.reshape(ck // 8, 8, tq), axis=0)
        acc = lax.fori_loop(0, nck, body, jnp.zeros((8, tq), jnp.int32))
        return jnp.sum(acc, axis=0, keepdims=True)

    def search():
        c0 = count(lambda k: k >= 0)
        t = jnp.where(c0 >= topk, 0, INT_MIN).astype(jnp.int32)

        def vstep(b, t):
            cand = t | lax.shift_left(jnp.int32(1), 30 - b)
            return jnp.where(count(lambda k: k >= cand) >= topk, cand, t)

        t = lax.fori_loop(0, 31, vstep, t)
        need = topk - count(lambda k: k > t)

        def break_ties():
            nbits = max(1, (s_len - 1).bit_length())

            def tie_positions(c, carry):
                k0 = pl.multiple_of(c * ck, ck)
                tie_ref[pl.ds(k0, ck), :] = jnp.where(keys_ref[pl.ds(k0, ck), :] == t, k0 + row_ck, INT_MAX)
                return carry

            lax.fori_loop(0, nck, tie_positions, 0)

            def istep(b, y):
                cand = y | lax.shift_left(jnp.int32(1), nbits - 1 - b)
                return jnp.where(count(lambda p: p < cand, tie_ref) < need, cand, y)

            y = lax.fori_loop(0, nbits, istep, jnp.zeros((1, tq), jnp.int32))
            return jnp.where(t == INT_MIN, -1, y)

        def keep_all_ties():
            return jnp.where(t == INT_MIN, -1, s_len).astype(jnp.int32)

        surplus = jnp.max(count(lambda k: k == t) - need)
        return t, lax.cond(surplus > 0, break_ties, keep_all_ties)

    def no_search():
        return (jnp.full((1, tq), INT_MIN, jnp.int32), jnp.full((1, tq), -1, jnp.int32))

    thr, ymax = lax.cond(q0 + tq > topk, search, no_search)

    qb = a_ref[0, pl.ds(q0, tq), 0:256].astype(F32)
    qn_all = heads_on_rows(_head_rms(qb, gq_ref[...], hd).astype(BF16))

    def p3(c, st):
        ms, ls, accs = st
        k0 = pl.multiple_of(c * ck, ck)
        s_all = _nt(kbn_ref[pl.ds(k0, ck), :], qn_all)
        vtc = vt_ref[c]
        key = keys_ref[pl.ds(k0, ck), :]
        sel = (key > thr) | ((key == thr) & (k0 + row_ck <= ymax))
        bidx = [jnp.clip(i - (c * sub + r), 0, 2) for r in range(sub)]
        heads = range(H_DSA)
        s_h = [s_all[:, h * tq:(h + 1) * tq]
               + jnp.concatenate([bias_ref[bidx[r], h] for r in range(sub)], axis=0) for h in heads]
        m_new = [jnp.maximum(ms[h], jnp.max(jnp.where(sel, s_h[h], NEG), axis=0, keepdims=True)) for h in heads]
        p = [jnp.where(sel, jnp.exp2(s_h[h] - m_new[h]), 0.0) for h in heads]
        alpha = [jnp.exp2(ms[h] - m_new[h]) for h in heads]
        pv = [_mm(vtc, p[h].astype(BF16)) for h in heads]
        nl = [ls[h] * alpha[h] + jnp.sum(p[h], axis=0, keepdims=True) for h in heads]
        na = [accs[h] * alpha[h] + pv[h] for h in heads]
        return tuple(m_new), tuple(nl), tuple(na)

    init = (tuple(jnp.full((1, tq), NEG, F32) for _ in range(H_DSA)),
            tuple(jnp.zeros((1, tq), F32) for _ in range(H_DSA)),
            tuple(jnp.zeros((hd, tq), F32) for _ in range(H_DSA)))
    _, ls, accs = lax.fori_loop(0, nck, p3, init)
    o_t = jnp.concatenate([accs[h] / ls[h] for h in range(H_DSA)], axis=0)
    o_ref[0] = _nt(_eye(tq, BF16), o_t.astype(BF16)).astype(o_ref.dtype)


def dsa_attention(proj, col_block, gq, gk, bias_t, *, tq=BLOCK, ck=512):
    b, s, _ = proj.shape
    topk = min(TOPK_MAX, s // 4)
    ck = min(ck, s)
    return pl.pallas_call(
        functools.partial(_dsa_kernel, tq=tq, ck=ck, topk=topk),
        grid=(b, s // tq),
        in_specs=[pl.BlockSpec((1, s, 768), lambda bi, i: (bi, 0, col_block)),
                  pl.BlockSpec((1, 256), lambda bi, i: (0, 0)),
                  pl.BlockSpec((1, LANES), lambda bi, i: (0, 0)),
                  pl.BlockSpec((3, H_DSA, tq, tq), lambda bi, i: (0, 0, 0, 0))],
        out_specs=pl.BlockSpec((1, tq, H_DSA * HEAD_DIM), lambda bi, i: (bi, i, 0)),
        out_shape=jax.ShapeDtypeStruct((b, s, H_DSA * HEAD_DIM), BF16),
        scratch_shapes=[pltpu.VMEM((s, HEAD_DIM), BF16),
                        pltpu.VMEM((s // ck, HEAD_DIM, ck), BF16),
                        pltpu.VMEM((s, tq), jnp.int32),
                        pltpu.VMEM((s, tq), jnp.int32)],
        compiler_params=_params(("parallel", "arbitrary")),
        name="dsa_attention",
    )(proj, gq, gk, bias_t)


def _t5_bucket(rel):
    n = jnp.maximum(rel, 0)
    max_exact = N_BUCKETS // 2
    nf = jnp.maximum(n, 1).astype(F32)
    large = max_exact + (jnp.log(nf / max_exact) / math.log(MAX_DISTANCE / max_exact)
                         * (N_BUCKETS - max_exact)).astype(jnp.int32)
    large = jnp.minimum(large, N_BUCKETS - 1)
    return jnp.where(n < max_exact, n, large)


def _bucket_lookup(tab, rel):
    hit = _t5_bucket(rel)[..., None, None] == jnp.arange(N_BUCKETS)[:, None]
    return jnp.sum(jnp.where(hit, tab.astype(F32), 0.0), axis=-2)


def dsa_bias_tiles(tab, tq=BLOCK):
    ks = jnp.arange(tq)[:, None]
    tl = jnp.arange(tq)[None, :]
    rel = jnp.stack([tl - ks, tq + tl - ks, jnp.full((tq, tq), 2 * tq + MAX_DISTANCE)])
    return _bucket_lookup(tab, rel).transpose(0, 3, 1, 2)


def _swa_kernel(sink_ref, cur_ref, prev_ref, gq_ref, gk_ref, bias_ref, o_ref, *, tq):
    i = pl.program_id(1)
    hd = HEAD_DIM
    g = H_SW // KV_SW
    qn = _head_rms(cur_ref[0, :, 0:512].astype(F32), gq_ref[...], hd).astype(BF16)
    k_cur = _head_rms(cur_ref[0, :, 512:640].astype(F32), gk_ref[...], hd).astype(BF16)
    k_prev = _head_rms(prev_ref[0, :, 512:640].astype(F32), gk_ref[...], hd).astype(BF16)
    k2 = jnp.concatenate([k_prev, k_cur], axis=0)
    v2 = jnp.concatenate([prev_ref[0, :, 640:768], cur_ref[0, :, 640:768]], axis=0)
    col = lax.broadcasted_iota(jnp.int32, (tq, 2 * tq), 1)
    has_prev = (col >= tq) | (i > 0)
    heads = range(H_SW)
    s = [jnp.where(has_prev, _nt(qn[:, h * hd:(h + 1) * hd], k2[:, (h // g) * hd:(h // g + 1) * hd]) + bias_ref[h], NEG)
         for h in heads]
    m = [jnp.maximum(jnp.max(s[h], axis=-1, keepdims=True), sink_ref[h]) for h in heads]
    e = [jnp.exp2(s[h] - m[h]) for h in heads]
    den = [jnp.sum(e[h], axis=-1, keepdims=True) + jnp.exp2(sink_ref[h] - m[h]) for h in heads]
    outs = [_mm(e[h].astype(BF16), v2[:, (h // g) * hd:(h // g + 1) * hd]) / den[h] for h in heads]
    o_ref[0] = jnp.concatenate(outs, axis=-1).astype(o_ref.dtype)


def swa_bias_tiles(tab, tq=BLOCK):
    rel = (jnp.arange(tq)[:, None] + tq) - jnp.arange(2 * tq)[None, :]
    in_win = (rel >= 0) & (rel < WINDOW)
    bias = _bucket_lookup(tab, rel).transpose(2, 0, 1)
    return jnp.where(in_win[None], bias, NEG)


def swa_attention(proj, col_block, sinks, gq, gk, bias, *, tq=BLOCK):
    b, s, _ = proj.shape
    return pl.pallas_call(
        functools.partial(_swa_kernel, tq=tq),
        grid_spec=pltpu.PrefetchScalarGridSpec(
            num_scalar_prefetch=0,
            grid=(b, s // tq),
            in_specs=[pl.BlockSpec(memory_space=pltpu.SMEM),
                      pl.BlockSpec((1, tq, 768), lambda bi, i: (bi, i, col_block)),
                      pl.BlockSpec((1, tq, 768), lambda bi, i: (bi, jnp.maximum(i - 1, 0), col_block)),
                      pl.BlockSpec((1, 512), lambda bi, i: (0, 0)),
                      pl.BlockSpec((1, LANES), lambda bi, i: (0, 0)),
                      pl.BlockSpec((H_SW, tq, 2 * tq), lambda bi, i: (0, 0, 0))],
            out_specs=pl.BlockSpec((1, tq, H_SW * HEAD_DIM), lambda bi, i: (bi, i, 0))),
        out_shape=jax.ShapeDtypeStruct((b, s, H_SW * HEAD_DIM), BF16),
        compiler_params=_params(("parallel", "arbitrary")),
        name="swa_attention",
    )(sinks, proj, proj, gq, gk, bias)


def _merge_kernel(x_ref, g_ref, oa_ref, ob_ref, oc_ref, bg_ref, wa_ref, wb_ref, wc_ref, wo_ref, o_ref):
    d = x_ref.shape[1]
    merged = None
    for k, (o_k, w_k) in enumerate(((oa_ref, wa_ref), (ob_ref, wb_ref), (oc_ref, wc_ref))):
        logit = g_ref[:, k * d:(k + 1) * d].astype(F32) + bg_ref[k:k + 1, :]
        gate = 1.0 / (1.0 + jnp.exp(-logit))
        term = gate * _mm(o_k[...], w_k[...])
        merged = term if merged is None else merged + term
    o_ref[...] = x_ref[...] + _mm(merged.astype(BF16), wo_ref[...])


def merge_project(x2d, proj2d, o_a, o_b, o_c, b_gate, w_pa, w_pb, w_pc, w_out, *, tm=512):
    n, d = x2d.shape
    full = lambda a: pl.BlockSpec(a.shape, lambda i: (0,) * a.ndim)
    row = lambda a: pl.BlockSpec((tm, a.shape[1]), lambda i: (i, 0))
    return pl.pallas_call(
        _merge_kernel,
        grid=(n // tm,),
        in_specs=[row(x2d), pl.BlockSpec((tm, N_BRANCH * d), lambda i: (i, 0)),
                  row(o_a), row(o_b), row(o_c),
                  full(b_gate), full(w_pa), full(w_pb), full(w_pc), full(w_out)],
        out_specs=pl.BlockSpec((tm, d), lambda i: (i, 0)),
        out_shape=jax.ShapeDtypeStruct((n, d), F32),
        compiler_params=_params(("parallel",)),
        name="merge_project",
    )(x2d, proj2d, o_a, o_b, o_c, b_gate, w_pa, w_pb, w_pc, w_out)


def _memkv_kernel(m_ref, g_ref, w_ref, gk_ref, k_ref, v_ref):
    w_x = k_ref.shape[2]
    mn = _rms_rows(m_ref[0], g_ref[...]).astype(BF16)
    kv = _mm(mn, w_ref[...])
    k_ref[0] = _head_rms(kv[:, :w_x], gk_ref[...], XHEAD_DIM).astype(k_ref.dtype)
    v_ref[0] = kv[:, w_x:].astype(v_ref.dtype)


def memory_kv(mem, gain, w_kv, gk):
    b, m, d = mem.shape
    w_x = w_kv.shape[1] // 2
    return pl.pallas_call(
        _memkv_kernel,
        grid=(b,),
        in_specs=[pl.BlockSpec((1, m, d), lambda i: (i, 0, 0)),
                  pl.BlockSpec((1, d), lambda i: (0, 0)),
                  pl.BlockSpec(w_kv.shape, lambda i: (0, 0)),
                  pl.BlockSpec((1, w_x), lambda i: (0, 0))],
        out_specs=[pl.BlockSpec((1, m, w_x), lambda i: (i, 0, 0)),
                   pl.BlockSpec((1, m, w_x), lambda i: (i, 0, 0))],
        out_shape=[jax.ShapeDtypeStruct((b, m, w_x), BF16)] * 2,
        compiler_params=_params(("parallel",)),
        name="memory_kv",
    )(mem, gain, w_kv, gk)


def _xattn_kernel(x_ref, g_ref, wq_ref, gq_ref, k_ref, v_ref, wo_ref, o_ref):
    x = x_ref[0]
    xn = _rms_rows(x, g_ref[...]).astype(BF16)
    q = _head_rms(_mm(xn, wq_ref[...]), gq_ref[...], XHEAD_DIM).astype(BF16)
    outs = []
    for h in range(H_X):
        sl = slice(h * XHEAD_DIM, (h + 1) * XHEAD_DIM)
        s = _nt(q[:, sl], k_ref[0, :, sl])
        e = jnp.exp2(s - jnp.max(s, axis=-1, keepdims=True))
        den = jnp.sum(e, axis=-1, keepdims=True)
        outs.append(_mm(e.astype(BF16), v_ref[0, :, sl]) / den)
    o = jnp.concatenate(outs, axis=-1).astype(BF16)
    o_ref[0] = x + _mm(o, wo_ref[...])


def cross_attention(x, gain, w_q, gq, k_mem, v_mem, w_o, *, tq=512):
    b, s, d = x.shape
    m, w_x = k_mem.shape[1:]
    full = lambda a: pl.BlockSpec(a.shape, lambda bi, i: (0,) * a.ndim)
    return pl.pallas_call(
        _xattn_kernel,
        grid=(b, s // tq),
        in_specs=[pl.BlockSpec((1, tq, d), lambda bi, i: (bi, i, 0)),
                  full(gain), full(w_q), full(gq),
                  pl.BlockSpec((1, m, w_x), lambda bi, i: (bi, 0, 0)),
                  pl.BlockSpec((1, m, w_x), lambda bi, i: (bi, 0, 0)),
                  full(w_o)],
        out_specs=pl.BlockSpec((1, tq, d), lambda bi, i: (bi, i, 0)),
        out_shape=jax.ShapeDtypeStruct((b, s, d), F32),
        compiler_params=_params(("parallel", "parallel")),
        name="cross_attention",
    )(x, gain, w_q, gq, k_mem, v_mem, w_o)


def _router_kernel(x_ref, g_ref, whi_ref, wlo_ref, o_ref):
    hf = _rms_rows(x_ref[...], g_ref[...])
    hi = hf.astype(BF16)
    lo = (hf - hi.astype(F32)).astype(BF16)
    logits = _mm(hi, whi_ref[...]) + (_mm(hi, wlo_ref[...]) + _mm(lo, whi_ref[...]))
    lane = lax.broadcasted_iota(jnp.int32, logits.shape, 1).astype(F32)
    logits = jnp.where(lane < N_EXPERTS, logits, NEG)
    m1 = jnp.max(logits, axis=-1, keepdims=True)
    i1 = jnp.min(jnp.where(logits == m1, lane, float(LANES)), axis=-1, keepdims=True)
    rest = jnp.where(lane == i1, NEG, logits)
    m2 = jnp.max(rest, axis=-1, keepdims=True)
    i2 = jnp.min(jnp.where(rest == m2, lane, float(LANES)), axis=-1, keepdims=True)
    e2 = jnp.exp(m2 - m1)
    den = 1.0 + e2
    o_ref[...] = (jnp.where(lane == 0.0, i1, 0.0) + jnp.where(lane == 1.0, i2, 0.0)
                  + jnp.where(lane == 2.0, 1.0 / den, 0.0) + jnp.where(lane == 3.0, e2 / den, 0.0))


def router_gates(x2d, gain, w_hi, w_lo, *, tm=512):
    n, d = x2d.shape
    return pl.pallas_call(
        _router_kernel,
        grid=(n // tm,),
        in_specs=[pl.BlockSpec((tm, d), lambda i: (i, 0)),
                  pl.BlockSpec((1, d), lambda i: (0, 0)),
                  pl.BlockSpec((d, LANES), lambda i: (0, 0)),
                  pl.BlockSpec((d, LANES), lambda i: (0, 0))],
        out_specs=pl.BlockSpec((tm, LANES), lambda i: (i, 0)),
        out_shape=jax.ShapeDtypeStruct((n, LANES), F32),
        compiler_params=_params(("parallel",)),
        name="router_gates",
    )(x2d, gain, w_hi, w_lo)


def _swiglu_tile(xn, wg, wu, wd):
    gg = _mm(xn, wg)
    uu = _mm(xn, wu)
    act = gg * (1.0 / (1.0 + jnp.exp(-gg))) * uu
    return _mm(act.astype(BF16), wd)


def _ffn_kernel(x_ref, g_ref, wg_ref, wu_ref, wd_ref, o_ref, xn_ref, acc_ref):
    j = pl.program_id(1)

    @pl.when(j == 0)
    def _():
        xn_ref[...] = _rms_rows(x_ref[...], g_ref[...]).astype(BF16)
        acc_ref[...] = jnp.zeros_like(acc_ref)

    acc_ref[...] += _swiglu_tile(xn_ref[...], wg_ref[...], wu_ref[...], wd_ref[...])

    @pl.when(j == pl.num_programs(1) - 1)
    def _():
        o_ref[...] = x_ref[...] + acc_ref[...]


def dense_ffn(x2d, gain, w_gu, w_down, *, tm=1024, tf=256):
    n, d = x2d.shape
    tm = min(tm, n)
    f = w_down.shape[0]
    nf = f // tf
    return pl.pallas_call(
        _ffn_kernel,
        grid=(n // tm, nf),
        in_specs=[pl.BlockSpec((tm, d), lambda i, j: (i, 0)),
                  pl.BlockSpec((1, d), lambda i, j: (0, 0)),
                  pl.BlockSpec((d, tf), lambda i, j: (0, j)),
                  pl.BlockSpec((d, tf), lambda i, j: (0, j + nf)),
                  pl.BlockSpec((tf, d), lambda i, j: (j, 0))],
        out_specs=pl.BlockSpec((tm, d), lambda i, j: (i, 0)),
        out_shape=jax.ShapeDtypeStruct((n, d), F32),
        scratch_shapes=[pltpu.VMEM((tm, d), BF16), pltpu.VMEM((tm, d), F32)],
        compiler_params=_params(("parallel", "arbitrary")),
        name="dense_ffn",
    )(x2d, gain, w_gu, w_gu, w_down)


def _moe_kernel(te_ref, tn_ref, tok_ref, tok_next_ref, dst_prev_ref, x_hbm, g_ref, gate_ref, wg_ref, wu_ref,
                wd_ref, y_hbm, xg_ref, xn_ref, acc_ref, yb_ref, gsem, ssem, *, tm, nf):
    i = pl.program_id(0)
    j = pl.program_id(1)
    nt = pl.num_programs(0)
    slot = i % 2
    per_step = tm // nf
    active = tn_ref[i] > 0
    prev_active = (i > 0) & (tn_ref[jnp.maximum(i - 1, 0)] > 0)
    prev_issued = (i > 0) & (tn_ref[jnp.maximum(i - 2, 0)] > 0)

    def gather_copy(tok, r, s):
        return pltpu.make_async_copy(x_hbm.at[pl.ds(tok, 1), :], xg_ref.at[s, pl.ds(r, 1), :], gsem.at[s])

    def scatter_copy(dst, r, s):
        return pltpu.make_async_copy(yb_ref.at[s, pl.ds(r, 1), :], y_hbm.at[pl.ds(dst, 1), :], ssem.at[s])

    def start_all_rows(ids_ref, s, make_copy):
        def body(r8, c):
            for u in range(8):
                r = r8 * 8 + u
                make_copy(ids_ref[0, 0, r], r, s).start()
            return c
        lax.fori_loop(0, tm // 8, body, 0)

    def wait_gather(s):
        pltpu.make_async_copy(x_hbm.at[pl.ds(0, tm), :], xg_ref.at[s], gsem.at[s]).wait()

    def wait_scatter(s):
        pltpu.make_async_copy(yb_ref.at[s], y_hbm.at[pl.ds(0, tm), :], ssem.at[s]).wait()

    @pl.when(j == 0)
    def _():
        @pl.when(i == 0)
        def _():
            start_all_rows(tok_ref, slot, gather_copy)
            yb_ref[1] = jnp.zeros(yb_ref.shape[1:], yb_ref.dtype)
            n_real = y_hbm.shape[0] - 2 * tm
            for half in range(2):
                spare = pltpu.make_async_copy(yb_ref.at[1], y_hbm.at[pl.ds(n_real + half * tm, tm), :], ssem.at[1])
                spare.start()
                spare.wait()

        @pl.when(active | prev_active)
        def _():
            wait_gather(slot)

        @pl.when(prev_issued)
        def _():
            wait_scatter(slot)

        @pl.when(active)
        def _():
            xn_ref[...] = _rms_rows(xg_ref[slot], g_ref[...]).astype(BF16)
            acc_ref[...] = jnp.zeros_like(acc_ref)

        @pl.when(prev_active & jnp.logical_not(active))
        def _():
            start_all_rows(dst_prev_ref, 1 - slot, scatter_copy)

    @pl.when(active)
    def _():
        acc_ref[...] += _swiglu_tile(xn_ref[...], wg_ref[0], wu_ref[0], wd_ref[0])
        for u in range(per_step):
            r = j * per_step + u
            gather_copy(tok_next_ref[0, 0, r], r, 1 - slot).start()
            scatter_copy(dst_prev_ref[0, 0, r], r, 1 - slot).start()

        @pl.when(j == nf - 1)
        def _():
            yb_ref[slot] = acc_ref[...] * gate_ref[...]

    @pl.when((j == nf - 1) & (i == nt - 1) & (active | prev_active))
    def _():
        wait_scatter(1 - slot)


def moe_experts(x2d, gain, plan, w_gu, w_down, *, tm, tf=512):
    n, d = x2d.shape
    ne, f, _ = w_down.shape
    nf = f // tf
    assert tm % nf == 0 and tm % 16 == 0
    tile_e, tile_n, row_tok, row_dst_prev, row_gate = plan
    nt = tile_e.shape[0]
    smem_rows = lambda imap: pl.BlockSpec((1, 1, tm), imap, memory_space=pltpu.SMEM)
    live = lambda j, tn, i: j * jnp.minimum(tn[i], 1)
    return pl.pallas_call(
        functools.partial(_moe_kernel, tm=tm, nf=nf),
        grid_spec=pltpu.PrefetchScalarGridSpec(
            num_scalar_prefetch=2,
            grid=(nt, nf),
            in_specs=[smem_rows(lambda i, j, te, tn: (i, 0, 0)),
                      smem_rows(lambda i, j, te, tn: (jnp.minimum(i + 1, nt - 1), 0, 0)),
                      smem_rows(lambda i, j, te, tn: (i, 0, 0)),
                      pl.BlockSpec(memory_space=pl.ANY),
                      pl.BlockSpec((1, d), lambda i, j, te, tn: (0, 0)),
                      pl.BlockSpec((tm, 1), lambda i, j, te, tn: (i, 0)),
                      pl.BlockSpec((1, d, tf), lambda i, j, te, tn: (te[i], 0, live(j, tn, i))),
                      pl.BlockSpec((1, d, tf), lambda i, j, te, tn: (te[i], 0, live(j, tn, i) + nf)),
                      pl.BlockSpec((1, tf, d), lambda i, j, te, tn: (te[i], live(j, tn, i), 0))],
            out_specs=pl.BlockSpec(memory_space=pl.ANY),
            scratch_shapes=[pltpu.VMEM((2, tm, d), F32), pltpu.VMEM((tm, d), BF16), pltpu.VMEM((tm, d), F32),
                            pltpu.VMEM((2, tm, d), F32),
                            pltpu.SemaphoreType.DMA((2,)), pltpu.SemaphoreType.DMA((2,))]),
        out_shape=jax.ShapeDtypeStruct((2 * n + 2 * tm, d), F32),
        compiler_params=_params(("arbitrary", "arbitrary")),
        name="moe_experts",
    )(tile_e, tile_n, row_tok, row_tok, row_dst_prev, x2d, gain, row_gate, w_gu, w_gu, w_down)


def moe_plan(route, *, tm):
    n = route.shape[0]
    flat_e = route[:, :2].astype(jnp.int32).reshape(-1)
    flat_g = route[:, 2:4].reshape(-1)
    nt = (2 * n) // tm + N_EXPERTS + 1
    order = jnp.argsort(flat_e, stable=True).astype(jnp.int32)
    counts = jnp.sum(flat_e[:, None] == jnp.arange(N_EXPERTS)[None, :], axis=0).astype(jnp.int32)
    off = jnp.cumsum(counts) - counts
    tiles = (counts + tm - 1) // tm
    tile_off = jnp.cumsum(tiles) - tiles
    tile_id = jnp.arange(nt, dtype=jnp.int32)
    used = tile_id < jnp.sum(tiles)
    tile_e = jnp.clip(jnp.sum(tile_id[:, None] >= tile_off[None, :], axis=1) - 1, 0, N_EXPERTS - 1)
    tile_e = jnp.where(used, tile_e, tile_e[jnp.maximum(jnp.sum(tiles) - 1, 0)]).astype(jnp.int32)
    first_row = (tile_id - tile_off[tile_e]) * tm
    tile_n = jnp.where(used, jnp.clip(counts[tile_e] - first_row, 0, tm), 0).astype(jnp.int32)
    r = jnp.arange(tm, dtype=jnp.int32)[None, :]
    valid = r < tile_n[:, None]
    a = order[jnp.clip(off[tile_e][:, None] + first_row[:, None] + r, 0, 2 * n - 1)]
    row_tok = jnp.where(valid, a // 2, 0).astype(jnp.int32).reshape(nt, 1, tm)
    spare = 2 * n + (tile_id[:, None] % 2) * tm + r
    row_dst = jnp.where(valid, (a % 2) * n + a // 2, spare).astype(jnp.int32)
    row_dst_prev = jnp.concatenate([2 * n + tm + r, row_dst[:-1]], axis=0).reshape(nt, 1, tm)
    row_gate = jnp.where(valid, flat_g[a], 0.0).astype(F32).reshape(nt * tm, 1)
    return tile_e, tile_n, row_tok, row_dst_prev, row_gate


def _combine_kernel(x_ref, y0_ref, y1_ref, o_ref):
    o_ref[...] = x_ref[...] + (y0_ref[...] + y1_ref[...])


def moe_combine(x2d, y, *, tm=512):
    n, d = x2d.shape
    nb = n // tm
    return pl.pallas_call(
        _combine_kernel,
        grid=(nb,),
        in_specs=[pl.BlockSpec((tm, d), lambda i: (i, 0)),
                  pl.BlockSpec((tm, d), lambda i: (i, 0)),
                  pl.BlockSpec((tm, d), lambda i: (i + nb, 0))],
        out_specs=pl.BlockSpec((tm, d), lambda i: (i, 0)),
        out_shape=jax.ShapeDtypeStruct((n, d), F32),
        compiler_params=_params(("parallel",)),
        name="moe_combine",
    )(x2d, y, y)


def _pack_w_in(w, d):
    sizes = (256, 256, 256, 256, 64, 64, 256, 64, 4, 512, 128, 128, N_BRANCH * d)
    qa, ka, va, qb, kb, vb, qi, ki, wi, qc, kc, vc, g = jnp.split(w, np.cumsum(sizes)[:-1].tolist(), axis=-1)
    pad = jnp.zeros((w.shape[0], 60), w.dtype)
    return jnp.concatenate([g, qa * (HEAD_DIM ** -0.5 * LOG2E), ka, va,
                            qb, qi * D_IDX ** -0.5, kb, vb, ki, wi, pad,
                            qc, kc, vc], axis=-1).astype(BF16)


def kernel(x, mem, rel_bias, norm_mix, w_in, b_gate, qn_dsa, kn_dsa, qn_swa, kn_swa, sinks, w_pa, w_pb, w_pc, w_out, norm_x, norm_mem, w_xq, w_xkv, w_xo, qn_x, kn_x, norm_ffn, w_gu_dense, w_down_dense, w_router, w_gu_moe, w_down_moe):
    b, s, d = x.shape
    depth = w_in.shape[0]
    n = b * s
    row = lambda v: v.reshape(1, -1).astype(F32)
    bias_dsa = dsa_bias_tiles(rel_bias[:, :H_DSA] * LOG2E)
    bias_swa = swa_bias_tiles(rel_bias[:, H_DSA:] * LOG2E)
    qscale = HEAD_DIM ** -0.5 * LOG2E
    gcol = (N_BRANCH * d) // 768

    x2 = x.reshape(n, d)
    for l in range(depth):
        proj = in_projection(x2, row(norm_mix[l]), _pack_w_in(w_in[l], d))
        proj3 = proj.reshape(b, s, -1)
        o_a = sb_attention(proj3, gcol)
        o_b = dsa_attention(proj3, gcol + 1, row(jnp.tile(qn_dsa[l] * qscale, H_DSA)),
                            row(jnp.concatenate([kn_dsa[l], jnp.zeros_like(kn_dsa[l])])), bias_dsa)
        o_c = swa_attention(proj3, gcol + 2, sinks[l].astype(F32) * LOG2E,
                            row(jnp.tile(qn_swa[l] * qscale, H_SW)), row(jnp.tile(kn_swa[l], KV_SW)), bias_swa)
        x2 = merge_project(x2, proj, o_a.reshape(n, -1), o_b.reshape(n, -1), o_c.reshape(n, -1),
                           b_gate[l].astype(F32), w_pa[l].astype(BF16), w_pb[l].astype(BF16),
                           w_pc[l].astype(BF16), w_out[l].astype(BF16))
        k_mem, v_mem = memory_kv(mem, row(norm_mem[l]), w_xkv[l].astype(BF16), row(jnp.tile(kn_x[l], H_X)))
        x2 = cross_attention(x2.reshape(b, s, d), row(norm_x[l]), w_xq[l].astype(BF16),
                             row(jnp.tile(qn_x[l] * (XHEAD_DIM ** -0.5 * LOG2E), H_X)), k_mem, v_mem,
                             w_xo[l].astype(BF16)).reshape(n, d)
        if l % 2 == 0:
            x2 = dense_ffn(x2, row(norm_ffn[l]), w_gu_dense[l // 2].astype(BF16),
                           w_down_dense[l // 2].astype(BF16), tf=256)
        else:
            wr = jnp.pad(w_router[l // 2].astype(F32), ((0, 0), (0, LANES - N_EXPERTS)))
            wr_hi = wr.astype(BF16)
            wr_lo = (wr - wr_hi.astype(F32)).astype(BF16)
            route = router_gates(x2, row(norm_ffn[l]), wr_hi, wr_lo)
            tm_moe = 672
            y = moe_experts(x2, row(norm_ffn[l]), moe_plan(route, tm=tm_moe), w_gu_moe[l // 2].astype(BF16),
                            w_down_moe[l // 2].astype(BF16), tm=tm_moe)
            x2 = moe_combine(x2, y)
    return x2.reshape(b, s, d)
```

```python
import functools
import math

import numpy as np
import jax
import jax.numpy as jnp
from jax import lax
from jax.experimental import pallas as pl
from jax.experimental.pallas import tpu as pltpu

HEAD_DIM = 64
H_SB = 4
H_DSA = 4
H_IDX = 4
D_IDX = 64
TOPK_MAX = 256
H_SW = 8
KV_SW = 2
WINDOW = 128
BLOCK = 128
N_BRANCH = 3
N_BUCKETS = 32
MAX_DISTANCE = 128
H_X = 4
XHEAD_DIM = 128
N_EXPERTS = 8
EPS = 1e-6

LANES = 128
VMEM_LIMIT = 56 * 1024 * 1024
NEG = -1e30
INT_MIN = -(2 ** 31)
LOG2E = math.log2(math.e)

F32 = jnp.float32
BF16 = jnp.bfloat16


def _nt(a, b):
    return lax.dot_general(a, b, (((1,), (1,)), ((), ())), preferred_element_type=F32)


def _mm(a, b):
    return jnp.dot(a, b, preferred_element_type=F32)


def _rms_rows(x, g):
    ms = jnp.mean(x * x, axis=-1, keepdims=True)
    return x * lax.rsqrt(ms + EPS) * g


def _params(sem):
    return pltpu.CompilerParams(dimension_semantics=sem, vmem_limit_bytes=VMEM_LIMIT)


def _sb_kernel(a_ref, o_ref, *, tq):
    i = pl.program_id(1)
    q0 = pl.multiple_of(i * tq, tq)
    hd = HEAD_DIM
    row = lax.broadcasted_iota(jnp.int32, (tq, tq), 0)
    col = lax.broadcasted_iota(jnp.int32, (tq, tq), 1)
    strict = col < row
    u_inc = jnp.where(row >= col, 1.0, 0.0).astype(BF16)
    qs = [a_ref[0, pl.ds(q0, tq), h * hd:(h + 1) * hd] for h in range(H_SB)]

    heads = range(H_SB)

    def block(k0s, accs, carries, diag):
        chains = [(b, h) for b in range(len(k0s)) for h in heads]
        ks = {(b, h): a_ref[0, pl.ds(k0s[b], tq), 256 + h * hd:256 + (h + 1) * hd] for b, h in chains}
        vs = {(b, h): a_ref[0, pl.ds(k0s[b], tq), 512 + h * hd:512 + (h + 1) * hd] for b, h in chains}
        zs = {c: _nt(qs[c[1]], ks[c]) for c in chains}
        lks = {c: -(jnp.maximum(zs[c], 0.0) + jnp.log2(1.0 + jnp.exp2(-jnp.abs(zs[c])))) for c in chains}
        if diag:
            lks = {c: jnp.where(strict, lks[c], 0.0) for c in chains}
        rs = {c: _mm(lks[c].astype(BF16), u_inc) for c in chains}
        carry = {h: (None if diag else carries[h]) for h in heads}
        atts = {}
        for b, h in chains:
            if diag:
                atts[(b, h)] = jnp.where(strict, jnp.exp2(zs[(b, h)] + rs[(b, h)]), 0.0)
                carry[h] = rs[(b, h)][:, 0:1]
            else:
                atts[(b, h)] = jnp.exp2(zs[(b, h)] + rs[(b, h)] + carry[h])
                carry[h] = carry[h] + rs[(b, h)][:, 0:1]
        pvs = {c: _mm(atts[c].astype(BF16), vs[c]) for c in chains}
        new_acc = []
        for h in heads:
            tot = pvs[(0, h)] if diag else accs[h] + pvs[(0, h)]
            for b in range(1, len(k0s)):
                tot = tot + pvs[(b, h)]
            new_acc.append(tot)
        return tuple(new_acc), tuple(carry[h] for h in heads)

    state = block([q0], None, None, True)

    def two_blocks(jj, st):
        near = pl.multiple_of((i - 1 - 2 * jj) * tq, tq)
        far = pl.multiple_of((i - 2 - 2 * jj) * tq, tq)
        return block([near, far], st[0], st[1], False)

    state = lax.fori_loop(0, i // 2, two_blocks, state)
    accs, _ = lax.fori_loop(0, i % 2, lambda jj, st: block([0], st[0], st[1], False), state)
    o_ref[0] = jnp.concatenate(accs, axis=-1).astype(o_ref.dtype)


def sb_attention(proj, col_block, *, tq=256):
    b, s, _ = proj.shape
    return pl.pallas_call(
        functools.partial(_sb_kernel, tq=tq),
        grid=(b, s // tq),
        in_specs=[pl.BlockSpec((1, s, 768), lambda bi, i: (bi, 0, col_block))],
        out_specs=pl.BlockSpec((1, tq, H_SB * HEAD_DIM), lambda bi, i: (bi, i, 0)),
        out_shape=jax.ShapeDtypeStruct((b, s, H_SB * HEAD_DIM), BF16),
        compiler_params=_params(("parallel", "arbitrary")),
        name="sb_attention",
    )(proj)


def _inproj_kernel(x_ref, g_ref, w_ref, o_ref, xn_ref):
    @pl.when(pl.program_id(1) == 0)
    def _():
        xn_ref[...] = _rms_rows(x_ref[...], g_ref[...]).astype(BF16)

    o_ref[...] = _mm(xn_ref[...], w_ref[...]).astype(o_ref.dtype)


def in_projection(x2d, gain, w, *, tm=1024, tn=768):
    n, d = x2d.shape
    tm = min(tm, n)
    c = w.shape[1]
    return pl.pallas_call(
        _inproj_kernel,
        grid=(n // tm, c // tn),
        in_specs=[pl.BlockSpec((tm, d), lambda i, j: (i, 0)),
                  pl.BlockSpec((1, d), lambda i, j: (0, 0)),
                  pl.BlockSpec((d, tn), lambda i, j: (0, j))],
        out_specs=pl.BlockSpec((tm, tn), lambda i, j: (i, j)),
        out_shape=jax.ShapeDtypeStruct((n, c), BF16),
        scratch_shapes=[pltpu.VMEM((tm, d), BF16)],
        compiler_params=_params(("parallel", "arbitrary")),
        name="in_projection",
    )(x2d, gain, w)


def _head_sumsq(x, head_dim):
    r_i = lax.broadcasted_iota(jnp.int32, (LANES, LANES), 0) // head_dim
    c_i = lax.broadcasted_iota(jnp.int32, (LANES, LANES), 1) // head_dim
    bd = jnp.where(r_i == c_i, 1.0, 0.0).astype(F32)
    x2 = x * x
    parts = [_mm(x2[:, b * LANES:(b + 1) * LANES], bd) for b in range(x.shape[1] // LANES)]
    return parts[0] if len(parts) == 1 else jnp.concatenate(parts, axis=-1)


def _head_rms(x, g, head_dim):
    ss = _head_sumsq(x, head_dim)
    return x * lax.rsqrt(ss * (1.0 / head_dim) + EPS) * g


def _eye(n, dtype):
    r = lax.broadcasted_iota(jnp.int32, (n, n), 0)
    c = lax.broadcasted_iota(jnp.int32, (n, n), 1)
    return jnp.where(r == c, 1.0, 0.0).astype(dtype)


def _dsa_kernel(a_ref, gq_ref, gk_ref, bias_ref, o_ref, kbn_ref, vt_ref, keys_ref, *, tq, ck, topk):
    s_len = a_ref.shape[1]
    i = pl.program_id(1)
    q0 = pl.multiple_of(i * tq, tq)
    hd = HEAD_DIM
    sub = ck // tq

    @pl.when(i == 0)
    def _():
        kv = a_ref[0, :, 512:640]
        kvf = kv.astype(F32)
        lane = lax.broadcasted_iota(jnp.int32, (1, LANES), 1)
        ss = _head_sumsq(jnp.where(lane < hd, kvf, 0.0), LANES)
        kn = kvf * lax.rsqrt(ss * (1.0 / hd) + EPS) * gk_ref[...]
        kbn_ref[...] = kn[:, :hd].astype(BF16)
        kvt = _nt(_eye(LANES, BF16), kv)
        for cb in range(s_len // ck):
            vt_ref[cb] = kvt[hd:, cb * ck:(cb + 1) * ck].astype(BF16)

    def heads_on_rows(x):
        return jnp.concatenate([x[:, h * hd:(h + 1) * hd] for h in range(x.shape[1] // hd)], axis=0)

    qi_all = heads_on_rows(a_ref[0, pl.ds(q0, tq), 256:512])
    wblk = a_ref[0, pl.ds(q0, tq), 640:768]
    sel_r = lax.broadcasted_iota(jnp.int32, (8, LANES), 0)
    sel_c = lax.broadcasted_iota(jnp.int32, (8, LANES), 1)
    w_t = _nt(jnp.where(sel_c == sel_r + D_IDX, 1.0, 0.0).astype(BF16), wblk)
    qpos = q0 + lax.broadcasted_iota(jnp.int32, (1, tq), 1)
    nck = (q0 + tq + ck - 1) // ck
    row_ck = lax.broadcasted_iota(jnp.int32, (ck, tq), 0)

    def p1(c, carry):
        k0 = pl.multiple_of(c * ck, ck)
        act = jnp.maximum(_nt(a_ref[0, pl.ds(k0, ck), 640:704], qi_all), 0.0)
        sc = w_t[0:1, :] * act[:, 0:tq]
        for h in range(1, H_IDX):
            sc = sc + w_t[h:h + 1, :] * act[:, h * tq:(h + 1) * tq]
        bits = pltpu.bitcast(sc, jnp.int32)
        key = jnp.where(bits < 0, -(bits & 0x7FFFFFFF), bits)
        key = jnp.where(k0 + row_ck <= qpos, key, INT_MIN)
        keys_ref[pl.ds(k0, ck), :] = key
        return carry

    lax.fori_loop(0, nck, p1, 0)

    def count(pred):
        def body(c, acc):
            k0 = pl.multiple_of(c * ck, ck)
            m = jnp.where(pred(keys_ref[pl.ds(k0, ck), :]), 1, 0)
            return acc + jnp.sum(m.reshape(ck // 8, 8, tq), axis=0)
        acc = lax.fori_loop(0, nck, body, jnp.zeros((8, tq), jnp.int32))
        return jnp.sum(acc, axis=0, keepdims=True)

    def search():
        c0 = count(lambda k: k >= 0)
        t = jnp.where(c0 >= topk, 0, INT_MIN).astype(jnp.int32)

        def vstep(b, t):
            cand = t | lax.shift_left(jnp.int32(1), 30 - b)
            return jnp.where(count(lambda k: k >= cand) >= topk, cand, t)

        t = lax.fori_loop(0, 31, vstep, t)
        need = topk - count(lambda k: k > t)
        return t, jnp.where(t == INT_MIN, 0, need).astype(F32)

    def no_search():
        return jnp.full((1, tq), INT_MIN, jnp.int32), jnp.zeros((1, tq), F32)

    thr, need = lax.cond(q0 + tq > topk, search, no_search)
    tri = jnp.where(lax.broadcasted_iota(jnp.int32, (ck, ck), 1) <= lax.broadcasted_iota(jnp.int32, (ck, ck), 0),
                    1.0, 0.0).astype(BF16)

    qb = a_ref[0, pl.ds(q0, tq), 0:256].astype(F32)
    qn_all = heads_on_rows(_head_rms(qb, gq_ref[...], hd).astype(BF16))

    def p3(c, st):
        ms, ls, accs, ties_before = st
        k0 = pl.multiple_of(c * ck, ck)
        s_all = _nt(kbn_ref[pl.ds(k0, ck), :], qn_all)
        vtc = vt_ref[c]
        key = keys_ref[pl.ds(k0, ck), :]
        tie = key == thr
        tie_rank = ties_before + _mm(tri, jnp.where(tie, 1.0, 0.0).astype(BF16))
        sel = (key > thr) | (tie & (tie_rank <= need))
        ties_before = tie_rank[ck - 1:ck, :]
        bidx = [jnp.clip(i - (c * sub + r), 0, 2) for r in range(sub)]
        heads = range(H_DSA)
        s_h = [s_all[:, h * tq:(h + 1) * tq]
               + jnp.concatenate([bias_ref[bidx[r], h] for r in range(sub)], axis=0) for h in heads]
        m_new = [jnp.maximum(ms[h], jnp.max(jnp.where(sel, s_h[h], NEG), axis=0, keepdims=True)) for h in heads]
        p = [jnp.where(sel, jnp.exp2(s_h[h] - m_new[h]), 0.0) for h in heads]
        alpha = [jnp.exp2(ms[h] - m_new[h]) for h in heads]
        pv = [_mm(vtc, p[h].astype(BF16)) for h in heads]
        nl = [ls[h] * alpha[h] + jnp.sum(p[h], axis=0, keepdims=True) for h in heads]
        na = [accs[h] * alpha[h] + pv[h] for h in heads]
        return tuple(m_new), tuple(nl), tuple(na), ties_before

    init = (tuple(jnp.full((1, tq), NEG, F32) for _ in range(H_DSA)),
            tuple(jnp.zeros((1, tq), F32) for _ in range(H_DSA)),
            tuple(jnp.zeros((hd, tq), F32) for _ in range(H_DSA)),
            jnp.zeros((1, tq), F32))
    _, ls, accs, _ = lax.fori_loop(0, nck, p3, init)
    o_t = jnp.concatenate([accs[h] / ls[h] for h in range(H_DSA)], axis=0)
    o_ref[0] = _nt(_eye(tq, BF16), o_t.astype(BF16)).astype(o_ref.dtype)


def dsa_attention(proj, col_block, gq, gk, bias_t, *, tq=BLOCK, ck=512):
    b, s, _ = proj.shape
    topk = min(TOPK_MAX, s // 4)
    ck = min(ck, s)
    return pl.pallas_call(
        functools.partial(_dsa_kernel, tq=tq, ck=ck, topk=topk),
        grid=(b, s // tq),
        in_specs=[pl.BlockSpec((1, s, 768), lambda bi, i: (bi, 0, col_block)),
                  pl.BlockSpec((1, 256), lambda bi, i: (0, 0)),
                  pl.BlockSpec((1, LANES), lambda bi, i: (0, 0)),
                  pl.BlockSpec((3, H_DSA, tq, tq), lambda bi, i: (0, 0, 0, 0))],
        out_specs=pl.BlockSpec((1, tq, H_DSA * HEAD_DIM), lambda bi, i: (bi, i, 0)),
        out_shape=jax.ShapeDtypeStruct((b, s, H_DSA * HEAD_DIM), BF16),
        scratch_shapes=[pltpu.VMEM((s, HEAD_DIM), BF16),
                        pltpu.VMEM((s // ck, HEAD_DIM, ck), BF16),
                        pltpu.VMEM((s, tq), jnp.int32)],
        compiler_params=_params(("parallel", "arbitrary")),
        name="dsa_attention",
    )(proj, gq, gk, bias_t)


def _t5_bucket(rel):
    n = jnp.maximum(rel, 0)
    max_exact = N_BUCKETS // 2
    nf = jnp.maximum(n, 1).astype(F32)
    large = max_exact + (jnp.log(nf / max_exact) / math.log(MAX_DISTANCE / max_exact)
                         * (N_BUCKETS - max_exact)).astype(jnp.int32)
    large = jnp.minimum(large, N_BUCKETS - 1)
    return jnp.where(n < max_exact, n, large)


def _bucket_lookup(tab, rel):
    hit = _t5_bucket(rel)[..., None, None] == jnp.arange(N_BUCKETS)[:, None]
    return jnp.sum(jnp.where(hit, tab.astype(F32), 0.0), axis=-2)


def dsa_bias_tiles(tab, tq=BLOCK):
    ks = jnp.arange(tq)[:, None]
    tl = jnp.arange(tq)[None, :]
    rel = jnp.stack([tl - ks, tq + tl - ks, jnp.full((tq, tq), 2 * tq + MAX_DISTANCE)])
    return _bucket_lookup(tab, rel).transpose(0, 3, 1, 2)


def _swa_kernel(sink_ref, cur_ref, prev_ref, gq_ref, gk_ref, bias_ref, o_ref, *, tq):
    i = pl.program_id(1)
    hd = HEAD_DIM
    g = H_SW // KV_SW
    qn = _head_rms(cur_ref[0, :, 0:512].astype(F32), gq_ref[...], hd).astype(BF16)
    k_cur = _head_rms(cur_ref[0, :, 512:640].astype(F32), gk_ref[...], hd).astype(BF16)
    k_prev = _head_rms(prev_ref[0, :, 512:640].astype(F32), gk_ref[...], hd).astype(BF16)
    k2 = jnp.concatenate([k_prev, k_cur], axis=0)
    v2 = jnp.concatenate([prev_ref[0, :, 640:768], cur_ref[0, :, 640:768]], axis=0)
    col = lax.broadcasted_iota(jnp.int32, (tq, 2 * tq), 1)
    has_prev = (col >= tq) | (i > 0)
    heads = range(H_SW)
    s = [jnp.where(has_prev, _nt(qn[:, h * hd:(h + 1) * hd], k2[:, (h // g) * hd:(h // g + 1) * hd]) + bias_ref[h], NEG)
         for h in heads]
    m = [jnp.maximum(jnp.max(s[h], axis=-1, keepdims=True), sink_ref[h]) for h in heads]
    e = [jnp.exp2(s[h] - m[h]) for h in heads]
    den = [jnp.sum(e[h], axis=-1, keepdims=True) + jnp.exp2(sink_ref[h] - m[h]) for h in heads]
    outs = [_mm(e[h].astype(BF16), v2[:, (h // g) * hd:(h // g + 1) * hd]) / den[h] for h in heads]
    o_ref[0] = jnp.concatenate(outs, axis=-1).astype(o_ref.dtype)


def swa_bias_tiles(tab, tq=BLOCK):
    rel = (jnp.arange(tq)[:, None] + tq) - jnp.arange(2 * tq)[None, :]
    in_win = (rel >= 0) & (rel < WINDOW)
    bias = _bucket_lookup(tab, rel).transpose(2, 0, 1)
    return jnp.where(in_win[None], bias, NEG)


def swa_attention(proj, col_block, sinks, gq, gk, bias, *, tq=BLOCK):
    b, s, _ = proj.shape
    return pl.pallas_call(
        functools.partial(_swa_kernel, tq=tq),
        grid_spec=pltpu.PrefetchScalarGridSpec(
            num_scalar_prefetch=0,
            grid=(b, s // tq),
            in_specs=[pl.BlockSpec(memory_space=pltpu.SMEM),
                      pl.BlockSpec((1, tq, 768), lambda bi, i: (bi, i, col_block)),
                      pl.BlockSpec((1, tq, 768), lambda bi, i: (bi, jnp.maximum(i - 1, 0), col_block)),
                      pl.BlockSpec((1, 512), lambda bi, i: (0, 0)),
                      pl.BlockSpec((1, LANES), lambda bi, i: (0, 0)),
                      pl.BlockSpec((H_SW, tq, 2 * tq), lambda bi, i: (0, 0, 0))],
            out_specs=pl.BlockSpec((1, tq, H_SW * HEAD_DIM), lambda bi, i: (bi, i, 0))),
        out_shape=jax.ShapeDtypeStruct((b, s, H_SW * HEAD_DIM), BF16),
        compiler_params=_params(("parallel", "arbitrary")),
        name="swa_attention",
    )(sinks, proj, proj, gq, gk, bias)


def _merge_kernel(x_ref, g_ref, oa_ref, ob_ref, oc_ref, bg_ref, wa_ref, wb_ref, wc_ref, wo_ref, o_ref):
    d = x_ref.shape[1]
    merged = None
    for k, (o_k, w_k) in enumerate(((oa_ref, wa_ref), (ob_ref, wb_ref), (oc_ref, wc_ref))):
        logit = g_ref[:, k * d:(k + 1) * d].astype(F32) + bg_ref[k:k + 1, :]
        gate = 1.0 / (1.0 + jnp.exp(-logit))
        term = gate * _mm(o_k[...], w_k[...])
        merged = term if merged is None else merged + term
    o_ref[...] = x_ref[...] + _mm(merged.astype(BF16), wo_ref[...])


def merge_project(x2d, proj2d, o_a, o_b, o_c, b_gate, w_pa, w_pb, w_pc, w_out, *, tm=512):
    n, d = x2d.shape
    full = lambda a: pl.BlockSpec(a.shape, lambda i: (0,) * a.ndim)
    row = lambda a: pl.BlockSpec((tm, a.shape[1]), lambda i: (i, 0))
    return pl.pallas_call(
        _merge_kernel,
        grid=(n // tm,),
        in_specs=[row(x2d), pl.BlockSpec((tm, N_BRANCH * d), lambda i: (i, 0)),
                  row(o_a), row(o_b), row(o_c),
                  full(b_gate), full(w_pa), full(w_pb), full(w_pc), full(w_out)],
        out_specs=pl.BlockSpec((tm, d), lambda i: (i, 0)),
        out_shape=jax.ShapeDtypeStruct((n, d), F32),
        compiler_params=_params(("parallel",)),
        name="merge_project",
    )(x2d, proj2d, o_a, o_b, o_c, b_gate, w_pa, w_pb, w_pc, w_out)


def _memkv_kernel(m_ref, g_ref, w_ref, gk_ref, k_ref, v_ref):
    w_x = k_ref.shape[2]
    mn = _rms_rows(m_ref[0], g_ref[...]).astype(BF16)
    kv = _mm(mn, w_ref[...])
    k_ref[0] = _head_rms(kv[:, :w_x], gk_ref[...], XHEAD_DIM).astype(k_ref.dtype)
    v_ref[0] = kv[:, w_x:].astype(v_ref.dtype)


def memory_kv(mem, gain, w_kv, gk):
    b, m, d = mem.shape
    w_x = w_kv.shape[1] // 2
    return pl.pallas_call(
        _memkv_kernel,
        grid=(b,),
        in_specs=[pl.BlockSpec((1, m, d), lambda i: (i, 0, 0)),
                  pl.BlockSpec((1, d), lambda i: (0, 0)),
                  pl.BlockSpec(w_kv.shape, lambda i: (0, 0)),
                  pl.BlockSpec((1, w_x), lambda i: (0, 0))],
        out_specs=[pl.BlockSpec((1, m, w_x), lambda i: (i, 0, 0)),
                   pl.BlockSpec((1, m, w_x), lambda i: (i, 0, 0))],
        out_shape=[jax.ShapeDtypeStruct((b, m, w_x), BF16)] * 2,
        compiler_params=_params(("parallel",)),
        name="memory_kv",
    )(mem, gain, w_kv, gk)


def _xattn_kernel(x_ref, g_ref, wq_ref, gq_ref, k_ref, v_ref, wo_ref, o_ref):
    x = x_ref[0]
    xn = _rms_rows(x, g_ref[...]).astype(BF16)
    q = _head_rms(_mm(xn, wq_ref[...]), gq_ref[...], XHEAD_DIM).astype(BF16)
    sl = [slice(h * XHEAD_DIM, (h + 1) * XHEAD_DIM) for h in range(H_X)]
    s = [_nt(q[:, c], k_ref[0, :, c]) for c in sl]
    e = [jnp.exp2(s_h - jnp.max(s_h, axis=-1, keepdims=True)) for s_h in s]
    den = [jnp.sum(e_h, axis=-1, keepdims=True) for e_h in e]
    outs = [_mm(e[h].astype(BF16), v_ref[0, :, sl[h]]) / den[h] for h in range(H_X)]
    o = jnp.concatenate(outs, axis=-1).astype(BF16)
    o_ref[0] = x + _mm(o, wo_ref[...])


def cross_attention(x, gain, w_q, gq, k_mem, v_mem, w_o, *, tq=512):
    b, s, d = x.shape
    m, w_x = k_mem.shape[1:]
    full = lambda a: pl.BlockSpec(a.shape, lambda bi, i: (0,) * a.ndim)
    return pl.pallas_call(
        _xattn_kernel,
        grid=(b, s // tq),
        in_specs=[pl.BlockSpec((1, tq, d), lambda bi, i: (bi, i, 0)),
                  full(gain), full(w_q), full(gq),
                  pl.BlockSpec((1, m, w_x), lambda bi, i: (bi, 0, 0)),
                  pl.BlockSpec((1, m, w_x), lambda bi, i: (bi, 0, 0)),
                  full(w_o)],
        out_specs=pl.BlockSpec((1, tq, d), lambda bi, i: (bi, i, 0)),
        out_shape=jax.ShapeDtypeStruct((b, s, d), F32),
        compiler_params=_params(("parallel", "parallel")),
        name="cross_attention",
    )(x, gain, w_q, gq, k_mem, v_mem, w_o)


def _router_kernel(x_ref, g_ref, whi_ref, wlo_ref, o_ref):
    hf = _rms_rows(x_ref[...], g_ref[...])
    hi = hf.astype(BF16)
    lo = (hf - hi.astype(F32)).astype(BF16)
    logits = _mm(hi, whi_ref[...]) + (_mm(hi, wlo_ref[...]) + _mm(lo, whi_ref[...]))
    lane = lax.broadcasted_iota(jnp.int32, logits.shape, 1).astype(F32)
    logits = jnp.where(lane < N_EXPERTS, logits, NEG)
    m1 = jnp.max(logits, axis=-1, keepdims=True)
    i1 = jnp.min(jnp.where(logits == m1, lane, float(LANES)), axis=-1, keepdims=True)
    rest = jnp.where(lane == i1, NEG, logits)
    m2 = jnp.max(rest, axis=-1, keepdims=True)
    i2 = jnp.min(jnp.where(rest == m2, lane, float(LANES)), axis=-1, keepdims=True)
    e2 = jnp.exp(m2 - m1)
    den = 1.0 + e2
    o_ref[...] = (jnp.where(lane == 0.0, i1, 0.0) + jnp.where(lane == 1.0, i2, 0.0)
                  + jnp.where(lane == 2.0, 1.0 / den, 0.0) + jnp.where(lane == 3.0, e2 / den, 0.0))


def router_gates(x2d, gain, w_hi, w_lo, *, tm=512):
    n, d = x2d.shape
    return pl.pallas_call(
        _router_kernel,
        grid=(n // tm,),
        in_specs=[pl.BlockSpec((tm, d), lambda i: (i, 0)),
                  pl.BlockSpec((1, d), lambda i: (0, 0)),
                  pl.BlockSpec((d, LANES), lambda i: (0, 0)),
                  pl.BlockSpec((d, LANES), lambda i: (0, 0))],
        out_specs=pl.BlockSpec((tm, LANES), lambda i: (i, 0)),
        out_shape=jax.ShapeDtypeStruct((n, LANES), F32),
        compiler_params=_params(("parallel",)),
        name="router_gates",
    )(x2d, gain, w_hi, w_lo)


def _swiglu_tile(xn, wg, wu, wd):
    gg = _mm(xn, wg)
    uu = _mm(xn, wu)
    act = gg * (1.0 / (1.0 + jnp.exp(-gg))) * uu
    return _mm(act.astype(BF16), wd)


def _ffn_kernel(x_ref, g_ref, wg_ref, wu_ref, wd_ref, o_ref, xn_ref, acc_ref):
    j = pl.program_id(1)

    @pl.when(j == 0)
    def _():
        xn_ref[...] = _rms_rows(x_ref[...], g_ref[...]).astype(BF16)
        acc_ref[...] = jnp.zeros_like(acc_ref)

    acc_ref[...] += _swiglu_tile(xn_ref[...], wg_ref[...], wu_ref[...], wd_ref[...])

    @pl.when(j == pl.num_programs(1) - 1)
    def _():
        o_ref[...] = x_ref[...] + acc_ref[...]


def dense_ffn(x2d, gain, w_gu, w_down, *, tm=1024, tf=256):
    n, d = x2d.shape
    tm = min(tm, n)
    f = w_down.shape[0]
    nf = f // tf
    return pl.pallas_call(
        _ffn_kernel,
        grid=(n // tm, nf),
        in_specs=[pl.BlockSpec((tm, d), lambda i, j: (i, 0)),
                  pl.BlockSpec((1, d), lambda i, j: (0, 0)),
                  pl.BlockSpec((d, tf), lambda i, j: (0, j)),
                  pl.BlockSpec((d, tf), lambda i, j: (0, j + nf)),
                  pl.BlockSpec((tf, d), lambda i, j: (j, 0))],
        out_specs=pl.BlockSpec((tm, d), lambda i, j: (i, 0)),
        out_shape=jax.ShapeDtypeStruct((n, d), F32),
        scratch_shapes=[pltpu.VMEM((tm, d), BF16), pltpu.VMEM((tm, d), F32)],
        compiler_params=_params(("parallel", "arbitrary")),
        name="dense_ffn",
    )(x2d, gain, w_gu, w_gu, w_down)


def _moe_kernel(te_ref, tn_ref, tok_ref, tok_next_ref, dst_prev_ref, x_hbm, g_ref, gate_ref, wg_ref, wu_ref,
                wd_ref, y_hbm, xg_ref, xn_ref, acc_ref, yb_ref, gsem, ssem, *, tm, nf):
    i = pl.program_id(0)
    j = pl.program_id(1)
    nt = pl.num_programs(0)
    slot = i % 2
    per_step = tm // nf
    active = tn_ref[i] > 0
    prev_active = (i > 0) & (tn_ref[jnp.maximum(i - 1, 0)] > 0)
    prev_issued = (i > 0) & (tn_ref[jnp.maximum(i - 2, 0)] > 0)

    def gather_copy(tok, r, s):
        return pltpu.make_async_copy(x_hbm.at[pl.ds(tok, 1), :], xg_ref.at[s, pl.ds(r, 1), :], gsem.at[s])

    def scatter_copy(dst, r, s):
        return pltpu.make_async_copy(yb_ref.at[s, pl.ds(r, 1), :], y_hbm.at[pl.ds(dst, 1), :], ssem.at[s])

    def start_all_rows(ids_ref, s, make_copy):
        def body(r8, c):
            for u in range(8):
                r = r8 * 8 + u
                make_copy(ids_ref[0, 0, r], r, s).start()
            return c
        lax.fori_loop(0, tm // 8, body, 0)

    def wait_gather(s):
        pltpu.make_async_copy(x_hbm.at[pl.ds(0, tm), :], xg_ref.at[s], gsem.at[s]).wait()

    def wait_scatter(s):
        pltpu.make_async_copy(yb_ref.at[s], y_hbm.at[pl.ds(0, tm), :], ssem.at[s]).wait()

    @pl.when(j == 0)
    def _():
        @pl.when(i == 0)
        def _():
            start_all_rows(tok_ref, slot, gather_copy)
            yb_ref[1] = jnp.zeros(yb_ref.shape[1:], yb_ref.dtype)
            n_real = y_hbm.shape[0] - 2 * tm
            for half in range(2):
                spare = pltpu.make_async_copy(yb_ref.at[1], y_hbm.at[pl.ds(n_real + half * tm, tm), :], ssem.at[1])
                spare.start()
                spare.wait()

        @pl.when(active | prev_active)
        def _():
            wait_gather(slot)

        @pl.when(prev_issued)
        def _():
            wait_scatter(slot)

        @pl.when(active)
        def _():
            xn_ref[...] = _rms_rows(xg_ref[slot], g_ref[...]).astype(BF16)
            acc_ref[...] = jnp.zeros_like(acc_ref)

        @pl.when(prev_active & jnp.logical_not(active))
        def _():
            start_all_rows(dst_prev_ref, 1 - slot, scatter_copy)

    @pl.when(active)
    def _():
        acc_ref[...] += _swiglu_tile(xn_ref[...], wg_ref[0], wu_ref[0], wd_ref[0])
        for u in range(per_step):
            r = j * per_step + u
            gather_copy(tok_next_ref[0, 0, r], r, 1 - slot).start()
            scatter_copy(dst_prev_ref[0, 0, r], r, 1 - slot).start()

        @pl.when(j == nf - 1)
        def _():
            yb_ref[slot] = acc_ref[...] * gate_ref[...]

    @pl.when((j == nf - 1) & (i == nt - 1) & (active | prev_active))
    def _():
        wait_scatter(1 - slot)


def moe_experts(x2d, gain, plan, w_gu, w_down, *, tm, tf=512):
    n, d = x2d.shape
    ne, f, _ = w_down.shape
    nf = f // tf
    assert tm % nf == 0 and tm % 16 == 0
    tile_e, tile_n, row_tok, row_dst_prev, row_gate = plan
    nt = tile_e.shape[0]
    smem_rows = lambda imap: pl.BlockSpec((1, 1, tm), imap, memory_space=pltpu.SMEM)
    live = lambda j, tn, i: j * jnp.minimum(tn[i], 1)
    return pl.pallas_call(
        functools.partial(_moe_kernel, tm=tm, nf=nf),
        grid_spec=pltpu.PrefetchScalarGridSpec(
            num_scalar_prefetch=2,
            grid=(nt, nf),
            in_specs=[smem_rows(lambda i, j, te, tn: (i, 0, 0)),
                      smem_rows(lambda i, j, te, tn: (jnp.minimum(i + 1, nt - 1), 0, 0)),
                      smem_rows(lambda i, j, te, tn: (i, 0, 0)),
                      pl.BlockSpec(memory_space=pl.ANY),
                      pl.BlockSpec((1, d), lambda i, j, te, tn: (0, 0)),
                      pl.BlockSpec((tm, 1), lambda i, j, te, tn: (i, 0)),
                      pl.BlockSpec((1, d, tf), lambda i, j, te, tn: (te[i], 0, live(j, tn, i))),
                      pl.BlockSpec((1, d, tf), lambda i, j, te, tn: (te[i], 0, live(j, tn, i) + nf)),
                      pl.BlockSpec((1, tf, d), lambda i, j, te, tn: (te[i], live(j, tn, i), 0))],
            out_specs=pl.BlockSpec(memory_space=pl.ANY),
            scratch_shapes=[pltpu.VMEM((2, tm, d), F32), pltpu.VMEM((tm, d), BF16), pltpu.VMEM((tm, d), F32),
                            pltpu.VMEM((2, tm, d), F32),
                            pltpu.SemaphoreType.DMA((2,)), pltpu.SemaphoreType.DMA((2,))]),
        out_shape=jax.ShapeDtypeStruct((2 * n + 2 * tm, d), F32),
        compiler_params=_params(("arbitrary", "arbitrary")),
        name="moe_experts",
    )(tile_e, tile_n, row_tok, row_tok, row_dst_prev, x2d, gain, row_gate, w_gu, w_gu, w_down)


def moe_plan(route, *, tm):
    n = route.shape[0]
    flat_e = route[:, :2].astype(jnp.int32).reshape(-1)
    flat_g = route[:, 2:4].reshape(-1)
    nt = (2 * n) // tm + N_EXPERTS + 1
    order = jnp.argsort(flat_e, stable=True).astype(jnp.int32)
    counts = jnp.sum(flat_e[:, None] == jnp.arange(N_EXPERTS)[None, :], axis=0).astype(jnp.int32)
    off = jnp.cumsum(counts) - counts
    tiles = (counts + tm - 1) // tm
    tile_off = jnp.cumsum(tiles) - tiles
    tile_id = jnp.arange(nt, dtype=jnp.int32)
    used = tile_id < jnp.sum(tiles)
    tile_e = jnp.clip(jnp.sum(tile_id[:, None] >= tile_off[None, :], axis=1) - 1, 0, N_EXPERTS - 1)
    tile_e = jnp.where(used, tile_e, tile_e[jnp.maximum(jnp.sum(tiles) - 1, 0)]).astype(jnp.int32)
    first_row = (tile_id - tile_off[tile_e]) * tm
    tile_n = jnp.where(used, jnp.clip(counts[tile_e] - first_row, 0, tm), 0).astype(jnp.int32)
    r = jnp.arange(tm, dtype=jnp.int32)[None, :]
    valid = r < tile_n[:, None]
    a = order[jnp.clip(off[tile_e][:, None] + first_row[:, None] + r, 0, 2 * n - 1)]
    row_tok = jnp.where(valid, a // 2, 0).astype(jnp.int32).reshape(nt, 1, tm)
    spare = 2 * n + (tile_id[:, None] % 2) * tm + r
    row_dst = jnp.where(valid, (a % 2) * n + a // 2, spare).astype(jnp.int32)
    row_dst_prev = jnp.concatenate([2 * n + tm + r, row_dst[:-1]], axis=0).reshape(nt, 1, tm)
    row_gate = jnp.where(valid, flat_g[a], 0.0).astype(F32).reshape(nt * tm, 1)
    return tile_e, tile_n, row_tok, row_dst_prev, row_gate


def _combine_kernel(x_ref, y0_ref, y1_ref, o_ref):
    o_ref[...] = x_ref[...] + (y0_ref[...] + y1_ref[...])


def moe_combine(x2d, y, *, tm=512):
    n, d = x2d.shape
    nb = n // tm
    return pl.pallas_call(
        _combine_kernel,
        grid=(nb,),
        in_specs=[pl.BlockSpec((tm, d), lambda i: (i, 0)),
                  pl.BlockSpec((tm, d), lambda i: (i, 0)),
                  pl.BlockSpec((tm, d), lambda i: (i + nb, 0))],
        out_specs=pl.BlockSpec((tm, d), lambda i: (i, 0)),
        out_shape=jax.ShapeDtypeStruct((n, d), F32),
        compiler_params=_params(("parallel",)),
        name="moe_combine",
    )(x2d, y, y)


def _pack_w_in(w, d):
    sizes = (256, 256, 256, 256, 64, 64, 256, 64, 4, 512, 128, 128, N_BRANCH * d)
    qa, ka, va, qb, kb, vb, qi, ki, wi, qc, kc, vc, g = jnp.split(w, np.cumsum(sizes)[:-1].tolist(), axis=-1)
    pad = jnp.zeros((w.shape[0], 60), w.dtype)
    return jnp.concatenate([g, qa * (HEAD_DIM ** -0.5 * LOG2E), ka, va,
                            qb, qi * D_IDX ** -0.5, kb, vb, ki, wi, pad,
                            qc, kc, vc], axis=-1).astype(BF16)


def kernel(x, mem, rel_bias, norm_mix, w_in, b_gate, qn_dsa, kn_dsa, qn_swa, kn_swa, sinks, w_pa, w_pb, w_pc, w_out, norm_x, norm_mem, w_xq, w_xkv, w_xo, qn_x, kn_x, norm_ffn, w_gu_dense, w_down_dense, w_router, w_gu_moe, w_down_moe):
    b, s, d = x.shape
    depth = w_in.shape[0]
    n = b * s
    row = lambda v: v.reshape(1, -1).astype(F32)
    bias_dsa = dsa_bias_tiles(rel_bias[:, :H_DSA] * LOG2E)
    bias_swa = swa_bias_tiles(rel_bias[:, H_DSA:] * LOG2E)
    qscale = HEAD_DIM ** -0.5 * LOG2E
    gcol = (N_BRANCH * d) // 768

    x2 = x.reshape(n, d)
    for l in range(depth):
        proj = in_projection(x2, row(norm_mix[l]), _pack_w_in(w_in[l], d))
        proj3 = proj.reshape(b, s, -1)
        o_a = sb_attention(proj3, gcol)
        o_b = dsa_attention(proj3, gcol + 1, row(jnp.tile(qn_dsa[l] * qscale, H_DSA)),
                            row(jnp.concatenate([kn_dsa[l], jnp.zeros_like(kn_dsa[l])])), bias_dsa)
        o_c = swa_attention(proj3, gcol + 2, sinks[l].astype(F32) * LOG2E,
                            row(jnp.tile(qn_swa[l] * qscale, H_SW)), row(jnp.tile(kn_swa[l], KV_SW)), bias_swa)
        x2 = merge_project(x2, proj, o_a.reshape(n, -1), o_b.reshape(n, -1), o_c.reshape(n, -1),
                           b_gate[l].astype(F32), w_pa[l].astype(BF16), w_pb[l].astype(BF16),
                           w_pc[l].astype(BF16), w_out[l].astype(BF16))
        k_mem, v_mem = memory_kv(mem, row(norm_mem[l]), w_xkv[l].astype(BF16), row(jnp.tile(kn_x[l], H_X)))
        x2 = cross_attention(x2.reshape(b, s, d), row(norm_x[l]), w_xq[l].astype(BF16),
                             row(jnp.tile(qn_x[l] * (XHEAD_DIM ** -0.5 * LOG2E), H_X)), k_mem, v_mem,
                             w_xo[l].astype(BF16)).reshape(n, d)
        if l % 2 == 0:
            x2 = dense_ffn(x2, row(norm_ffn[l]), w_gu_dense[l // 2].astype(BF16),
                           w_down_dense[l // 2].astype(BF16), tf=256)
        else:
            wr = jnp.pad(w_router[l // 2].astype(F32), ((0, 0), (0, LANES - N_EXPERTS)))
            wr_hi = wr.astype(BF16)
            wr_lo = (wr - wr_hi.astype(F32)).astype(BF16)
            route = router_gates(x2, row(norm_ffn[l]), wr_hi, wr_lo)
            tm_moe = 672
            y = moe_experts(x2, row(norm_ffn[l]), moe_plan(route, tm=tm_moe), w_gu_moe[l // 2].astype(BF16),
                            w_down_moe[l // 2].astype(BF16), tm=tm_moe)
            x2 = moe_combine(x2, y)
    return x2.reshape(b, s, d)
```

```python
import functools
import math

import numpy as np
import jax
import jax.numpy as jnp
from jax import lax
from jax.experimental import pallas as pl
from jax.experimental.pallas import tpu as pltpu

HEAD_DIM = 64
H_SB = 4
H_DSA = 4
H_IDX = 4
D_IDX = 64
TOPK_MAX = 256
H_SW = 8
KV_SW = 2
WINDOW = 128
BLOCK = 128
N_BRANCH = 3
N_BUCKETS = 32
MAX_DISTANCE = 128
H_X = 4
XHEAD_DIM = 128
N_EXPERTS = 8
EPS = 1e-6

LANES = 128
VMEM_LIMIT = 56 * 1024 * 1024
NEG = -1e30
INT_MIN = -(2 ** 31)
LOG2E = math.log2(math.e)

F32 = jnp.float32
BF16 = jnp.bfloat16


def _nt(a, b):
    return lax.dot_general(a, b, (((1,), (1,)), ((), ())), preferred_element_type=F32)


def _mm(a, b):
    return jnp.dot(a, b, preferred_element_type=F32)


def _rms_rows(x, g):
    ms = jnp.mean(x * x, axis=-1, keepdims=True)
    return x * lax.rsqrt(ms + EPS) * g


def _params(sem):
    return pltpu.CompilerParams(dimension_semantics=sem, vmem_limit_bytes=VMEM_LIMIT)


def _sb_kernel(a_ref, o_ref, *, tq):
    i = pl.program_id(1)
    q0 = pl.multiple_of(i * tq, tq)
    hd = HEAD_DIM
    row = lax.broadcasted_iota(jnp.int32, (tq, tq), 0)
    col = lax.broadcasted_iota(jnp.int32, (tq, tq), 1)
    strict = col < row
    u_inc = jnp.where(row >= col, 1.0, 0.0).astype(BF16)
    qs = [a_ref[0, pl.ds(q0, tq), h * hd:(h + 1) * hd] for h in range(H_SB)]

    heads = range(H_SB)

    def block(k0s, accs, carries, diag):
        chains = [(b, h) for b in range(len(k0s)) for h in heads]
        ks = {(b, h): a_ref[0, pl.ds(k0s[b], tq), 256 + h * hd:256 + (h + 1) * hd] for b, h in chains}
        vs = {(b, h): a_ref[0, pl.ds(k0s[b], tq), 512 + h * hd:512 + (h + 1) * hd] for b, h in chains}
        zs = {c: _nt(qs[c[1]], ks[c]) for c in chains}
        lks = {c: -(jnp.maximum(zs[c], 0.0) + jnp.log2(1.0 + jnp.exp2(-jnp.abs(zs[c])))) for c in chains}
        if diag:
            lks = {c: jnp.where(strict, lks[c], 0.0) for c in chains}
        rs = {c: _mm(lks[c].astype(BF16), u_inc) for c in chains}
        carry = {h: (None if diag else carries[h]) for h in heads}
        atts = {}
        for b, h in chains:
            if diag:
                atts[(b, h)] = jnp.where(strict, jnp.exp2(zs[(b, h)] + rs[(b, h)]), 0.0)
                carry[h] = rs[(b, h)][:, 0:1]
            else:
                atts[(b, h)] = jnp.exp2(zs[(b, h)] + rs[(b, h)] + carry[h])
                carry[h] = carry[h] + rs[(b, h)][:, 0:1]
        pvs = {c: _mm(atts[c].astype(BF16), vs[c]) for c in chains}
        new_acc = []
        for h in heads:
            tot = pvs[(0, h)] if diag else accs[h] + pvs[(0, h)]
            for b in range(1, len(k0s)):
                tot = tot + pvs[(b, h)]
            new_acc.append(tot)
        return tuple(new_acc), tuple(carry[h] for h in heads)

    state = block([q0], None, None, True)

    def two_blocks(jj, st):
        near = pl.multiple_of((i - 1 - 2 * jj) * tq, tq)
        far = pl.multiple_of((i - 2 - 2 * jj) * tq, tq)
        return block([near, far], st[0], st[1], False)

    state = lax.fori_loop(0, i // 2, two_blocks, state)
    accs, _ = lax.fori_loop(0, i % 2, lambda jj, st: block([0], st[0], st[1], False), state)
    o_ref[0] = jnp.concatenate(accs, axis=-1).astype(o_ref.dtype)


def sb_attention(proj, col_block, *, tq=256):
    b, s, _ = proj.shape
    return pl.pallas_call(
        functools.partial(_sb_kernel, tq=tq),
        grid=(b, s // tq),
        in_specs=[pl.BlockSpec((1, s, 768), lambda bi, i: (bi, 0, col_block))],
        out_specs=pl.BlockSpec((1, tq, H_SB * HEAD_DIM), lambda bi, i: (bi, i, 0)),
        out_shape=jax.ShapeDtypeStruct((b, s, H_SB * HEAD_DIM), BF16),
        compiler_params=_params(("parallel", "arbitrary")),
        name="sb_attention",
    )(proj)


def _inproj_kernel(x_ref, g_ref, w_ref, o_ref, xn_ref):
    @pl.when(pl.program_id(1) == 0)
    def _():
        xn_ref[...] = _rms_rows(x_ref[...], g_ref[...]).astype(BF16)

    o_ref[...] = _mm(xn_ref[...], w_ref[...]).astype(o_ref.dtype)


def in_projection(x2d, gain, w, *, tm=1024, tn=768):
    n, d = x2d.shape
    tm = min(tm, n)
    c = w.shape[1]
    return pl.pallas_call(
        _inproj_kernel,
        grid=(n // tm, c // tn),
        in_specs=[pl.BlockSpec((tm, d), lambda i, j: (i, 0)),
                  pl.BlockSpec((1, d), lambda i, j: (0, 0)),
                  pl.BlockSpec((d, tn), lambda i, j: (0, j))],
        out_specs=pl.BlockSpec((tm, tn), lambda i, j: (i, j)),
        out_shape=jax.ShapeDtypeStruct((n, c), BF16),
        scratch_shapes=[pltpu.VMEM((tm, d), BF16)],
        compiler_params=_params(("parallel", "arbitrary")),
        name="in_projection",
    )(x2d, gain, w)


def _head_sumsq(x, head_dim):
    r_i = lax.broadcasted_iota(jnp.int32, (LANES, LANES), 0) // head_dim
    c_i = lax.broadcasted_iota(jnp.int32, (LANES, LANES), 1) // head_dim
    bd = jnp.where(r_i == c_i, 1.0, 0.0).astype(F32)
    x2 = x * x
    parts = [_mm(x2[:, b * LANES:(b + 1) * LANES], bd) for b in range(x.shape[1] // LANES)]
    return parts[0] if len(parts) == 1 else jnp.concatenate(parts, axis=-1)


def _head_rms(x, g, head_dim):
    ss = _head_sumsq(x, head_dim)
    return x * lax.rsqrt(ss * (1.0 / head_dim) + EPS) * g


def _eye(n, dtype):
    r = lax.broadcasted_iota(jnp.int32, (n, n), 0)
    c = lax.broadcasted_iota(jnp.int32, (n, n), 1)
    return jnp.where(r == c, 1.0, 0.0).astype(dtype)


def _dsa_kernel(a_ref, gq_ref, gk_ref, bias_ref, o_ref, kbn_ref, vt_ref, keys_ref, tri_ref, *, tq, ck, topk):
    s_len = a_ref.shape[1]
    i = pl.program_id(1)
    q0 = pl.multiple_of(i * tq, tq)
    hd = HEAD_DIM
    sub = ck // tq

    @pl.when(i == 0)
    def _():
        kv = a_ref[0, :, 512:640]
        kvf = kv.astype(F32)
        lane = lax.broadcasted_iota(jnp.int32, (1, LANES), 1)
        ss = _head_sumsq(jnp.where(lane < hd, kvf, 0.0), LANES)
        kn = kvf * lax.rsqrt(ss * (1.0 / hd) + EPS) * gk_ref[...]
        kbn_ref[...] = kn[:, :hd].astype(BF16)
        kvt = _nt(_eye(LANES, BF16), kv)
        for cb in range(s_len // ck):
            vt_ref[cb] = kvt[hd:, cb * ck:(cb + 1) * ck].astype(BF16)
        tri_ref[...] = jnp.where(lax.broadcasted_iota(jnp.int32, (ck, ck), 1)
                                 <= lax.broadcasted_iota(jnp.int32, (ck, ck), 0), 1.0, 0.0).astype(BF16)

    def heads_on_rows(x):
        return jnp.concatenate([x[:, h * hd:(h + 1) * hd] for h in range(x.shape[1] // hd)], axis=0)

    qi_all = heads_on_rows(a_ref[0, pl.ds(q0, tq), 256:512])
    wblk = a_ref[0, pl.ds(q0, tq), 640:768]
    sel_r = lax.broadcasted_iota(jnp.int32, (8, LANES), 0)
    sel_c = lax.broadcasted_iota(jnp.int32, (8, LANES), 1)
    w_t = _nt(jnp.where(sel_c == sel_r + D_IDX, 1.0, 0.0).astype(BF16), wblk)
    qpos = q0 + lax.broadcasted_iota(jnp.int32, (1, tq), 1)
    nck = (q0 + tq + ck - 1) // ck
    row_ck = lax.broadcasted_iota(jnp.int32, (ck, tq), 0)

    def p1(c, carry):
        k0 = pl.multiple_of(c * ck, ck)
        act = jnp.maximum(_nt(a_ref[0, pl.ds(k0, ck), 640:704], qi_all), 0.0)
        sc = w_t[0:1, :] * act[:, 0:tq]
        for h in range(1, H_IDX):
            sc = sc + w_t[h:h + 1, :] * act[:, h * tq:(h + 1) * tq]
        bits = pltpu.bitcast(sc, jnp.int32)
        key = jnp.where(bits < 0, -(bits & 0x7FFFFFFF), bits)
        key = jnp.where(k0 + row_ck <= qpos, key, INT_MIN)
        keys_ref[pl.ds(k0, ck), :] = key
        return carry

    lax.fori_loop(0, nck, p1, 0)

    def count(pred):
        def body(c, acc):
            k0 = pl.multiple_of(c * ck, ck)
            m = jnp.where(pred(keys_ref[pl.ds(k0, ck), :]), 1, 0)
            return acc + jnp.sum(m.reshape(ck // 8, 8, tq), axis=0)
        acc = lax.fori_loop(0, nck, body, jnp.zeros((8, tq), jnp.int32))
        return jnp.sum(acc, axis=0, keepdims=True)

    def search():
        c0 = count(lambda k: k >= 0)
        t = jnp.where(c0 >= topk, 0, INT_MIN).astype(jnp.int32)

        def vstep(b, t):
            cand = t | lax.shift_left(jnp.int32(1), 30 - b)
            return jnp.where(count(lambda k: k >= cand) >= topk, cand, t)

        t = lax.fori_loop(0, 31, vstep, t)
        need = topk - count(lambda k: k > t)
        return t, jnp.where(t == INT_MIN, 0, need).astype(F32)

    def no_search():
        return jnp.full((1, tq), INT_MIN, jnp.int32), jnp.zeros((1, tq), F32)

    thr, need = lax.cond(q0 + tq > topk, search, no_search)

    qb = a_ref[0, pl.ds(q0, tq), 0:256].astype(F32)
    qn_all = heads_on_rows(_head_rms(qb, gq_ref[...], hd).astype(BF16))

    def p3(c, st):
        ms, ls, accs, ties_before = st
        k0 = pl.multiple_of(c * ck, ck)
        s_all = _nt(kbn_ref[pl.ds(k0, ck), :], qn_all)
        vtc = vt_ref[c]
        key = keys_ref[pl.ds(k0, ck), :]
        tie = key == thr
        tie_rank = ties_before + _mm(tri_ref[...], jnp.where(tie, 1.0, 0.0).astype(BF16))
        sel = (key > thr) | (tie & (tie_rank <= need))
        ties_before = tie_rank[ck - 1:ck, :]
        bidx = [jnp.clip(i - (c * sub + r), 0, 2) for r in range(sub)]
        heads = range(H_DSA)
        s_h = [s_all[:, h * tq:(h + 1) * tq]
               + jnp.concatenate([bias_ref[bidx[r], h] for r in range(sub)], axis=0) for h in heads]
        m_new = [jnp.maximum(ms[h], jnp.max(jnp.where(sel, s_h[h], NEG), axis=0, keepdims=True)) for h in heads]
        p = [jnp.where(sel, jnp.exp2(s_h[h] - m_new[h]), 0.0) for h in heads]
        alpha = [jnp.exp2(ms[h] - m_new[h]) for h in heads]
        pv = [_mm(vtc, p[h].astype(BF16)) for h in heads]
        nl = [ls[h] * alpha[h] + jnp.sum(p[h], axis=0, keepdims=True) for h in heads]
        na = [accs[h] * alpha[h] + pv[h] for h in heads]
        return tuple(m_new), tuple(nl), tuple(na), ties_before

    init = (tuple(jnp.full((1, tq), NEG, F32) for _ in range(H_DSA)),
            tuple(jnp.zeros((1, tq), F32) for _ in range(H_DSA)),
            tuple(jnp.zeros((hd, tq), F32) for _ in range(H_DSA)),
            jnp.zeros((1, tq), F32))
    _, ls, accs, _ = lax.fori_loop(0, nck, p3, init)
    o_t = jnp.concatenate([accs[h] / ls[h] for h in range(H_DSA)], axis=0)
    o_ref[0] = _nt(_eye(tq, BF16), o_t.astype(BF16)).astype(o_ref.dtype)


def dsa_attention(proj, col_block, gq, gk, bias_t, *, tq=BLOCK, ck=512):
    b, s, _ = proj.shape
    topk = min(TOPK_MAX, s // 4)
    ck = min(ck, s)
    return pl.pallas_call(
        functools.partial(_dsa_kernel, tq=tq, ck=ck, topk=topk),
        grid=(b, s // tq),
        in_specs=[pl.BlockSpec((1, s, 768), lambda bi, i: (bi, 0, col_block)),
                  pl.BlockSpec((1, 256), lambda bi, i: (0, 0)),
                  pl.BlockSpec((1, LANES), lambda bi, i: (0, 0)),
                  pl.BlockSpec((3, H_DSA, tq, tq), lambda bi, i: (0, 0, 0, 0))],
        out_specs=pl.BlockSpec((1, tq, H_DSA * HEAD_DIM), lambda bi, i: (bi, i, 0)),
        out_shape=jax.ShapeDtypeStruct((b, s, H_DSA * HEAD_DIM), BF16),
        scratch_shapes=[pltpu.VMEM((s, HEAD_DIM), BF16),
                        pltpu.VMEM((s // ck, HEAD_DIM, ck), BF16),
                        pltpu.VMEM((s, tq), jnp.int32),
                        pltpu.VMEM((ck, ck), BF16)],
        compiler_params=_params(("parallel", "arbitrary")),
        name="dsa_attention",
    )(proj, gq, gk, bias_t)


def _t5_bucket(rel):
    n = jnp.maximum(rel, 0)
    max_exact = N_BUCKETS // 2
    nf = jnp.maximum(n, 1).astype(F32)
    large = max_exact + (jnp.log(nf / max_exact) / math.log(MAX_DISTANCE / max_exact)
                         * (N_BUCKETS - max_exact)).astype(jnp.int32)
    large = jnp.minimum(large, N_BUCKETS - 1)
    return jnp.where(n < max_exact, n, large)


def _bucket_lookup(tab, rel):
    hit = _t5_bucket(rel)[..., None, None] == jnp.arange(N_BUCKETS)[:, None]
    return jnp.sum(jnp.where(hit, tab.astype(F32), 0.0), axis=-2)


def dsa_bias_tiles(tab, tq=BLOCK):
    ks = jnp.arange(tq)[:, None]
    tl = jnp.arange(tq)[None, :]
    rel = jnp.stack([tl - ks, tq + tl - ks, jnp.full((tq, tq), 2 * tq + MAX_DISTANCE)])
    return _bucket_lookup(tab, rel).transpose(0, 3, 1, 2)


def _swa_kernel(sink_ref, cur_ref, prev_ref, gq_ref, gk_ref, bias_ref, o_ref, *, tq, nb):
    i = pl.program_id(1)
    hd = HEAD_DIM
    g = H_SW // KV_SW
    qn = _head_rms(cur_ref[0, :, 0:512].astype(F32), gq_ref[...], hd).astype(BF16)
    kn = jnp.concatenate([_head_rms(prev_ref[0, :, 512:640].astype(F32), gk_ref[...], hd),
                          _head_rms(cur_ref[0, :, 512:640].astype(F32), gk_ref[...], hd)], axis=0).astype(BF16)
    vv = jnp.concatenate([prev_ref[0, :, 640:768], cur_ref[0, :, 640:768]], axis=0)
    col = lax.broadcasted_iota(jnp.int32, (tq, 2 * tq), 1)
    first = (col >= tq) | (i > 0)
    chains = [(b, h) for b in range(nb) for h in range(H_SW)]
    s = {}
    for b, h in chains:
        kv = h // g
        s_bh = _nt(qn[b * tq:(b + 1) * tq, h * hd:(h + 1) * hd], kn[b * tq:(b + 2) * tq, kv * hd:(kv + 1) * hd])
        s_bh = s_bh + bias_ref[h]
        s[(b, h)] = jnp.where(first, s_bh, NEG) if b == 0 else s_bh
    m = {c: jnp.maximum(jnp.max(s[c], axis=-1, keepdims=True), sink_ref[c[1]]) for c in chains}
    e = {c: jnp.exp2(s[c] - m[c]) for c in chains}
    den = {c: jnp.sum(e[c], axis=-1, keepdims=True) + jnp.exp2(sink_ref[c[1]] - m[c]) for c in chains}
    out = {(b, h): _mm(e[(b, h)].astype(BF16), vv[b * tq:(b + 2) * tq, (h // g) * hd:(h // g + 1) * hd]) / den[(b, h)]
           for b, h in chains}
    for b in range(nb):
        o_ref[0, b * tq:(b + 1) * tq, :] = jnp.concatenate([out[(b, h)] for h in range(H_SW)],
                                                           axis=-1).astype(o_ref.dtype)


def swa_bias_tiles(tab, tq=BLOCK):
    rel = (jnp.arange(tq)[:, None] + tq) - jnp.arange(2 * tq)[None, :]
    in_win = (rel >= 0) & (rel < WINDOW)
    bias = _bucket_lookup(tab, rel).transpose(2, 0, 1)
    return jnp.where(in_win[None], bias, NEG)


def swa_attention(proj, col_block, sinks, gq, gk, bias, *, tq=BLOCK, nb=4):
    b, s, _ = proj.shape
    return pl.pallas_call(
        functools.partial(_swa_kernel, tq=tq, nb=nb),
        grid_spec=pltpu.PrefetchScalarGridSpec(
            num_scalar_prefetch=0,
            grid=(b, s // (nb * tq)),
            in_specs=[pl.BlockSpec(memory_space=pltpu.SMEM),
                      pl.BlockSpec((1, nb * tq, 768), lambda bi, i: (bi, i, col_block)),
                      pl.BlockSpec((1, tq, 768), lambda bi, i: (bi, jnp.maximum(nb * i - 1, 0), col_block)),
                      pl.BlockSpec((1, 512), lambda bi, i: (0, 0)),
                      pl.BlockSpec((1, LANES), lambda bi, i: (0, 0)),
                      pl.BlockSpec((H_SW, tq, 2 * tq), lambda bi, i: (0, 0, 0))],
            out_specs=pl.BlockSpec((1, nb * tq, H_SW * HEAD_DIM), lambda bi, i: (bi, i, 0))),
        out_shape=jax.ShapeDtypeStruct((b, s, H_SW * HEAD_DIM), BF16),
        compiler_params=_params(("parallel", "arbitrary")),
        name="swa_attention",
    )(sinks, proj, proj, gq, gk, bias)


def _merge_kernel(x_ref, g_ref, oa_ref, ob_ref, oc_ref, bg_ref, wa_ref, wb_ref, wc_ref, wo_ref, o_ref):
    d = x_ref.shape[1]
    merged = None
    for k, (o_k, w_k) in enumerate(((oa_ref, wa_ref), (ob_ref, wb_ref), (oc_ref, wc_ref))):
        logit = g_ref[:, k * d:(k + 1) * d].astype(F32) + bg_ref[k:k + 1, :]
        gate = 1.0 / (1.0 + jnp.exp(-logit))
        term = gate * _mm(o_k[...], w_k[...])
        merged = term if merged is None else merged + term
    o_ref[...] = x_ref[...] + _mm(merged.astype(BF16), wo_ref[...])


def merge_project(x2d, proj2d, o_a, o_b, o_c, b_gate, w_pa, w_pb, w_pc, w_out, *, tm=512):
    n, d = x2d.shape
    full = lambda a: pl.BlockSpec(a.shape, lambda i: (0,) * a.ndim)
    row = lambda a: pl.BlockSpec((tm, a.shape[1]), lambda i: (i, 0))
    return pl.pallas_call(
        _merge_kernel,
        grid=(n // tm,),
        in_specs=[row(x2d), pl.BlockSpec((tm, N_BRANCH * d), lambda i: (i, 0)),
                  row(o_a), row(o_b), row(o_c),
                  full(b_gate), full(w_pa), full(w_pb), full(w_pc), full(w_out)],
        out_specs=pl.BlockSpec((tm, d), lambda i: (i, 0)),
        out_shape=jax.ShapeDtypeStruct((n, d), F32),
        compiler_params=_params(("parallel",)),
        name="merge_project",
    )(x2d, proj2d, o_a, o_b, o_c, b_gate, w_pa, w_pb, w_pc, w_out)


def _memkv_kernel(m_ref, g_ref, w_ref, gk_ref, k_ref, v_ref):
    w_x = k_ref.shape[2]
    mn = _rms_rows(m_ref[0], g_ref[...]).astype(BF16)
    kv = _mm(mn, w_ref[...])
    k_ref[0] = _head_rms(kv[:, :w_x], gk_ref[...], XHEAD_DIM).astype(k_ref.dtype)
    v_ref[0] = kv[:, w_x:].astype(v_ref.dtype)


def memory_kv(mem, gain, w_kv, gk):
    b, m, d = mem.shape
    w_x = w_kv.shape[1] // 2
    return pl.pallas_call(
        _memkv_kernel,
        grid=(b,),
        in_specs=[pl.BlockSpec((1, m, d), lambda i: (i, 0, 0)),
                  pl.BlockSpec((1, d), lambda i: (0, 0)),
                  pl.BlockSpec(w_kv.shape, lambda i: (0, 0)),
                  pl.BlockSpec((1, w_x), lambda i: (0, 0))],
        out_specs=[pl.BlockSpec((1, m, w_x), lambda i: (i, 0, 0)),
                   pl.BlockSpec((1, m, w_x), lambda i: (i, 0, 0))],
        out_shape=[jax.ShapeDtypeStruct((b, m, w_x), BF16)] * 2,
        compiler_params=_params(("parallel",)),
        name="memory_kv",
    )(mem, gain, w_kv, gk)


def _xattn_kernel(x_ref, g_ref, wq_ref, gq_ref, k_ref, v_ref, wo_ref, o_ref):
    x = x_ref[0]
    xn = _rms_rows(x, g_ref[...]).astype(BF16)
    q = _head_rms(_mm(xn, wq_ref[...]), gq_ref[...], XHEAD_DIM).astype(BF16)
    sl = [slice(h * XHEAD_DIM, (h + 1) * XHEAD_DIM) for h in range(H_X)]
    s = [_nt(q[:, c], k_ref[0, :, c]) for c in sl]
    e = [jnp.exp2(s_h - jnp.max(s_h, axis=-1, keepdims=True)) for s_h in s]
    den = [jnp.sum(e_h, axis=-1, keepdims=True) for e_h in e]
    outs = [_mm(e[h].astype(BF16), v_ref[0, :, sl[h]]) / den[h] for h in range(H_X)]
    o = jnp.concatenate(outs, axis=-1).astype(BF16)
    o_ref[0] = x + _mm(o, wo_ref[...])


def cross_attention(x, gain, w_q, gq, k_mem, v_mem, w_o, *, tq=512):
    b, s, d = x.shape
    m, w_x = k_mem.shape[1:]
    full = lambda a: pl.BlockSpec(a.shape, lambda bi, i: (0,) * a.ndim)
    return pl.pallas_call(
        _xattn_kernel,
        grid=(b, s // tq),
        in_specs=[pl.BlockSpec((1, tq, d), lambda bi, i: (bi, i, 0)),
                  full(gain), full(w_q), full(gq),
                  pl.BlockSpec((1, m, w_x), lambda bi, i: (bi, 0, 0)),
                  pl.BlockSpec((1, m, w_x), lambda bi, i: (bi, 0, 0)),
                  full(w_o)],
        out_specs=pl.BlockSpec((1, tq, d), lambda bi, i: (bi, i, 0)),
        out_shape=jax.ShapeDtypeStruct((b, s, d), F32),
        compiler_params=_params(("parallel", "parallel")),
        name="cross_attention",
    )(x, gain, w_q, gq, k_mem, v_mem, w_o)


def _router_kernel(x_ref, g_ref, whi_ref, wlo_ref, o_ref):
    hf = _rms_rows(x_ref[...], g_ref[...])
    hi = hf.astype(BF16)
    lo = (hf - hi.astype(F32)).astype(BF16)
    logits = _mm(hi, whi_ref[...]) + (_mm(hi, wlo_ref[...]) + _mm(lo, whi_ref[...]))
    lane = lax.broadcasted_iota(jnp.int32, logits.shape, 1).astype(F32)
    logits = jnp.where(lane < N_EXPERTS, logits, NEG)
    m1 = jnp.max(logits, axis=-1, keepdims=True)
    i1 = jnp.min(jnp.where(logits == m1, lane, float(LANES)), axis=-1, keepdims=True)
    rest = jnp.where(lane == i1, NEG, logits)
    m2 = jnp.max(rest, axis=-1, keepdims=True)
    i2 = jnp.min(jnp.where(rest == m2, lane, float(LANES)), axis=-1, keepdims=True)
    e2 = jnp.exp(m2 - m1)
    den = 1.0 + e2
    o_ref[...] = (jnp.where(lane == 0.0, i1, 0.0) + jnp.where(lane == 1.0, i2, 0.0)
                  + jnp.where(lane == 2.0, 1.0 / den, 0.0) + jnp.where(lane == 3.0, e2 / den, 0.0))


def router_gates(x2d, gain, w_hi, w_lo, *, tm=512):
    n, d = x2d.shape
    return pl.pallas_call(
        _router_kernel,
        grid=(n // tm,),
        in_specs=[pl.BlockSpec((tm, d), lambda i: (i, 0)),
                  pl.BlockSpec((1, d), lambda i: (0, 0)),
                  pl.BlockSpec((d, LANES), lambda i: (0, 0)),
                  pl.BlockSpec((d, LANES), lambda i: (0, 0))],
        out_specs=pl.BlockSpec((tm, LANES), lambda i: (i, 0)),
        out_shape=jax.ShapeDtypeStruct((n, LANES), F32),
        compiler_params=_params(("parallel",)),
        name="router_gates",
    )(x2d, gain, w_hi, w_lo)


def _swiglu_tile(xn, wg, wu, wd):
    gg = _mm(xn, wg)
    uu = _mm(xn, wu)
    act = gg * (1.0 / (1.0 + jnp.exp(-gg))) * uu
    return _mm(act.astype(BF16), wd)


def _ffn_kernel(x_ref, g_ref, wg_ref, wu_ref, wd_ref, o_ref, xn_ref, acc_ref):
    j = pl.program_id(1)

    @pl.when(j == 0)
    def _():
        xn_ref[...] = _rms_rows(x_ref[...], g_ref[...]).astype(BF16)
        acc_ref[...] = jnp.zeros_like(acc_ref)

    acc_ref[...] += _swiglu_tile(xn_ref[...], wg_ref[...], wu_ref[...], wd_ref[...])

    @pl.when(j == pl.num_programs(1) - 1)
    def _():
        o_ref[...] = x_ref[...] + acc_ref[...]


def dense_ffn(x2d, gain, w_gu, w_down, *, tm=1024, tf=256):
    n, d = x2d.shape
    tm = min(tm, n)
    f = w_down.shape[0]
    nf = f // tf
    return pl.pallas_call(
        _ffn_kernel,
        grid=(n // tm, nf),
        in_specs=[pl.BlockSpec((tm, d), lambda i, j: (i, 0)),
                  pl.BlockSpec((1, d), lambda i, j: (0, 0)),
                  pl.BlockSpec((d, tf), lambda i, j: (0, j)),
                  pl.BlockSpec((d, tf), lambda i, j: (0, j + nf)),
                  pl.BlockSpec((tf, d), lambda i, j: (j, 0))],
        out_specs=pl.BlockSpec((tm, d), lambda i, j: (i, 0)),
        out_shape=jax.ShapeDtypeStruct((n, d), F32),
        scratch_shapes=[pltpu.VMEM((tm, d), BF16), pltpu.VMEM((tm, d), F32)],
        compiler_params=_params(("parallel", "arbitrary")),
        name="dense_ffn",
    )(x2d, gain, w_gu, w_gu, w_down)


def _moe_kernel(te_ref, tn_ref, tok_ref, tok_next_ref, dst_prev_ref, x_hbm, g_ref, gate_ref, wg_ref, wu_ref,
                wd_ref, y_hbm, xg_ref, xn_ref, acc_ref, yb_ref, gsem, ssem, *, tm, nf):
    i = pl.program_id(0)
    j = pl.program_id(1)
    nt = pl.num_programs(0)
    slot = i % 2
    per_step = tm // nf
    active = tn_ref[i] > 0
    prev_active = (i > 0) & (tn_ref[jnp.maximum(i - 1, 0)] > 0)
    prev_issued = (i > 0) & (tn_ref[jnp.maximum(i - 2, 0)] > 0)

    def gather_copy(tok, r, s):
        return pltpu.make_async_copy(x_hbm.at[pl.ds(tok, 1), :], xg_ref.at[s, pl.ds(r, 1), :], gsem.at[s])

    def scatter_copy(dst, r, s):
        return pltpu.make_async_copy(yb_ref.at[s, pl.ds(r, 1), :], y_hbm.at[pl.ds(dst, 1), :], ssem.at[s])

    def start_all_rows(ids_ref, s, make_copy):
        def body(r8, c):
            for u in range(8):
                r = r8 * 8 + u
                make_copy(ids_ref[0, 0, r], r, s).start()
            return c
        lax.fori_loop(0, tm // 8, body, 0)

    def wait_gather(s):
        pltpu.make_async_copy(x_hbm.at[pl.ds(0, tm), :], xg_ref.at[s], gsem.at[s]).wait()

    def wait_scatter(s):
        pltpu.make_async_copy(yb_ref.at[s], y_hbm.at[pl.ds(0, tm), :], ssem.at[s]).wait()

    @pl.when(j == 0)
    def _():
        @pl.when(i == 0)
        def _():
            start_all_rows(tok_ref, slot, gather_copy)
            yb_ref[1] = jnp.zeros(yb_ref.shape[1:], yb_ref.dtype)
            n_real = y_hbm.shape[0] - 2 * tm
            for half in range(2):
                spare = pltpu.make_async_copy(yb_ref.at[1], y_hbm.at[pl.ds(n_real + half * tm, tm), :], ssem.at[1])
                spare.start()
                spare.wait()

        @pl.when(active | prev_active)
        def _():
            wait_gather(slot)

        @pl.when(prev_issued)
        def _():
            wait_scatter(slot)

        @pl.when(active)
        def _():
            xn_ref[...] = _rms_rows(xg_ref[slot], g_ref[...]).astype(BF16)
            acc_ref[...] = jnp.zeros_like(acc_ref)

        @pl.when(prev_active & jnp.logical_not(active))
        def _():
            start_all_rows(dst_prev_ref, 1 - slot, scatter_copy)

    @pl.when(active)
    def _():
        acc_ref[...] += _swiglu_tile(xn_ref[...], wg_ref[0], wu_ref[0], wd_ref[0])
        for u in range(per_step):
            r = j * per_step + u
            gather_copy(tok_next_ref[0, 0, r], r, 1 - slot).start(priority=1)
            scatter_copy(dst_prev_ref[0, 0, r], r, 1 - slot).start(priority=1)

        @pl.when(j == nf - 1)
        def _():
            yb_ref[slot] = acc_ref[...] * gate_ref[...]

    @pl.when((j == nf - 1) & (i == nt - 1) & (active | prev_active))
    def _():
        wait_scatter(1 - slot)


def moe_experts(x2d, gain, plan, w_gu, w_down, *, tm, tf=512):
    n, d = x2d.shape
    ne, f, _ = w_down.shape
    nf = f // tf
    assert tm % nf == 0 and tm % 16 == 0
    tile_e, tile_n, row_tok, row_dst_prev, row_gate = plan
    nt = tile_e.shape[0]
    smem_rows = lambda imap: pl.BlockSpec((1, 1, tm), imap, memory_space=pltpu.SMEM)
    live = lambda j, tn, i: j * jnp.minimum(tn[i], 1)
    return pl.pallas_call(
        functools.partial(_moe_kernel, tm=tm, nf=nf),
        grid_spec=pltpu.PrefetchScalarGridSpec(
            num_scalar_prefetch=2,
            grid=(nt, nf),
            in_specs=[smem_rows(lambda i, j, te, tn: (i, 0, 0)),
                      smem_rows(lambda i, j, te, tn: (jnp.minimum(i + 1, nt - 1), 0, 0)),
                      smem_rows(lambda i, j, te, tn: (i, 0, 0)),
                      pl.BlockSpec(memory_space=pl.ANY),
                      pl.BlockSpec((1, d), lambda i, j, te, tn: (0, 0)),
                      pl.BlockSpec((tm, 1), lambda i, j, te, tn: (i, 0)),
                      pl.BlockSpec((1, d, tf), lambda i, j, te, tn: (te[i], 0, live(j, tn, i))),
                      pl.BlockSpec((1, d, tf), lambda i, j, te, tn: (te[i], 0, live(j, tn, i) + nf)),
                      pl.BlockSpec((1, tf, d), lambda i, j, te, tn: (te[i], live(j, tn, i), 0))],
            out_specs=pl.BlockSpec(memory_space=pl.ANY),
            scratch_shapes=[pltpu.VMEM((2, tm, d), F32), pltpu.VMEM((tm, d), BF16), pltpu.VMEM((tm, d), F32),
                            pltpu.VMEM((2, tm, d), F32),
                            pltpu.SemaphoreType.DMA((2,)), pltpu.SemaphoreType.DMA((2,))]),
        out_shape=jax.ShapeDtypeStruct((2 * n + 2 * tm, d), F32),
        compiler_params=_params(("arbitrary", "arbitrary")),
        name="moe_experts",
    )(tile_e, tile_n, row_tok, row_tok, row_dst_prev, x2d, gain, row_gate, w_gu, w_gu, w_down)


def moe_plan(route, *, tm):
    n = route.shape[0]
    flat_e = route[:, :2].astype(jnp.int32).reshape(-1)
    flat_g = route[:, 2:4].reshape(-1)
    nt = (2 * n) // tm + N_EXPERTS + 1
    order = jnp.argsort(flat_e, stable=True).astype(jnp.int32)
    counts = jnp.sum(flat_e[:, None] == jnp.arange(N_EXPERTS)[None, :], axis=0).astype(jnp.int32)
    off = jnp.cumsum(counts) - counts
    tiles = (counts + tm - 1) // tm
    tile_off = jnp.cumsum(tiles) - tiles
    tile_id = jnp.arange(nt, dtype=jnp.int32)
    used = tile_id < jnp.sum(tiles)
    tile_e = jnp.clip(jnp.sum(tile_id[:, None] >= tile_off[None, :], axis=1) - 1, 0, N_EXPERTS - 1)
    tile_e = jnp.where(used, tile_e, tile_e[jnp.maximum(jnp.sum(tiles) - 1, 0)]).astype(jnp.int32)
    first_row = (tile_id - tile_off[tile_e]) * tm
    tile_n = jnp.where(used, jnp.clip(counts[tile_e] - first_row, 0, tm), 0).astype(jnp.int32)
    r = jnp.arange(tm, dtype=jnp.int32)[None, :]
    valid = r < tile_n[:, None]
    a = order[jnp.clip(off[tile_e][:, None] + first_row[:, None] + r, 0, 2 * n - 1)]
    row_tok = jnp.where(valid, a // 2, 0).astype(jnp.int32).reshape(nt, 1, tm)
    spare = 2 * n + (tile_id[:, None] % 2) * tm + r
    row_dst = jnp.where(valid, (a % 2) * n + a // 2, spare).astype(jnp.int32)
    row_dst_prev = jnp.concatenate([2 * n + tm + r, row_dst[:-1]], axis=0).reshape(nt, 1, tm)
    row_gate = jnp.where(valid, flat_g[a], 0.0).astype(F32).reshape(nt * tm, 1)
    return tile_e, tile_n, row_tok, row_dst_prev, row_gate


def _combine_kernel(x_ref, y0_ref, y1_ref, o_ref):
    o_ref[...] = x_ref[...] + (y0_ref[...] + y1_ref[...])


def moe_combine(x2d, y, *, tm=512):
    n, d = x2d.shape
    nb = n // tm
    return pl.pallas_call(
        _combine_kernel,
        grid=(nb,),
        in_specs=[pl.BlockSpec((tm, d), lambda i: (i, 0)),
                  pl.BlockSpec((tm, d), lambda i: (i, 0)),
                  pl.BlockSpec((tm, d), lambda i: (i + nb, 0))],
        out_specs=pl.BlockSpec((tm, d), lambda i: (i, 0)),
        out_shape=jax.ShapeDtypeStruct((n, d), F32),
        compiler_params=_params(("parallel",)),
        name="moe_combine",
    )(x2d, y, y)


def _pack_w_in(w, d):
    sizes = (256, 256, 256, 256, 64, 64, 256, 64, 4, 512, 128, 128, N_BRANCH * d)
    qa, ka, va, qb, kb, vb, qi, ki, wi, qc, kc, vc, g = jnp.split(w, np.cumsum(sizes)[:-1].tolist(), axis=-1)
    pad = jnp.zeros((w.shape[0], 60), w.dtype)
    return jnp.concatenate([g, qa * (HEAD_DIM ** -0.5 * LOG2E), ka, va,
                            qb, qi * D_IDX ** -0.5, kb, vb, ki, wi, pad,
                            qc, kc, vc], axis=-1).astype(BF16)


def kernel(x, mem, rel_bias, norm_mix, w_in, b_gate, qn_dsa, kn_dsa, qn_swa, kn_swa, sinks, w_pa, w_pb, w_pc, w_out, norm_x, norm_mem, w_xq, w_xkv, w_xo, qn_x, kn_x, norm_ffn, w_gu_dense, w_down_dense, w_router, w_gu_moe, w_down_moe):
    b, s, d = x.shape
    depth = w_in.shape[0]
    n = b * s
    row = lambda v: v.reshape(1, -1).astype(F32)
    bias_dsa = dsa_bias_tiles(rel_bias[:, :H_DSA] * LOG2E)
    bias_swa = swa_bias_tiles(rel_bias[:, H_DSA:] * LOG2E)
    qscale = HEAD_DIM ** -0.5 * LOG2E
    gcol = (N_BRANCH * d) // 768

    x2 = x.reshape(n, d)
    for l in range(depth):
        proj = in_projection(x2, row(norm_mix[l]), _pack_w_in(w_in[l], d))
        proj3 = proj.reshape(b, s, -1)
        o_a = sb_attention(proj3, gcol)
        o_b = dsa_attention(proj3, gcol + 1, row(jnp.tile(qn_dsa[l] * qscale, H_DSA)),
                            row(jnp.concatenate([kn_dsa[l], jnp.zeros_like(kn_dsa[l])])), bias_dsa)
        o_c = swa_attention(proj3, gcol + 2, sinks[l].astype(F32) * LOG2E,
                            row(jnp.tile(qn_swa[l] * qscale, H_SW)), row(jnp.tile(kn_swa[l], KV_SW)), bias_swa)
        x2 = merge_project(x2, proj, o_a.reshape(n, -1), o_b.reshape(n, -1), o_c.reshape(n, -1),
                           b_gate[l].astype(F32), w_pa[l].astype(BF16), w_pb[l].astype(BF16),
                           w_pc[l].astype(BF16), w_out[l].astype(BF16))
        k_mem, v_mem = memory_kv(mem, row(norm_mem[l]), w_xkv[l].astype(BF16), row(jnp.tile(kn_x[l], H_X)))
        x2 = cross_attention(x2.reshape(b, s, d), row(norm_x[l]), w_xq[l].astype(BF16),
                             row(jnp.tile(qn_x[l] * (XHEAD_DIM ** -0.5 * LOG2E), H_X)), k_mem, v_mem,
                             w_xo[l].astype(BF16)).reshape(n, d)
        if l % 2 == 0:
            x2 = dense_ffn(x2, row(norm_ffn[l]), w_gu_dense[l // 2].astype(BF16),
                           w_down_dense[l // 2].astype(BF16), tf=256)
        else:
            wr = jnp.pad(w_router[l // 2].astype(F32), ((0, 0), (0, LANES - N_EXPERTS)))
            wr_hi = wr.astype(BF16)
            wr_lo = (wr - wr_hi.astype(F32)).astype(BF16)
            route = router_gates(x2, row(norm_ffn[l]), wr_hi, wr_lo)
            tm_moe = 672
            y = moe_experts(x2, row(norm_ffn[l]), moe_plan(route, tm=tm_moe), w_gu_moe[l // 2].astype(BF16),
                            w_down_moe[l // 2].astype(BF16), tm=tm_moe)
            x2 = moe_combine(x2, y)
    return x2.reshape(b, s, d)
```

```python
import functools
import math

import numpy as np
import jax
import jax.numpy as jnp
from jax import lax
from jax.experimental import pallas as pl
from jax.experimental.pallas import tpu as pltpu

HEAD_DIM = 64
H_SB = 4
H_DSA = 4
H_IDX = 4
D_IDX = 64
TOPK_MAX = 256
H_SW = 8
KV_SW = 2
WINDOW = 128
BLOCK = 128
N_BRANCH = 3
N_BUCKETS = 32
MAX_DISTANCE = 128
H_X = 4
XHEAD_DIM = 128
N_EXPERTS = 8
EPS = 1e-6

LANES = 128
VMEM_LIMIT = 56 * 1024 * 1024
NEG = -1e30
INT_MIN = -(2 ** 31)
LOG2E = math.log2(math.e)

F32 = jnp.float32
BF16 = jnp.bfloat16


def _nt(a, b):
    return lax.dot_general(a, b, (((1,), (1,)), ((), ())), preferred_element_type=F32)


def _mm(a, b):
    return jnp.dot(a, b, preferred_element_type=F32)


def _rms_rows(x, g):
    ms = jnp.mean(x * x, axis=-1, keepdims=True)
    return x * lax.rsqrt(ms + EPS) * g


def _params(sem):
    return pltpu.CompilerParams(dimension_semantics=sem, vmem_limit_bytes=VMEM_LIMIT)


def _sb_kernel(a_ref, o_ref, *, tq):
    i = pl.program_id(1)
    q0 = pl.multiple_of(i * tq, tq)
    hd = HEAD_DIM
    row = lax.broadcasted_iota(jnp.int32, (tq, tq), 0)
    col = lax.broadcasted_iota(jnp.int32, (tq, tq), 1)
    strict = col < row
    u_inc = jnp.where(row >= col, 1.0, 0.0).astype(BF16)
    qs = [a_ref[0, pl.ds(q0, tq), h * hd:(h + 1) * hd] for h in range(H_SB)]

    heads = range(H_SB)

    def block(k0s, accs, carries, diag):
        chains = [(b, h) for b in range(len(k0s)) for h in heads]
        ks = {(b, h): a_ref[0, pl.ds(k0s[b], tq), 256 + h * hd:256 + (h + 1) * hd] for b, h in chains}
        vs = {(b, h): a_ref[0, pl.ds(k0s[b], tq), 512 + h * hd:512 + (h + 1) * hd] for b, h in chains}
        zs = {c: _nt(qs[c[1]], ks[c]) for c in chains}
        lks = {c: -(jnp.maximum(zs[c], 0.0) + jnp.log2(1.0 + jnp.exp2(-jnp.abs(zs[c])))) for c in chains}
        if diag:
            lks = {c: jnp.where(strict, lks[c], 0.0) for c in chains}
        rs = {c: _mm(lks[c].astype(BF16), u_inc) for c in chains}
        carry = {h: (None if diag else carries[h]) for h in heads}
        atts = {}
        for b, h in chains:
            if diag:
                atts[(b, h)] = jnp.where(strict, jnp.exp2(zs[(b, h)] + rs[(b, h)]), 0.0)
                carry[h] = rs[(b, h)][:, 0:1]
            else:
                atts[(b, h)] = jnp.exp2(zs[(b, h)] + rs[(b, h)] + carry[h])
                carry[h] = carry[h] + rs[(b, h)][:, 0:1]
        pvs = {c: _mm(atts[c].astype(BF16), vs[c]) for c in chains}
        new_acc = []
        for h in heads:
            tot = pvs[(0, h)] if diag else accs[h] + pvs[(0, h)]
            for b in range(1, len(k0s)):
                tot = tot + pvs[(b, h)]
            new_acc.append(tot)
        return tuple(new_acc), tuple(carry[h] for h in heads)

    state = block([q0], None, None, True)

    def two_blocks(jj, st):
        near = pl.multiple_of((i - 1 - 2 * jj) * tq, tq)
        far = pl.multiple_of((i - 2 - 2 * jj) * tq, tq)
        return block([near, far], st[0], st[1], False)

    state = lax.fori_loop(0, i // 2, two_blocks, state)
    accs, _ = lax.fori_loop(0, i % 2, lambda jj, st: block([0], st[0], st[1], False), state)
    o_ref[0] = jnp.concatenate(accs, axis=-1).astype(o_ref.dtype)


def sb_attention(proj, col_block, *, tq=256):
    b, s, _ = proj.shape
    return pl.pallas_call(
        functools.partial(_sb_kernel, tq=tq),
        grid=(b, s // tq),
        in_specs=[pl.BlockSpec((1, s, 768), lambda bi, i: (bi, 0, col_block))],
        out_specs=pl.BlockSpec((1, tq, H_SB * HEAD_DIM), lambda bi, i: (bi, i, 0)),
        out_shape=jax.ShapeDtypeStruct((b, s, H_SB * HEAD_DIM), BF16),
        compiler_params=_params(("parallel", "arbitrary")),
        name="sb_attention",
    )(proj)


def _inproj_kernel(x_ref, g_ref, w_ref, o_ref, xn_ref):
    @pl.when(pl.program_id(1) == 0)
    def _():
        xn_ref[...] = _rms_rows(x_ref[...], g_ref[...]).astype(BF16)

    o_ref[...] = _mm(xn_ref[...], w_ref[...]).astype(o_ref.dtype)


def in_projection(x2d, gain, w, *, tm=1024, tn=768):
    n, d = x2d.shape
    tm = min(tm, n)
    c = w.shape[1]
    return pl.pallas_call(
        _inproj_kernel,
        grid=(n // tm, c // tn),
        in_specs=[pl.BlockSpec((tm, d), lambda i, j: (i, 0)),
                  pl.BlockSpec((1, d), lambda i, j: (0, 0)),
                  pl.BlockSpec((d, tn), lambda i, j: (0, j))],
        out_specs=pl.BlockSpec((tm, tn), lambda i, j: (i, j)),
        out_shape=jax.ShapeDtypeStruct((n, c), BF16),
        scratch_shapes=[pltpu.VMEM((tm, d), BF16)],
        compiler_params=_params(("parallel", "arbitrary")),
        name="in_projection",
    )(x2d, gain, w)


def _head_sumsq(x, head_dim):
    r_i = lax.broadcasted_iota(jnp.int32, (LANES, LANES), 0) // head_dim
    c_i = lax.broadcasted_iota(jnp.int32, (LANES, LANES), 1) // head_dim
    bd = jnp.where(r_i == c_i, 1.0, 0.0).astype(F32)
    x2 = x * x
    parts = [_mm(x2[:, b * LANES:(b + 1) * LANES], bd) for b in range(x.shape[1] // LANES)]
    return parts[0] if len(parts) == 1 else jnp.concatenate(parts, axis=-1)


def _head_rms(x, g, head_dim):
    ss = _head_sumsq(x, head_dim)
    return x * lax.rsqrt(ss * (1.0 / head_dim) + EPS) * g


def _eye(n, dtype):
    r = lax.broadcasted_iota(jnp.int32, (n, n), 0)
    c = lax.broadcasted_iota(jnp.int32, (n, n), 1)
    return jnp.where(r == c, 1.0, 0.0).astype(dtype)


def _dsa_kernel(a_ref, gq_ref, gk_ref, bias_ref, o_ref, kbn_ref, vt_ref, keys_ref, tri_ref, *, tq, ck, topk):
    s_len = a_ref.shape[1]
    i = pl.program_id(1)
    q0 = pl.multiple_of(i * tq, tq)
    hd = HEAD_DIM
    sub = ck // tq

    @pl.when(i == 0)
    def _():
        kv = a_ref[0, :, 512:640]
        kvf = kv.astype(F32)
        lane = lax.broadcasted_iota(jnp.int32, (1, LANES), 1)
        ss = _head_sumsq(jnp.where(lane < hd, kvf, 0.0), LANES)
        kn = kvf * lax.rsqrt(ss * (1.0 / hd) + EPS) * gk_ref[...]
        kbn_ref[...] = kn[:, :hd].astype(BF16)
        kvt = _nt(_eye(LANES, BF16), kv)
        for cb in range(s_len // ck):
            vt_ref[cb] = kvt[hd:, cb * ck:(cb + 1) * ck].astype(BF16)
        tri_ref[...] = jnp.where(lax.broadcasted_iota(jnp.int32, (ck, ck), 1)
                                 <= lax.broadcasted_iota(jnp.int32, (ck, ck), 0), 1.0, 0.0).astype(BF16)

    def heads_on_rows(x):
        return jnp.concatenate([x[:, h * hd:(h + 1) * hd] for h in range(x.shape[1] // hd)], axis=0)

    qi_all = heads_on_rows(a_ref[0, pl.ds(q0, tq), 256:512])
    wblk = a_ref[0, pl.ds(q0, tq), 640:768]
    sel_r = lax.broadcasted_iota(jnp.int32, (8, LANES), 0)
    sel_c = lax.broadcasted_iota(jnp.int32, (8, LANES), 1)
    w_t = _nt(jnp.where(sel_c == sel_r + D_IDX, 1.0, 0.0).astype(BF16), wblk)
    qpos = q0 + lax.broadcasted_iota(jnp.int32, (1, tq), 1)
    nck = (q0 + tq + ck - 1) // ck
    row_ck = lax.broadcasted_iota(jnp.int32, (ck, tq), 0)

    def p1(c, carry):
        k0 = pl.multiple_of(c * ck, ck)
        act = jnp.maximum(_nt(a_ref[0, pl.ds(k0, ck), 640:704], qi_all), 0.0)
        sc = w_t[0:1, :] * act[:, 0:tq]
        for h in range(1, H_IDX):
            sc = sc + w_t[h:h + 1, :] * act[:, h * tq:(h + 1) * tq]
        bits = pltpu.bitcast(sc, jnp.int32)
        key = jnp.where(bits < 0, -(bits & 0x7FFFFFFF), bits)
        key = jnp.where(k0 + row_ck <= qpos, key, INT_MIN)
        keys_ref[pl.ds(k0, ck), :] = key
        return carry

    lax.fori_loop(0, nck, p1, 0)

    def search(n_chunks):
        def count(pred):
            acc = jnp.zeros((8, tq), jnp.int32)
            for c in range(n_chunks):
                m = jnp.where(pred(keys_ref[c * ck:(c + 1) * ck, :]), 1, 0)
                acc = acc + jnp.sum(m.reshape(ck // 8, 8, tq), axis=0)
            return jnp.sum(acc, axis=0, keepdims=True)

        c0 = count(lambda k: k >= 0)
        t = jnp.where(c0 >= topk, 0, INT_MIN).astype(jnp.int32)

        def vstep(b, t):
            cand = t | lax.shift_left(jnp.int32(1), 30 - b)
            return jnp.where(count(lambda k: k >= cand) >= topk, cand, t)

        t = lax.fori_loop(0, 31, vstep, t)
        need = topk - count(lambda k: k > t)
        return t, jnp.where(t == INT_MIN, 0, need).astype(F32)

    def no_search():
        return jnp.full((1, tq), INT_MIN, jnp.int32), jnp.zeros((1, tq), F32)

    by_chunks = [functools.partial(search, n) for n in range(1, s_len // ck + 1)]
    thr, need = lax.cond(q0 + tq > topk, lambda: lax.switch(nck - 1, by_chunks), no_search)

    qb = a_ref[0, pl.ds(q0, tq), 0:256].astype(F32)
    qn_all = heads_on_rows(_head_rms(qb, gq_ref[...], hd).astype(BF16))

    def p3(c, st):
        ms, ls, accs, ties_before = st
        k0 = pl.multiple_of(c * ck, ck)
        s_all = _nt(kbn_ref[pl.ds(k0, ck), :], qn_all)
        vtc = vt_ref[c]
        key = keys_ref[pl.ds(k0, ck), :]
        tie = key == thr
        tie_rank = ties_before + _mm(tri_ref[...], jnp.where(tie, 1.0, 0.0).astype(BF16))
        sel = (key > thr) | (tie & (tie_rank <= need))
        ties_before = tie_rank[ck - 1:ck, :]
        bidx = [jnp.clip(i - (c * sub + r), 0, 2) for r in range(sub)]
        heads = range(H_DSA)
        s_h = [s_all[:, h * tq:(h + 1) * tq]
               + jnp.concatenate([bias_ref[bidx[r], h] for r in range(sub)], axis=0) for h in heads]
        m_new = [jnp.maximum(ms[h], jnp.max(jnp.where(sel, s_h[h], NEG), axis=0, keepdims=True)) for h in heads]
        p = [jnp.where(sel, jnp.exp2(s_h[h] - m_new[h]), 0.0) for h in heads]
        alpha = [jnp.exp2(ms[h] - m_new[h]) for h in heads]
        pv = [_mm(vtc, p[h].astype(BF16)) for h in heads]
        nl = [ls[h] * alpha[h] + jnp.sum(p[h], axis=0, keepdims=True) for h in heads]
        na = [accs[h] * alpha[h] + pv[h] for h in heads]
        return tuple(m_new), tuple(nl), tuple(na), ties_before

    init = (tuple(jnp.full((1, tq), NEG, F32) for _ in range(H_DSA)),
            tuple(jnp.zeros((1, tq), F32) for _ in range(H_DSA)),
            tuple(jnp.zeros((hd, tq), F32) for _ in range(H_DSA)),
            jnp.zeros((1, tq), F32))
    _, ls, accs, _ = lax.fori_loop(0, nck, p3, init)
    o_t = jnp.concatenate([accs[h] / ls[h] for h in range(H_DSA)], axis=0)
    o_ref[0] = _nt(_eye(tq, BF16), o_t.astype(BF16)).astype(o_ref.dtype)


def dsa_attention(proj, col_block, gq, gk, bias_t, *, tq=BLOCK, ck=512):
    b, s, _ = proj.shape
    topk = min(TOPK_MAX, s // 4)
    ck = min(ck, s)
    return pl.pallas_call(
        functools.partial(_dsa_kernel, tq=tq, ck=ck, topk=topk),
        grid=(b, s // tq),
        in_specs=[pl.BlockSpec((1, s, 768), lambda bi, i: (bi, 0, col_block)),
                  pl.BlockSpec((1, 256), lambda bi, i: (0, 0)),
                  pl.BlockSpec((1, LANES), lambda bi, i: (0, 0)),
                  pl.BlockSpec((3, H_DSA, tq, tq), lambda bi, i: (0, 0, 0, 0))],
        out_specs=pl.BlockSpec((1, tq, H_DSA * HEAD_DIM), lambda bi, i: (bi, i, 0)),
        out_shape=jax.ShapeDtypeStruct((b, s, H_DSA * HEAD_DIM), BF16),
        scratch_shapes=[pltpu.VMEM((s, HEAD_DIM), BF16),
                        pltpu.VMEM((s // ck, HEAD_DIM, ck), BF16),
                        pltpu.VMEM((s, tq), jnp.int32),
                        pltpu.VMEM((ck, ck), BF16)],
        compiler_params=_params(("parallel", "arbitrary")),
        name="dsa_attention",
    )(proj, gq, gk, bias_t)


def _t5_bucket(rel):
    n = jnp.maximum(rel, 0)
    max_exact = N_BUCKETS // 2
    nf = jnp.maximum(n, 1).astype(F32)
    large = max_exact + (jnp.log(nf / max_exact) / math.log(MAX_DISTANCE / max_exact)
                         * (N_BUCKETS - max_exact)).astype(jnp.int32)
    large = jnp.minimum(large, N_BUCKETS - 1)
    return jnp.where(n < max_exact, n, large)


def _bucket_lookup(tab, rel):
    hit = _t5_bucket(rel)[..., None, None] == jnp.arange(N_BUCKETS)[:, None]
    return jnp.sum(jnp.where(hit, tab.astype(F32), 0.0), axis=-2)


def dsa_bias_tiles(tab, tq=BLOCK):
    ks = jnp.arange(tq)[:, None]
    tl = jnp.arange(tq)[None, :]
    rel = jnp.stack([tl - ks, tq + tl - ks, jnp.full((tq, tq), 2 * tq + MAX_DISTANCE)])
    return _bucket_lookup(tab, rel).transpose(0, 3, 1, 2)


def _swa_kernel(sink_ref, cur_ref, prev_ref, gq_ref, gk_ref, bias_ref, o_ref, *, tq, nb):
    i = pl.program_id(1)
    hd = HEAD_DIM
    g = H_SW // KV_SW
    qn = _head_rms(cur_ref[0, :, 0:512].astype(F32), gq_ref[...], hd).astype(BF16)
    kn = jnp.concatenate([_head_rms(prev_ref[0, :, 512:640].astype(F32), gk_ref[...], hd),
                          _head_rms(cur_ref[0, :, 512:640].astype(F32), gk_ref[...], hd)], axis=0).astype(BF16)
    vv = jnp.concatenate([prev_ref[0, :, 640:768], cur_ref[0, :, 640:768]], axis=0)
    col = lax.broadcasted_iota(jnp.int32, (tq, 2 * tq), 1)
    first = (col >= tq) | (i > 0)
    chains = [(b, h) for b in range(nb) for h in range(H_SW)]
    s = {}
    for b, h in chains:
        kv = h // g
        s_bh = _nt(qn[b * tq:(b + 1) * tq, h * hd:(h + 1) * hd], kn[b * tq:(b + 2) * tq, kv * hd:(kv + 1) * hd])
        s_bh = s_bh + bias_ref[h]
        s[(b, h)] = jnp.where(first, s_bh, NEG) if b == 0 else s_bh
    m = {c: jnp.maximum(jnp.max(s[c], axis=-1, keepdims=True), sink_ref[c[1]]) for c in chains}
    e = {c: jnp.exp2(s[c] - m[c]) for c in chains}
    den = {c: jnp.sum(e[c], axis=-1, keepdims=True) + jnp.exp2(sink_ref[c[1]] - m[c]) for c in chains}
    out = {(b, h): _mm(e[(b, h)].astype(BF16), vv[b * tq:(b + 2) * tq, (h // g) * hd:(h // g + 1) * hd]) / den[(b, h)]
           for b, h in chains}
    for b in range(nb):
        o_ref[0, b * tq:(b + 1) * tq, :] = jnp.concatenate([out[(b, h)] for h in range(H_SW)],
                                                           axis=-1).astype(o_ref.dtype)


def swa_bias_tiles(tab, tq=BLOCK):
    rel = (jnp.arange(tq)[:, None] + tq) - jnp.arange(2 * tq)[None, :]
    in_win = (rel >= 0) & (rel < WINDOW)
    bias = _bucket_lookup(tab, rel).transpose(2, 0, 1)
    return jnp.where(in_win[None], bias, NEG)


def swa_attention(proj, col_block, sinks, gq, gk, bias, *, tq=BLOCK, nb=4):
    b, s, _ = proj.shape
    return pl.pallas_call(
        functools.partial(_swa_kernel, tq=tq, nb=nb),
        grid_spec=pltpu.PrefetchScalarGridSpec(
            num_scalar_prefetch=0,
            grid=(b, s // (nb * tq)),
            in_specs=[pl.BlockSpec(memory_space=pltpu.SMEM),
                      pl.BlockSpec((1, nb * tq, 768), lambda bi, i: (bi, i, col_block)),
                      pl.BlockSpec((1, tq, 768), lambda bi, i: (bi, jnp.maximum(nb * i - 1, 0), col_block)),
                      pl.BlockSpec((1, 512), lambda bi, i: (0, 0)),
                      pl.BlockSpec((1, LANES), lambda bi, i: (0, 0)),
                      pl.BlockSpec((H_SW, tq, 2 * tq), lambda bi, i: (0, 0, 0))],
            out_specs=pl.BlockSpec((1, nb * tq, H_SW * HEAD_DIM), lambda bi, i: (bi, i, 0))),
        out_shape=jax.ShapeDtypeStruct((b, s, H_SW * HEAD_DIM), BF16),
        compiler_params=_params(("parallel", "arbitrary")),
        name="swa_attention",
    )(sinks, proj, proj, gq, gk, bias)


def _merge_kernel(x_ref, g_ref, oa_ref, ob_ref, oc_ref, bg_ref, wa_ref, wb_ref, wc_ref, wo_ref, o_ref):
    d = x_ref.shape[1]
    merged = None
    for k, (o_k, w_k) in enumerate(((oa_ref, wa_ref), (ob_ref, wb_ref), (oc_ref, wc_ref))):
        logit = g_ref[:, k * d:(k + 1) * d].astype(F32) + bg_ref[k:k + 1, :]
        gate = 1.0 / (1.0 + jnp.exp(-logit))
        term = gate * _mm(o_k[...], w_k[...])
        merged = term if merged is None else merged + term
    o_ref[...] = x_ref[...] + _mm(merged.astype(BF16), wo_ref[...])


def merge_project(x2d, proj2d, o_a, o_b, o_c, b_gate, w_pa, w_pb, w_pc, w_out, *, tm=512):
    n, d = x2d.shape
    full = lambda a: pl.BlockSpec(a.shape, lambda i: (0,) * a.ndim)
    row = lambda a: pl.BlockSpec((tm, a.shape[1]), lambda i: (i, 0))
    return pl.pallas_call(
        _merge_kernel,
        grid=(n // tm,),
        in_specs=[row(x2d), pl.BlockSpec((tm, N_BRANCH * d), lambda i: (i, 0)),
                  row(o_a), row(o_b), row(o_c),
                  full(b_gate), full(w_pa), full(w_pb), full(w_pc), full(w_out)],
        out_specs=pl.BlockSpec((tm, d), lambda i: (i, 0)),
        out_shape=jax.ShapeDtypeStruct((n, d), F32),
        compiler_params=_params(("parallel",)),
        name="merge_project",
    )(x2d, proj2d, o_a, o_b, o_c, b_gate, w_pa, w_pb, w_pc, w_out)


def _memkv_kernel(m_ref, g_ref, w_ref, gk_ref, k_ref, v_ref):
    w_x = k_ref.shape[2]
    mn = _rms_rows(m_ref[0], g_ref[...]).astype(BF16)
    kv = _mm(mn, w_ref[...])
    k_ref[0] = _head_rms(kv[:, :w_x], gk_ref[...], XHEAD_DIM).astype(k_ref.dtype)
    v_ref[0] = kv[:, w_x:].astype(v_ref.dtype)


def memory_kv(mem, gain, w_kv, gk):
    b, m, d = mem.shape
    w_x = w_kv.shape[1] // 2
    return pl.pallas_call(
        _memkv_kernel,
        grid=(b,),
        in_specs=[pl.BlockSpec((1, m, d), lambda i: (i, 0, 0)),
                  pl.BlockSpec((1, d), lambda i: (0, 0)),
                  pl.BlockSpec(w_kv.shape, lambda i: (0, 0)),
                  pl.BlockSpec((1, w_x), lambda i: (0, 0))],
        out_specs=[pl.BlockSpec((1, m, w_x), lambda i: (i, 0, 0)),
                   pl.BlockSpec((1, m, w_x), lambda i: (i, 0, 0))],
        out_shape=[jax.ShapeDtypeStruct((b, m, w_x), BF16)] * 2,
        compiler_params=_params(("parallel",)),
        name="memory_kv",
    )(mem, gain, w_kv, gk)


def _xattn_kernel(x_ref, g_ref, wq_ref, gq_ref, k_ref, v_ref, wo_ref, o_ref):
    x = x_ref[0]
    xn = _rms_rows(x, g_ref[...]).astype(BF16)
    q = _head_rms(_mm(xn, wq_ref[...]), gq_ref[...], XHEAD_DIM).astype(BF16)
    sl = [slice(h * XHEAD_DIM, (h + 1) * XHEAD_DIM) for h in range(H_X)]
    s = [_nt(q[:, c], k_ref[0, :, c]) for c in sl]
    e = [jnp.exp2(s_h - jnp.max(s_h, axis=-1, keepdims=True)) for s_h in s]
    den = [jnp.sum(e_h, axis=-1, keepdims=True) for e_h in e]
    outs = [_mm(e[h].astype(BF16), v_ref[0, :, sl[h]]) / den[h] for h in range(H_X)]
    o = jnp.concatenate(outs, axis=-1).astype(BF16)
    o_ref[0] = x + _mm(o, wo_ref[...])


def cross_attention(x, gain, w_q, gq, k_mem, v_mem, w_o, *, tq=512):
    b, s, d = x.shape
    m, w_x = k_mem.shape[1:]
    full = lambda a: pl.BlockSpec(a.shape, lambda bi, i: (0,) * a.ndim)
    return pl.pallas_call(
        _xattn_kernel,
        grid=(b, s // tq),
        in_specs=[pl.BlockSpec((1, tq, d), lambda bi, i: (bi, i, 0)),
                  full(gain), full(w_q), full(gq),
                  pl.BlockSpec((1, m, w_x), lambda bi, i: (bi, 0, 0)),
                  pl.BlockSpec((1, m, w_x), lambda bi, i: (bi, 0, 0)),
                  full(w_o)],
        out_specs=pl.BlockSpec((1, tq, d), lambda bi, i: (bi, i, 0)),
        out_shape=jax.ShapeDtypeStruct((b, s, d), F32),
        compiler_params=_params(("parallel", "parallel")),
        name="cross_attention",
    )(x, gain, w_q, gq, k_mem, v_mem, w_o)


def _router_kernel(x_ref, g_ref, whi_ref, wlo_ref, o_ref):
    hf = _rms_rows(x_ref[...], g_ref[...])
    hi = hf.astype(BF16)
    lo = (hf - hi.astype(F32)).astype(BF16)
    logits = _mm(hi, whi_ref[...]) + (_mm(hi, wlo_ref[...]) + _mm(lo, whi_ref[...]))
    lane = lax.broadcasted_iota(jnp.int32, logits.shape, 1).astype(F32)
    logits = jnp.where(lane < N_EXPERTS, logits, NEG)
    m1 = jnp.max(logits, axis=-1, keepdims=True)
    i1 = jnp.min(jnp.where(logits == m1, lane, float(LANES)), axis=-1, keepdims=True)
    rest = jnp.where(lane == i1, NEG, logits)
    m2 = jnp.max(rest, axis=-1, keepdims=True)
    i2 = jnp.min(jnp.where(rest == m2, lane, float(LANES)), axis=-1, keepdims=True)
    e2 = jnp.exp(m2 - m1)
    den = 1.0 + e2
    o_ref[...] = (jnp.where(lane == 0.0, i1, 0.0) + jnp.where(lane == 1.0, i2, 0.0)
                  + jnp.where(lane == 2.0, 1.0 / den, 0.0) + jnp.where(lane == 3.0, e2 / den, 0.0))


def router_gates(x2d, gain, w_hi, w_lo, *, tm=512):
    n, d = x2d.shape
    return pl.pallas_call(
        _router_kernel,
        grid=(n // tm,),
        in_specs=[pl.BlockSpec((tm, d), lambda i: (i, 0)),
                  pl.BlockSpec((1, d), lambda i: (0, 0)),
                  pl.BlockSpec((d, LANES), lambda i: (0, 0)),
                  pl.BlockSpec((d, LANES), lambda i: (0, 0))],
        out_specs=pl.BlockSpec((tm, LANES), lambda i: (i, 0)),
        out_shape=jax.ShapeDtypeStruct((n, LANES), F32),
        compiler_params=_params(("parallel",)),
        name="router_gates",
    )(x2d, gain, w_hi, w_lo)


def _swiglu_tile(xn, wg, wu, wd):
    gg = _mm(xn, wg)
    uu = _mm(xn, wu)
    act = gg * (1.0 / (1.0 + jnp.exp(-gg))) * uu
    return _mm(act.astype(BF16), wd)


def _ffn_kernel(x_ref, g_ref, wg_ref, wu_ref, wd_ref, o_ref, xn_ref, acc_ref):
    j = pl.program_id(1)

    @pl.when(j == 0)
    def _():
        xn_ref[...] = _rms_rows(x_ref[...], g_ref[...]).astype(BF16)
        acc_ref[...] = jnp.zeros_like(acc_ref)

    acc_ref[...] += _swiglu_tile(xn_ref[...], wg_ref[...], wu_ref[...], wd_ref[...])

    @pl.when(j == pl.num_programs(1) - 1)
    def _():
        o_ref[...] = x_ref[...] + acc_ref[...]


def dense_ffn(x2d, gain, w_gu, w_down, *, tm=1024, tf=256):
    n, d = x2d.shape
    tm = min(tm, n)
    f = w_down.shape[0]
    nf = f // tf
    return pl.pallas_call(
        _ffn_kernel,
        grid=(n // tm, nf),
        in_specs=[pl.BlockSpec((tm, d), lambda i, j: (i, 0)),
                  pl.BlockSpec((1, d), lambda i, j: (0, 0)),
                  pl.BlockSpec((d, tf), lambda i, j: (0, j)),
                  pl.BlockSpec((d, tf), lambda i, j: (0, j + nf)),
                  pl.BlockSpec((tf, d), lambda i, j: (j, 0))],
        out_specs=pl.BlockSpec((tm, d), lambda i, j: (i, 0)),
        out_shape=jax.ShapeDtypeStruct((n, d), F32),
        scratch_shapes=[pltpu.VMEM((tm, d), BF16), pltpu.VMEM((tm, d), F32)],
        compiler_params=_params(("parallel", "arbitrary")),
        name="dense_ffn",
    )(x2d, gain, w_gu, w_gu, w_down)


def _moe_kernel(te_ref, tn_ref, tok_ref, tok_next_ref, dst_prev_ref, x_hbm, g_ref, wg_ref, wu_ref,
                wd_ref, y_hbm, xg_ref, xn_ref, acc_ref, yb_ref, gsem, ssem, *, tm, nf):
    i = pl.program_id(0)
    j = pl.program_id(1)
    nt = pl.num_programs(0)
    slot = i % 2
    per_step = tm // nf
    active = tn_ref[i] > 0
    prev_active = (i > 0) & (tn_ref[jnp.maximum(i - 1, 0)] > 0)
    prev_issued = (i > 0) & (tn_ref[jnp.maximum(i - 2, 0)] > 0)

    def gather_copy(tok, r, s):
        return pltpu.make_async_copy(x_hbm.at[pl.ds(tok, 1), :], xg_ref.at[s, pl.ds(r, 1), :], gsem.at[s])

    def scatter_copy(dst, r, s):
        return pltpu.make_async_copy(yb_ref.at[s, pl.ds(r, 1), :], y_hbm.at[pl.ds(dst, 1), :], ssem.at[s])

    def start_all_rows(ids_ref, s, make_copy):
        def body(r8, c):
            for u in range(8):
                r = r8 * 8 + u
                make_copy(ids_ref[0, 0, r], r, s).start()
            return c
        lax.fori_loop(0, tm // 8, body, 0)

    def wait_gather(s):
        pltpu.make_async_copy(x_hbm.at[pl.ds(0, tm), :], xg_ref.at[s], gsem.at[s]).wait()

    def wait_scatter(s):
        pltpu.make_async_copy(yb_ref.at[s], y_hbm.at[pl.ds(0, tm), :], ssem.at[s]).wait()

    @pl.when(j == 0)
    def _():
        @pl.when(i == 0)
        def _():
            start_all_rows(tok_ref, slot, gather_copy)
            yb_ref[1] = jnp.zeros(yb_ref.shape[1:], yb_ref.dtype)
            n_real = y_hbm.shape[0] - 2 * tm
            for half in range(2):
                spare = pltpu.make_async_copy(yb_ref.at[1], y_hbm.at[pl.ds(n_real + half * tm, tm), :], ssem.at[1])
                spare.start()
                spare.wait()

        @pl.when(active | prev_active)
        def _():
            wait_gather(slot)

        @pl.when(prev_issued)
        def _():
            wait_scatter(slot)

        @pl.when(active)
        def _():
            xn_ref[...] = _rms_rows(xg_ref[slot], g_ref[...]).astype(BF16)
            acc_ref[...] = jnp.zeros_like(acc_ref)

        @pl.when(prev_active & jnp.logical_not(active))
        def _():
            start_all_rows(dst_prev_ref, 1 - slot, scatter_copy)

    @pl.when(active)
    def _():
        acc_ref[...] += _swiglu_tile(xn_ref[...], wg_ref[0], wu_ref[0], wd_ref[0])
        for u in range(per_step):
            r = j * per_step + u
            gather_copy(tok_next_ref[0, 0, r], r, 1 - slot).start(priority=1)
            scatter_copy(dst_prev_ref[0, 0, r], r, 1 - slot).start(priority=1)

        @pl.when(j == nf - 1)
        def _():
            yb_ref[slot] = acc_ref[...]

    @pl.when((j == nf - 1) & (i == nt - 1) & (active | prev_active))
    def _():
        wait_scatter(1 - slot)


def moe_experts(x2d, gain, plan, w_gu, w_down, *, tm, tf=512):
    n, d = x2d.shape
    ne, f, _ = w_down.shape
    nf = f // tf
    assert tm % nf == 0 and tm % 16 == 0
    tile_e, tile_n, row_tok, row_dst_prev = plan
    nt = tile_e.shape[0]
    smem_rows = lambda imap: pl.BlockSpec((1, 1, tm), imap, memory_space=pltpu.SMEM)
    live = lambda j, tn, i: j * jnp.minimum(tn[i], 1)
    return pl.pallas_call(
        functools.partial(_moe_kernel, tm=tm, nf=nf),
        grid_spec=pltpu.PrefetchScalarGridSpec(
            num_scalar_prefetch=2,
            grid=(nt, nf),
            in_specs=[smem_rows(lambda i, j, te, tn: (i, 0, 0)),
                      smem_rows(lambda i, j, te, tn: (jnp.minimum(i + 1, nt - 1), 0, 0)),
                      smem_rows(lambda i, j, te, tn: (i, 0, 0)),
                      pl.BlockSpec(memory_space=pl.ANY),
                      pl.BlockSpec((1, d), lambda i, j, te, tn: (0, 0)),
                      pl.BlockSpec((1, d, tf), lambda i, j, te, tn: (te[i], 0, live(j, tn, i))),
                      pl.BlockSpec((1, d, tf), lambda i, j, te, tn: (te[i], 0, live(j, tn, i) + nf)),
                      pl.BlockSpec((1, tf, d), lambda i, j, te, tn: (te[i], live(j, tn, i), 0))],
            out_specs=pl.BlockSpec(memory_space=pl.ANY),
            scratch_shapes=[pltpu.VMEM((2, tm, d), F32), pltpu.VMEM((tm, d), BF16), pltpu.VMEM((tm, d), F32),
                            pltpu.VMEM((2, tm, d), F32),
                            pltpu.SemaphoreType.DMA((2,)), pltpu.SemaphoreType.DMA((2,))]),
        out_shape=jax.ShapeDtypeStruct((2 * n + 2 * tm, d), F32),
        compiler_params=_params(("arbitrary", "arbitrary")),
        name="moe_experts",
    )(tile_e, tile_n, row_tok, row_tok, row_dst_prev, x2d, gain, w_gu, w_gu, w_down)


def moe_plan(route, *, tm):
    n = route.shape[0]
    flat_e = route[:, :2].astype(jnp.int32).reshape(-1)
    nt = (2 * n) // tm + N_EXPERTS + 1
    order = jnp.argsort(flat_e, stable=True).astype(jnp.int32)
    counts = jnp.sum(flat_e[:, None] == jnp.arange(N_EXPERTS)[None, :], axis=0).astype(jnp.int32)
    off = jnp.cumsum(counts) - counts
    tiles = (counts + tm - 1) // tm
    tile_off = jnp.cumsum(tiles) - tiles
    tile_id = jnp.arange(nt, dtype=jnp.int32)
    used = tile_id < jnp.sum(tiles)
    tile_e = jnp.clip(jnp.sum(tile_id[:, None] >= tile_off[None, :], axis=1) - 1, 0, N_EXPERTS - 1)
    tile_e = jnp.where(used, tile_e, tile_e[jnp.maximum(jnp.sum(tiles) - 1, 0)]).astype(jnp.int32)
    first_row = (tile_id - tile_off[tile_e]) * tm
    tile_n = jnp.where(used, jnp.clip(counts[tile_e] - first_row, 0, tm), 0).astype(jnp.int32)
    r = jnp.arange(tm, dtype=jnp.int32)[None, :]
    valid = r < tile_n[:, None]
    a = order[jnp.clip(off[tile_e][:, None] + first_row[:, None] + r, 0, 2 * n - 1)]
    row_tok = jnp.where(valid, a // 2, 0).astype(jnp.int32).reshape(nt, 1, tm)
    spare = 2 * n + (tile_id[:, None] % 2) * tm + r
    row_dst = jnp.where(valid, (a % 2) * n + a // 2, spare).astype(jnp.int32)
    row_dst_prev = jnp.concatenate([2 * n + tm + r, row_dst[:-1]], axis=0).reshape(nt, 1, tm)
    return tile_e, tile_n, row_tok, row_dst_prev


def _combine_kernel(x_ref, route_ref, y0_ref, y1_ref, o_ref):
    o_ref[...] = x_ref[...] + (route_ref[:, 2:3] * y0_ref[...] + route_ref[:, 3:4] * y1_ref[...])


def moe_combine(x2d, route, y, *, tm=512):
    n, d = x2d.shape
    nb = n // tm
    return pl.pallas_call(
        _combine_kernel,
        grid=(nb,),
        in_specs=[pl.BlockSpec((tm, d), lambda i: (i, 0)),
                  pl.BlockSpec((tm, LANES), lambda i: (i, 0)),
                  pl.BlockSpec((tm, d), lambda i: (i, 0)),
                  pl.BlockSpec((tm, d), lambda i: (i + nb, 0))],
        out_specs=pl.BlockSpec((tm, d), lambda i: (i, 0)),
        out_shape=jax.ShapeDtypeStruct((n, d), F32),
        compiler_params=_params(("parallel",)),
        name="moe_combine",
    )(x2d, route, y, y)


def _pack_w_in(w, d):
    sizes = (256, 256, 256, 256, 64, 64, 256, 64, 4, 512, 128, 128, N_BRANCH * d)
    qa, ka, va, qb, kb, vb, qi, ki, wi, qc, kc, vc, g = jnp.split(w, np.cumsum(sizes)[:-1].tolist(), axis=-1)
    pad = jnp.zeros((w.shape[0], 60), w.dtype)
    return jnp.concatenate([g, qa * (HEAD_DIM ** -0.5 * LOG2E), ka, va,
                            qb, qi * D_IDX ** -0.5, kb, vb, ki, wi, pad,
                            qc, kc, vc], axis=-1).astype(BF16)


def kernel(x, mem, rel_bias, norm_mix, w_in, b_gate, qn_dsa, kn_dsa, qn_swa, kn_swa, sinks, w_pa, w_pb, w_pc, w_out, norm_x, norm_mem, w_xq, w_xkv, w_xo, qn_x, kn_x, norm_ffn, w_gu_dense, w_down_dense, w_router, w_gu_moe, w_down_moe):
    b, s, d = x.shape
    depth = w_in.shape[0]
    n = b * s
    row = lambda v: v.reshape(1, -1).astype(F32)
    bias_dsa = dsa_bias_tiles(rel_bias[:, :H_DSA] * LOG2E)
    bias_swa = swa_bias_tiles(rel_bias[:, H_DSA:] * LOG2E)
    qscale = HEAD_DIM ** -0.5 * LOG2E
    gcol = (N_BRANCH * d) // 768

    x2 = x.reshape(n, d)
    for l in range(depth):
        proj = in_projection(x2, row(norm_mix[l]), _pack_w_in(w_in[l], d))
        proj3 = proj.reshape(b, s, -1)
        o_a = sb_attention(proj3, gcol)
        o_b = dsa_attention(proj3, gcol + 1, row(jnp.tile(qn_dsa[l] * qscale, H_DSA)),
                            row(jnp.concatenate([kn_dsa[l], jnp.zeros_like(kn_dsa[l])])), bias_dsa)
        o_c = swa_attention(proj3, gcol + 2, sinks[l].astype(F32) * LOG2E,
                            row(jnp.tile(qn_swa[l] * qscale, H_SW)), row(jnp.tile(kn_swa[l], KV_SW)), bias_swa)
        x2 = merge_project(x2, proj, o_a.reshape(n, -1), o_b.reshape(n, -1), o_c.reshape(n, -1),
                           b_gate[l].astype(F32), w_pa[l].astype(BF16), w_pb[l].astype(BF16),
                           w_pc[l].astype(BF16), w_out[l].astype(BF16))
        k_mem, v_mem = memory_kv(mem, row(norm_mem[l]), w_xkv[l].astype(BF16), row(jnp.tile(kn_x[l], H_X)))
        x2 = cross_attention(x2.reshape(b, s, d), row(norm_x[l]), w_xq[l].astype(BF16),
                             row(jnp.tile(qn_x[l] * (XHEAD_DIM ** -0.5 * LOG2E), H_X)), k_mem, v_mem,
                             w_xo[l].astype(BF16)).reshape(n, d)
        if l % 2 == 0:
            x2 = dense_ffn(x2, row(norm_ffn[l]), w_gu_dense[l // 2].astype(BF16),
                           w_down_dense[l // 2].astype(BF16), tf=256)
        else:
            wr = jnp.pad(w_router[l // 2].astype(F32), ((0, 0), (0, LANES - N_EXPERTS)))
            wr_hi = wr.astype(BF16)
            wr_lo = (wr - wr_hi.astype(F32)).astype(BF16)
            route = router_gates(x2, row(norm_ffn[l]), wr_hi, wr_lo)
            tm_moe = 672
            y = moe_experts(x2, row(norm_ffn[l]), moe_plan(route, tm=tm_moe), w_gu_moe[l // 2].astype(BF16),
                            w_down_moe[l // 2].astype(BF16), tm=tm_moe)
            x2 = moe_combine(x2, route, y)
    return x2.reshape(b, s, d)
```

```python
import functools
import math

import numpy as np
import jax
import jax.numpy as jnp
from jax import lax
from jax.experimental import pallas as pl
from jax.experimental.pallas import tpu as pltpu

HEAD_DIM = 64
H_SB = 4
H_DSA = 4
H_IDX = 4
D_IDX = 64
TOPK_MAX = 256
H_SW = 8
KV_SW = 2
WINDOW = 128
BLOCK = 128
N_BRANCH = 3
N_BUCKETS = 32
MAX_DISTANCE = 128
H_X = 4
XHEAD_DIM = 128
N_EXPERTS = 8
EPS = 1e-6

LANES = 128
VMEM_LIMIT = 56 * 1024 * 1024
NEG = -1e30
INT_MIN = -(2 ** 31)
LOG2E = math.log2(math.e)

F32 = jnp.float32
BF16 = jnp.bfloat16


def _nt(a, b):
    return lax.dot_general(a, b, (((1,), (1,)), ((), ())), preferred_element_type=F32)


def _mm(a, b):
    return jnp.dot(a, b, preferred_element_type=F32)


def _rms_rows(x, g):
    ms = jnp.mean(x * x, axis=-1, keepdims=True)
    return x * lax.rsqrt(ms + EPS) * g


def _params(sem):
    return pltpu.CompilerParams(dimension_semantics=sem, vmem_limit_bytes=VMEM_LIMIT)


def _sb_kernel(a_ref, o_ref, *, tq):
    i = pl.program_id(1)
    q0 = pl.multiple_of(i * tq, tq)
    hd = HEAD_DIM
    row = lax.broadcasted_iota(jnp.int32, (tq, tq), 0)
    col = lax.broadcasted_iota(jnp.int32, (tq, tq), 1)
    strict = col < row
    u_inc = jnp.where(row >= col, 1.0, 0.0).astype(BF16)
    qs = [a_ref[0, pl.ds(q0, tq), h * hd:(h + 1) * hd] for h in range(H_SB)]

    heads = range(H_SB)

    def block(k0s, accs, carries, diag):
        chains = [(b, h) for b in range(len(k0s)) for h in heads]
        ks = {(b, h): a_ref[0, pl.ds(k0s[b], tq), 256 + h * hd:256 + (h + 1) * hd] for b, h in chains}
        vs = {(b, h): a_ref[0, pl.ds(k0s[b], tq), 512 + h * hd:512 + (h + 1) * hd] for b, h in chains}
        zs = {c: _nt(qs[c[1]], ks[c]) for c in chains}
        lks = {c: -(jnp.maximum(zs[c], 0.0) + jnp.log2(1.0 + jnp.exp2(-jnp.abs(zs[c])))) for c in chains}
        if diag:
            lks = {c: jnp.where(strict, lks[c], 0.0) for c in chains}
        rs = {c: _mm(lks[c].astype(BF16), u_inc) for c in chains}
        carry = {h: (None if diag else carries[h]) for h in heads}
        atts = {}
        for b, h in chains:
            if diag:
                atts[(b, h)] = jnp.where(strict, jnp.exp2(zs[(b, h)] + rs[(b, h)]), 0.0)
                carry[h] = rs[(b, h)][:, 0:1]
            else:
                atts[(b, h)] = jnp.exp2(zs[(b, h)] + rs[(b, h)] + carry[h])
                carry[h] = carry[h] + rs[(b, h)][:, 0:1]
        pvs = {c: _mm(atts[c].astype(BF16), vs[c]) for c in chains}
        new_acc = []
        for h in heads:
            tot = pvs[(0, h)] if diag else accs[h] + pvs[(0, h)]
            for b in range(1, len(k0s)):
                tot = tot + pvs[(b, h)]
            new_acc.append(tot)
        return tuple(new_acc), tuple(carry[h] for h in heads)

    state = block([q0], None, None, True)

    def two_blocks(jj, st):
        near = pl.multiple_of((i - 1 - 2 * jj) * tq, tq)
        far = pl.multiple_of((i - 2 - 2 * jj) * tq, tq)
        return block([near, far], st[0], st[1], False)

    state = lax.fori_loop(0, i // 2, two_blocks, state)
    accs, _ = lax.fori_loop(0, i % 2, lambda jj, st: block([0], st[0], st[1], False), state)
    o_ref[0] = jnp.concatenate(accs, axis=-1).astype(o_ref.dtype)


def sb_attention(proj, col_block, *, tq=256):
    b, s, _ = proj.shape
    return pl.pallas_call(
        functools.partial(_sb_kernel, tq=tq),
        grid=(b, s // tq),
        in_specs=[pl.BlockSpec((1, s, 768), lambda bi, i: (bi, 0, col_block))],
        out_specs=pl.BlockSpec((1, tq, H_SB * HEAD_DIM), lambda bi, i: (bi, i, 0)),
        out_shape=jax.ShapeDtypeStruct((b, s, H_SB * HEAD_DIM), BF16),
        compiler_params=_params(("parallel", "arbitrary")),
        name="sb_attention",
    )(proj)


def _inproj_kernel(x_ref, g_ref, w_ref, o_ref, xn_ref):
    @pl.when(pl.program_id(1) == 0)
    def _():
        xn_ref[...] = _rms_rows(x_ref[...], g_ref[...]).astype(BF16)

    o_ref[...] = _mm(xn_ref[...], w_ref[...]).astype(o_ref.dtype)


def in_projection(x2d, gain, w, *, tm=1024, tn=768):
    n, d = x2d.shape
    tm = min(tm, n)
    c = w.shape[1]
    return pl.pallas_call(
        _inproj_kernel,
        grid=(n // tm, c // tn),
        in_specs=[pl.BlockSpec((tm, d), lambda i, j: (i, 0)),
                  pl.BlockSpec((1, d), lambda i, j: (0, 0)),
                  pl.BlockSpec((d, tn), lambda i, j: (0, j))],
        out_specs=pl.BlockSpec((tm, tn), lambda i, j: (i, j)),
        out_shape=jax.ShapeDtypeStruct((n, c), BF16),
        scratch_shapes=[pltpu.VMEM((tm, d), BF16)],
        compiler_params=_params(("parallel", "arbitrary")),
        name="in_projection",
    )(x2d, gain, w)


def _head_sumsq(x, head_dim):
    r_i = lax.broadcasted_iota(jnp.int32, (LANES, LANES), 0) // head_dim
    c_i = lax.broadcasted_iota(jnp.int32, (LANES, LANES), 1) // head_dim
    bd = jnp.where(r_i == c_i, 1.0, 0.0).astype(F32)
    x2 = x * x
    parts = [_mm(x2[:, b * LANES:(b + 1) * LANES], bd) for b in range(x.shape[1] // LANES)]
    return parts[0] if len(parts) == 1 else jnp.concatenate(parts, axis=-1)


def _head_rms(x, g, head_dim):
    ss = _head_sumsq(x, head_dim)
    return x * lax.rsqrt(ss * (1.0 / head_dim) + EPS) * g


def _eye(n, dtype):
    r = lax.broadcasted_iota(jnp.int32, (n, n), 0)
    c = lax.broadcasted_iota(jnp.int32, (n, n), 1)
    return jnp.where(r == c, 1.0, 0.0).astype(dtype)


def _dsa_kernel(a_ref, gq_ref, gk_ref, bias_ref, o_ref, kbn_ref, vt_ref, keys_ref, tri_ref, *, tq, ck, topk):
    s_len = a_ref.shape[1]
    i = pl.program_id(1)
    q0 = pl.multiple_of(i * tq, tq)
    hd = HEAD_DIM
    sub = ck // tq

    @pl.when(i == 0)
    def _():
        kv = a_ref[0, :, 512:640]
        kvf = kv.astype(F32)
        lane = lax.broadcasted_iota(jnp.int32, (1, LANES), 1)
        ss = _head_sumsq(jnp.where(lane < hd, kvf, 0.0), LANES)
        kn = kvf * lax.rsqrt(ss * (1.0 / hd) + EPS) * gk_ref[...]
        kbn_ref[...] = kn[:, :hd].astype(BF16)
        kvt = _nt(_eye(LANES, BF16), kv)
        for cb in range(s_len // ck):
            vt_ref[cb] = kvt[hd:, cb * ck:(cb + 1) * ck].astype(BF16)
        tri_ref[...] = jnp.where(lax.broadcasted_iota(jnp.int32, (ck, ck), 1)
                                 <= lax.broadcasted_iota(jnp.int32, (ck, ck), 0), 1.0, 0.0).astype(BF16)

    def heads_on_rows(x):
        return jnp.concatenate([x[:, h * hd:(h + 1) * hd] for h in range(x.shape[1] // hd)], axis=0)

    qi_all = heads_on_rows(a_ref[0, pl.ds(q0, tq), 256:512])
    wblk = a_ref[0, pl.ds(q0, tq), 640:768]
    sel_r = lax.broadcasted_iota(jnp.int32, (8, LANES), 0)
    sel_c = lax.broadcasted_iota(jnp.int32, (8, LANES), 1)
    w_t = _nt(jnp.where(sel_c == sel_r + D_IDX, 1.0, 0.0).astype(BF16), wblk)
    qpos = q0 + lax.broadcasted_iota(jnp.int32, (1, tq), 1)
    nck = (q0 + tq + ck - 1) // ck
    row_ck = lax.broadcasted_iota(jnp.int32, (ck, tq), 0)

    qb = a_ref[0, pl.ds(q0, tq), 0:256].astype(F32)
    qn_all = heads_on_rows(_head_rms(qb, gq_ref[...], hd).astype(BF16))

    def index_chunk(c):
        act = jnp.maximum(_nt(a_ref[0, c * ck:(c + 1) * ck, 640:704], qi_all), 0.0)
        sc = w_t[0:1, :] * act[:, 0:tq]
        for h in range(1, H_IDX):
            sc = sc + w_t[h:h + 1, :] * act[:, h * tq:(h + 1) * tq]
        bits = pltpu.bitcast(sc, jnp.int32)
        key = jnp.where(bits < 0, -(bits & 0x7FFFFFFF), bits)
        keys_ref[c * ck:(c + 1) * ck, :] = jnp.where(c * ck + row_ck <= qpos, key, INT_MIN)

    def search(n_chunks):
        def count(pred):
            acc = jnp.zeros((8, tq), jnp.int32)
            for c in range(n_chunks):
                m = jnp.where(pred(keys_ref[c * ck:(c + 1) * ck, :]), 1, 0)
                acc = acc + jnp.sum(m.reshape(ck // 8, 8, tq), axis=0)
            return jnp.sum(acc, axis=0, keepdims=True)

        c0 = count(lambda k: k >= 0)
        t = jnp.where(c0 >= topk, 0, INT_MIN).astype(jnp.int32)

        def vstep(b, t):
            cand = t | lax.shift_left(jnp.int32(1), 30 - b)
            return jnp.where(count(lambda k: k >= cand) >= topk, cand, t)

        t = lax.fori_loop(0, 31, vstep, t)
        need = topk - count(lambda k: k > t)
        return t, jnp.where(t == INT_MIN, 0, need).astype(F32)

    def no_search():
        return jnp.full((1, tq), INT_MIN, jnp.int32), jnp.zeros((1, tq), F32)

    def attend_chunk(c, st, thr, need):
        ms, ls, accs, ties_before = st
        s_all = _nt(kbn_ref[c * ck:(c + 1) * ck, :], qn_all)
        vtc = vt_ref[c]
        key = keys_ref[c * ck:(c + 1) * ck, :]
        tie = key == thr
        tie_rank = ties_before + _mm(tri_ref[...], jnp.where(tie, 1.0, 0.0).astype(BF16))
        sel = (key > thr) | (tie & (tie_rank <= need))
        ties_before = tie_rank[ck - 1:ck, :]
        bidx = [jnp.clip(i - (c * sub + r), 0, 2) for r in range(sub)]
        heads = range(H_DSA)
        s_h = [s_all[:, h * tq:(h + 1) * tq]
               + jnp.concatenate([bias_ref[bidx[r], h] for r in range(sub)], axis=0) for h in heads]
        m_new = [jnp.maximum(ms[h], jnp.max(jnp.where(sel, s_h[h], NEG), axis=0, keepdims=True)) for h in heads]
        p = [jnp.where(sel, jnp.exp2(s_h[h] - m_new[h]), 0.0) for h in heads]
        alpha = [jnp.exp2(ms[h] - m_new[h]) for h in heads]
        pv = [_mm(vtc, p[h].astype(BF16)) for h in heads]
        nl = [ls[h] * alpha[h] + jnp.sum(p[h], axis=0, keepdims=True) for h in heads]
        na = [accs[h] * alpha[h] + pv[h] for h in heads]
        return tuple(m_new), tuple(nl), tuple(na), ties_before

    def query_block(n_chunks):
        for c in range(n_chunks):
            index_chunk(c)
        thr, need = lax.cond(q0 + tq > topk, functools.partial(search, n_chunks), no_search)
        st = (tuple(jnp.full((1, tq), NEG, F32) for _ in range(H_DSA)),
              tuple(jnp.zeros((1, tq), F32) for _ in range(H_DSA)),
              tuple(jnp.zeros((hd, tq), F32) for _ in range(H_DSA)),
              jnp.zeros((1, tq), F32))
        for c in range(n_chunks):
            st = attend_chunk(c, st, thr, need)
        _, ls, accs, _ = st
        o_t = jnp.concatenate([accs[h] / ls[h] for h in range(H_DSA)], axis=0)
        o_ref[0] = _nt(_eye(tq, BF16), o_t.astype(BF16)).astype(o_ref.dtype)
        return 0

    lax.switch(nck - 1, [functools.partial(query_block, n) for n in range(1, s_len // ck + 1)])


def dsa_attention(proj, col_block, gq, gk, bias_t, *, tq=BLOCK, ck=512):
    b, s, _ = proj.shape
    topk = min(TOPK_MAX, s // 4)
    ck = min(ck, s)
    return pl.pallas_call(
        functools.partial(_dsa_kernel, tq=tq, ck=ck, topk=topk),
        grid=(b, s // tq),
        in_specs=[pl.BlockSpec((1, s, 768), lambda bi, i: (bi, 0, col_block)),
                  pl.BlockSpec((1, 256), lambda bi, i: (0, 0)),
                  pl.BlockSpec((1, LANES), lambda bi, i: (0, 0)),
                  pl.BlockSpec((3, H_DSA, tq, tq), lambda bi, i: (0, 0, 0, 0))],
        out_specs=pl.BlockSpec((1, tq, H_DSA * HEAD_DIM), lambda bi, i: (bi, i, 0)),
        out_shape=jax.ShapeDtypeStruct((b, s, H_DSA * HEAD_DIM), BF16),
        scratch_shapes=[pltpu.VMEM((s, HEAD_DIM), BF16),
                        pltpu.VMEM((s // ck, HEAD_DIM, ck), BF16),
                        pltpu.VMEM((s, tq), jnp.int32),
                        pltpu.VMEM((ck, ck), BF16)],
        compiler_params=_params(("parallel", "arbitrary")),
        name="dsa_attention",
    )(proj, gq, gk, bias_t)


def _t5_bucket(rel):
    n = jnp.maximum(rel, 0)
    max_exact = N_BUCKETS // 2
    nf = jnp.maximum(n, 1).astype(F32)
    large = max_exact + (jnp.log(nf / max_exact) / math.log(MAX_DISTANCE / max_exact)
                         * (N_BUCKETS - max_exact)).astype(jnp.int32)
    large = jnp.minimum(large, N_BUCKETS - 1)
    return jnp.where(n < max_exact, n, large)


def _bucket_lookup(tab, rel):
    hit = _t5_bucket(rel)[..., None, None] == jnp.arange(N_BUCKETS)[:, None]
    return jnp.sum(jnp.where(hit, tab.astype(F32), 0.0), axis=-2)


def dsa_bias_tiles(tab, tq=BLOCK):
    ks = jnp.arange(tq)[:, None]
    tl = jnp.arange(tq)[None, :]
    rel = jnp.stack([tl - ks, tq + tl - ks, jnp.full((tq, tq), 2 * tq + MAX_DISTANCE)])
    return _bucket_lookup(tab, rel).transpose(0, 3, 1, 2)


def _swa_kernel(sink_ref, cur_ref, prev_ref, gq_ref, gk_ref, bias_ref, o_ref, *, tq, nb):
    i = pl.program_id(1)
    hd = HEAD_DIM
    g = H_SW // KV_SW
    qn = _head_rms(cur_ref[0, :, 0:512].astype(F32), gq_ref[...], hd).astype(BF16)
    kn = jnp.concatenate([_head_rms(prev_ref[0, :, 512:640].astype(F32), gk_ref[...], hd),
                          _head_rms(cur_ref[0, :, 512:640].astype(F32), gk_ref[...], hd)], axis=0).astype(BF16)
    vv = jnp.concatenate([prev_ref[0, :, 640:768], cur_ref[0, :, 640:768]], axis=0)
    col = lax.broadcasted_iota(jnp.int32, (tq, 2 * tq), 1)
    first = (col >= tq) | (i > 0)
    chains = [(b, h) for b in range(nb) for h in range(H_SW)]
    s = {}
    for b, h in chains:
        kv = h // g
        s_bh = _nt(qn[b * tq:(b + 1) * tq, h * hd:(h + 1) * hd], kn[b * tq:(b + 2) * tq, kv * hd:(kv + 1) * hd])
        s_bh = s_bh + bias_ref[h]
        s[(b, h)] = jnp.where(first, s_bh, NEG) if b == 0 else s_bh
    m = {c: jnp.maximum(jnp.max(s[c], axis=-1, keepdims=True), sink_ref[c[1]]) for c in chains}
    e = {c: jnp.exp2(s[c] - m[c]) for c in chains}
    den = {c: jnp.sum(e[c], axis=-1, keepdims=True) + jnp.exp2(sink_ref[c[1]] - m[c]) for c in chains}
    out = {(b, h): _mm(e[(b, h)].astype(BF16), vv[b * tq:(b + 2) * tq, (h // g) * hd:(h // g + 1) * hd]) / den[(b, h)]
           for b, h in chains}
    for b in range(nb):
        o_ref[0, b * tq:(b + 1) * tq, :] = jnp.concatenate([out[(b, h)] for h in range(H_SW)],
                                                           axis=-1).astype(o_ref.dtype)


def swa_bias_tiles(tab, tq=BLOCK):
    rel = (jnp.arange(tq)[:, None] + tq) - jnp.arange(2 * tq)[None, :]
    in_win = (rel >= 0) & (rel < WINDOW)
    bias = _bucket_lookup(tab, rel).transpose(2, 0, 1)
    return jnp.where(in_win[None], bias, NEG)


def swa_attention(proj, col_block, sinks, gq, gk, bias, *, tq=BLOCK, nb=4):
    b, s, _ = proj.shape
    return pl.pallas_call(
        functools.partial(_swa_kernel, tq=tq, nb=nb),
        grid_spec=pltpu.PrefetchScalarGridSpec(
            num_scalar_prefetch=0,
            grid=(b, s // (nb * tq)),
            in_specs=[pl.BlockSpec(memory_space=pltpu.SMEM),
                      pl.BlockSpec((1, nb * tq, 768), lambda bi, i: (bi, i, col_block)),
                      pl.BlockSpec((1, tq, 768), lambda bi, i: (bi, jnp.maximum(nb * i - 1, 0), col_block)),
                      pl.BlockSpec((1, 512), lambda bi, i: (0, 0)),
                      pl.BlockSpec((1, LANES), lambda bi, i: (0, 0)),
                      pl.BlockSpec((H_SW, tq, 2 * tq), lambda bi, i: (0, 0, 0))],
            out_specs=pl.BlockSpec((1, nb * tq, H_SW * HEAD_DIM), lambda bi, i: (bi, i, 0))),
        out_shape=jax.ShapeDtypeStruct((b, s, H_SW * HEAD_DIM), BF16),
        compiler_params=_params(("parallel", "arbitrary")),
        name="swa_attention",
    )(sinks, proj, proj, gq, gk, bias)


def _merge_kernel(x_ref, g_ref, oa_ref, ob_ref, oc_ref, bg_ref, wa_ref, wb_ref, wc_ref, wo_ref, o_ref):
    d = x_ref.shape[1]
    merged = None
    for k, (o_k, w_k) in enumerate(((oa_ref, wa_ref), (ob_ref, wb_ref), (oc_ref, wc_ref))):
        logit = g_ref[:, k * d:(k + 1) * d].astype(F32) + bg_ref[k:k + 1, :]
        gate = 1.0 / (1.0 + jnp.exp(-logit))
        term = gate * _mm(o_k[...], w_k[...])
        merged = term if merged is None else merged + term
    o_ref[...] = x_ref[...] + _mm(merged.astype(BF16), wo_ref[...])


def merge_project(x2d, proj2d, o_a, o_b, o_c, b_gate, w_pa, w_pb, w_pc, w_out, *, tm=512):
    n, d = x2d.shape
    full = lambda a: pl.BlockSpec(a.shape, lambda i: (0,) * a.ndim)
    row = lambda a: pl.BlockSpec((tm, a.shape[1]), lambda i: (i, 0))
    return pl.pallas_call(
        _merge_kernel,
        grid=(n // tm,),
        in_specs=[row(x2d), pl.BlockSpec((tm, N_BRANCH * d), lambda i: (i, 0)),
                  row(o_a), row(o_b), row(o_c),
                  full(b_gate), full(w_pa), full(w_pb), full(w_pc), full(w_out)],
        out_specs=pl.BlockSpec((tm, d), lambda i: (i, 0)),
        out_shape=jax.ShapeDtypeStruct((n, d), F32),
        compiler_params=_params(("parallel",)),
        name="merge_project",
    )(x2d, proj2d, o_a, o_b, o_c, b_gate, w_pa, w_pb, w_pc, w_out)


def _memkv_kernel(m_ref, g_ref, w_ref, gk_ref, k_ref, v_ref):
    w_x = k_ref.shape[2]
    mn = _rms_rows(m_ref[0], g_ref[...]).astype(BF16)
    kv = _mm(mn, w_ref[...])
    k_ref[0] = _head_rms(kv[:, :w_x], gk_ref[...], XHEAD_DIM).astype(k_ref.dtype)
    v_ref[0] = kv[:, w_x:].astype(v_ref.dtype)


def memory_kv(mem, gain, w_kv, gk):
    b, m, d = mem.shape
    w_x = w_kv.shape[1] // 2
    return pl.pallas_call(
        _memkv_kernel,
        grid=(b,),
        in_specs=[pl.BlockSpec((1, m, d), lambda i: (i, 0, 0)),
                  pl.BlockSpec((1, d), lambda i: (0, 0)),
                  pl.BlockSpec(w_kv.shape, lambda i: (0, 0)),
                  pl.BlockSpec((1, w_x), lambda i: (0, 0))],
        out_specs=[pl.BlockSpec((1, m, w_x), lambda i: (i, 0, 0)),
                   pl.BlockSpec((1, m, w_x), lambda i: (i, 0, 0))],
        out_shape=[jax.ShapeDtypeStruct((b, m, w_x), BF16)] * 2,
        compiler_params=_params(("parallel",)),
        name="memory_kv",
    )(mem, gain, w_kv, gk)


def _xattn_kernel(x_ref, g_ref, wq_ref, gq_ref, k_ref, v_ref, wo_ref, o_ref):
    x = x_ref[0]
    xn = _rms_rows(x, g_ref[...]).astype(BF16)
    q = _head_rms(_mm(xn, wq_ref[...]), gq_ref[...], XHEAD_DIM).astype(BF16)
    sl = [slice(h * XHEAD_DIM, (h + 1) * XHEAD_DIM) for h in range(H_X)]
    s = [_nt(q[:, c], k_ref[0, :, c]) for c in sl]
    e = [jnp.exp2(s_h - jnp.max(s_h, axis=-1, keepdims=True)) for s_h in s]
    den = [jnp.sum(e_h, axis=-1, keepdims=True) for e_h in e]
    outs = [_mm(e[h].astype(BF16), v_ref[0, :, sl[h]]) / den[h] for h in range(H_X)]
    o = jnp.concatenate(outs, axis=-1).astype(BF16)
    o_ref[0] = x + _mm(o, wo_ref[...])


def cross_attention(x, gain, w_q, gq, k_mem, v_mem, w_o, *, tq=512):
    b, s, d = x.shape
    m, w_x = k_mem.shape[1:]
    full = lambda a: pl.BlockSpec(a.shape, lambda bi, i: (0,) * a.ndim)
    return pl.pallas_call(
        _xattn_kernel,
        grid=(b, s // tq),
        in_specs=[pl.BlockSpec((1, tq, d), lambda bi, i: (bi, i, 0)),
                  full(gain), full(w_q), full(gq),
                  pl.BlockSpec((1, m, w_x), lambda bi, i: (bi, 0, 0)),
                  pl.BlockSpec((1, m, w_x), lambda bi, i: (bi, 0, 0)),
                  full(w_o)],
        out_specs=pl.BlockSpec((1, tq, d), lambda bi, i: (bi, i, 0)),
        out_shape=jax.ShapeDtypeStruct((b, s, d), F32),
        compiler_params=_params(("parallel", "parallel")),
        name="cross_attention",
    )(x, gain, w_q, gq, k_mem, v_mem, w_o)


def _router_kernel(x_ref, g_ref, whi_ref, wlo_ref, o_ref):
    hf = _rms_rows(x_ref[...], g_ref[...])
    hi = hf.astype(BF16)
    lo = (hf - hi.astype(F32)).astype(BF16)
    logits = _mm(hi, whi_ref[...]) + (_mm(hi, wlo_ref[...]) + _mm(lo, whi_ref[...]))
    lane = lax.broadcasted_iota(jnp.int32, logits.shape, 1).astype(F32)
    logits = jnp.where(lane < N_EXPERTS, logits, NEG)
    m1 = jnp.max(logits, axis=-1, keepdims=True)
    i1 = jnp.min(jnp.where(logits == m1, lane, float(LANES)), axis=-1, keepdims=True)
    rest = jnp.where(lane == i1, NEG, logits)
    m2 = jnp.max(rest, axis=-1, keepdims=True)
    i2 = jnp.min(jnp.where(rest == m2, lane, float(LANES)), axis=-1, keepdims=True)
    e2 = jnp.exp(m2 - m1)
    den = 1.0 + e2
    o_ref[...] = (jnp.where(lane == 0.0, i1, 0.0) + jnp.where(lane == 1.0, i2, 0.0)
                  + jnp.where(lane == 2.0, 1.0 / den, 0.0) + jnp.where(lane == 3.0, e2 / den, 0.0))


def router_gates(x2d, gain, w_hi, w_lo, *, tm=512):
    n, d = x2d.shape
    return pl.pallas_call(
        _router_kernel,
        grid=(n // tm,),
        in_specs=[pl.BlockSpec((tm, d), lambda i: (i, 0)),
                  pl.BlockSpec((1, d), lambda i: (0, 0)),
                  pl.BlockSpec((d, LANES), lambda i: (0, 0)),
                  pl.BlockSpec((d, LANES), lambda i: (0, 0))],
        out_specs=pl.BlockSpec((tm, LANES), lambda i: (i, 0)),
        out_shape=jax.ShapeDtypeStruct((n, LANES), F32),
        compiler_params=_params(("parallel",)),
        name="router_gates",
    )(x2d, gain, w_hi, w_lo)


def _swiglu_tile(xn, wg, wu, wd):
    gg = _mm(xn, wg)
    uu = _mm(xn, wu)
    act = gg * (1.0 / (1.0 + jnp.exp(-gg))) * uu
    return _mm(act.astype(BF16), wd)


def _ffn_kernel(x_ref, g_ref, wg_ref, wu_ref, wd_ref, o_ref, xn_ref, acc_ref):
    j = pl.program_id(1)

    @pl.when(j == 0)
    def _():
        xn_ref[...] = _rms_rows(x_ref[...], g_ref[...]).astype(BF16)
        acc_ref[...] = jnp.zeros_like(acc_ref)

    acc_ref[...] += _swiglu_tile(xn_ref[...], wg_ref[...], wu_ref[...], wd_ref[...])

    @pl.when(j == pl.num_programs(1) - 1)
    def _():
        o_ref[...] = x_ref[...] + acc_ref[...]


def dense_ffn(x2d, gain, w_gu, w_down, *, tm=1024, tf=256):
    n, d = x2d.shape
    tm = min(tm, n)
    f = w_down.shape[0]
    nf = f // tf
    return pl.pallas_call(
        _ffn_kernel,
        grid=(n // tm, nf),
        in_specs=[pl.BlockSpec((tm, d), lambda i, j: (i, 0)),
                  pl.BlockSpec((1, d), lambda i, j: (0, 0)),
                  pl.BlockSpec((d, tf), lambda i, j: (0, j)),
                  pl.BlockSpec((d, tf), lambda i, j: (0, j + nf)),
                  pl.BlockSpec((tf, d), lambda i, j: (j, 0))],
        out_specs=pl.BlockSpec((tm, d), lambda i, j: (i, 0)),
        out_shape=jax.ShapeDtypeStruct((n, d), F32),
        scratch_shapes=[pltpu.VMEM((tm, d), BF16), pltpu.VMEM((tm, d), F32)],
        compiler_params=_params(("parallel", "arbitrary")),
        name="dense_ffn",
    )(x2d, gain, w_gu, w_gu, w_down)


def _moe_kernel(te_ref, tn_ref, tok_ref, tok_next_ref, dst_prev_ref, x_hbm, g_ref, wg_ref, wu_ref,
                wd_ref, y_hbm, xg_ref, xn_ref, acc_ref, yb_ref, gsem, ssem, *, tm, nf):
    i = pl.program_id(0)
    j = pl.program_id(1)
    nt = pl.num_programs(0)
    slot = i % 2
    per_step = tm // nf
    active = tn_ref[i] > 0
    prev_active = (i > 0) & (tn_ref[jnp.maximum(i - 1, 0)] > 0)
    prev_issued = (i > 0) & (tn_ref[jnp.maximum(i - 2, 0)] > 0)

    def gather_copy(tok, r, s):
        return pltpu.make_async_copy(x_hbm.at[pl.ds(tok, 1), :], xg_ref.at[s, pl.ds(r, 1), :], gsem.at[s])

    def scatter_copy(dst, r, s):
        return pltpu.make_async_copy(yb_ref.at[s, pl.ds(r, 1), :], y_hbm.at[pl.ds(dst, 1), :], ssem.at[s])

    def start_all_rows(ids_ref, s, make_copy):
        def body(r8, c):
            for u in range(8):
                r = r8 * 8 + u
                make_copy(ids_ref[0, 0, r], r, s).start()
            return c
        lax.fori_loop(0, tm // 8, body, 0)

    def wait_gather(s):
        pltpu.make_async_copy(x_hbm.at[pl.ds(0, tm), :], xg_ref.at[s], gsem.at[s]).wait()

    def wait_scatter(s):
        pltpu.make_async_copy(yb_ref.at[s], y_hbm.at[pl.ds(0, tm), :], ssem.at[s]).wait()

    @pl.when(j == 0)
    def _():
        @pl.when(i == 0)
        def _():
            start_all_rows(tok_ref, slot, gather_copy)
            yb_ref[1] = jnp.zeros(yb_ref.shape[1:], yb_ref.dtype)
            n_real = y_hbm.shape[0] - 2 * tm
            for half in range(2):
                spare = pltpu.make_async_copy(yb_ref.at[1], y_hbm.at[pl.ds(n_real + half * tm, tm), :], ssem.at[1])
                spare.start()
                spare.wait()

        @pl.when(active | prev_active)
        def _():
            wait_gather(slot)

        @pl.when(prev_issued)
        def _():
            wait_scatter(slot)

        @pl.when(active)
        def _():
            xn_ref[...] = _rms_rows(xg_ref[slot], g_ref[...]).astype(BF16)
            acc_ref[...] = jnp.zeros_like(acc_ref)

        @pl.when(prev_active & jnp.logical_not(active))
        def _():
            start_all_rows(dst_prev_ref, 1 - slot, scatter_copy)

    @pl.when(active)
    def _():
        acc_ref[...] += _swiglu_tile(xn_ref[...], wg_ref[0], wu_ref[0], wd_ref[0])
        for u in range(per_step):
            r = j * per_step + u
            gather_copy(tok_next_ref[0, 0, r], r, 1 - slot).start(priority=1)
            scatter_copy(dst_prev_ref[0, 0, r], r, 1 - slot).start(priority=1)

        @pl.when(j == nf - 1)
        def _():
            yb_ref[slot] = acc_ref[...]

    @pl.when((j == nf - 1) & (i == nt - 1) & (active | prev_active))
    def _():
        wait_scatter(1 - slot)


def moe_experts(x2d, gain, plan, w_gu, w_down, *, tm, tf=512):
    n, d = x2d.shape
    ne, f, _ = w_down.shape
    nf = f // tf
    assert tm % nf == 0 and tm % 16 == 0
    tile_e, tile_n, row_tok, row_dst_prev = plan
    nt = tile_e.shape[0]
    smem_rows = lambda imap: pl.BlockSpec((1, 1, tm), imap, memory_space=pltpu.SMEM)
    live = lambda j, tn, i: j * jnp.minimum(tn[i], 1)
    return pl.pallas_call(
        functools.partial(_moe_kernel, tm=tm, nf=nf),
        grid_spec=pltpu.PrefetchScalarGridSpec(
            num_scalar_prefetch=2,
            grid=(nt, nf),
            in_specs=[smem_rows(lambda i, j, te, tn: (i, 0, 0)),
                      smem_rows(lambda i, j, te, tn: (jnp.minimum(i + 1, nt - 1), 0, 0)),
                      smem_rows(lambda i, j, te, tn: (i, 0, 0)),
                      pl.BlockSpec(memory_space=pl.ANY),
                      pl.BlockSpec((1, d), lambda i, j, te, tn: (0, 0)),
                      pl.BlockSpec((1, d, tf), lambda i, j, te, tn: (te[i], 0, live(j, tn, i))),
                      pl.BlockSpec((1, d, tf), lambda i, j, te, tn: (te[i], 0, live(j, tn, i) + nf)),
                      pl.BlockSpec((1, tf, d), lambda i, j, te, tn: (te[i], live(j, tn, i), 0))],
            out_specs=pl.BlockSpec(memory_space=pl.ANY),
            scratch_shapes=[pltpu.VMEM((2, tm, d), F32), pltpu.VMEM((tm, d), BF16), pltpu.VMEM((tm, d), F32),
                            pltpu.VMEM((2, tm, d), F32),
                            pltpu.SemaphoreType.DMA((2,)), pltpu.SemaphoreType.DMA((2,))]),
        out_shape=jax.ShapeDtypeStruct((2 * n + 2 * tm, d), F32),
        compiler_params=_params(("arbitrary", "arbitrary")),
        name="moe_experts",
    )(tile_e, tile_n, row_tok, row_tok, row_dst_prev, x2d, gain, w_gu, w_gu, w_down)


def moe_plan(route, *, tm):
    n = route.shape[0]
    flat_e = route[:, :2].astype(jnp.int32).reshape(-1)
    nt = (2 * n) // tm + N_EXPERTS + 1
    order = jnp.argsort(flat_e, stable=True).astype(jnp.int32)
    counts = jnp.sum(flat_e[:, None] == jnp.arange(N_EXPERTS)[None, :], axis=0).astype(jnp.int32)
    off = jnp.cumsum(counts) - counts
    tiles = (counts + tm - 1) // tm
    tile_off = jnp.cumsum(tiles) - tiles
    tile_id = jnp.arange(nt, dtype=jnp.int32)
    used = tile_id < jnp.sum(tiles)
    tile_e = jnp.clip(jnp.sum(tile_id[:, None] >= tile_off[None, :], axis=1) - 1, 0, N_EXPERTS - 1)
    tile_e = jnp.where(used, tile_e, tile_e[jnp.maximum(jnp.sum(tiles) - 1, 0)]).astype(jnp.int32)
    first_row = (tile_id - tile_off[tile_e]) * tm
    tile_n = jnp.where(used, jnp.clip(counts[tile_e] - first_row, 0, tm), 0).astype(jnp.int32)
    r = jnp.arange(tm, dtype=jnp.int32)[None, :]
    valid = r < tile_n[:, None]
    a = order[jnp.clip(off[tile_e][:, None] + first_row[:, None] + r, 0, 2 * n - 1)]
    row_tok = jnp.where(valid, a // 2, 0).astype(jnp.int32).reshape(nt, 1, tm)
    spare = 2 * n + (tile_id[:, None] % 2) * tm + r
    row_dst = jnp.where(valid, (a % 2) * n + a // 2, spare).astype(jnp.int32)
    row_dst_prev = jnp.concatenate([2 * n + tm + r, row_dst[:-1]], axis=0).reshape(nt, 1, tm)
    return tile_e, tile_n, row_tok, row_dst_prev


def _combine_kernel(x_ref, route_ref, y0_ref, y1_ref, o_ref):
    o_ref[...] = x_ref[...] + (route_ref[:, 2:3] * y0_ref[...] + route_ref[:, 3:4] * y1_ref[...])


def moe_combine(x2d, route, y, *, tm=512):
    n, d = x2d.shape
    nb = n // tm
    return pl.pallas_call(
        _combine_kernel,
        grid=(nb,),
        in_specs=[pl.BlockSpec((tm, d), lambda i: (i, 0)),
                  pl.BlockSpec((tm, LANES), lambda i: (i, 0)),
                  pl.BlockSpec((tm, d), lambda i: (i, 0)),
                  pl.BlockSpec((tm, d), lambda i: (i + nb, 0))],
        out_specs=pl.BlockSpec((tm, d), lambda i: (i, 0)),
        out_shape=jax.ShapeDtypeStruct((n, d), F32),
        compiler_params=_params(("parallel",)),
        name="moe_combine",
    )(x2d, route, y, y)


def _pack_w_in(w, d):
    sizes = (256, 256, 256, 256, 64, 64, 256, 64, 4, 512, 128, 128, N_BRANCH * d)
    qa, ka, va, qb, kb, vb, qi, ki, wi, qc, kc, vc, g = jnp.split(w, np.cumsum(sizes)[:-1].tolist(), axis=-1)
    pad = jnp.zeros((w.shape[0], 60), w.dtype)
    return jnp.concatenate([g, qa * (HEAD_DIM ** -0.5 * LOG2E), ka, va,
                            qb, qi * D_IDX ** -0.5, kb, vb, ki, wi, pad,
                            qc, kc, vc], axis=-1).astype(BF16)


def kernel(x, mem, rel_bias, norm_mix, w_in, b_gate, qn_dsa, kn_dsa, qn_swa, kn_swa, sinks, w_pa, w_pb, w_pc, w_out, norm_x, norm_mem, w_xq, w_xkv, w_xo, qn_x, kn_x, norm_ffn, w_gu_dense, w_down_dense, w_router, w_gu_moe, w_down_moe):
    b, s, d = x.shape
    depth = w_in.shape[0]
    n = b * s
    row = lambda v: v.reshape(1, -1).astype(F32)
    bias_dsa = dsa_bias_tiles(rel_bias[:, :H_DSA] * LOG2E)
    bias_swa = swa_bias_tiles(rel_bias[:, H_DSA:] * LOG2E)
    qscale = HEAD_DIM ** -0.5 * LOG2E
    gcol = (N_BRANCH * d) // 768

    x2 = x.reshape(n, d)
    for l in range(depth):
        proj = in_projection(x2, row(norm_mix[l]), _pack_w_in(w_in[l], d))
        proj3 = proj.reshape(b, s, -1)
        o_a = sb_attention(proj3, gcol)
        o_b = dsa_attention(proj3, gcol + 1, row(jnp.tile(qn_dsa[l] * qscale, H_DSA)),
                            row(jnp.concatenate([kn_dsa[l], jnp.zeros_like(kn_dsa[l])])), bias_dsa)
        o_c = swa_attention(proj3, gcol + 2, sinks[l].astype(F32) * LOG2E,
                            row(jnp.tile(qn_swa[l] * qscale, H_SW)), row(jnp.tile(kn_swa[l], KV_SW)), bias_swa)
        x2 = merge_project(x2, proj, o_a.reshape(n, -1), o_b.reshape(n, -1), o_c.reshape(n, -1),
                           b_gate[l].astype(F32), w_pa[l].astype(BF16), w_pb[l].astype(BF16),
                           w_pc[l].astype(BF16), w_out[l].astype(BF16))
        k_mem, v_mem = memory_kv(mem, row(norm_mem[l]), w_xkv[l].astype(BF16), row(jnp.tile(kn_x[l], H_X)))
        x2 = cross_attention(x2.reshape(b, s, d), row(norm_x[l]), w_xq[l].astype(BF16),
                             row(jnp.tile(qn_x[l] * (XHEAD_DIM ** -0.5 * LOG2E), H_X)), k_mem, v_mem,
                             w_xo[l].astype(BF16)).reshape(n, d)
        if l % 2 == 0:
            x2 = dense_ffn(x2, row(norm_ffn[l]), w_gu_dense[l // 2].astype(BF16),
                           w_down_dense[l // 2].astype(BF16), tf=256)
        else:
            wr = jnp.pad(w_router[l // 2].astype(F32), ((0, 0), (0, LANES - N_EXPERTS)))
            wr_hi = wr.astype(BF16)
            wr_lo = (wr - wr_hi.astype(F32)).astype(BF16)
            route = router_gates(x2, row(norm_ffn[l]), wr_hi, wr_lo)
            tm_moe = 672
            y = moe_experts(x2, row(norm_ffn[l]), moe_plan(route, tm=tm_moe), w_gu_moe[l // 2].astype(BF16),
                            w_down_moe[l // 2].astype(BF16), tm=tm_moe)
            x2 = moe_combine(x2, route, y)
    return x2.reshape(b, s, d)
```

```python
import functools
import math

import numpy as np
import jax
import jax.numpy as jnp
from jax import lax
from jax.experimental import pallas as pl
from jax.experimental.pallas import tpu as pltpu

HEAD_DIM = 64
H_SB = 4
H_DSA = 4
H_IDX = 4
D_IDX = 64
TOPK_MAX = 256
H_SW = 8
KV_SW = 2
WINDOW = 128
BLOCK = 128
N_BRANCH = 3
N_BUCKETS = 32
MAX_DISTANCE = 128
H_X = 4
XHEAD_DIM = 128
N_EXPERTS = 8
EPS = 1e-6

LANES = 128
VMEM_LIMIT = 56 * 1024 * 1024
NEG = -1e30
INT_MIN = -(2 ** 31)
LOG2E = math.log2(math.e)

F32 = jnp.float32
BF16 = jnp.bfloat16


def _nt(a, b):
    return lax.dot_general(a, b, (((1,), (1,)), ((), ())), preferred_element_type=F32)


def _mm(a, b):
    return jnp.dot(a, b, preferred_element_type=F32)


def _rms_rows(x, g):
    ms = jnp.mean(x * x, axis=-1, keepdims=True)
    return x * lax.rsqrt(ms + EPS) * g


def _params(sem):
    return pltpu.CompilerParams(dimension_semantics=sem, vmem_limit_bytes=VMEM_LIMIT)


def _sb_kernel(a_ref, o_ref, *, tq):
    i = pl.program_id(1)
    q0 = pl.multiple_of(i * tq, tq)
    hd = HEAD_DIM
    row = lax.broadcasted_iota(jnp.int32, (tq, tq), 0)
    col = lax.broadcasted_iota(jnp.int32, (tq, tq), 1)
    strict = col < row
    u_inc = jnp.where(row >= col, 1.0, 0.0).astype(BF16)
    qs = [a_ref[0, pl.ds(q0, tq), h * hd:(h + 1) * hd] for h in range(H_SB)]

    heads = range(H_SB)

    def block(k0s, accs, carries, diag):
        chains = [(b, h) for b in range(len(k0s)) for h in heads]
        ks = {(b, h): a_ref[0, pl.ds(k0s[b], tq), 256 + h * hd:256 + (h + 1) * hd] for b, h in chains}
        vs = {(b, h): a_ref[0, pl.ds(k0s[b], tq), 512 + h * hd:512 + (h + 1) * hd] for b, h in chains}
        zs = {c: _nt(qs[c[1]], ks[c]) for c in chains}
        lks = {c: -(jnp.maximum(zs[c], 0.0) + jnp.log2(1.0 + jnp.exp2(-jnp.abs(zs[c])))) for c in chains}
        if diag:
            lks = {c: jnp.where(strict, lks[c], 0.0) for c in chains}
        rs = {c: _mm(lks[c].astype(BF16), u_inc) for c in chains}
        carry = {h: (None if diag else carries[h]) for h in heads}
        atts = {}
        for b, h in chains:
            if diag:
                atts[(b, h)] = jnp.where(strict, jnp.exp2(zs[(b, h)] + rs[(b, h)]), 0.0)
                carry[h] = rs[(b, h)][:, 0:1]
            else:
                atts[(b, h)] = jnp.exp2(zs[(b, h)] + rs[(b, h)] + carry[h])
                carry[h] = carry[h] + rs[(b, h)][:, 0:1]
        pvs = {c: _mm(atts[c].astype(BF16), vs[c]) for c in chains}
        new_acc = []
        for h in heads:
            tot = pvs[(0, h)] if diag else accs[h] + pvs[(0, h)]
            for b in range(1, len(k0s)):
                tot = tot + pvs[(b, h)]
            new_acc.append(tot)
        return tuple(new_acc), tuple(carry[h] for h in heads)

    def query_tile(n_before):
        state = block([q0], None, None, True)
        b = n_before - 1
        while b >= 1:
            state = block([b * tq, (b - 1) * tq], state[0], state[1], False)
            b -= 2
        if b == 0:
            state = block([0], state[0], state[1], False)
        o_ref[0] = jnp.concatenate(state[0], axis=-1).astype(o_ref.dtype)
        return 0

    lax.switch(i, [functools.partial(query_tile, n) for n in range(a_ref.shape[1] // tq)])


def sb_attention(proj, col_block, *, tq=256):
    b, s, _ = proj.shape
    return pl.pallas_call(
        functools.partial(_sb_kernel, tq=tq),
        grid=(b, s // tq),
        in_specs=[pl.BlockSpec((1, s, 768), lambda bi, i: (bi, 0, col_block))],
        out_specs=pl.BlockSpec((1, tq, H_SB * HEAD_DIM), lambda bi, i: (bi, i, 0)),
        out_shape=jax.ShapeDtypeStruct((b, s, H_SB * HEAD_DIM), BF16),
        compiler_params=_params(("parallel", "arbitrary")),
        name="sb_attention",
    )(proj)


def _inproj_kernel(x_ref, g_ref, w_ref, o_ref, xn_ref):
    @pl.when(pl.program_id(1) == 0)
    def _():
        xn_ref[...] = _rms_rows(x_ref[...], g_ref[...]).astype(BF16)

    o_ref[...] = _mm(xn_ref[...], w_ref[...]).astype(o_ref.dtype)


def in_projection(x2d, gain, w, *, tm=1024, tn=768):
    n, d = x2d.shape
    tm = min(tm, n)
    c = w.shape[1]
    return pl.pallas_call(
        _inproj_kernel,
        grid=(n // tm, c // tn),
        in_specs=[pl.BlockSpec((tm, d), lambda i, j: (i, 0)),
                  pl.BlockSpec((1, d), lambda i, j: (0, 0)),
                  pl.BlockSpec((d, tn), lambda i, j: (0, j))],
        out_specs=pl.BlockSpec((tm, tn), lambda i, j: (i, j)),
        out_shape=jax.ShapeDtypeStruct((n, c), BF16),
        scratch_shapes=[pltpu.VMEM((tm, d), BF16)],
        compiler_params=_params(("parallel", "arbitrary")),
        name="in_projection",
    )(x2d, gain, w)


def _head_sumsq(x, head_dim):
    r_i = lax.broadcasted_iota(jnp.int32, (LANES, LANES), 0) // head_dim
    c_i = lax.broadcasted_iota(jnp.int32, (LANES, LANES), 1) // head_dim
    bd = jnp.where(r_i == c_i, 1.0, 0.0).astype(F32)
    x2 = x * x
    parts = [_mm(x2[:, b * LANES:(b + 1) * LANES], bd) for b in range(x.shape[1] // LANES)]
    return parts[0] if len(parts) == 1 else jnp.concatenate(parts, axis=-1)


def _head_rms(x, g, head_dim):
    ss = _head_sumsq(x, head_dim)
    return x * lax.rsqrt(ss * (1.0 / head_dim) + EPS) * g


def _eye(n, dtype):
    r = lax.broadcasted_iota(jnp.int32, (n, n), 0)
    c = lax.broadcasted_iota(jnp.int32, (n, n), 1)
    return jnp.where(r == c, 1.0, 0.0).astype(dtype)


def _dsa_kernel(a_ref, gq_ref, gk_ref, bias_ref, o_ref, kbn_ref, vt_ref, keys_ref, tri_ref, *, tq, ck, topk):
    s_len = a_ref.shape[1]
    i = pl.program_id(1)
    q0 = pl.multiple_of(i * tq, tq)
    hd = HEAD_DIM
    sub = ck // tq

    @pl.when(i == 0)
    def _():
        kv = a_ref[0, :, 512:640]
        kvf = kv.astype(F32)
        lane = lax.broadcasted_iota(jnp.int32, (1, LANES), 1)
        ss = _head_sumsq(jnp.where(lane < hd, kvf, 0.0), LANES)
        kn = kvf * lax.rsqrt(ss * (1.0 / hd) + EPS) * gk_ref[...]
        kbn_ref[...] = kn[:, :hd].astype(BF16)
        kvt = _nt(_eye(LANES, BF16), kv)
        for cb in range(s_len // ck):
            vt_ref[cb] = kvt[hd:, cb * ck:(cb + 1) * ck].astype(BF16)
        tri_ref[...] = jnp.where(lax.broadcasted_iota(jnp.int32, (ck, ck), 1)
                                 <= lax.broadcasted_iota(jnp.int32, (ck, ck), 0), 1.0, 0.0).astype(BF16)

    def heads_on_rows(x):
        return jnp.concatenate([x[:, h * hd:(h + 1) * hd] for h in range(x.shape[1] // hd)], axis=0)

    qi_all = heads_on_rows(a_ref[0, pl.ds(q0, tq), 256:512])
    wblk = a_ref[0, pl.ds(q0, tq), 640:768]
    sel_r = lax.broadcasted_iota(jnp.int32, (8, LANES), 0)
    sel_c = lax.broadcasted_iota(jnp.int32, (8, LANES), 1)
    w_t = _nt(jnp.where(sel_c == sel_r + D_IDX, 1.0, 0.0).astype(BF16), wblk)
    qpos = q0 + lax.broadcasted_iota(jnp.int32, (1, tq), 1)
    nck = (q0 + tq + ck - 1) // ck
    row_ck = lax.broadcasted_iota(jnp.int32, (ck, tq), 0)

    qb = a_ref[0, pl.ds(q0, tq), 0:256].astype(F32)
    qn_all = heads_on_rows(_head_rms(qb, gq_ref[...], hd).astype(BF16))

    def index_chunk(c):
        act = jnp.maximum(_nt(a_ref[0, c * ck:(c + 1) * ck, 640:704], qi_all), 0.0)
        sc = w_t[0:1, :] * act[:, 0:tq]
        for h in range(1, H_IDX):
            sc = sc + w_t[h:h + 1, :] * act[:, h * tq:(h + 1) * tq]
        bits = pltpu.bitcast(sc, jnp.int32)
        key = jnp.where(bits < 0, -(bits & 0x7FFFFFFF), bits)
        keys_ref[c * ck:(c + 1) * ck, :] = jnp.where(c * ck + row_ck <= qpos, key, INT_MIN)

    def search(n_chunks):
        def count(pred):
            acc = jnp.zeros((8, tq), jnp.int32)
            for c in range(n_chunks):
                m = jnp.where(pred(keys_ref[c * ck:(c + 1) * ck, :]), 1, 0)
                acc = acc + jnp.sum(m.reshape(ck // 8, 8, tq), axis=0)
            return jnp.sum(acc, axis=0, keepdims=True)

        c0 = count(lambda k: k >= 0)
        t = jnp.where(c0 >= topk, 0, INT_MIN).astype(jnp.int32)

        def vstep(b, t):
            cand = t | lax.shift_left(jnp.int32(1), 30 - b)
            return jnp.where(count(lambda k: k >= cand) >= topk, cand, t)

        t = lax.fori_loop(0, 31, vstep, t)
        need = topk - count(lambda k: k > t)
        return t, jnp.where(t == INT_MIN, 0, need).astype(F32)

    def no_search():
        return jnp.full((1, tq), INT_MIN, jnp.int32), jnp.zeros((1, tq), F32)

    def attend_chunk(c, st, thr, need):
        ms, ls, accs, ties_before = st
        s_all = _nt(kbn_ref[c * ck:(c + 1) * ck, :], qn_all)
        vtc = vt_ref[c]
        key = keys_ref[c * ck:(c + 1) * ck, :]
        tie = key == thr
        tie_rank = ties_before + _mm(tri_ref[...], jnp.where(tie, 1.0, 0.0).astype(BF16))
        sel = (key > thr) | (tie & (tie_rank <= need))
        ties_before = tie_rank[ck - 1:ck, :]
        bidx = [jnp.clip(i - (c * sub + r), 0, 2) for r in range(sub)]
        heads = range(H_DSA)
        s_h = [s_all[:, h * tq:(h + 1) * tq]
               + jnp.concatenate([bias_ref[bidx[r], h] for r in range(sub)], axis=0) for h in heads]
        m_new = [jnp.maximum(ms[h], jnp.max(jnp.where(sel, s_h[h], NEG), axis=0, keepdims=True)) for h in heads]
        p = [jnp.where(sel, jnp.exp2(s_h[h] - m_new[h]), 0.0) for h in heads]
        alpha = [jnp.exp2(ms[h] - m_new[h]) for h in heads]
        pv = [_mm(vtc, p[h].astype(BF16)) for h in heads]
        nl = [ls[h] * alpha[h] + jnp.sum(p[h], axis=0, keepdims=True) for h in heads]
        na = [accs[h] * alpha[h] + pv[h] for h in heads]
        return tuple(m_new), tuple(nl), tuple(na), ties_before

    def query_block(n_chunks):
        for c in range(n_chunks):
            index_chunk(c)
        thr, need = lax.cond(q0 + tq > topk, functools.partial(search, n_chunks), no_search)
        st = (tuple(jnp.full((1, tq), NEG, F32) for _ in range(H_DSA)),
              tuple(jnp.zeros((1, tq), F32) for _ in range(H_DSA)),
              tuple(jnp.zeros((hd, tq), F32) for _ in range(H_DSA)),
              jnp.zeros((1, tq), F32))
        for c in range(n_chunks):
            st = attend_chunk(c, st, thr, need)
        _, ls, accs, _ = st
        o_t = jnp.concatenate([accs[h] / ls[h] for h in range(H_DSA)], axis=0)
        o_ref[0] = _nt(_eye(tq, BF16), o_t.astype(BF16)).astype(o_ref.dtype)
        return 0

    lax.switch(nck - 1, [functools.partial(query_block, n) for n in range(1, s_len // ck + 1)])


def dsa_attention(proj, col_block, gq, gk, bias_t, *, tq=BLOCK, ck=512):
    b, s, _ = proj.shape
    topk = min(TOPK_MAX, s // 4)
    ck = min(ck, s)
    return pl.pallas_call(
        functools.partial(_dsa_kernel, tq=tq, ck=ck, topk=topk),
        grid=(b, s // tq),
        in_specs=[pl.BlockSpec((1, s, 768), lambda bi, i: (bi, 0, col_block)),
                  pl.BlockSpec((1, 256), lambda bi, i: (0, 0)),
                  pl.BlockSpec((1, LANES), lambda bi, i: (0, 0)),
                  pl.BlockSpec((3, H_DSA, tq, tq), lambda bi, i: (0, 0, 0, 0))],
        out_specs=pl.BlockSpec((1, tq, H_DSA * HEAD_DIM), lambda bi, i: (bi, i, 0)),
        out_shape=jax.ShapeDtypeStruct((b, s, H_DSA * HEAD_DIM), BF16),
        scratch_shapes=[pltpu.VMEM((s, HEAD_DIM), BF16),
                        pltpu.VMEM((s // ck, HEAD_DIM, ck), BF16),
                        pltpu.VMEM((s, tq), jnp.int32),
                        pltpu.VMEM((ck, ck), BF16)],
        compiler_params=_params(("parallel", "arbitrary")),
        name="dsa_attention",
    )(proj, gq, gk, bias_t)


def _t5_bucket(rel):
    n = jnp.maximum(rel, 0)
    max_exact = N_BUCKETS // 2
    nf = jnp.maximum(n, 1).astype(F32)
    large = max_exact + (jnp.log(nf / max_exact) / math.log(MAX_DISTANCE / max_exact)
                         * (N_BUCKETS - max_exact)).astype(jnp.int32)
    large = jnp.minimum(large, N_BUCKETS - 1)
    return jnp.where(n < max_exact, n, large)


def _bucket_lookup(tab, rel):
    hit = _t5_bucket(rel)[..., None, None] == jnp.arange(N_BUCKETS)[:, None]
    return jnp.sum(jnp.where(hit, tab.astype(F32), 0.0), axis=-2)


def dsa_bias_tiles(tab, tq=BLOCK):
    ks = jnp.arange(tq)[:, None]
    tl = jnp.arange(tq)[None, :]
    rel = jnp.stack([tl - ks, tq + tl - ks, jnp.full((tq, tq), 2 * tq + MAX_DISTANCE)])
    return _bucket_lookup(tab, rel).transpose(0, 3, 1, 2)


def _swa_kernel(sink_ref, cur_ref, prev_ref, gq_ref, gk_ref, bias_ref, o_ref, *, tq, nb):
    i = pl.program_id(1)
    hd = HEAD_DIM
    g = H_SW // KV_SW
    qn = _head_rms(cur_ref[0, :, 0:512].astype(F32), gq_ref[...], hd).astype(BF16)
    kn = jnp.concatenate([_head_rms(prev_ref[0, :, 512:640].astype(F32), gk_ref[...], hd),
                          _head_rms(cur_ref[0, :, 512:640].astype(F32), gk_ref[...], hd)], axis=0).astype(BF16)
    vv = jnp.concatenate([prev_ref[0, :, 640:768], cur_ref[0, :, 640:768]], axis=0)
    col = lax.broadcasted_iota(jnp.int32, (tq, 2 * tq), 1)
    first = (col >= tq) | (i > 0)
    chains = [(b, h) for b in range(nb) for h in range(H_SW)]
    s = {}
    for b, h in chains:
        kv = h // g
        s_bh = _nt(qn[b * tq:(b + 1) * tq, h * hd:(h + 1) * hd], kn[b * tq:(b + 2) * tq, kv * hd:(kv + 1) * hd])
        s_bh = s_bh + bias_ref[h]
        s[(b, h)] = jnp.where(first, s_bh, NEG) if b == 0 else s_bh
    m = {c: jnp.maximum(jnp.max(s[c], axis=-1, keepdims=True), sink_ref[c[1]]) for c in chains}
    e = {c: jnp.exp2(s[c] - m[c]) for c in chains}
    den = {c: jnp.sum(e[c], axis=-1, keepdims=True) + jnp.exp2(sink_ref[c[1]] - m[c]) for c in chains}
    out = {(b, h): _mm(e[(b, h)].astype(BF16), vv[b * tq:(b + 2) * tq, (h // g) * hd:(h // g + 1) * hd]) / den[(b, h)]
           for b, h in chains}
    for b in range(nb):
        o_ref[0, b * tq:(b + 1) * tq, :] = jnp.concatenate([out[(b, h)] for h in range(H_SW)],
                                                           axis=-1).astype(o_ref.dtype)


def swa_bias_tiles(tab, tq=BLOCK):
    rel = (jnp.arange(tq)[:, None] + tq) - jnp.arange(2 * tq)[None, :]
    in_win = (rel >= 0) & (rel < WINDOW)
    bias = _bucket_lookup(tab, rel).transpose(2, 0, 1)
    return jnp.where(in_win[None], bias, NEG)


def swa_attention(proj, col_block, sinks, gq, gk, bias, *, tq=BLOCK, nb=4):
    b, s, _ = proj.shape
    return pl.pallas_call(
        functools.partial(_swa_kernel, tq=tq, nb=nb),
        grid_spec=pltpu.PrefetchScalarGridSpec(
            num_scalar_prefetch=0,
            grid=(b, s // (nb * tq)),
            in_specs=[pl.BlockSpec(memory_space=pltpu.SMEM),
                      pl.BlockSpec((1, nb * tq, 768), lambda bi, i: (bi, i, col_block)),
                      pl.BlockSpec((1, tq, 768), lambda bi, i: (bi, jnp.maximum(nb * i - 1, 0), col_block)),
                      pl.BlockSpec((1, 512), lambda bi, i: (0, 0)),
                      pl.BlockSpec((1, LANES), lambda bi, i: (0, 0)),
                      pl.BlockSpec((H_SW, tq, 2 * tq), lambda bi, i: (0, 0, 0))],
            out_specs=pl.BlockSpec((1, nb * tq, H_SW * HEAD_DIM), lambda bi, i: (bi, i, 0))),
        out_shape=jax.ShapeDtypeStruct((b, s, H_SW * HEAD_DIM), BF16),
        compiler_params=_params(("parallel", "arbitrary")),
        name="swa_attention",
    )(sinks, proj, proj, gq, gk, bias)


def _merge_kernel(x_ref, g_ref, oa_ref, ob_ref, oc_ref, bg_ref, wa_ref, wb_ref, wc_ref, wo_ref, o_ref):
    d = x_ref.shape[1]
    merged = None
    for k, (o_k, w_k) in enumerate(((oa_ref, wa_ref), (ob_ref, wb_ref), (oc_ref, wc_ref))):
        logit = g_ref[:, k * d:(k + 1) * d].astype(F32) + bg_ref[k:k + 1, :]
        gate = 1.0 / (1.0 + jnp.exp(-logit))
        term = gate * _mm(o_k[...], w_k[...])
        merged = term if merged is None else merged + term
    o_ref[...] = x_ref[...] + _mm(merged.astype(BF16), wo_ref[...])


def merge_project(x2d, proj2d, o_a, o_b, o_c, b_gate, w_pa, w_pb, w_pc, w_out, *, tm=512):
    n, d = x2d.shape
    full = lambda a: pl.BlockSpec(a.shape, lambda i: (0,) * a.ndim)
    row = lambda a: pl.BlockSpec((tm, a.shape[1]), lambda i: (i, 0))
    return pl.pallas_call(
        _merge_kernel,
        grid=(n // tm,),
        in_specs=[row(x2d), pl.BlockSpec((tm, N_BRANCH * d), lambda i: (i, 0)),
                  row(o_a), row(o_b), row(o_c),
                  full(b_gate), full(w_pa), full(w_pb), full(w_pc), full(w_out)],
        out_specs=pl.BlockSpec((tm, d), lambda i: (i, 0)),
        out_shape=jax.ShapeDtypeStruct((n, d), F32),
        compiler_params=_params(("parallel",)),
        name="merge_project",
    )(x2d, proj2d, o_a, o_b, o_c, b_gate, w_pa, w_pb, w_pc, w_out)


def _memkv_kernel(m_ref, g_ref, w_ref, gk_ref, k_ref, v_ref):
    w_x = k_ref.shape[2]
    mn = _rms_rows(m_ref[0], g_ref[...]).astype(BF16)
    kv = _mm(mn, w_ref[...])
    k_ref[0] = _head_rms(kv[:, :w_x], gk_ref[...], XHEAD_DIM).astype(k_ref.dtype)
    v_ref[0] = kv[:, w_x:].astype(v_ref.dtype)


def memory_kv(mem, gain, w_kv, gk):
    b, m, d = mem.shape
    w_x = w_kv.shape[1] // 2
    return pl.pallas_call(
        _memkv_kernel,
        grid=(b,),
        in_specs=[pl.BlockSpec((1, m, d), lambda i: (i, 0, 0)),
                  pl.BlockSpec((1, d), lambda i: (0, 0)),
                  pl.BlockSpec(w_kv.shape, lambda i: (0, 0)),
                  pl.BlockSpec((1, w_x), lambda i: (0, 0))],
        out_specs=[pl.BlockSpec((1, m, w_x), lambda i: (i, 0, 0)),
                   pl.BlockSpec((1, m, w_x), lambda i: (i, 0, 0))],
        out_shape=[jax.ShapeDtypeStruct((b, m, w_x), BF16)] * 2,
        compiler_params=_params(("parallel",)),
        name="memory_kv",
    )(mem, gain, w_kv, gk)


def _xattn_kernel(x_ref, g_ref, wq_ref, gq_ref, k_ref, v_ref, wo_ref, o_ref):
    x = x_ref[0]
    xn = _rms_rows(x, g_ref[...]).astype(BF16)
    q = _head_rms(_mm(xn, wq_ref[...]), gq_ref[...], XHEAD_DIM).astype(BF16)
    sl = [slice(h * XHEAD_DIM, (h + 1) * XHEAD_DIM) for h in range(H_X)]
    s = [_nt(q[:, c], k_ref[0, :, c]) for c in sl]
    e = [jnp.exp2(s_h - jnp.max(s_h, axis=-1, keepdims=True)) for s_h in s]
    den = [jnp.sum(e_h, axis=-1, keepdims=True) for e_h in e]
    outs = [_mm(e[h].astype(BF16), v_ref[0, :, sl[h]]) / den[h] for h in range(H_X)]
    o = jnp.concatenate(outs, axis=-1).astype(BF16)
    o_ref[0] = x + _mm(o, wo_ref[...])


def cross_attention(x, gain, w_q, gq, k_mem, v_mem, w_o, *, tq=512):
    b, s, d = x.shape
    m, w_x = k_mem.shape[1:]
    full = lambda a: pl.BlockSpec(a.shape, lambda bi, i: (0,) * a.ndim)
    return pl.pallas_call(
        _xattn_kernel,
        grid=(b, s // tq),
        in_specs=[pl.BlockSpec((1, tq, d), lambda bi, i: (bi, i, 0)),
                  full(gain), full(w_q), full(gq),
                  pl.BlockSpec((1, m, w_x), lambda bi, i: (bi, 0, 0)),
                  pl.BlockSpec((1, m, w_x), lambda bi, i: (bi, 0, 0)),
                  full(w_o)],
        out_specs=pl.BlockSpec((1, tq, d), lambda bi, i: (bi, i, 0)),
        out_shape=jax.ShapeDtypeStruct((b, s, d), F32),
        compiler_params=_params(("parallel", "parallel")),
        name="cross_attention",
    )(x, gain, w_q, gq, k_mem, v_mem, w_o)


def _router_kernel(x_ref, g_ref, whi_ref, wlo_ref, o_ref):
    hf = _rms_rows(x_ref[...], g_ref[...])
    hi = hf.astype(BF16)
    lo = (hf - hi.astype(F32)).astype(BF16)
    logits = _mm(hi, whi_ref[...]) + (_mm(hi, wlo_ref[...]) + _mm(lo, whi_ref[...]))
    lane = lax.broadcasted_iota(jnp.int32, logits.shape, 1).astype(F32)
    logits = jnp.where(lane < N_EXPERTS, logits, NEG)
    m1 = jnp.max(logits, axis=-1, keepdims=True)
    i1 = jnp.min(jnp.where(logits == m1, lane, float(LANES)), axis=-1, keepdims=True)
    rest = jnp.where(lane == i1, NEG, logits)
    m2 = jnp.max(rest, axis=-1, keepdims=True)
    i2 = jnp.min(jnp.where(rest == m2, lane, float(LANES)), axis=-1, keepdims=True)
    e2 = jnp.exp(m2 - m1)
    den = 1.0 + e2
    o_ref[...] = (jnp.where(lane == 0.0, i1, 0.0) + jnp.where(lane == 1.0, i2, 0.0)
                  + jnp.where(lane == 2.0, 1.0 / den, 0.0) + jnp.where(lane == 3.0, e2 / den, 0.0))


def router_gates(x2d, gain, w_hi, w_lo, *, tm=512):
    n, d = x2d.shape
    return pl.pallas_call(
        _router_kernel,
        grid=(n // tm,),
        in_specs=[pl.BlockSpec((tm, d), lambda i: (i, 0)),
                  pl.BlockSpec((1, d), lambda i: (0, 0)),
                  pl.BlockSpec((d, LANES), lambda i: (0, 0)),
                  pl.BlockSpec((d, LANES), lambda i: (0, 0))],
        out_specs=pl.BlockSpec((tm, LANES), lambda i: (i, 0)),
        out_shape=jax.ShapeDtypeStruct((n, LANES), F32),
        compiler_params=_params(("parallel",)),
        name="router_gates",
    )(x2d, gain, w_hi, w_lo)


def _swiglu_tile(xn, wg, wu, wd):
    gg = _mm(xn, wg)
    uu = _mm(xn, wu)
    act = gg * (1.0 / (1.0 + jnp.exp(-gg))) * uu
    return _mm(act.astype(BF16), wd)


def _ffn_kernel(x_ref, g_ref, wg_ref, wu_ref, wd_ref, o_ref, xn_ref, acc_ref):
    j = pl.program_id(1)

    @pl.when(j == 0)
    def _():
        xn_ref[...] = _rms_rows(x_ref[...], g_ref[...]).astype(BF16)
        acc_ref[...] = jnp.zeros_like(acc_ref)

    acc_ref[...] += _swiglu_tile(xn_ref[...], wg_ref[...], wu_ref[...], wd_ref[...])

    @pl.when(j == pl.num_programs(1) - 1)
    def _():
        o_ref[...] = x_ref[...] + acc_ref[...]


def dense_ffn(x2d, gain, w_gu, w_down, *, tm=1024, tf=256):
    n, d = x2d.shape
    tm = min(tm, n)
    f = w_down.shape[0]
    nf = f // tf
    return pl.pallas_call(
        _ffn_kernel,
        grid=(n // tm, nf),
        in_specs=[pl.BlockSpec((tm, d), lambda i, j: (i, 0)),
                  pl.BlockSpec((1, d), lambda i, j: (0, 0)),
                  pl.BlockSpec((d, tf), lambda i, j: (0, j)),
                  pl.BlockSpec((d, tf), lambda i, j: (0, j + nf)),
                  pl.BlockSpec((tf, d), lambda i, j: (j, 0))],
        out_specs=pl.BlockSpec((tm, d), lambda i, j: (i, 0)),
        out_shape=jax.ShapeDtypeStruct((n, d), F32),
        scratch_shapes=[pltpu.VMEM((tm, d), BF16), pltpu.VMEM((tm, d), F32)],
        compiler_params=_params(("parallel", "arbitrary")),
        name="dense_ffn",
    )(x2d, gain, w_gu, w_gu, w_down)


def _moe_kernel(te_ref, tn_ref, tok_ref, tok_next_ref, dst_prev_ref, x_hbm, g_ref, wg_ref, wu_ref,
                wd_ref, y_hbm, xg_ref, xn_ref, acc_ref, yb_ref, gsem, ssem, *, tm, nf):
    i = pl.program_id(0)
    j = pl.program_id(1)
    nt = pl.num_programs(0)
    slot = i % 2
    per_step = tm // nf
    active = tn_ref[i] > 0
    prev_active = (i > 0) & (tn_ref[jnp.maximum(i - 1, 0)] > 0)
    prev_issued = (i > 0) & (tn_ref[jnp.maximum(i - 2, 0)] > 0)

    def gather_copy(tok, r, s):
        return pltpu.make_async_copy(x_hbm.at[pl.ds(tok, 1), :], xg_ref.at[s, pl.ds(r, 1), :], gsem.at[s])

    def scatter_copy(dst, r, s):
        return pltpu.make_async_copy(yb_ref.at[s, pl.ds(r, 1), :], y_hbm.at[pl.ds(dst, 1), :], ssem.at[s])

    def start_all_rows(ids_ref, s, make_copy):
        def body(r8, c):
            for u in range(8):
                r = r8 * 8 + u
                make_copy(ids_ref[0, 0, r], r, s).start()
            return c
        lax.fori_loop(0, tm // 8, body, 0)

    def wait_gather(s):
        pltpu.make_async_copy(x_hbm.at[pl.ds(0, tm), :], xg_ref.at[s], gsem.at[s]).wait()

    def wait_scatter(s):
        pltpu.make_async_copy(yb_ref.at[s], y_hbm.at[pl.ds(0, tm), :], ssem.at[s]).wait()

    @pl.when(j == 0)
    def _():
        @pl.when(i == 0)
        def _():
            start_all_rows(tok_ref, slot, gather_copy)
            yb_ref[1] = jnp.zeros(yb_ref.shape[1:], yb_ref.dtype)
            n_real = y_hbm.shape[0] - 2 * tm
            for half in range(2):
                spare = pltpu.make_async_copy(yb_ref.at[1], y_hbm.at[pl.ds(n_real + half * tm, tm), :], ssem.at[1])
                spare.start()
                spare.wait()

        @pl.when(active | prev_active)
        def _():
            wait_gather(slot)

        @pl.when(prev_issued)
        def _():
            wait_scatter(slot)

        @pl.when(active)
        def _():
            xn_ref[...] = _rms_rows(xg_ref[slot], g_ref[...]).astype(BF16)
            acc_ref[...] = jnp.zeros_like(acc_ref)

        @pl.when(prev_active & jnp.logical_not(active))
        def _():
            start_all_rows(dst_prev_ref, 1 - slot, scatter_copy)

    @pl.when(active)
    def _():
        acc_ref[...] += _swiglu_tile(xn_ref[...], wg_ref[0], wu_ref[0], wd_ref[0])
        for u in range(per_step):
            r = j * per_step + u
            gather_copy(tok_next_ref[0, 0, r], r, 1 - slot).start(priority=1)
            scatter_copy(dst_prev_ref[0, 0, r], r, 1 - slot).start(priority=1)

        @pl.when(j == nf - 1)
        def _():
            yb_ref[slot] = acc_ref[...]

    @pl.when((j == nf - 1) & (i == nt - 1) & (active | prev_active))
    def _():
        wait_scatter(1 - slot)


def moe_experts(x2d, gain, plan, w_gu, w_down, *, tm, tf=512):
    n, d = x2d.shape
    ne, f, _ = w_down.shape
    nf = f // tf
    assert tm % nf == 0 and tm % 16 == 0
    tile_e, tile_n, row_tok, row_dst_prev = plan
    nt = tile_e.shape[0]
    smem_rows = lambda imap: pl.BlockSpec((1, 1, tm), imap, memory_space=pltpu.SMEM)
    live = lambda j, tn, i: j * jnp.minimum(tn[i], 1)
    return pl.pallas_call(
        functools.partial(_moe_kernel, tm=tm, nf=nf),
        grid_spec=pltpu.PrefetchScalarGridSpec(
            num_scalar_prefetch=2,
            grid=(nt, nf),
            in_specs=[smem_rows(lambda i, j, te, tn: (i, 0, 0)),
                      smem_rows(lambda i, j, te, tn: (jnp.minimum(i + 1, nt - 1), 0, 0)),
                      smem_rows(lambda i, j, te, tn: (i, 0, 0)),
                      pl.BlockSpec(memory_space=pl.ANY),
                      pl.BlockSpec((1, d), lambda i, j, te, tn: (0, 0)),
                      pl.BlockSpec((1, d, tf), lambda i, j, te, tn: (te[i], 0, live(j, tn, i))),
                      pl.BlockSpec((1, d, tf), lambda i, j, te, tn: (te[i], 0, live(j, tn, i) + nf)),
                      pl.BlockSpec((1, tf, d), lambda i, j, te, tn: (te[i], live(j, tn, i), 0))],
            out_specs=pl.BlockSpec(memory_space=pl.ANY),
            scratch_shapes=[pltpu.VMEM((2, tm, d), F32), pltpu.VMEM((tm, d), BF16), pltpu.VMEM((tm, d), F32),
                            pltpu.VMEM((2, tm, d), F32),
                            pltpu.SemaphoreType.DMA((2,)), pltpu.SemaphoreType.DMA((2,))]),
        out_shape=jax.ShapeDtypeStruct((2 * n + 2 * tm, d), F32),
        compiler_params=_params(("arbitrary", "arbitrary")),
        name="moe_experts",
    )(tile_e, tile_n, row_tok, row_tok, row_dst_prev, x2d, gain, w_gu, w_gu, w_down)


def moe_plan(route, *, tm):
    n = route.shape[0]
    flat_e = route[:, :2].astype(jnp.int32).reshape(-1)
    nt = (2 * n) // tm + N_EXPERTS + 1
    order = jnp.argsort(flat_e, stable=True).astype(jnp.int32)
    counts = jnp.sum(flat_e[:, None] == jnp.arange(N_EXPERTS)[None, :], axis=0).astype(jnp.int32)
    off = jnp.cumsum(counts) - counts
    tiles = (counts + tm - 1) // tm
    tile_off = jnp.cumsum(tiles) - tiles
    tile_id = jnp.arange(nt, dtype=jnp.int32)
    used = tile_id < jnp.sum(tiles)
    tile_e = jnp.clip(jnp.sum(tile_id[:, None] >= tile_off[None, :], axis=1) - 1, 0, N_EXPERTS - 1)
    tile_e = jnp.where(used, tile_e, tile_e[jnp.maximum(jnp.sum(tiles) - 1, 0)]).astype(jnp.int32)
    first_row = (tile_id - tile_off[tile_e]) * tm
    tile_n = jnp.where(used, jnp.clip(counts[tile_e] - first_row, 0, tm), 0).astype(jnp.int32)
    r = jnp.arange(tm, dtype=jnp.int32)[None, :]
    valid = r < tile_n[:, None]
    a = order[jnp.clip(off[tile_e][:, None] + first_row[:, None] + r, 0, 2 * n - 1)]
    row_tok = jnp.where(valid, a // 2, 0).astype(jnp.int32).reshape(nt, 1, tm)
    spare = 2 * n + (tile_id[:, None] % 2) * tm + r
    row_dst = jnp.where(valid, (a % 2) * n + a // 2, spare).astype(jnp.int32)
    row_dst_prev = jnp.concatenate([2 * n + tm + r, row_dst[:-1]], axis=0).reshape(nt, 1, tm)
    return tile_e, tile_n, row_tok, row_dst_prev


def _combine_kernel(x_ref, route_ref, y0_ref, y1_ref, o_ref):
    o_ref[...] = x_ref[...] + (route_ref[:, 2:3] * y0_ref[...] + route_ref[:, 3:4] * y1_ref[...])


def moe_combine(x2d, route, y, *, tm=512):
    n, d = x2d.shape
    nb = n // tm
    return pl.pallas_call(
        _combine_kernel,
        grid=(nb,),
        in_specs=[pl.BlockSpec((tm, d), lambda i: (i, 0)),
                  pl.BlockSpec((tm, LANES), lambda i: (i, 0)),
                  pl.BlockSpec((tm, d), lambda i: (i, 0)),
                  pl.BlockSpec((tm, d), lambda i: (i + nb, 0))],
        out_specs=pl.BlockSpec((tm, d), lambda i: (i, 0)),
        out_shape=jax.ShapeDtypeStruct((n, d), F32),
        compiler_params=_params(("parallel",)),
        name="moe_combine",
    )(x2d, route, y, y)


def _pack_w_in(w, d):
    sizes = (256, 256, 256, 256, 64, 64, 256, 64, 4, 512, 128, 128, N_BRANCH * d)
    qa, ka, va, qb, kb, vb, qi, ki, wi, qc, kc, vc, g = jnp.split(w, np.cumsum(sizes)[:-1].tolist(), axis=-1)
    pad = jnp.zeros((w.shape[0], 60), w.dtype)
    return jnp.concatenate([g, qa * (HEAD_DIM ** -0.5 * LOG2E), ka, va,
                            qb, qi * D_IDX ** -0.5, kb, vb, ki, wi, pad,
                            qc, kc, vc], axis=-1).astype(BF16)


def kernel(x, mem, rel_bias, norm_mix, w_in, b_gate, qn_dsa, kn_dsa, qn_swa, kn_swa, sinks, w_pa, w_pb, w_pc, w_out, norm_x, norm_mem, w_xq, w_xkv, w_xo, qn_x, kn_x, norm_ffn, w_gu_dense, w_down_dense, w_router, w_gu_moe, w_down_moe):
    b, s, d = x.shape
    depth = w_in.shape[0]
    n = b * s
    row = lambda v: v.reshape(1, -1).astype(F32)
    bias_dsa = dsa_bias_tiles(rel_bias[:, :H_DSA] * LOG2E)
    bias_swa = swa_bias_tiles(rel_bias[:, H_DSA:] * LOG2E)
    qscale = HEAD_DIM ** -0.5 * LOG2E
    gcol = (N_BRANCH * d) // 768

    x2 = x.reshape(n, d)
    for l in range(depth):
        proj = in_projection(x2, row(norm_mix[l]), _pack_w_in(w_in[l], d))
        proj3 = proj.reshape(b, s, -1)
        o_a = sb_attention(proj3, gcol)
        o_b = dsa_attention(proj3, gcol + 1, row(jnp.tile(qn_dsa[l] * qscale, H_DSA)),
                            row(jnp.concatenate([kn_dsa[l], jnp.zeros_like(kn_dsa[l])])), bias_dsa)
        o_c = swa_attention(proj3, gcol + 2, sinks[l].astype(F32) * LOG2E,
                            row(jnp.tile(qn_swa[l] * qscale, H_SW)), row(jnp.tile(kn_swa[l], KV_SW)), bias_swa)
        x2 = merge_project(x2, proj, o_a.reshape(n, -1), o_b.reshape(n, -1), o_c.reshape(n, -1),
                           b_gate[l].astype(F32), w_pa[l].astype(BF16), w_pb[l].astype(BF16),
                           w_pc[l].astype(BF16), w_out[l].astype(BF16))
        k_mem, v_mem = memory_kv(mem, row(norm_mem[l]), w_xkv[l].astype(BF16), row(jnp.tile(kn_x[l], H_X)))
        x2 = cross_attention(x2.reshape(b, s, d), row(norm_x[l]), w_xq[l].astype(BF16),
                             row(jnp.tile(qn_x[l] * (XHEAD_DIM ** -0.5 * LOG2E), H_X)), k_mem, v_mem,
                             w_xo[l].astype(BF16)).reshape(n, d)
        if l % 2 == 0:
            x2 = dense_ffn(x2, row(norm_ffn[l]), w_gu_dense[l // 2].astype(BF16),
                           w_down_dense[l // 2].astype(BF16), tf=256)
        else:
            wr = jnp.pad(w_router[l // 2].astype(F32), ((0, 0), (0, LANES - N_EXPERTS)))
            wr_hi = wr.astype(BF16)
            wr_lo = (wr - wr_hi.astype(F32)).astype(BF16)
            route = router_gates(x2, row(norm_ffn[l]), wr_hi, wr_lo)
            tm_moe = 672
            y = moe_experts(x2, row(norm_ffn[l]), moe_plan(route, tm=tm_moe), w_gu_moe[l // 2].astype(BF16),
                            w_down_moe[l // 2].astype(BF16), tm=tm_moe)
            x2 = moe_combine(x2, route, y)
    return x2.reshape(b, s, d)
```

```python
import functools
import math

import numpy as np
import jax
import jax.numpy as jnp
from jax import lax
from jax.experimental import pallas as pl
from jax.experimental.pallas import tpu as pltpu

HEAD_DIM = 64
H_SB = 4
H_DSA = 4
H_IDX = 4
D_IDX = 64
TOPK_MAX = 256
H_SW = 8
KV_SW = 2
WINDOW = 128
BLOCK = 128
N_BRANCH = 3
N_BUCKETS = 32
MAX_DISTANCE = 128
H_X = 4
XHEAD_DIM = 128
N_EXPERTS = 8
EPS = 1e-6

LANES = 128
VMEM_LIMIT = 56 * 1024 * 1024
NEG = -1e30
INT_MIN = -(2 ** 31)
LOG2E = math.log2(math.e)

F32 = jnp.float32
BF16 = jnp.bfloat16


def _nt(a, b):
    return lax.dot_general(a, b, (((1,), (1,)), ((), ())), preferred_element_type=F32)


def _mm(a, b):
    return jnp.dot(a, b, preferred_element_type=F32)


def _rms_rows(x, g):
    ms = jnp.mean(x * x, axis=-1, keepdims=True)
    return x * lax.rsqrt(ms + EPS) * g


def _params(sem):
    return pltpu.CompilerParams(dimension_semantics=sem, vmem_limit_bytes=VMEM_LIMIT)


def _sb_kernel(a_ref, o_ref, *, tq):
    i = pl.program_id(1)
    q0 = pl.multiple_of(i * tq, tq)
    hd = HEAD_DIM
    row = lax.broadcasted_iota(jnp.int32, (tq, tq), 0)
    col = lax.broadcasted_iota(jnp.int32, (tq, tq), 1)
    strict = col < row
    u_inc = jnp.where(row >= col, 1.0, 0.0).astype(BF16)
    qs = [a_ref[0, pl.ds(q0, tq), h * hd:(h + 1) * hd] for h in range(H_SB)]

    heads = range(H_SB)

    def block(k0s, accs, carries, diag):
        chains = [(b, h) for b in range(len(k0s)) for h in heads]
        ks = {(b, h): a_ref[0, pl.ds(k0s[b], tq), 256 + h * hd:256 + (h + 1) * hd] for b, h in chains}
        vs = {(b, h): a_ref[0, pl.ds(k0s[b], tq), 512 + h * hd:512 + (h + 1) * hd] for b, h in chains}
        zs = {c: _nt(qs[c[1]], ks[c]) for c in chains}
        lks = {c: -(jnp.maximum(zs[c], 0.0) + jnp.log2(1.0 + jnp.exp2(-jnp.abs(zs[c])))) for c in chains}
        if diag:
            lks = {c: jnp.where(strict, lks[c], 0.0) for c in chains}
        rs = {c: _mm(lks[c].astype(BF16), u_inc) for c in chains}
        carry = {h: (None if diag else carries[h]) for h in heads}
        atts = {}
        for b, h in chains:
            if diag:
                atts[(b, h)] = jnp.where(strict, jnp.exp2(zs[(b, h)] + rs[(b, h)]), 0.0)
                carry[h] = rs[(b, h)][:, 0:1]
            else:
                atts[(b, h)] = jnp.exp2(zs[(b, h)] + rs[(b, h)] + carry[h])
                carry[h] = carry[h] + rs[(b, h)][:, 0:1]
        pvs = {c: _mm(atts[c].astype(BF16), vs[c]) for c in chains}
        new_acc = []
        for h in heads:
            tot = pvs[(0, h)] if diag else accs[h] + pvs[(0, h)]
            for b in range(1, len(k0s)):
                tot = tot + pvs[(b, h)]
            new_acc.append(tot)
        return tuple(new_acc), tuple(carry[h] for h in heads)

    def query_tile(n_before):
        state = block([q0], None, None, True)
        b = n_before - 1
        while b >= 1:
            state = block([b * tq, (b - 1) * tq], state[0], state[1], False)
            b -= 2
        if b == 0:
            state = block([0], state[0], state[1], False)
        o_ref[0] = jnp.concatenate(state[0], axis=-1).astype(o_ref.dtype)
        return 0

    lax.switch(i, [functools.partial(query_tile, n) for n in range(a_ref.shape[1] // tq)])


def sb_attention(proj, col_block, *, tq=256):
    b, s, _ = proj.shape
    return pl.pallas_call(
        functools.partial(_sb_kernel, tq=tq),
        grid=(b, s // tq),
        in_specs=[pl.BlockSpec((1, s, 768), lambda bi, i: (bi, 0, col_block))],
        out_specs=pl.BlockSpec((1, tq, H_SB * HEAD_DIM), lambda bi, i: (bi, i, 0)),
        out_shape=jax.ShapeDtypeStruct((b, s, H_SB * HEAD_DIM), BF16),
        compiler_params=_params(("parallel", "arbitrary")),
        name="sb_attention",
    )(proj)


def _inproj_kernel(x_ref, g_ref, w_ref, o_ref, xn_ref):
    @pl.when(pl.program_id(1) == 0)
    def _():
        xn_ref[...] = _rms_rows(x_ref[...], g_ref[...]).astype(BF16)

    o_ref[...] = _mm(xn_ref[...], w_ref[...]).astype(o_ref.dtype)


def in_projection(x2d, gain, w, *, tm=1024, tn=768):
    n, d = x2d.shape
    tm = min(tm, n)
    c = w.shape[1]
    return pl.pallas_call(
        _inproj_kernel,
        grid=(n // tm, c // tn),
        in_specs=[pl.BlockSpec((tm, d), lambda i, j: (i, 0)),
                  pl.BlockSpec((1, d), lambda i, j: (0, 0)),
                  pl.BlockSpec((d, tn), lambda i, j: (0, j))],
        out_specs=pl.BlockSpec((tm, tn), lambda i, j: (i, j)),
        out_shape=jax.ShapeDtypeStruct((n, c), BF16),
        scratch_shapes=[pltpu.VMEM((tm, d), BF16)],
        compiler_params=_params(("parallel", "arbitrary")),
        name="in_projection",
    )(x2d, gain, w)


def _head_sumsq(x, head_dim):
    r_i = lax.broadcasted_iota(jnp.int32, (LANES, LANES), 0) // head_dim
    c_i = lax.broadcasted_iota(jnp.int32, (LANES, LANES), 1) // head_dim
    bd = jnp.where(r_i == c_i, 1.0, 0.0).astype(F32)
    x2 = x * x
    parts = [_mm(x2[:, b * LANES:(b + 1) * LANES], bd) for b in range(x.shape[1] // LANES)]
    return parts[0] if len(parts) == 1 else jnp.concatenate(parts, axis=-1)


def _head_rms(x, g, head_dim):
    ss = _head_sumsq(x, head_dim)
    return x * lax.rsqrt(ss * (1.0 / head_dim) + EPS) * g


def _eye(n, dtype):
    r = lax.broadcasted_iota(jnp.int32, (n, n), 0)
    c = lax.broadcasted_iota(jnp.int32, (n, n), 1)
    return jnp.where(r == c, 1.0, 0.0).astype(dtype)


def _dsa_kernel(a_ref, gq_ref, gk_ref, bias_ref, o_ref, kbn_ref, vt_ref, keys_ref, tri_ref, *, tq, ck, topk):
    s_len = a_ref.shape[1]
    i = pl.program_id(1)
    q0 = pl.multiple_of(i * tq, tq)
    hd = HEAD_DIM
    sub = ck // tq

    @pl.when(i == 0)
    def _():
        kv = a_ref[0, :, 512:640]
        kvf = kv.astype(F32)
        lane = lax.broadcasted_iota(jnp.int32, (1, LANES), 1)
        ss = _head_sumsq(jnp.where(lane < hd, kvf, 0.0), LANES)
        kn = kvf * lax.rsqrt(ss * (1.0 / hd) + EPS) * gk_ref[...]
        kbn_ref[...] = kn[:, :hd].astype(BF16)
        kvt = _nt(_eye(LANES, BF16), kv)
        for cb in range(s_len // ck):
            vt_ref[cb] = kvt[hd:, cb * ck:(cb + 1) * ck].astype(BF16)
        tri_ref[...] = jnp.where(lax.broadcasted_iota(jnp.int32, (ck, ck), 1)
                                 <= lax.broadcasted_iota(jnp.int32, (ck, ck), 0), 1.0, 0.0).astype(BF16)

    def heads_on_rows(x):
        return jnp.concatenate([x[:, h * hd:(h + 1) * hd] for h in range(x.shape[1] // hd)], axis=0)

    qi_all = heads_on_rows(a_ref[0, pl.ds(q0, tq), 256:512])
    wblk = a_ref[0, pl.ds(q0, tq), 640:768]
    sel_r = lax.broadcasted_iota(jnp.int32, (8, LANES), 0)
    sel_c = lax.broadcasted_iota(jnp.int32, (8, LANES), 1)
    w_t = _nt(jnp.where(sel_c == sel_r + D_IDX, 1.0, 0.0).astype(BF16), wblk)
    qpos = q0 + lax.broadcasted_iota(jnp.int32, (1, tq), 1)
    nck = (q0 + tq + ck - 1) // ck
    row_ck = lax.broadcasted_iota(jnp.int32, (ck, tq), 0)

    qb = a_ref[0, pl.ds(q0, tq), 0:256].astype(F32)
    qn_all = heads_on_rows(_head_rms(qb, gq_ref[...], hd).astype(BF16))

    def index_chunk(c):
        act = jnp.maximum(_nt(a_ref[0, c * ck:(c + 1) * ck, 640:704], qi_all), 0.0)
        sc = w_t[0:1, :] * act[:, 0:tq]
        for h in range(1, H_IDX):
            sc = sc + w_t[h:h + 1, :] * act[:, h * tq:(h + 1) * tq]
        bits = pltpu.bitcast(sc, jnp.int32)
        key = jnp.where(bits < 0, -(bits & 0x7FFFFFFF), bits)
        keys_ref[c * ck:(c + 1) * ck, :] = jnp.where(c * ck + row_ck <= qpos, key, INT_MIN)

    def search(n_chunks):
        def count(pred):
            acc = jnp.zeros((8, tq), jnp.int32)
            for c in range(n_chunks):
                m = jnp.where(pred(keys_ref[c * ck:(c + 1) * ck, :]), 1, 0)
                acc = acc + jnp.sum(m.reshape(ck // 8, 8, tq), axis=0)
            return jnp.sum(acc, axis=0, keepdims=True)

        c0 = count(lambda k: k >= 0)
        t = jnp.where(c0 >= topk, 0, INT_MIN).astype(jnp.int32)

        def vstep(b, t):
            cand = t | lax.shift_left(jnp.int32(1), 30 - b)
            return jnp.where(count(lambda k: k >= cand) >= topk, cand, t)

        t = lax.fori_loop(0, 31, vstep, t)
        need = topk - count(lambda k: k > t)
        return t, jnp.where(t == INT_MIN, 0, need).astype(F32)

    def no_search():
        return jnp.full((1, tq), INT_MIN, jnp.int32), jnp.zeros((1, tq), F32)

    def attend_chunk(c, st, thr, need):
        ms, ls, accs, ties_before = st
        s_all = _nt(kbn_ref[c * ck:(c + 1) * ck, :], qn_all)
        vtc = vt_ref[c]
        key = keys_ref[c * ck:(c + 1) * ck, :]
        tie = key == thr
        tie_rank = ties_before + _mm(tri_ref[...], jnp.where(tie, 1.0, 0.0).astype(BF16))
        sel = (key > thr) | (tie & (tie_rank <= need))
        ties_before = tie_rank[ck - 1:ck, :]
        bidx = [jnp.clip(i - (c * sub + r), 0, 2) for r in range(sub)]
        heads = range(H_DSA)
        s_h = [s_all[:, h * tq:(h + 1) * tq]
               + jnp.concatenate([bias_ref[bidx[r], h] for r in range(sub)], axis=0) for h in heads]
        m_new = [jnp.maximum(ms[h], jnp.max(jnp.where(sel, s_h[h], NEG), axis=0, keepdims=True)) for h in heads]
        p = [jnp.where(sel, jnp.exp2(s_h[h] - m_new[h]), 0.0) for h in heads]
        alpha = [jnp.exp2(ms[h] - m_new[h]) for h in heads]
        pv = [_mm(vtc, p[h].astype(BF16)) for h in heads]
        nl = [ls[h] * alpha[h] + jnp.sum(p[h], axis=0, keepdims=True) for h in heads]
        na = [accs[h] * alpha[h] + pv[h] for h in heads]
        return tuple(m_new), tuple(nl), tuple(na), ties_before

    def query_block(n_chunks):
        for c in range(n_chunks):
            index_chunk(c)
        thr, need = lax.cond(q0 + tq > topk, functools.partial(search, n_chunks), no_search)
        st = (tuple(jnp.full((1, tq), NEG, F32) for _ in range(H_DSA)),
              tuple(jnp.zeros((1, tq), F32) for _ in range(H_DSA)),
              tuple(jnp.zeros((hd, tq), F32) for _ in range(H_DSA)),
              jnp.zeros((1, tq), F32))
        for c in range(n_chunks):
            st = attend_chunk(c, st, thr, need)
        _, ls, accs, _ = st
        o_t = jnp.concatenate([accs[h] / ls[h] for h in range(H_DSA)], axis=0)
        o_ref[0] = _nt(_eye(tq, BF16), o_t.astype(BF16)).astype(o_ref.dtype)
        return 0

    lax.switch(nck - 1, [functools.partial(query_block, n) for n in range(1, s_len // ck + 1)])


def dsa_attention(proj, col_block, gq, gk, bias_t, *, tq=BLOCK, ck=512):
    b, s, _ = proj.shape
    topk = min(TOPK_MAX, s // 4)
    ck = min(ck, s)
    assert tq >= MAX_DISTANCE and s % ck == 0 and ck % tq == 0
    return pl.pallas_call(
        functools.partial(_dsa_kernel, tq=tq, ck=ck, topk=topk),
        grid=(b, s // tq),
        in_specs=[pl.BlockSpec((1, s, 768), lambda bi, i: (bi, 0, col_block)),
                  pl.BlockSpec((1, 256), lambda bi, i: (0, 0)),
                  pl.BlockSpec((1, LANES), lambda bi, i: (0, 0)),
                  pl.BlockSpec((3, H_DSA, tq, tq), lambda bi, i: (0, 0, 0, 0))],
        out_specs=pl.BlockSpec((1, tq, H_DSA * HEAD_DIM), lambda bi, i: (bi, i, 0)),
        out_shape=jax.ShapeDtypeStruct((b, s, H_DSA * HEAD_DIM), BF16),
        scratch_shapes=[pltpu.VMEM((s, HEAD_DIM), BF16),
                        pltpu.VMEM((s // ck, HEAD_DIM, ck), BF16),
                        pltpu.VMEM((s, tq), jnp.int32),
                        pltpu.VMEM((ck, ck), BF16)],
        compiler_params=_params(("parallel", "arbitrary")),
        name="dsa_attention",
    )(proj, gq, gk, bias_t)


def _t5_bucket(rel):
    n = jnp.maximum(rel, 0)
    max_exact = N_BUCKETS // 2
    nf = jnp.maximum(n, 1).astype(F32)
    large = max_exact + (jnp.log(nf / max_exact) / math.log(MAX_DISTANCE / max_exact)
                         * (N_BUCKETS - max_exact)).astype(jnp.int32)
    large = jnp.minimum(large, N_BUCKETS - 1)
    return jnp.where(n < max_exact, n, large)


def _bucket_lookup(tab, rel):
    hit = _t5_bucket(rel)[..., None, None] == jnp.arange(N_BUCKETS)[:, None]
    return jnp.sum(jnp.where(hit, tab.astype(F32), 0.0), axis=-2)


def dsa_bias_tiles(tab, tq=BLOCK):
    ks = jnp.arange(tq)[:, None]
    tl = jnp.arange(tq)[None, :]
    rel = jnp.stack([tl - ks, tq + tl - ks, jnp.full((tq, tq), 2 * tq + MAX_DISTANCE)])
    return _bucket_lookup(tab, rel).transpose(0, 3, 1, 2)


def _swa_kernel(sink_ref, cur_ref, prev_ref, gq_ref, gk_ref, bias_ref, o_ref, *, tq, nb):
    i = pl.program_id(1)
    hd = HEAD_DIM
    g = H_SW // KV_SW
    qn = _head_rms(cur_ref[0, :, 0:512].astype(F32), gq_ref[...], hd).astype(BF16)
    kn = jnp.concatenate([_head_rms(prev_ref[0, :, 512:640].astype(F32), gk_ref[...], hd),
                          _head_rms(cur_ref[0, :, 512:640].astype(F32), gk_ref[...], hd)], axis=0).astype(BF16)
    vv = jnp.concatenate([prev_ref[0, :, 640:768], cur_ref[0, :, 640:768]], axis=0)
    col = lax.broadcasted_iota(jnp.int32, (tq, 2 * tq), 1)
    first = (col >= tq) | (i > 0)
    chains = [(b, h) for b in range(nb) for h in range(H_SW)]
    s = {}
    for b, h in chains:
        kv = h // g
        s_bh = _nt(qn[b * tq:(b + 1) * tq, h * hd:(h + 1) * hd], kn[b * tq:(b + 2) * tq, kv * hd:(kv + 1) * hd])
        s_bh = s_bh + bias_ref[h]
        s[(b, h)] = jnp.where(first, s_bh, NEG) if b == 0 else s_bh
    m = {c: jnp.maximum(jnp.max(s[c], axis=-1, keepdims=True), sink_ref[c[1]]) for c in chains}
    e = {c: jnp.exp2(s[c] - m[c]) for c in chains}
    den = {c: jnp.sum(e[c], axis=-1, keepdims=True) + jnp.exp2(sink_ref[c[1]] - m[c]) for c in chains}
    out = {(b, h): _mm(e[(b, h)].astype(BF16), vv[b * tq:(b + 2) * tq, (h // g) * hd:(h // g + 1) * hd]) / den[(b, h)]
           for b, h in chains}
    for b in range(nb):
        o_ref[0, b * tq:(b + 1) * tq, :] = jnp.concatenate([out[(b, h)] for h in range(H_SW)],
                                                           axis=-1).astype(o_ref.dtype)


def swa_bias_tiles(tab, tq=BLOCK):
    rel = (jnp.arange(tq)[:, None] + tq) - jnp.arange(2 * tq)[None, :]
    in_win = (rel >= 0) & (rel < WINDOW)
    bias = _bucket_lookup(tab, rel).transpose(2, 0, 1)
    return jnp.where(in_win[None], bias, NEG)


def swa_attention(proj, col_block, sinks, gq, gk, bias, *, tq=BLOCK, nb=4):
    b, s, _ = proj.shape
    assert WINDOW <= tq and s % (nb * tq) == 0
    return pl.pallas_call(
        functools.partial(_swa_kernel, tq=tq, nb=nb),
        grid_spec=pltpu.PrefetchScalarGridSpec(
            num_scalar_prefetch=0,
            grid=(b, s // (nb * tq)),
            in_specs=[pl.BlockSpec(memory_space=pltpu.SMEM),
                      pl.BlockSpec((1, nb * tq, 768), lambda bi, i: (bi, i, col_block)),
                      pl.BlockSpec((1, tq, 768), lambda bi, i: (bi, jnp.maximum(nb * i - 1, 0), col_block)),
                      pl.BlockSpec((1, 512), lambda bi, i: (0, 0)),
                      pl.BlockSpec((1, LANES), lambda bi, i: (0, 0)),
                      pl.BlockSpec((H_SW, tq, 2 * tq), lambda bi, i: (0, 0, 0))],
            out_specs=pl.BlockSpec((1, nb * tq, H_SW * HEAD_DIM), lambda bi, i: (bi, i, 0))),
        out_shape=jax.ShapeDtypeStruct((b, s, H_SW * HEAD_DIM), BF16),
        compiler_params=_params(("parallel", "arbitrary")),
        name="swa_attention",
    )(sinks, proj, proj, gq, gk, bias)


def _merge_kernel(x_ref, gn_ref, wg_ref, oa_ref, ob_ref, oc_ref, bg_ref, wa_ref, wb_ref, wc_ref, wo_ref, o_ref):
    d = x_ref.shape[1]
    x = x_ref[...]
    xn = _rms_rows(x, gn_ref[...]).astype(BF16)
    merged = None
    for k, (o_k, w_k) in enumerate(((oa_ref, wa_ref), (ob_ref, wb_ref), (oc_ref, wc_ref))):
        logit = _mm(xn, wg_ref[:, k * d:(k + 1) * d]) + bg_ref[k:k + 1, :]
        gate = 1.0 / (1.0 + jnp.exp(-logit))
        term = gate * _mm(o_k[...], w_k[...])
        merged = term if merged is None else merged + term
    o_ref[...] = x + _mm(merged.astype(BF16), wo_ref[...])


def merge_project(x2d, gain, w_gate, o_a, o_b, o_c, b_gate, w_pa, w_pb, w_pc, w_out, *, tm=512):
    n, d = x2d.shape
    full = lambda a: pl.BlockSpec(a.shape, lambda i: (0,) * a.ndim)
    row = lambda a: pl.BlockSpec((tm, a.shape[1]), lambda i: (i, 0))
    return pl.pallas_call(
        _merge_kernel,
        grid=(n // tm,),
        in_specs=[row(x2d), full(gain), full(w_gate), row(o_a), row(o_b), row(o_c),
                  full(b_gate), full(w_pa), full(w_pb), full(w_pc), full(w_out)],
        out_specs=pl.BlockSpec((tm, d), lambda i: (i, 0)),
        out_shape=jax.ShapeDtypeStruct((n, d), F32),
        compiler_params=_params(("parallel",)),
        name="merge_project",
    )(x2d, gain, w_gate, o_a, o_b, o_c, b_gate, w_pa, w_pb, w_pc, w_out)


def _memkv_kernel(m_ref, g_ref, w_ref, gk_ref, k_ref, v_ref):
    w_x = k_ref.shape[2]
    mn = _rms_rows(m_ref[0], g_ref[...]).astype(BF16)
    kv = _mm(mn, w_ref[...])
    k_ref[0] = _head_rms(kv[:, :w_x], gk_ref[...], XHEAD_DIM).astype(k_ref.dtype)
    v_ref[0] = kv[:, w_x:].astype(v_ref.dtype)


def memory_kv(mem, gain, w_kv, gk):
    b, m, d = mem.shape
    w_x = w_kv.shape[1] // 2
    return pl.pallas_call(
        _memkv_kernel,
        grid=(b,),
        in_specs=[pl.BlockSpec((1, m, d), lambda i: (i, 0, 0)),
                  pl.BlockSpec((1, d), lambda i: (0, 0)),
                  pl.BlockSpec(w_kv.shape, lambda i: (0, 0)),
                  pl.BlockSpec((1, w_x), lambda i: (0, 0))],
        out_specs=[pl.BlockSpec((1, m, w_x), lambda i: (i, 0, 0)),
                   pl.BlockSpec((1, m, w_x), lambda i: (i, 0, 0))],
        out_shape=[jax.ShapeDtypeStruct((b, m, w_x), BF16)] * 2,
        compiler_params=_params(("parallel",)),
        name="memory_kv",
    )(mem, gain, w_kv, gk)


def _xattn_kernel(x_ref, g_ref, wq_ref, gq_ref, k_ref, v_ref, wo_ref, o_ref):
    x = x_ref[0]
    xn = _rms_rows(x, g_ref[...]).astype(BF16)
    q = _head_rms(_mm(xn, wq_ref[...]), gq_ref[...], XHEAD_DIM).astype(BF16)
    sl = [slice(h * XHEAD_DIM, (h + 1) * XHEAD_DIM) for h in range(H_X)]
    s = [_nt(q[:, c], k_ref[0, :, c]) for c in sl]
    e = [jnp.exp2(s_h - jnp.max(s_h, axis=-1, keepdims=True)) for s_h in s]
    den = [jnp.sum(e_h, axis=-1, keepdims=True) for e_h in e]
    outs = [_mm(e[h].astype(BF16), v_ref[0, :, sl[h]]) / den[h] for h in range(H_X)]
    o = jnp.concatenate(outs, axis=-1).astype(BF16)
    o_ref[0] = x + _mm(o, wo_ref[...])


def cross_attention(x, gain, w_q, gq, k_mem, v_mem, w_o, *, tq=512):
    b, s, d = x.shape
    m, w_x = k_mem.shape[1:]
    full = lambda a: pl.BlockSpec(a.shape, lambda bi, i: (0,) * a.ndim)
    return pl.pallas_call(
        _xattn_kernel,
        grid=(b, s // tq),
        in_specs=[pl.BlockSpec((1, tq, d), lambda bi, i: (bi, i, 0)),
                  full(gain), full(w_q), full(gq),
                  pl.BlockSpec((1, m, w_x), lambda bi, i: (bi, 0, 0)),
                  pl.BlockSpec((1, m, w_x), lambda bi, i: (bi, 0, 0)),
                  full(w_o)],
        out_specs=pl.BlockSpec((1, tq, d), lambda bi, i: (bi, i, 0)),
        out_shape=jax.ShapeDtypeStruct((b, s, d), F32),
        compiler_params=_params(("parallel", "parallel")),
        name="cross_attention",
    )(x, gain, w_q, gq, k_mem, v_mem, w_o)


def _router_kernel(x_ref, g_ref, whi_ref, wlo_ref, o_ref):
    hf = _rms_rows(x_ref[...], g_ref[...])
    hi = hf.astype(BF16)
    lo = (hf - hi.astype(F32)).astype(BF16)
    logits = _mm(hi, whi_ref[...]) + (_mm(hi, wlo_ref[...]) + _mm(lo, whi_ref[...]))
    lane = lax.broadcasted_iota(jnp.int32, logits.shape, 1).astype(F32)
    logits = jnp.where(lane < N_EXPERTS, logits, NEG)
    m1 = jnp.max(logits, axis=-1, keepdims=True)
    i1 = jnp.min(jnp.where(logits == m1, lane, float(LANES)), axis=-1, keepdims=True)
    rest = jnp.where(lane == i1, NEG, logits)
    m2 = jnp.max(rest, axis=-1, keepdims=True)
    i2 = jnp.min(jnp.where(rest == m2, lane, float(LANES)), axis=-1, keepdims=True)
    e2 = jnp.exp(m2 - m1)
    den = 1.0 + e2
    o_ref[...] = (jnp.where(lane == 0.0, i1, 0.0) + jnp.where(lane == 1.0, i2, 0.0)
                  + jnp.where(lane == 2.0, 1.0 / den, 0.0) + jnp.where(lane == 3.0, e2 / den, 0.0))


def router_gates(x2d, gain, w_hi, w_lo, *, tm=512):
    n, d = x2d.shape
    return pl.pallas_call(
        _router_kernel,
        grid=(n // tm,),
        in_specs=[pl.BlockSpec((tm, d), lambda i: (i, 0)),
                  pl.BlockSpec((1, d), lambda i: (0, 0)),
                  pl.BlockSpec((d, LANES), lambda i: (0, 0)),
                  pl.BlockSpec((d, LANES), lambda i: (0, 0))],
        out_specs=pl.BlockSpec((tm, LANES), lambda i: (i, 0)),
        out_shape=jax.ShapeDtypeStruct((n, LANES), F32),
        compiler_params=_params(("parallel",)),
        name="router_gates",
    )(x2d, gain, w_hi, w_lo)


def _swiglu_tile(xn, wg, wu, wd):
    gg = _mm(xn, wg)
    uu = _mm(xn, wu)
    act = gg * (1.0 / (1.0 + jnp.exp(-gg))) * uu
    return _mm(act.astype(BF16), wd)


def _ffn_kernel(x_ref, g_ref, wg_ref, wu_ref, wd_ref, o_ref, xn_ref, acc_ref):
    j = pl.program_id(1)

    @pl.when(j == 0)
    def _():
        xn_ref[...] = _rms_rows(x_ref[...], g_ref[...]).astype(BF16)
        acc_ref[...] = jnp.zeros_like(acc_ref)

    acc_ref[...] += _swiglu_tile(xn_ref[...], wg_ref[...], wu_ref[...], wd_ref[...])

    @pl.when(j == pl.num_programs(1) - 1)
    def _():
        o_ref[...] = x_ref[...] + acc_ref[...]


def dense_ffn(x2d, gain, w_gu, w_down, *, tm=1024, tf=256):
    n, d = x2d.shape
    tm = min(tm, n)
    f = w_down.shape[0]
    nf = f // tf
    return pl.pallas_call(
        _ffn_kernel,
        grid=(n // tm, nf),
        in_specs=[pl.BlockSpec((tm, d), lambda i, j: (i, 0)),
                  pl.BlockSpec((1, d), lambda i, j: (0, 0)),
                  pl.BlockSpec((d, tf), lambda i, j: (0, j)),
                  pl.BlockSpec((d, tf), lambda i, j: (0, j + nf)),
                  pl.BlockSpec((tf, d), lambda i, j: (j, 0))],
        out_specs=pl.BlockSpec((tm, d), lambda i, j: (i, 0)),
        out_shape=jax.ShapeDtypeStruct((n, d), F32),
        scratch_shapes=[pltpu.VMEM((tm, d), BF16), pltpu.VMEM((tm, d), F32)],
        compiler_params=_params(("parallel", "arbitrary")),
        name="dense_ffn",
    )(x2d, gain, w_gu, w_gu, w_down)


def _moe_kernel(te_ref, tn_ref, tok_ref, tok_next_ref, dst_prev_ref, x_hbm, g_ref, wg_ref, wu_ref,
                wd_ref, y_hbm, xg_ref, xn_ref, acc_ref, yb_ref, gsem, ssem, *, tm, nf):
    i = pl.program_id(0)
    j = pl.program_id(1)
    nt = pl.num_programs(0)
    slot = i % 2
    per_step = tm // nf
    active = tn_ref[i] > 0
    prev_active = (i > 0) & (tn_ref[jnp.maximum(i - 1, 0)] > 0)
    prev_issued = (i > 0) & (tn_ref[jnp.maximum(i - 2, 0)] > 0)

    def gather_copy(tok, r, s):
        return pltpu.make_async_copy(x_hbm.at[pl.ds(tok, 1), :], xg_ref.at[s, pl.ds(r, 1), :], gsem.at[s])

    def scatter_copy(dst, r, s):
        return pltpu.make_async_copy(yb_ref.at[s, pl.ds(r, 1), :], y_hbm.at[pl.ds(dst, 1), :], ssem.at[s])

    def start_all_rows(ids_ref, s, make_copy):
        def body(r8, c):
            for u in range(8):
                r = r8 * 8 + u
                make_copy(ids_ref[0, 0, r], r, s).start()
            return c
        lax.fori_loop(0, tm // 8, body, 0)

    def wait_gather(s):
        pltpu.make_async_copy(x_hbm.at[pl.ds(0, tm), :], xg_ref.at[s], gsem.at[s]).wait()

    def wait_scatter(s):
        pltpu.make_async_copy(yb_ref.at[s], y_hbm.at[pl.ds(0, tm), :], ssem.at[s]).wait()

    @pl.when(j == 0)
    def _():
        @pl.when(i == 0)
        def _():
            start_all_rows(tok_ref, slot, gather_copy)
            yb_ref[1] = jnp.zeros(yb_ref.shape[1:], yb_ref.dtype)
            n_real = y_hbm.shape[0] - 2 * tm
            for half in range(2):
                spare = pltpu.make_async_copy(yb_ref.at[1], y_hbm.at[pl.ds(n_real + half * tm, tm), :], ssem.at[1])
                spare.start()
                spare.wait()

        @pl.when(active | prev_active)
        def _():
            wait_gather(slot)

        @pl.when(prev_issued)
        def _():
            wait_scatter(slot)

        @pl.when(active)
        def _():
            xn_ref[...] = _rms_rows(xg_ref[slot], g_ref[...]).astype(BF16)
            acc_ref[...] = jnp.zeros_like(acc_ref)

        @pl.when(prev_active & jnp.logical_not(active))
        def _():
            start_all_rows(dst_prev_ref, 1 - slot, scatter_copy)

    @pl.when(active)
    def _():
        acc_ref[...] += _swiglu_tile(xn_ref[...], wg_ref[0], wu_ref[0], wd_ref[0])
        for u in range(per_step):
            r = j * per_step + u
            gather_copy(tok_next_ref[0, 0, r], r, 1 - slot).start(priority=1)
            scatter_copy(dst_prev_ref[0, 0, r], r, 1 - slot).start(priority=1)

        @pl.when(j == nf - 1)
        def _():
            yb_ref[slot] = acc_ref[...]

    @pl.when((j == nf - 1) & (i == nt - 1) & (active | prev_active))
    def _():
        wait_scatter(1 - slot)


def moe_experts(x2d, gain, plan, w_gu, w_down, *, tm, tf=512):
    n, d = x2d.shape
    ne, f, _ = w_down.shape
    nf = f // tf
    assert tm % nf == 0 and tm % 16 == 0
    tile_e, tile_n, row_tok, row_dst_prev = plan
    nt = tile_e.shape[0]
    smem_rows = lambda imap: pl.BlockSpec((1, 1, tm), imap, memory_space=pltpu.SMEM)
    live = lambda j, tn, i: j * jnp.minimum(tn[i], 1)
    return pl.pallas_call(
        functools.partial(_moe_kernel, tm=tm, nf=nf),
        grid_spec=pltpu.PrefetchScalarGridSpec(
            num_scalar_prefetch=2,
            grid=(nt, nf),
            in_specs=[smem_rows(lambda i, j, te, tn: (i, 0, 0)),
                      smem_rows(lambda i, j, te, tn: (jnp.minimum(i + 1, nt - 1), 0, 0)),
                      smem_rows(lambda i, j, te, tn: (i, 0, 0)),
                      pl.BlockSpec(memory_space=pl.ANY),
                      pl.BlockSpec((1, d), lambda i, j, te, tn: (0, 0)),
                      pl.BlockSpec((1, d, tf), lambda i, j, te, tn: (te[i], 0, live(j, tn, i))),
                      pl.BlockSpec((1, d, tf), lambda i, j, te, tn: (te[i], 0, live(j, tn, i) + nf)),
                      pl.BlockSpec((1, tf, d), lambda i, j, te, tn: (te[i], live(j, tn, i), 0))],
            out_specs=pl.BlockSpec(memory_space=pl.ANY),
            scratch_shapes=[pltpu.VMEM((2, tm, d), F32), pltpu.VMEM((tm, d), BF16), pltpu.VMEM((tm, d), F32),
                            pltpu.VMEM((2, tm, d), F32),
                            pltpu.SemaphoreType.DMA((2,)), pltpu.SemaphoreType.DMA((2,))]),
        out_shape=jax.ShapeDtypeStruct((2 * n + 2 * tm, d), F32),
        compiler_params=_params(("arbitrary", "arbitrary")),
        name="moe_experts",
    )(tile_e, tile_n, row_tok, row_tok, row_dst_prev, x2d, gain, w_gu, w_gu, w_down)


def moe_plan(route, *, tm):
    n = route.shape[0]
    flat_e = route[:, :2].astype(jnp.int32).reshape(-1)
    nt = (2 * n) // tm + N_EXPERTS + 1
    order = jnp.argsort(flat_e, stable=True).astype(jnp.int32)
    counts = jnp.sum(flat_e[:, None] == jnp.arange(N_EXPERTS)[None, :], axis=0).astype(jnp.int32)
    off = jnp.cumsum(counts) - counts
    tiles = (counts + tm - 1) // tm
    tile_off = jnp.cumsum(tiles) - tiles
    tile_id = jnp.arange(nt, dtype=jnp.int32)
    used = tile_id < jnp.sum(tiles)
    tile_e = jnp.clip(jnp.sum(tile_id[:, None] >= tile_off[None, :], axis=1) - 1, 0, N_EXPERTS - 1)
    tile_e = jnp.where(used, tile_e, tile_e[jnp.maximum(jnp.sum(tiles) - 1, 0)]).astype(jnp.int32)
    first_row = (tile_id - tile_off[tile_e]) * tm
    tile_n = jnp.where(used, jnp.clip(counts[tile_e] - first_row, 0, tm), 0).astype(jnp.int32)
    r = jnp.arange(tm, dtype=jnp.int32)[None, :]
    valid = r < tile_n[:, None]
    a = order[jnp.clip(off[tile_e][:, None] + first_row[:, None] + r, 0, 2 * n - 1)]
    row_tok = jnp.where(valid, a // 2, 0).astype(jnp.int32).reshape(nt, 1, tm)
    spare = 2 * n + (tile_id[:, None] % 2) * tm + r
    row_dst = jnp.where(valid, (a % 2) * n + a // 2, spare).astype(jnp.int32)
    row_dst_prev = jnp.concatenate([2 * n + tm + r, row_dst[:-1]], axis=0).reshape(nt, 1, tm)
    return tile_e, tile_n, row_tok, row_dst_prev


def _combine_kernel(x_ref, route_ref, y0_ref, y1_ref, o_ref):
    o_ref[...] = x_ref[...] + (route_ref[:, 2:3] * y0_ref[...] + route_ref[:, 3:4] * y1_ref[...])


def moe_combine(x2d, route, y, *, tm=512):
    n, d = x2d.shape
    nb = n // tm
    return pl.pallas_call(
        _combine_kernel,
        grid=(nb,),
        in_specs=[pl.BlockSpec((tm, d), lambda i: (i, 0)),
                  pl.BlockSpec((tm, LANES), lambda i: (i, 0)),
                  pl.BlockSpec((tm, d), lambda i: (i, 0)),
                  pl.BlockSpec((tm, d), lambda i: (i + nb, 0))],
        out_specs=pl.BlockSpec((tm, d), lambda i: (i, 0)),
        out_shape=jax.ShapeDtypeStruct((n, d), F32),
        compiler_params=_params(("parallel",)),
        name="moe_combine",
    )(x2d, route, y, y)


def _pack_w_in(w, d):
    sizes = (256, 256, 256, 256, 64, 64, 256, 64, 4, 512, 128, 128, N_BRANCH * d)
    qa, ka, va, qb, kb, vb, qi, ki, wi, qc, kc, vc, g = jnp.split(w, np.cumsum(sizes)[:-1].tolist(), axis=-1)
    pad = jnp.zeros((w.shape[0], 60), w.dtype)
    mixers = jnp.concatenate([qa * (HEAD_DIM ** -0.5 * LOG2E), ka, va,
                              qb, qi * D_IDX ** -0.5, kb, vb, ki, wi, pad,
                              qc, kc, vc], axis=-1)
    return mixers.astype(BF16), g.astype(BF16)


def kernel(x, mem, rel_bias, norm_mix, w_in, b_gate, qn_dsa, kn_dsa, qn_swa, kn_swa, sinks, w_pa, w_pb, w_pc, w_out, norm_x, norm_mem, w_xq, w_xkv, w_xo, qn_x, kn_x, norm_ffn, w_gu_dense, w_down_dense, w_router, w_gu_moe, w_down_moe):
    b, s, d = x.shape
    depth = w_in.shape[0]
    n = b * s
    row = lambda v: v.reshape(1, -1).astype(F32)
    bias_dsa = dsa_bias_tiles(rel_bias[:, :H_DSA] * LOG2E)
    bias_swa = swa_bias_tiles(rel_bias[:, H_DSA:] * LOG2E)
    qscale = HEAD_DIM ** -0.5 * LOG2E

    x2 = x.reshape(n, d)
    for l in range(depth):
        w_mix, w_gate = _pack_w_in(w_in[l], d)
        proj3 = in_projection(x2, row(norm_mix[l]), w_mix).reshape(b, s, -1)
        o_a = sb_attention(proj3, 0)
        o_b = dsa_attention(proj3, 1, row(jnp.tile(qn_dsa[l] * qscale, H_DSA)),
                            row(jnp.concatenate([kn_dsa[l], jnp.zeros_like(kn_dsa[l])])), bias_dsa)
        o_c = swa_attention(proj3, 2, sinks[l].astype(F32) * LOG2E,
                            row(jnp.tile(qn_swa[l] * qscale, H_SW)), row(jnp.tile(kn_swa[l], KV_SW)), bias_swa)
        x2 = merge_project(x2, row(norm_mix[l]), w_gate, o_a.reshape(n, -1), o_b.reshape(n, -1), o_c.reshape(n, -1),
                           b_gate[l].astype(F32), w_pa[l].astype(BF16), w_pb[l].astype(BF16),
                           w_pc[l].astype(BF16), w_out[l].astype(BF16))
        k_mem, v_mem = memory_kv(mem, row(norm_mem[l]), w_xkv[l].astype(BF16), row(jnp.tile(kn_x[l], H_X)))
        x2 = cross_attention(x2.reshape(b, s, d), row(norm_x[l]), w_xq[l].astype(BF16),
                             row(jnp.tile(qn_x[l] * (XHEAD_DIM ** -0.5 * LOG2E), H_X)), k_mem, v_mem,
                             w_xo[l].astype(BF16)).reshape(n, d)
        if l % 2 == 0:
            x2 = dense_ffn(x2, row(norm_ffn[l]), w_gu_dense[l // 2].astype(BF16),
                           w_down_dense[l // 2].astype(BF16), tf=256)
        else:
            wr = jnp.pad(w_router[l // 2].astype(F32), ((0, 0), (0, LANES - N_EXPERTS)))
            wr_hi = wr.astype(BF16)
            wr_lo = (wr - wr_hi.astype(F32)).astype(BF16)
            route = router_gates(x2, row(norm_ffn[l]), wr_hi, wr_lo)
            tm_moe = 672
            y = moe_experts(x2, row(norm_ffn[l]), moe_plan(route, tm=tm_moe), w_gu_moe[l // 2].astype(BF16),
                            w_down_moe[l // 2].astype(BF16), tm=tm_moe)
            x2 = moe_combine(x2, route, y)
    return x2.reshape(b, s, d)
```

```python
import functools
import math

import numpy as np
import jax
import jax.numpy as jnp
from jax import lax
from jax.experimental import pallas as pl
from jax.experimental.pallas import tpu as pltpu

HEAD_DIM = 64
H_SB = 4
H_DSA = 4
H_IDX = 4
D_IDX = 64
TOPK_MAX = 256
H_SW = 8
KV_SW = 2
WINDOW = 128
BLOCK = 128
N_BRANCH = 3
N_BUCKETS = 32
MAX_DISTANCE = 128
H_X = 4
XHEAD_DIM = 128
N_EXPERTS = 8
EPS = 1e-6

LANES = 128
VMEM_LIMIT = 56 * 1024 * 1024
NEG = -1e30
INT_MIN = -(2 ** 31)
LOG2E = math.log2(math.e)

F32 = jnp.float32
BF16 = jnp.bfloat16


def _nt(a, b):
    return lax.dot_general(a, b, (((1,), (1,)), ((), ())), preferred_element_type=F32)


def _mm(a, b):
    return jnp.dot(a, b, preferred_element_type=F32)


def _rms_rows(x, g):
    ms = jnp.mean(x * x, axis=-1, keepdims=True)
    return x * lax.rsqrt(ms + EPS) * g


def _params(sem):
    return pltpu.CompilerParams(dimension_semantics=sem, vmem_limit_bytes=VMEM_LIMIT)


def _sb_kernel(a_ref, o_ref, *, tq):
    i = pl.program_id(1)
    q0 = pl.multiple_of(i * tq, tq)
    hd = HEAD_DIM
    row = lax.broadcasted_iota(jnp.int32, (tq, tq), 0)
    col = lax.broadcasted_iota(jnp.int32, (tq, tq), 1)
    strict = col < row
    u_inc = jnp.where(row >= col, 1.0, 0.0).astype(BF16)
    qs = [a_ref[0, pl.ds(q0, tq), h * hd:(h + 1) * hd] for h in range(H_SB)]

    heads = range(H_SB)

    def block(k0s, accs, carries, diag):
        chains = [(b, h) for b in range(len(k0s)) for h in heads]
        ks = {(b, h): a_ref[0, pl.ds(k0s[b], tq), 256 + h * hd:256 + (h + 1) * hd] for b, h in chains}
        vs = {(b, h): a_ref[0, pl.ds(k0s[b], tq), 512 + h * hd:512 + (h + 1) * hd] for b, h in chains}
        zs = {c: _nt(qs[c[1]], ks[c]) for c in chains}
        lks = {c: -(jnp.maximum(zs[c], 0.0) + jnp.log2(1.0 + jnp.exp2(-jnp.abs(zs[c])))) for c in chains}
        if diag:
            lks = {c: jnp.where(strict, lks[c], 0.0) for c in chains}
        rs = {c: _mm(lks[c].astype(BF16), u_inc) for c in chains}
        carry = {h: (None if diag else carries[h]) for h in heads}
        atts = {}
        for b, h in chains:
            if diag:
                atts[(b, h)] = jnp.where(strict, jnp.exp2(zs[(b, h)] + rs[(b, h)]), 0.0)
                carry[h] = rs[(b, h)][:, 0:1]
            else:
                atts[(b, h)] = jnp.exp2(zs[(b, h)] + rs[(b, h)] + carry[h])
                carry[h] = carry[h] + rs[(b, h)][:, 0:1]
        pvs = {c: _mm(atts[c].astype(BF16), vs[c]) for c in chains}
        new_acc = []
        for h in heads:
            tot = pvs[(0, h)] if diag else accs[h] + pvs[(0, h)]
            for b in range(1, len(k0s)):
                tot = tot + pvs[(b, h)]
            new_acc.append(tot)
        return tuple(new_acc), tuple(carry[h] for h in heads)

    def query_tile(n_before):
        state = block([q0], None, None, True)
        b = n_before - 1
        while b >= 1:
            state = block([b * tq, (b - 1) * tq], state[0], state[1], False)
            b -= 2
        if b == 0:
            state = block([0], state[0], state[1], False)
        o_ref[0] = jnp.concatenate(state[0], axis=-1).astype(o_ref.dtype)
        return 0

    lax.switch(i, [functools.partial(query_tile, n) for n in range(a_ref.shape[1] // tq)])


def sb_attention(proj, col_block, *, tq=256):
    b, s, _ = proj.shape
    return pl.pallas_call(
        functools.partial(_sb_kernel, tq=tq),
        grid=(b, s // tq),
        in_specs=[pl.BlockSpec((1, s, 768), lambda bi, i: (bi, 0, col_block))],
        out_specs=pl.BlockSpec((1, tq, H_SB * HEAD_DIM), lambda bi, i: (bi, i, 0)),
        out_shape=jax.ShapeDtypeStruct((b, s, H_SB * HEAD_DIM), BF16),
        compiler_params=_params(("parallel", "arbitrary")),
        name="sb_attention",
    )(proj)


def _inproj_kernel(x_ref, g_ref, w_ref, o_ref, xn_ref):
    @pl.when(pl.program_id(1) == 0)
    def _():
        xn_ref[...] = _rms_rows(x_ref[...], g_ref[...]).astype(BF16)

    o_ref[...] = _mm(xn_ref[...], w_ref[...]).astype(o_ref.dtype)


def in_projection(x2d, gain, w, *, tm=1024, tn=768):
    n, d = x2d.shape
    tm = min(tm, n)
    c = w.shape[1]
    return pl.pallas_call(
        _inproj_kernel,
        grid=(n // tm, c // tn),
        in_specs=[pl.BlockSpec((tm, d), lambda i, j: (i, 0)),
                  pl.BlockSpec((1, d), lambda i, j: (0, 0)),
                  pl.BlockSpec((d, tn), lambda i, j: (0, j))],
        out_specs=pl.BlockSpec((tm, tn), lambda i, j: (i, j)),
        out_shape=jax.ShapeDtypeStruct((n, c), BF16),
        scratch_shapes=[pltpu.VMEM((tm, d), BF16)],
        compiler_params=_params(("parallel", "arbitrary")),
        name="in_projection",
    )(x2d, gain, w)


def _head_sumsq(x, head_dim):
    r_i = lax.broadcasted_iota(jnp.int32, (LANES, LANES), 0) // head_dim
    c_i = lax.broadcasted_iota(jnp.int32, (LANES, LANES), 1) // head_dim
    bd = jnp.where(r_i == c_i, 1.0, 0.0).astype(F32)
    x2 = x * x
    parts = [_mm(x2[:, b * LANES:(b + 1) * LANES], bd) for b in range(x.shape[1] // LANES)]
    return parts[0] if len(parts) == 1 else jnp.concatenate(parts, axis=-1)


def _head_rms(x, g, head_dim):
    ss = _head_sumsq(x, head_dim)
    return x * lax.rsqrt(ss * (1.0 / head_dim) + EPS) * g


def _eye(n, dtype):
    r = lax.broadcasted_iota(jnp.int32, (n, n), 0)
    c = lax.broadcasted_iota(jnp.int32, (n, n), 1)
    return jnp.where(r == c, 1.0, 0.0).astype(dtype)


def _dsa_kernel(a_ref, gq_ref, gk_ref, bias_ref, o_ref, kbn_ref, vt_ref, keys_ref, tri_ref, *, tq, ck, topk):
    s_len = a_ref.shape[1]
    i = pl.program_id(1)
    q0 = pl.multiple_of(i * tq, tq)
    hd = HEAD_DIM
    sub = ck // tq

    @pl.when(i == 0)
    def _():
        kv = a_ref[0, :, 512:640]
        kvf = kv.astype(F32)
        lane = lax.broadcasted_iota(jnp.int32, (1, LANES), 1)
        ss = _head_sumsq(jnp.where(lane < hd, kvf, 0.0), LANES)
        kn = kvf * lax.rsqrt(ss * (1.0 / hd) + EPS) * gk_ref[...]
        kbn_ref[...] = kn[:, :hd].astype(BF16)
        kvt = _nt(_eye(LANES, BF16), kv)
        for cb in range(s_len // ck):
            vt_ref[cb] = kvt[hd:, cb * ck:(cb + 1) * ck].astype(BF16)
        tri_ref[...] = jnp.where(lax.broadcasted_iota(jnp.int32, (ck, ck), 1)
                                 <= lax.broadcasted_iota(jnp.int32, (ck, ck), 0), 1.0, 0.0).astype(BF16)

    def heads_on_rows(x):
        return jnp.concatenate([x[:, h * hd:(h + 1) * hd] for h in range(x.shape[1] // hd)], axis=0)

    qi_all = heads_on_rows(a_ref[0, pl.ds(q0, tq), 256:512])
    wblk = a_ref[0, pl.ds(q0, tq), 640:768]
    sel_r = lax.broadcasted_iota(jnp.int32, (8, LANES), 0)
    sel_c = lax.broadcasted_iota(jnp.int32, (8, LANES), 1)
    w_t = _nt(jnp.where(sel_c == sel_r + D_IDX, 1.0, 0.0).astype(BF16), wblk)
    qpos = q0 + lax.broadcasted_iota(jnp.int32, (1, tq), 1)
    nck = (q0 + tq + ck - 1) // ck
    row_ck = lax.broadcasted_iota(jnp.int32, (ck, tq), 0)

    qb = a_ref[0, pl.ds(q0, tq), 0:256].astype(F32)
    qn_all = heads_on_rows(_head_rms(qb, gq_ref[...], hd).astype(BF16))

    def index_chunk(c):
        act = jnp.maximum(_nt(a_ref[0, c * ck:(c + 1) * ck, 640:704], qi_all), 0.0)
        sc = w_t[0:1, :] * act[:, 0:tq]
        for h in range(1, H_IDX):
            sc = sc + w_t[h:h + 1, :] * act[:, h * tq:(h + 1) * tq]
        bits = pltpu.bitcast(sc, jnp.int32)
        key = jnp.where(bits < 0, -(bits & 0x7FFFFFFF), bits)
        keys_ref[c * ck:(c + 1) * ck, :] = jnp.where(c * ck + row_ck <= qpos, key, INT_MIN)

    def search(n_chunks):
        def count(pred):
            acc = jnp.zeros((8, tq), jnp.int32)
            for c in range(n_chunks):
                m = jnp.where(pred(keys_ref[c * ck:(c + 1) * ck, :]), 1, 0)
                acc = acc + jnp.sum(m.reshape(ck // 8, 8, tq), axis=0)
            return jnp.sum(acc, axis=0, keepdims=True)

        c0 = count(lambda k: k >= 0)
        t = jnp.where(c0 >= topk, 0, INT_MIN).astype(jnp.int32)

        def vstep(b, t):
            cand = t | lax.shift_left(jnp.int32(1), 30 - b)
            return jnp.where(count(lambda k: k >= cand) >= topk, cand, t)

        t = lax.fori_loop(0, 31, vstep, t)
        need = topk - count(lambda k: k > t)
        return t, jnp.where(t == INT_MIN, 0, need).astype(F32)

    def no_search():
        return jnp.full((1, tq), INT_MIN, jnp.int32), jnp.zeros((1, tq), F32)

    def attend_chunk(c, st, thr, need):
        ms, ls, accs, ties_before = st
        s_all = _nt(kbn_ref[c * ck:(c + 1) * ck, :], qn_all)
        vtc = vt_ref[c]
        key = keys_ref[c * ck:(c + 1) * ck, :]
        tie = key == thr
        tie_rank = ties_before + _mm(tri_ref[...], jnp.where(tie, 1.0, 0.0).astype(BF16))
        sel = (key > thr) | (tie & (tie_rank <= need))
        ties_before = tie_rank[ck - 1:ck, :]
        bidx = [jnp.clip(i - (c * sub + r), 0, 2) for r in range(sub)]
        heads = range(H_DSA)
        s_h = [s_all[:, h * tq:(h + 1) * tq]
               + jnp.concatenate([bias_ref[bidx[r], h] for r in range(sub)], axis=0) for h in heads]
        m_new = [jnp.maximum(ms[h], jnp.max(jnp.where(sel, s_h[h], NEG), axis=0, keepdims=True)) for h in heads]
        p = [jnp.where(sel, jnp.exp2(s_h[h] - m_new[h]), 0.0) for h in heads]
        alpha = [jnp.exp2(ms[h] - m_new[h]) for h in heads]
        pv = [_mm(vtc, p[h].astype(BF16)) for h in heads]
        nl = [ls[h] * alpha[h] + jnp.sum(p[h], axis=0, keepdims=True) for h in heads]
        na = [accs[h] * alpha[h] + pv[h] for h in heads]
        return tuple(m_new), tuple(nl), tuple(na), ties_before

    def query_block(n_chunks):
        for c in range(n_chunks):
            index_chunk(c)
        thr, need = lax.cond(q0 + tq > topk, functools.partial(search, n_chunks), no_search)
        st = (tuple(jnp.full((1, tq), NEG, F32) for _ in range(H_DSA)),
              tuple(jnp.zeros((1, tq), F32) for _ in range(H_DSA)),
              tuple(jnp.zeros((hd, tq), F32) for _ in range(H_DSA)),
              jnp.zeros((1, tq), F32))
        for c in range(n_chunks):
            st = attend_chunk(c, st, thr, need)
        _, ls, accs, _ = st
        o_t = jnp.concatenate([accs[h] / ls[h] for h in range(H_DSA)], axis=0)
        o_ref[0] = _nt(_eye(tq, BF16), o_t.astype(BF16)).astype(o_ref.dtype)
        return 0

    lax.switch(nck - 1, [functools.partial(query_block, n) for n in range(1, s_len // ck + 1)])


def dsa_attention(proj, col_block, gq, gk, bias_t, *, tq=BLOCK, ck=512):
    b, s, _ = proj.shape
    topk = min(TOPK_MAX, s // 4)
    ck = min(ck, s)
    assert tq >= MAX_DISTANCE and s % ck == 0 and ck % tq == 0
    return pl.pallas_call(
        functools.partial(_dsa_kernel, tq=tq, ck=ck, topk=topk),
        grid=(b, s // tq),
        in_specs=[pl.BlockSpec((1, s, 768), lambda bi, i: (bi, 0, col_block)),
                  pl.BlockSpec((1, 256), lambda bi, i: (0, 0)),
                  pl.BlockSpec((1, LANES), lambda bi, i: (0, 0)),
                  pl.BlockSpec((3, H_DSA, tq, tq), lambda bi, i: (0, 0, 0, 0))],
        out_specs=pl.BlockSpec((1, tq, H_DSA * HEAD_DIM), lambda bi, i: (bi, i, 0)),
        out_shape=jax.ShapeDtypeStruct((b, s, H_DSA * HEAD_DIM), BF16),
        scratch_shapes=[pltpu.VMEM((s, HEAD_DIM), BF16),
                        pltpu.VMEM((s // ck, HEAD_DIM, ck), BF16),
                        pltpu.VMEM((s, tq), jnp.int32),
                        pltpu.VMEM((ck, ck), BF16)],
        compiler_params=_params(("parallel", "arbitrary")),
        name="dsa_attention",
    )(proj, gq, gk, bias_t)


def _t5_bucket(rel):
    n = jnp.maximum(rel, 0)
    max_exact = N_BUCKETS // 2
    nf = jnp.maximum(n, 1).astype(F32)
    large = max_exact + (jnp.log(nf / max_exact) / math.log(MAX_DISTANCE / max_exact)
                         * (N_BUCKETS - max_exact)).astype(jnp.int32)
    large = jnp.minimum(large, N_BUCKETS - 1)
    return jnp.where(n < max_exact, n, large)


def _bucket_lookup(tab, rel):
    hit = _t5_bucket(rel)[..., None, None] == jnp.arange(N_BUCKETS)[:, None]
    return jnp.sum(jnp.where(hit, tab.astype(F32), 0.0), axis=-2)


def dsa_bias_tiles(tab, tq=BLOCK):
    ks = jnp.arange(tq)[:, None]
    tl = jnp.arange(tq)[None, :]
    rel = jnp.stack([tl - ks, tq + tl - ks, jnp.full((tq, tq), 2 * tq + MAX_DISTANCE)])
    return _bucket_lookup(tab, rel).transpose(0, 3, 1, 2)


def _swa_kernel(sink_ref, cur_ref, prev_ref, gq_ref, gk_ref, bias_ref, o_ref, *, tq, nb):
    i = pl.program_id(1)
    hd = HEAD_DIM
    g = H_SW // KV_SW
    qn = _head_rms(cur_ref[0, :, 0:512].astype(F32), gq_ref[...], hd).astype(BF16)
    kn = jnp.concatenate([_head_rms(prev_ref[0, :, 512:640].astype(F32), gk_ref[...], hd),
                          _head_rms(cur_ref[0, :, 512:640].astype(F32), gk_ref[...], hd)], axis=0).astype(BF16)
    vv = jnp.concatenate([prev_ref[0, :, 640:768], cur_ref[0, :, 640:768]], axis=0)
    col = lax.broadcasted_iota(jnp.int32, (tq, 2 * tq), 1)
    first = (col >= tq) | (i > 0)
    chains = [(b, h) for b in range(nb) for h in range(H_SW)]
    s = {}
    for b, h in chains:
        kv = h // g
        s_bh = _nt(qn[b * tq:(b + 1) * tq, h * hd:(h + 1) * hd], kn[b * tq:(b + 2) * tq, kv * hd:(kv + 1) * hd])
        s_bh = s_bh + bias_ref[h]
        s[(b, h)] = jnp.where(first, s_bh, NEG) if b == 0 else s_bh
    m = {c: jnp.maximum(jnp.max(s[c], axis=-1, keepdims=True), sink_ref[c[1]]) for c in chains}
    e = {c: jnp.exp2(s[c] - m[c]) for c in chains}
    den = {c: jnp.sum(e[c], axis=-1, keepdims=True) + jnp.exp2(sink_ref[c[1]] - m[c]) for c in chains}
    out = {(b, h): _mm(e[(b, h)].astype(BF16), vv[b * tq:(b + 2) * tq, (h // g) * hd:(h // g + 1) * hd]) / den[(b, h)]
           for b, h in chains}
    for b in range(nb):
        o_ref[0, b * tq:(b + 1) * tq, :] = jnp.concatenate([out[(b, h)] for h in range(H_SW)],
                                                           axis=-1).astype(o_ref.dtype)


def swa_bias_tiles(tab, tq=BLOCK):
    rel = (jnp.arange(tq)[:, None] + tq) - jnp.arange(2 * tq)[None, :]
    in_win = (rel >= 0) & (rel < WINDOW)
    bias = _bucket_lookup(tab, rel).transpose(2, 0, 1)
    return jnp.where(in_win[None], bias, NEG)


def swa_attention(proj, col_block, sinks, gq, gk, bias, *, tq=BLOCK, nb=4):
    b, s, _ = proj.shape
    assert WINDOW <= tq and s % (nb * tq) == 0
    return pl.pallas_call(
        functools.partial(_swa_kernel, tq=tq, nb=nb),
        grid_spec=pltpu.PrefetchScalarGridSpec(
            num_scalar_prefetch=0,
            grid=(b, s // (nb * tq)),
            in_specs=[pl.BlockSpec(memory_space=pltpu.SMEM),
                      pl.BlockSpec((1, nb * tq, 768), lambda bi, i: (bi, i, col_block)),
                      pl.BlockSpec((1, tq, 768), lambda bi, i: (bi, jnp.maximum(nb * i - 1, 0), col_block)),
                      pl.BlockSpec((1, 512), lambda bi, i: (0, 0)),
                      pl.BlockSpec((1, LANES), lambda bi, i: (0, 0)),
                      pl.BlockSpec((H_SW, tq, 2 * tq), lambda bi, i: (0, 0, 0))],
            out_specs=pl.BlockSpec((1, nb * tq, H_SW * HEAD_DIM), lambda bi, i: (bi, i, 0))),
        out_shape=jax.ShapeDtypeStruct((b, s, H_SW * HEAD_DIM), BF16),
        compiler_params=_params(("parallel", "arbitrary")),
        name="swa_attention",
    )(sinks, proj, proj, gq, gk, bias)


def _merge_kernel(x_ref, gn_ref, wg_ref, oa_ref, ob_ref, oc_ref, bg_ref, wa_ref, wb_ref, wc_ref, wo_ref, o_ref):
    d = x_ref.shape[1]
    x = x_ref[...]
    xn = _rms_rows(x, gn_ref[...]).astype(BF16)
    merged = None
    for k, (o_k, w_k) in enumerate(((oa_ref, wa_ref), (ob_ref, wb_ref), (oc_ref, wc_ref))):
        logit = _mm(xn, wg_ref[:, k * d:(k + 1) * d]) + bg_ref[k:k + 1, :]
        gate = 1.0 / (1.0 + jnp.exp(-logit))
        term = gate * _mm(o_k[...], w_k[...])
        merged = term if merged is None else merged + term
    o_ref[...] = x + _mm(merged.astype(BF16), wo_ref[...])


def merge_project(x2d, gain, w_gate, o_a, o_b, o_c, b_gate, w_pa, w_pb, w_pc, w_out, *, tm=512):
    n, d = x2d.shape
    full = lambda a: pl.BlockSpec(a.shape, lambda i: (0,) * a.ndim)
    row = lambda a: pl.BlockSpec((tm, a.shape[1]), lambda i: (i, 0))
    return pl.pallas_call(
        _merge_kernel,
        grid=(n // tm,),
        in_specs=[row(x2d), full(gain), full(w_gate), row(o_a), row(o_b), row(o_c),
                  full(b_gate), full(w_pa), full(w_pb), full(w_pc), full(w_out)],
        out_specs=pl.BlockSpec((tm, d), lambda i: (i, 0)),
        out_shape=jax.ShapeDtypeStruct((n, d), F32),
        compiler_params=_params(("parallel",)),
        name="merge_project",
    )(x2d, gain, w_gate, o_a, o_b, o_c, b_gate, w_pa, w_pb, w_pc, w_out)


def _memkv_kernel(m_ref, g_ref, w_ref, gk_ref, k_ref, v_ref):
    w_x = k_ref.shape[2]
    mn = _rms_rows(m_ref[0], g_ref[...]).astype(BF16)
    kv = _mm(mn, w_ref[...])
    k_ref[0] = _head_rms(kv[:, :w_x], gk_ref[...], XHEAD_DIM).astype(k_ref.dtype)
    v_ref[0] = kv[:, w_x:].astype(v_ref.dtype)


def memory_kv(mem, gain, w_kv, gk):
    b, m, d = mem.shape
    w_x = w_kv.shape[1] // 2
    return pl.pallas_call(
        _memkv_kernel,
        grid=(b,),
        in_specs=[pl.BlockSpec((1, m, d), lambda i: (i, 0, 0)),
                  pl.BlockSpec((1, d), lambda i: (0, 0)),
                  pl.BlockSpec(w_kv.shape, lambda i: (0, 0)),
                  pl.BlockSpec((1, w_x), lambda i: (0, 0))],
        out_specs=[pl.BlockSpec((1, m, w_x), lambda i: (i, 0, 0)),
                   pl.BlockSpec((1, m, w_x), lambda i: (i, 0, 0))],
        out_shape=[jax.ShapeDtypeStruct((b, m, w_x), BF16)] * 2,
        compiler_params=_params(("parallel",)),
        name="memory_kv",
    )(mem, gain, w_kv, gk)


def _xattn_kernel(x_ref, g_ref, wq_ref, gq_ref, k_ref, v_ref, wo_ref, o_ref):
    x = x_ref[0]
    xn = _rms_rows(x, g_ref[...]).astype(BF16)
    q = _head_rms(_mm(xn, wq_ref[...]), gq_ref[...], XHEAD_DIM).astype(BF16)
    sl = [slice(h * XHEAD_DIM, (h + 1) * XHEAD_DIM) for h in range(H_X)]
    s = [_nt(q[:, c], k_ref[0, :, c]) for c in sl]
    e = [jnp.exp2(s_h - jnp.max(s_h, axis=-1, keepdims=True)) for s_h in s]
    den = [jnp.sum(e_h, axis=-1, keepdims=True) for e_h in e]
    outs = [_mm(e[h].astype(BF16), v_ref[0, :, sl[h]]) / den[h] for h in range(H_X)]
    o = jnp.concatenate(outs, axis=-1).astype(BF16)
    o_ref[0] = x + _mm(o, wo_ref[...])


def cross_attention(x, gain, w_q, gq, k_mem, v_mem, w_o, *, tq=512):
    b, s, d = x.shape
    m, w_x = k_mem.shape[1:]
    full = lambda a: pl.BlockSpec(a.shape, lambda bi, i: (0,) * a.ndim)
    return pl.pallas_call(
        _xattn_kernel,
        grid=(b, s // tq),
        in_specs=[pl.BlockSpec((1, tq, d), lambda bi, i: (bi, i, 0)),
                  full(gain), full(w_q), full(gq),
                  pl.BlockSpec((1, m, w_x), lambda bi, i: (bi, 0, 0)),
                  pl.BlockSpec((1, m, w_x), lambda bi, i: (bi, 0, 0)),
                  full(w_o)],
        out_specs=pl.BlockSpec((1, tq, d), lambda bi, i: (bi, i, 0)),
        out_shape=jax.ShapeDtypeStruct((b, s, d), F32),
        compiler_params=_params(("parallel", "parallel")),
        name="cross_attention",
    )(x, gain, w_q, gq, k_mem, v_mem, w_o)


def _router_kernel(x_ref, g_ref, whi_ref, wlo_ref, o_ref):
    hf = _rms_rows(x_ref[...], g_ref[...])
    hi = hf.astype(BF16)
    lo = (hf - hi.astype(F32)).astype(BF16)
    logits = _mm(hi, whi_ref[...]) + (_mm(hi, wlo_ref[...]) + _mm(lo, whi_ref[...]))
    lane = lax.broadcasted_iota(jnp.int32, logits.shape, 1).astype(F32)
    logits = jnp.where(lane < N_EXPERTS, logits, NEG)
    m1 = jnp.max(logits, axis=-1, keepdims=True)
    i1 = jnp.min(jnp.where(logits == m1, lane, float(LANES)), axis=-1, keepdims=True)
    rest = jnp.where(lane == i1, NEG, logits)
    m2 = jnp.max(rest, axis=-1, keepdims=True)
    i2 = jnp.min(jnp.where(rest == m2, lane, float(LANES)), axis=-1, keepdims=True)
    e2 = jnp.exp(m2 - m1)
    den = 1.0 + e2
    o_ref[...] = (jnp.where(lane == 0.0, i1, 0.0) + jnp.where(lane == 1.0, i2, 0.0)
                  + jnp.where(lane == 2.0, 1.0 / den, 0.0) + jnp.where(lane == 3.0, e2 / den, 0.0))


def router_gates(x2d, gain, w_hi, w_lo, *, tm=1024):
    n, d = x2d.shape
    tm = min(tm, n)
    return pl.pallas_call(
        _router_kernel,
        grid=(n // tm,),
        in_specs=[pl.BlockSpec((tm, d), lambda i: (i, 0)),
                  pl.BlockSpec((1, d), lambda i: (0, 0)),
                  pl.BlockSpec((d, LANES), lambda i: (0, 0)),
                  pl.BlockSpec((d, LANES), lambda i: (0, 0))],
        out_specs=pl.BlockSpec((tm, LANES), lambda i: (i, 0)),
        out_shape=jax.ShapeDtypeStruct((n, LANES), F32),
        compiler_params=_params(("parallel",)),
        name="router_gates",
    )(x2d, gain, w_hi, w_lo)


def _swiglu_tile(xn, wg, wu, wd):
    gg = _mm(xn, wg)
    uu = _mm(xn, wu)
    act = gg * (1.0 / (1.0 + jnp.exp(-gg))) * uu
    return _mm(act.astype(BF16), wd)


def _ffn_kernel(x_ref, g_ref, wg_ref, wu_ref, wd_ref, o_ref, xn_ref, acc_ref):
    j = pl.program_id(1)

    @pl.when(j == 0)
    def _():
        xn_ref[...] = _rms_rows(x_ref[...], g_ref[...]).astype(BF16)
        acc_ref[...] = jnp.zeros_like(acc_ref)

    acc_ref[...] += _swiglu_tile(xn_ref[...], wg_ref[...], wu_ref[...], wd_ref[...])

    @pl.when(j == pl.num_programs(1) - 1)
    def _():
        o_ref[...] = x_ref[...] + acc_ref[...]


def dense_ffn(x2d, gain, w_gu, w_down, *, tm=1024, tf=256):
    n, d = x2d.shape
    tm = min(tm, n)
    f = w_down.shape[0]
    nf = f // tf
    return pl.pallas_call(
        _ffn_kernel,
        grid=(n // tm, nf),
        in_specs=[pl.BlockSpec((tm, d), lambda i, j: (i, 0)),
                  pl.BlockSpec((1, d), lambda i, j: (0, 0)),
                  pl.BlockSpec((d, tf), lambda i, j: (0, j)),
                  pl.BlockSpec((d, tf), lambda i, j: (0, j + nf)),
                  pl.BlockSpec((tf, d), lambda i, j: (j, 0))],
        out_specs=pl.BlockSpec((tm, d), lambda i, j: (i, 0)),
        out_shape=jax.ShapeDtypeStruct((n, d), F32),
        scratch_shapes=[pltpu.VMEM((tm, d), BF16), pltpu.VMEM((tm, d), F32)],
        compiler_params=_params(("parallel", "arbitrary")),
        name="dense_ffn",
    )(x2d, gain, w_gu, w_gu, w_down)


def _moe_kernel(te_ref, tn_ref, tok_ref, tok_next_ref, dst_prev_ref, x_hbm, g_ref, wg_ref, wu_ref,
                wd_ref, y_hbm, xg_ref, xn_ref, acc_ref, yb_ref, gsem, ssem, *, tm, nf):
    i = pl.program_id(0)
    j = pl.program_id(1)
    nt = pl.num_programs(0)
    slot = i % 2
    per_step = tm // nf
    active = tn_ref[i] > 0
    prev_active = (i > 0) & (tn_ref[jnp.maximum(i - 1, 0)] > 0)
    prev_issued = (i > 0) & (tn_ref[jnp.maximum(i - 2, 0)] > 0)

    def gather_copy(tok, r, s):
        return pltpu.make_async_copy(x_hbm.at[pl.ds(tok, 1), :], xg_ref.at[s, pl.ds(r, 1), :], gsem.at[s])

    def scatter_copy(dst, r, s):
        return pltpu.make_async_copy(yb_ref.at[s, pl.ds(r, 1), :], y_hbm.at[pl.ds(dst, 1), :], ssem.at[s])

    def start_all_rows(ids_ref, s, make_copy):
        def body(r8, c):
            for u in range(8):
                r = r8 * 8 + u
                make_copy(ids_ref[0, 0, r], r, s).start()
            return c
        lax.fori_loop(0, tm // 8, body, 0)

    def wait_gather(s):
        pltpu.make_async_copy(x_hbm.at[pl.ds(0, tm), :], xg_ref.at[s], gsem.at[s]).wait()

    def wait_scatter(s):
        pltpu.make_async_copy(yb_ref.at[s], y_hbm.at[pl.ds(0, tm), :], ssem.at[s]).wait()

    @pl.when(j == 0)
    def _():
        @pl.when(i == 0)
        def _():
            start_all_rows(tok_ref, slot, gather_copy)
            yb_ref[1] = jnp.zeros(yb_ref.shape[1:], yb_ref.dtype)
            n_real = y_hbm.shape[0] - 2 * tm
            for half in range(2):
                spare = pltpu.make_async_copy(yb_ref.at[1], y_hbm.at[pl.ds(n_real + half * tm, tm), :], ssem.at[1])
                spare.start()
                spare.wait()

        @pl.when(active | prev_active)
        def _():
            wait_gather(slot)

        @pl.when(prev_issued)
        def _():
            wait_scatter(slot)

        @pl.when(active)
        def _():
            xn_ref[...] = _rms_rows(xg_ref[slot], g_ref[...]).astype(BF16)
            acc_ref[...] = jnp.zeros_like(acc_ref)

        @pl.when(prev_active & jnp.logical_not(active))
        def _():
            start_all_rows(dst_prev_ref, 1 - slot, scatter_copy)

    @pl.when(active)
    def _():
        acc_ref[...] += _swiglu_tile(xn_ref[...], wg_ref[0], wu_ref[0], wd_ref[0])
        for u in range(per_step):
            r = j * per_step + u
            gather_copy(tok_next_ref[0, 0, r], r, 1 - slot).start(priority=1)
            scatter_copy(dst_prev_ref[0, 0, r], r, 1 - slot).start(priority=1)

        @pl.when(j == nf - 1)
        def _():
            yb_ref[slot] = acc_ref[...]

    @pl.when((j == nf - 1) & (i == nt - 1) & (active | prev_active))
    def _():
        wait_scatter(1 - slot)


def moe_experts(x2d, gain, plan, w_gu, w_down, *, tm, tf=512):
    n, d = x2d.shape
    ne, f, _ = w_down.shape
    nf = f // tf
    assert tm % nf == 0 and tm % 16 == 0
    tile_e, tile_n, row_tok, row_dst_prev = plan
    nt = tile_e.shape[0]
    smem_rows = lambda imap: pl.BlockSpec((1, 1, tm), imap, memory_space=pltpu.SMEM)
    live = lambda j, tn, i: j * jnp.minimum(tn[i], 1)
    return pl.pallas_call(
        functools.partial(_moe_kernel, tm=tm, nf=nf),
        grid_spec=pltpu.PrefetchScalarGridSpec(
            num_scalar_prefetch=2,
            grid=(nt, nf),
            in_specs=[smem_rows(lambda i, j, te, tn: (i, 0, 0)),
                      smem_rows(lambda i, j, te, tn: (jnp.minimum(i + 1, nt - 1), 0, 0)),
                      smem_rows(lambda i, j, te, tn: (i, 0, 0)),
                      pl.BlockSpec(memory_space=pl.ANY),
                      pl.BlockSpec((1, d), lambda i, j, te, tn: (0, 0)),
                      pl.BlockSpec((1, d, tf), lambda i, j, te, tn: (te[i], 0, live(j, tn, i))),
                      pl.BlockSpec((1, d, tf), lambda i, j, te, tn: (te[i], 0, live(j, tn, i) + nf)),
                      pl.BlockSpec((1, tf, d), lambda i, j, te, tn: (te[i], live(j, tn, i), 0))],
            out_specs=pl.BlockSpec(memory_space=pl.ANY),
            scratch_shapes=[pltpu.VMEM((2, tm, d), F32), pltpu.VMEM((tm, d), BF16), pltpu.VMEM((tm, d), F32),
                            pltpu.VMEM((2, tm, d), F32),
                            pltpu.SemaphoreType.DMA((2,)), pltpu.SemaphoreType.DMA((2,))]),
        out_shape=jax.ShapeDtypeStruct((2 * n + 2 * tm, d), F32),
        compiler_params=_params(("arbitrary", "arbitrary")),
        name="moe_experts",
    )(tile_e, tile_n, row_tok, row_tok, row_dst_prev, x2d, gain, w_gu, w_gu, w_down)


def moe_plan(route, *, tm):
    n = route.shape[0]
    flat_e = route[:, :2].astype(jnp.int32).reshape(-1)
    nt = (2 * n) // tm + N_EXPERTS + 1
    order = jnp.argsort(flat_e, stable=True).astype(jnp.int32)
    counts = jnp.sum(flat_e[:, None] == jnp.arange(N_EXPERTS)[None, :], axis=0).astype(jnp.int32)
    off = jnp.cumsum(counts) - counts
    tiles = (counts + tm - 1) // tm
    tile_off = jnp.cumsum(tiles) - tiles
    tile_id = jnp.arange(nt, dtype=jnp.int32)
    used = tile_id < jnp.sum(tiles)
    tile_e = jnp.clip(jnp.sum(tile_id[:, None] >= tile_off[None, :], axis=1) - 1, 0, N_EXPERTS - 1)
    tile_e = jnp.where(used, tile_e, tile_e[jnp.maximum(jnp.sum(tiles) - 1, 0)]).astype(jnp.int32)
    first_row = (tile_id - tile_off[tile_e]) * tm
    tile_n = jnp.where(used, jnp.clip(counts[tile_e] - first_row, 0, tm), 0).astype(jnp.int32)
    r = jnp.arange(tm, dtype=jnp.int32)[None, :]
    valid = r < tile_n[:, None]
    a = order[jnp.clip(off[tile_e][:, None] + first_row[:, None] + r, 0, 2 * n - 1)]
    row_tok = jnp.where(valid, a // 2, 0).astype(jnp.int32).reshape(nt, 1, tm)
    spare = 2 * n + (tile_id[:, None] % 2) * tm + r
    row_dst = jnp.where(valid, (a % 2) * n + a // 2, spare).astype(jnp.int32)
    row_dst_prev = jnp.concatenate([2 * n + tm + r, row_dst[:-1]], axis=0).reshape(nt, 1, tm)
    return tile_e, tile_n, row_tok, row_dst_prev


def _combine_kernel(x_ref, route_ref, y0_ref, y1_ref, o_ref):
    o_ref[...] = x_ref[...] + (route_ref[:, 2:3] * y0_ref[...] + route_ref[:, 3:4] * y1_ref[...])


def moe_combine(x2d, route, y, *, tm=1024):
    n, d = x2d.shape
    tm = min(tm, n)
    nb = n // tm
    return pl.pallas_call(
        _combine_kernel,
        grid=(nb,),
        in_specs=[pl.BlockSpec((tm, d), lambda i: (i, 0)),
                  pl.BlockSpec((tm, LANES), lambda i: (i, 0)),
                  pl.BlockSpec((tm, d), lambda i: (i, 0)),
                  pl.BlockSpec((tm, d), lambda i: (i + nb, 0))],
        out_specs=pl.BlockSpec((tm, d), lambda i: (i, 0)),
        out_shape=jax.ShapeDtypeStruct((n, d), F32),
        compiler_params=_params(("parallel",)),
        name="moe_combine",
    )(x2d, route, y, y)


def _pack_w_in(w, d):
    sizes = (256, 256, 256, 256, 64, 64, 256, 64, 4, 512, 128, 128, N_BRANCH * d)
    qa, ka, va, qb, kb, vb, qi, ki, wi, qc, kc, vc, g = jnp.split(w, np.cumsum(sizes)[:-1].tolist(), axis=-1)
    pad = jnp.zeros((w.shape[0], 60), w.dtype)
    mixers = jnp.concatenate([qa * (HEAD_DIM ** -0.5 * LOG2E), ka, va,
                              qb, qi * D_IDX ** -0.5, kb, vb, ki, wi, pad,
                              qc, kc, vc], axis=-1)
    return mixers.astype(BF16), g.astype(BF16)


def kernel(x, mem, rel_bias, norm_mix, w_in, b_gate, qn_dsa, kn_dsa, qn_swa, kn_swa, sinks, w_pa, w_pb, w_pc, w_out, norm_x, norm_mem, w_xq, w_xkv, w_xo, qn_x, kn_x, norm_ffn, w_gu_dense, w_down_dense, w_router, w_gu_moe, w_down_moe):
    b, s, d = x.shape
    depth = w_in.shape[0]
    n = b * s
    row = lambda v: v.reshape(1, -1).astype(F32)
    bias_dsa = dsa_bias_tiles(rel_bias[:, :H_DSA] * LOG2E)
    bias_swa = swa_bias_tiles(rel_bias[:, H_DSA:] * LOG2E)
    qscale = HEAD_DIM ** -0.5 * LOG2E

    x2 = x.reshape(n, d)
    for l in range(depth):
        w_mix, w_gate = _pack_w_in(w_in[l], d)
        proj3 = in_projection(x2, row(norm_mix[l]), w_mix, tn=w_mix.shape[1]).reshape(b, s, -1)
        o_a = sb_attention(proj3, 0)
        o_b = dsa_attention(proj3, 1, row(jnp.tile(qn_dsa[l] * qscale, H_DSA)),
                            row(jnp.concatenate([kn_dsa[l], jnp.zeros_like(kn_dsa[l])])), bias_dsa)
        o_c = swa_attention(proj3, 2, sinks[l].astype(F32) * LOG2E,
                            row(jnp.tile(qn_swa[l] * qscale, H_SW)), row(jnp.tile(kn_swa[l], KV_SW)), bias_swa)
        x2 = merge_project(x2, row(norm_mix[l]), w_gate, o_a.reshape(n, -1), o_b.reshape(n, -1), o_c.reshape(n, -1),
                           b_gate[l].astype(F32), w_pa[l].astype(BF16), w_pb[l].astype(BF16),
                           w_pc[l].astype(BF16), w_out[l].astype(BF16))
        k_mem, v_mem = memory_kv(mem, row(norm_mem[l]), w_xkv[l].astype(BF16), row(jnp.tile(kn_x[l], H_X)))
        x2 = cross_attention(x2.reshape(b, s, d), row(norm_x[l]), w_xq[l].astype(BF16),
                             row(jnp.tile(qn_x[l] * (XHEAD_DIM ** -0.5 * LOG2E), H_X)), k_mem, v_mem,
                             w_xo[l].astype(BF16)).reshape(n, d)
        if l % 2 == 0:
            x2 = dense_ffn(x2, row(norm_ffn[l]), w_gu_dense[l // 2].astype(BF16),
                           w_down_dense[l // 2].astype(BF16), tf=256)
        else:
            wr = jnp.pad(w_router[l // 2].astype(F32), ((0, 0), (0, LANES - N_EXPERTS)))
            wr_hi = wr.astype(BF16)
            wr_lo = (wr - wr_hi.astype(F32)).astype(BF16)
            route = router_gates(x2, row(norm_ffn[l]), wr_hi, wr_lo)
            tm_moe = 672
            y = moe_experts(x2, row(norm_ffn[l]), moe_plan(route, tm=tm_moe), w_gu_moe[l // 2].astype(BF16),
                            w_down_moe[l // 2].astype(BF16), tm=tm_moe)
            x2 = moe_combine(x2, route, y)
    return x2.reshape(b, s, d)
```

```python
import functools
import math

import numpy as np
import jax
import jax.numpy as jnp
from jax import lax
from jax.experimental import pallas as pl
from jax.experimental.pallas import tpu as pltpu

HEAD_DIM = 64
H_SB = 4
H_DSA = 4
H_IDX = 4
D_IDX = 64
TOPK_MAX = 256
H_SW = 8
KV_SW = 2
WINDOW = 128
BLOCK = 128
N_BRANCH = 3
N_BUCKETS = 32
MAX_DISTANCE = 128
H_X = 4
XHEAD_DIM = 128
N_EXPERTS = 8
EPS = 1e-6

LANES = 128
VMEM_LIMIT = 56 * 1024 * 1024
NEG = -1e30
INT_MIN = -(2 ** 31)
LOG2E = math.log2(math.e)

F32 = jnp.float32
BF16 = jnp.bfloat16


def _nt(a, b):
    return lax.dot_general(a, b, (((1,), (1,)), ((), ())), preferred_element_type=F32)


def _mm(a, b):
    return jnp.dot(a, b, preferred_element_type=F32)


def _rms_rows(x, g):
    ms = jnp.mean(x * x, axis=-1, keepdims=True)
    return x * lax.rsqrt(ms + EPS) * g


def _params(sem):
    return pltpu.CompilerParams(dimension_semantics=sem, vmem_limit_bytes=VMEM_LIMIT)


def _sb_kernel(a_ref, o_ref, *, tq):
    i = pl.program_id(1)
    q0 = pl.multiple_of(i * tq, tq)
    hd = HEAD_DIM
    row = lax.broadcasted_iota(jnp.int32, (tq, tq), 0)
    col = lax.broadcasted_iota(jnp.int32, (tq, tq), 1)
    strict = col < row
    u_inc = jnp.where(row >= col, 1.0, 0.0).astype(BF16)
    qs = [a_ref[0, pl.ds(q0, tq), h * hd:(h + 1) * hd] for h in range(H_SB)]

    heads = range(H_SB)

    def block(k0s, accs, carries, diag):
        chains = [(b, h) for b in range(len(k0s)) for h in heads]
        ks = {(b, h): a_ref[0, pl.ds(k0s[b], tq), 256 + h * hd:256 + (h + 1) * hd] for b, h in chains}
        vs = {(b, h): a_ref[0, pl.ds(k0s[b], tq), 512 + h * hd:512 + (h + 1) * hd] for b, h in chains}
        zs = {c: _nt(qs[c[1]], ks[c]) for c in chains}
        lks = {c: -(jnp.maximum(zs[c], 0.0) + jnp.log2(1.0 + jnp.exp2(-jnp.abs(zs[c])))) for c in chains}
        if diag:
            lks = {c: jnp.where(strict, lks[c], 0.0) for c in chains}
        rs = {c: _mm(lks[c].astype(BF16), u_inc) for c in chains}
        carry = {h: (None if diag else carries[h]) for h in heads}
        atts = {}
        for b, h in chains:
            if diag:
                atts[(b, h)] = jnp.where(strict, jnp.exp2(zs[(b, h)] + rs[(b, h)]), 0.0)
                carry[h] = rs[(b, h)][:, 0:1]
            else:
                atts[(b, h)] = jnp.exp2(zs[(b, h)] + rs[(b, h)] + carry[h])
                carry[h] = carry[h] + rs[(b, h)][:, 0:1]
        pvs = {c: _mm(atts[c].astype(BF16), vs[c]) for c in chains}
        new_acc = []
        for h in heads:
            tot = pvs[(0, h)] if diag else accs[h] + pvs[(0, h)]
            for b in range(1, len(k0s)):
                tot = tot + pvs[(b, h)]
            new_acc.append(tot)
        return tuple(new_acc), tuple(carry[h] for h in heads)

    def query_tile(n_before):
        state = block([q0], None, None, True)
        b = n_before - 1
        while b >= 1:
            state = block([b * tq, (b - 1) * tq], state[0], state[1], False)
            b -= 2
        if b == 0:
            state = block([0], state[0], state[1], False)
        o_ref[0] = jnp.concatenate(state[0], axis=-1).astype(o_ref.dtype)
        return 0

    lax.switch(i, [functools.partial(query_tile, n) for n in range(a_ref.shape[1] // tq)])


def sb_attention(proj, col_block, *, tq=256):
    b, s, _ = proj.shape
    return pl.pallas_call(
        functools.partial(_sb_kernel, tq=tq),
        grid=(b, s // tq),
        in_specs=[pl.BlockSpec((1, s, 768), lambda bi, i: (bi, 0, col_block))],
        out_specs=pl.BlockSpec((1, tq, H_SB * HEAD_DIM), lambda bi, i: (bi, i, 0)),
        out_shape=jax.ShapeDtypeStruct((b, s, H_SB * HEAD_DIM), BF16),
        compiler_params=_params(("parallel", "arbitrary")),
        name="sb_attention",
    )(proj)


def _inproj_kernel(x_ref, g_ref, w_ref, o_ref, xn_ref):
    @pl.when(pl.program_id(1) == 0)
    def _():
        xn_ref[...] = _rms_rows(x_ref[...], g_ref[...]).astype(BF16)

    o_ref[...] = _mm(xn_ref[...], w_ref[...]).astype(o_ref.dtype)


def in_projection(x2d, gain, w, *, tm=1024, tn=768):
    n, d = x2d.shape
    tm = min(tm, n)
    c = w.shape[1]
    return pl.pallas_call(
        _inproj_kernel,
        grid=(n // tm, c // tn),
        in_specs=[pl.BlockSpec((tm, d), lambda i, j: (i, 0)),
                  pl.BlockSpec((1, d), lambda i, j: (0, 0)),
                  pl.BlockSpec((d, tn), lambda i, j: (0, j))],
        out_specs=pl.BlockSpec((tm, tn), lambda i, j: (i, j)),
        out_shape=jax.ShapeDtypeStruct((n, c), BF16),
        scratch_shapes=[pltpu.VMEM((tm, d), BF16)],
        compiler_params=_params(("parallel", "arbitrary")),
        name="in_projection",
    )(x2d, gain, w)


def _head_sumsq(x, head_dim):
    r_i = lax.broadcasted_iota(jnp.int32, (LANES, LANES), 0) // head_dim
    c_i = lax.broadcasted_iota(jnp.int32, (LANES, LANES), 1) // head_dim
    bd = jnp.where(r_i == c_i, 1.0, 0.0).astype(F32)
    x2 = x * x
    parts = [_mm(x2[:, b * LANES:(b + 1) * LANES], bd) for b in range(x.shape[1] // LANES)]
    return parts[0] if len(parts) == 1 else jnp.concatenate(parts, axis=-1)


def _head_rms(x, g, head_dim):
    ss = _head_sumsq(x, head_dim)
    return x * lax.rsqrt(ss * (1.0 / head_dim) + EPS) * g


def _eye(n, dtype):
    r = lax.broadcasted_iota(jnp.int32, (n, n), 0)
    c = lax.broadcasted_iota(jnp.int32, (n, n), 1)
    return jnp.where(r == c, 1.0, 0.0).astype(dtype)


def _dsa_kernel(a_ref, gq_ref, gk_ref, bias_ref, o_ref, kbn_ref, vt_ref, keys_ref, tri_ref, *, tq, ck, topk):
    s_len = a_ref.shape[1]
    i = pl.program_id(1)
    q0 = pl.multiple_of(i * tq, tq)
    hd = HEAD_DIM
    sub = ck // tq

    @pl.when(i == 0)
    def _():
        kv = a_ref[0, :, 512:640]
        kvf = kv.astype(F32)
        lane = lax.broadcasted_iota(jnp.int32, (1, LANES), 1)
        ss = _head_sumsq(jnp.where(lane < hd, kvf, 0.0), LANES)
        kn = kvf * lax.rsqrt(ss * (1.0 / hd) + EPS) * gk_ref[...]
        kbn_ref[...] = kn[:, :hd].astype(BF16)
        kvt = _nt(_eye(LANES, BF16), kv)
        for cb in range(s_len // ck):
            vt_ref[cb] = kvt[hd:, cb * ck:(cb + 1) * ck].astype(BF16)
        tri_ref[...] = jnp.where(lax.broadcasted_iota(jnp.int32, (ck, ck), 1)
                                 <= lax.broadcasted_iota(jnp.int32, (ck, ck), 0), 1.0, 0.0).astype(BF16)

    def heads_on_rows(x):
        return jnp.concatenate([x[:, h * hd:(h + 1) * hd] for h in range(x.shape[1] // hd)], axis=0)

    qi_all = heads_on_rows(a_ref[0, pl.ds(q0, tq), 256:512])
    wblk = a_ref[0, pl.ds(q0, tq), 640:768]
    sel_r = lax.broadcasted_iota(jnp.int32, (8, LANES), 0)
    sel_c = lax.broadcasted_iota(jnp.int32, (8, LANES), 1)
    w_t = _nt(jnp.where(sel_c == sel_r + D_IDX, 1.0, 0.0).astype(BF16), wblk)
    qpos = q0 + lax.broadcasted_iota(jnp.int32, (1, tq), 1)
    nck = (q0 + tq + ck - 1) // ck
    row_ck = lax.broadcasted_iota(jnp.int32, (ck, tq), 0)

    qb = a_ref[0, pl.ds(q0, tq), 0:256].astype(F32)
    qn_all = heads_on_rows(_head_rms(qb, gq_ref[...], hd).astype(BF16))

    def index_chunk(c):
        act = jnp.maximum(_nt(a_ref[0, c * ck:(c + 1) * ck, 640:704], qi_all), 0.0)
        sc = w_t[0:1, :] * act[:, 0:tq]
        for h in range(1, H_IDX):
            sc = sc + w_t[h:h + 1, :] * act[:, h * tq:(h + 1) * tq]
        bits = pltpu.bitcast(sc, jnp.int32)
        key = jnp.where(bits < 0, -(bits & 0x7FFFFFFF), bits)
        keys_ref[c * ck:(c + 1) * ck, :] = jnp.where(c * ck + row_ck <= qpos, key, INT_MIN)

    def search(n_chunks):
        def count(pred):
            acc = jnp.zeros((8, tq), jnp.int32)
            for c in range(n_chunks):
                m = jnp.where(pred(keys_ref[c * ck:(c + 1) * ck, :]), 1, 0)
                acc = acc + jnp.sum(m.reshape(ck // 8, 8, tq), axis=0)
            return jnp.sum(acc, axis=0, keepdims=True)

        c0 = count(lambda k: k >= 0)
        t = jnp.where(c0 >= topk, 0, INT_MIN).astype(jnp.int32)

        def vstep(b, t):
            cand = t | lax.shift_left(jnp.int32(1), 30 - b)
            return jnp.where(count(lambda k: k >= cand) >= topk, cand, t)

        t = lax.fori_loop(0, 31, vstep, t)
        need = topk - count(lambda k: k > t)
        return t, jnp.where(t == INT_MIN, 0, need).astype(F32)

    def no_search():
        return jnp.full((1, tq), INT_MIN, jnp.int32), jnp.zeros((1, tq), F32)

    def attend_chunk(c, st, thr, need):
        ms, ls, accs, ties_before = st
        s_all = _nt(kbn_ref[c * ck:(c + 1) * ck, :], qn_all)
        vtc = vt_ref[c]
        key = keys_ref[c * ck:(c + 1) * ck, :]
        tie = key == thr
        tie_rank = ties_before + _mm(tri_ref[...], jnp.where(tie, 1.0, 0.0).astype(BF16))
        sel = (key > thr) | (tie & (tie_rank <= need))
        ties_before = tie_rank[ck - 1:ck, :]
        bidx = [jnp.clip(i - (c * sub + r), 0, 2) for r in range(sub)]
        heads = range(H_DSA)
        s_h = [s_all[:, h * tq:(h + 1) * tq]
               + jnp.concatenate([bias_ref[bidx[r], h] for r in range(sub)], axis=0) for h in heads]
        m_new = [jnp.maximum(ms[h], jnp.max(jnp.where(sel, s_h[h], NEG), axis=0, keepdims=True)) for h in heads]
        p = [jnp.where(sel, jnp.exp2(s_h[h] - m_new[h]), 0.0) for h in heads]
        alpha = [jnp.exp2(ms[h] - m_new[h]) for h in heads]
        pv = [_mm(vtc, p[h].astype(BF16)) for h in heads]
        nl = [ls[h] * alpha[h] + jnp.sum(p[h], axis=0, keepdims=True) for h in heads]
        na = [accs[h] * alpha[h] + pv[h] for h in heads]
        return tuple(m_new), tuple(nl), tuple(na), ties_before

    def query_block(n_chunks):
        for c in range(n_chunks):
            index_chunk(c)
        thr, need = lax.cond(q0 + tq > topk, functools.partial(search, n_chunks), no_search)
        st = (tuple(jnp.full((1, tq), NEG, F32) for _ in range(H_DSA)),
              tuple(jnp.zeros((1, tq), F32) for _ in range(H_DSA)),
              tuple(jnp.zeros((hd, tq), F32) for _ in range(H_DSA)),
              jnp.zeros((1, tq), F32))
        for c in range(n_chunks):
            st = attend_chunk(c, st, thr, need)
        _, ls, accs, _ = st
        o_t = jnp.concatenate([accs[h] / ls[h] for h in range(H_DSA)], axis=0)
        o_ref[0] = _nt(_eye(tq, BF16), o_t.astype(BF16)).astype(o_ref.dtype)
        return 0

    lax.switch(nck - 1, [functools.partial(query_block, n) for n in range(1, s_len // ck + 1)])


def dsa_attention(proj, col_block, gq, gk, bias_t, *, tq=BLOCK, ck=512):
    b, s, _ = proj.shape
    topk = min(TOPK_MAX, s // 4)
    ck = min(ck, s)
    assert tq >= MAX_DISTANCE and s % ck == 0 and ck % tq == 0
    return pl.pallas_call(
        functools.partial(_dsa_kernel, tq=tq, ck=ck, topk=topk),
        grid=(b, s // tq),
        in_specs=[pl.BlockSpec((1, s, 768), lambda bi, i: (bi, 0, col_block)),
                  pl.BlockSpec((1, 256), lambda bi, i: (0, 0)),
                  pl.BlockSpec((1, LANES), lambda bi, i: (0, 0)),
                  pl.BlockSpec((3, H_DSA, tq, tq), lambda bi, i: (0, 0, 0, 0))],
        out_specs=pl.BlockSpec((1, tq, H_DSA * HEAD_DIM), lambda bi, i: (bi, i, 0)),
        out_shape=jax.ShapeDtypeStruct((b, s, H_DSA * HEAD_DIM), BF16),
        scratch_shapes=[pltpu.VMEM((s, HEAD_DIM), BF16),
                        pltpu.VMEM((s // ck, HEAD_DIM, ck), BF16),
                        pltpu.VMEM((s, tq), jnp.int32),
                        pltpu.VMEM((ck, ck), BF16)],
        compiler_params=_params(("parallel", "arbitrary")),
        name="dsa_attention",
    )(proj, gq, gk, bias_t)


def _t5_bucket(rel):
    n = jnp.maximum(rel, 0)
    max_exact = N_BUCKETS // 2
    nf = jnp.maximum(n, 1).astype(F32)
    large = max_exact + (jnp.log(nf / max_exact) / math.log(MAX_DISTANCE / max_exact)
                         * (N_BUCKETS - max_exact)).astype(jnp.int32)
    large = jnp.minimum(large, N_BUCKETS - 1)
    return jnp.where(n < max_exact, n, large)


def _bucket_lookup(tab, rel):
    hit = _t5_bucket(rel)[..., None, None] == jnp.arange(N_BUCKETS)[:, None]
    return jnp.sum(jnp.where(hit, tab.astype(F32), 0.0), axis=-2)


def dsa_bias_tiles(tab, tq=BLOCK):
    ks = jnp.arange(tq)[:, None]
    tl = jnp.arange(tq)[None, :]
    rel = jnp.stack([tl - ks, tq + tl - ks, jnp.full((tq, tq), 2 * tq + MAX_DISTANCE)])
    return _bucket_lookup(tab, rel).transpose(0, 3, 1, 2)


def _swa_kernel(sink_ref, cur_ref, prev_ref, gq_ref, gk_ref, bias_ref, o_ref, *, tq, nb):
    i = pl.program_id(1)
    hd = HEAD_DIM
    g = H_SW // KV_SW
    qn = _head_rms(cur_ref[0, :, 0:512].astype(F32), gq_ref[...], hd).astype(BF16)
    kn = jnp.concatenate([_head_rms(prev_ref[0, :, 512:640].astype(F32), gk_ref[...], hd),
                          _head_rms(cur_ref[0, :, 512:640].astype(F32), gk_ref[...], hd)], axis=0).astype(BF16)
    vv = jnp.concatenate([prev_ref[0, :, 640:768], cur_ref[0, :, 640:768]], axis=0)
    col = lax.broadcasted_iota(jnp.int32, (tq, 2 * tq), 1)
    first = (col >= tq) | (i > 0)
    chains = [(b, h) for b in range(nb) for h in range(H_SW)]
    s = {}
    for b, h in chains:
        kv = h // g
        s_bh = _nt(qn[b * tq:(b + 1) * tq, h * hd:(h + 1) * hd], kn[b * tq:(b + 2) * tq, kv * hd:(kv + 1) * hd])
        s_bh = s_bh + bias_ref[h]
        s[(b, h)] = jnp.where(first, s_bh, NEG) if b == 0 else s_bh
    m = {c: jnp.maximum(jnp.max(s[c], axis=-1, keepdims=True), sink_ref[c[1]]) for c in chains}
    e = {c: jnp.exp2(s[c] - m[c]) for c in chains}
    den = {c: jnp.sum(e[c], axis=-1, keepdims=True) + jnp.exp2(sink_ref[c[1]] - m[c]) for c in chains}
    out = {(b, h): _mm(e[(b, h)].astype(BF16), vv[b * tq:(b + 2) * tq, (h // g) * hd:(h // g + 1) * hd]) / den[(b, h)]
           for b, h in chains}
    for b in range(nb):
        o_ref[0, b * tq:(b + 1) * tq, :] = jnp.concatenate([out[(b, h)] for h in range(H_SW)],
                                                           axis=-1).astype(o_ref.dtype)


def swa_bias_tiles(tab, tq=BLOCK):
    rel = (jnp.arange(tq)[:, None] + tq) - jnp.arange(2 * tq)[None, :]
    in_win = (rel >= 0) & (rel < WINDOW)
    bias = _bucket_lookup(tab, rel).transpose(2, 0, 1)
    return jnp.where(in_win[None], bias, NEG)


def swa_attention(proj, col_block, sinks, gq, gk, bias, *, tq=BLOCK, nb=4):
    b, s, _ = proj.shape
    assert WINDOW <= tq and s % (nb * tq) == 0
    return pl.pallas_call(
        functools.partial(_swa_kernel, tq=tq, nb=nb),
        grid_spec=pltpu.PrefetchScalarGridSpec(
            num_scalar_prefetch=0,
            grid=(b, s // (nb * tq)),
            in_specs=[pl.BlockSpec(memory_space=pltpu.SMEM),
                      pl.BlockSpec((1, nb * tq, 768), lambda bi, i: (bi, i, col_block)),
                      pl.BlockSpec((1, tq, 768), lambda bi, i: (bi, jnp.maximum(nb * i - 1, 0), col_block)),
                      pl.BlockSpec((1, 512), lambda bi, i: (0, 0)),
                      pl.BlockSpec((1, LANES), lambda bi, i: (0, 0)),
                      pl.BlockSpec((H_SW, tq, 2 * tq), lambda bi, i: (0, 0, 0))],
            out_specs=pl.BlockSpec((1, nb * tq, H_SW * HEAD_DIM), lambda bi, i: (bi, i, 0))),
        out_shape=jax.ShapeDtypeStruct((b, s, H_SW * HEAD_DIM), BF16),
        compiler_params=_params(("parallel", "arbitrary")),
        name="swa_attention",
    )(sinks, proj, proj, gq, gk, bias)


def _merge_kernel(x_ref, gn_ref, wg_ref, oa_ref, ob_ref, oc_ref, bg_ref, wa_ref, wb_ref, wc_ref, wo_ref, o_ref):
    d = x_ref.shape[1]
    x = x_ref[...]
    xn = _rms_rows(x, gn_ref[...]).astype(BF16)
    merged = None
    for k, (o_k, w_k) in enumerate(((oa_ref, wa_ref), (ob_ref, wb_ref), (oc_ref, wc_ref))):
        logit = _mm(xn, wg_ref[:, k * d:(k + 1) * d]) + bg_ref[k:k + 1, :]
        gate = 1.0 / (1.0 + jnp.exp(-logit))
        term = gate * _mm(o_k[...], w_k[...])
        merged = term if merged is None else merged + term
    o_ref[...] = x + _mm(merged.astype(BF16), wo_ref[...])


def merge_project(x2d, gain, w_gate, o_a, o_b, o_c, b_gate, w_pa, w_pb, w_pc, w_out, *, tm=512):
    n, d = x2d.shape
    full = lambda a: pl.BlockSpec(a.shape, lambda i: (0,) * a.ndim)
    row = lambda a: pl.BlockSpec((tm, a.shape[1]), lambda i: (i, 0))
    return pl.pallas_call(
        _merge_kernel,
        grid=(n // tm,),
        in_specs=[row(x2d), full(gain), full(w_gate), row(o_a), row(o_b), row(o_c),
                  full(b_gate), full(w_pa), full(w_pb), full(w_pc), full(w_out)],
        out_specs=pl.BlockSpec((tm, d), lambda i: (i, 0)),
        out_shape=jax.ShapeDtypeStruct((n, d), F32),
        compiler_params=_params(("parallel",)),
        name="merge_project",
    )(x2d, gain, w_gate, o_a, o_b, o_c, b_gate, w_pa, w_pb, w_pc, w_out)


def _memkv_kernel(m_ref, g_ref, w_ref, gk_ref, k_ref, v_ref):
    w_x = k_ref.shape[2]
    mn = _rms_rows(m_ref[0], g_ref[...]).astype(BF16)
    kv = _mm(mn, w_ref[...])
    k_ref[0] = _head_rms(kv[:, :w_x], gk_ref[...], XHEAD_DIM).astype(k_ref.dtype)
    v_ref[0] = kv[:, w_x:].astype(v_ref.dtype)


def memory_kv(mem, gain, w_kv, gk):
    b, m, d = mem.shape
    w_x = w_kv.shape[1] // 2
    return pl.pallas_call(
        _memkv_kernel,
        grid=(b,),
        in_specs=[pl.BlockSpec((1, m, d), lambda i: (i, 0, 0)),
                  pl.BlockSpec((1, d), lambda i: (0, 0)),
                  pl.BlockSpec(w_kv.shape, lambda i: (0, 0)),
                  pl.BlockSpec((1, w_x), lambda i: (0, 0))],
        out_specs=[pl.BlockSpec((1, m, w_x), lambda i: (i, 0, 0)),
                   pl.BlockSpec((1, m, w_x), lambda i: (i, 0, 0))],
        out_shape=[jax.ShapeDtypeStruct((b, m, w_x), BF16)] * 2,
        compiler_params=_params(("parallel",)),
        name="memory_kv",
    )(mem, gain, w_kv, gk)


def _xattn_kernel(x_ref, g_ref, wq_ref, gq_ref, k_ref, v_ref, wo_ref, o_ref):
    x = x_ref[0]
    xn = _rms_rows(x, g_ref[...]).astype(BF16)
    q = _head_rms(_mm(xn, wq_ref[...]), gq_ref[...], XHEAD_DIM).astype(BF16)
    sl = [slice(h * XHEAD_DIM, (h + 1) * XHEAD_DIM) for h in range(H_X)]
    s = [_nt(q[:, c], k_ref[0, :, c]) for c in sl]
    e = [jnp.exp2(s_h - jnp.max(s_h, axis=-1, keepdims=True)) for s_h in s]
    den = [jnp.sum(e_h, axis=-1, keepdims=True) for e_h in e]
    outs = [_mm(e[h].astype(BF16), v_ref[0, :, sl[h]]) / den[h] for h in range(H_X)]
    o = jnp.concatenate(outs, axis=-1).astype(BF16)
    o_ref[0] = x + _mm(o, wo_ref[...])


def cross_attention(x, gain, w_q, gq, k_mem, v_mem, w_o, *, tq=512):
    b, s, d = x.shape
    m, w_x = k_mem.shape[1:]
    full = lambda a: pl.BlockSpec(a.shape, lambda bi, i: (0,) * a.ndim)
    return pl.pallas_call(
        _xattn_kernel,
        grid=(b, s // tq),
        in_specs=[pl.BlockSpec((1, tq, d), lambda bi, i: (bi, i, 0)),
                  full(gain), full(w_q), full(gq),
                  pl.BlockSpec((1, m, w_x), lambda bi, i: (bi, 0, 0)),
                  pl.BlockSpec((1, m, w_x), lambda bi, i: (bi, 0, 0)),
                  full(w_o)],
        out_specs=pl.BlockSpec((1, tq, d), lambda bi, i: (bi, i, 0)),
        out_shape=jax.ShapeDtypeStruct((b, s, d), F32),
        compiler_params=_params(("parallel", "parallel")),
        name="cross_attention",
    )(x, gain, w_q, gq, k_mem, v_mem, w_o)


def _router_kernel(x_ref, g_ref, whi_ref, wlo_ref, o_ref):
    hf = _rms_rows(x_ref[...], g_ref[...])
    hi = hf.astype(BF16)
    lo = (hf - hi.astype(F32)).astype(BF16)
    logits = _mm(hi, whi_ref[...]) + (_mm(hi, wlo_ref[...]) + _mm(lo, whi_ref[...]))
    lane = lax.broadcasted_iota(jnp.int32, logits.shape, 1).astype(F32)
    logits = jnp.where(lane < N_EXPERTS, logits, NEG)
    m1 = jnp.max(logits, axis=-1, keepdims=True)
    i1 = jnp.min(jnp.where(logits == m1, lane, float(LANES)), axis=-1, keepdims=True)
    rest = jnp.where(lane == i1, NEG, logits)
    m2 = jnp.max(rest, axis=-1, keepdims=True)
    i2 = jnp.min(jnp.where(rest == m2, lane, float(LANES)), axis=-1, keepdims=True)
    e2 = jnp.exp(m2 - m1)
    den = 1.0 + e2
    o_ref[...] = (jnp.where(lane == 0.0, i1, 0.0) + jnp.where(lane == 1.0, i2, 0.0)
                  + jnp.where(lane == 2.0, 1.0 / den, 0.0) + jnp.where(lane == 3.0, e2 / den, 0.0))


def router_gates(x2d, gain, w_hi, w_lo, *, tm=1024):
    n, d = x2d.shape
    tm = min(tm, n)
    return pl.pallas_call(
        _router_kernel,
        grid=(n // tm,),
        in_specs=[pl.BlockSpec((tm, d), lambda i: (i, 0)),
                  pl.BlockSpec((1, d), lambda i: (0, 0)),
                  pl.BlockSpec((d, LANES), lambda i: (0, 0)),
                  pl.BlockSpec((d, LANES), lambda i: (0, 0))],
        out_specs=pl.BlockSpec((tm, LANES), lambda i: (i, 0)),
        out_shape=jax.ShapeDtypeStruct((n, LANES), F32),
        compiler_params=_params(("parallel",)),
        name="router_gates",
    )(x2d, gain, w_hi, w_lo)


def _swiglu_tile(xn, wg, wu, wd):
    gg = _mm(xn, wg)
    uu = _mm(xn, wu)
    act = gg * (1.0 / (1.0 + jnp.exp(-gg))) * uu
    return _mm(act.astype(BF16), wd)


def _ffn_kernel(x_ref, g_ref, wgu_ref, wd_ref, o_ref, *, tf):
    f = wd_ref.shape[0]
    x = x_ref[...]
    xn = _rms_rows(x, g_ref[...]).astype(BF16)
    acc = None
    for c in range(f // tf):
        part = _swiglu_tile(xn, wgu_ref[:, c * tf:(c + 1) * tf], wgu_ref[:, f + c * tf:f + (c + 1) * tf],
                            wd_ref[c * tf:(c + 1) * tf, :])
        acc = part if acc is None else acc + part
    o_ref[...] = x + acc


def dense_ffn(x2d, gain, w_gu, w_down, *, tm=512, tf=256):
    n, d = x2d.shape
    f = w_down.shape[0]
    tm = min(tm, n)
    assert f % tf == 0
    resident = lambda a: pl.BlockSpec(a.shape, lambda i: (0,) * a.ndim, pipeline_mode=pl.Buffered(1))
    return pl.pallas_call(
        functools.partial(_ffn_kernel, tf=tf),
        grid=(n // tm,),
        in_specs=[pl.BlockSpec((tm, d), lambda i: (i, 0)),
                  pl.BlockSpec((1, d), lambda i: (0, 0)),
                  resident(w_gu), resident(w_down)],
        out_specs=pl.BlockSpec((tm, d), lambda i: (i, 0)),
        out_shape=jax.ShapeDtypeStruct((n, d), F32),
        compiler_params=_params(("parallel",)),
        name="dense_ffn",
    )(x2d, gain, w_gu, w_down)


def _moe_kernel(te_ref, tn_ref, tok_ref, tok_next_ref, dst_prev_ref, x_hbm, g_ref, wg_ref, wu_ref,
                wd_ref, y_hbm, xg_ref, xn_ref, acc_ref, yb_ref, gsem, ssem, *, tm, nf):
    i = pl.program_id(0)
    j = pl.program_id(1)
    nt = pl.num_programs(0)
    slot = i % 2
    per_step = tm // nf
    active = tn_ref[i] > 0
    prev_active = (i > 0) & (tn_ref[jnp.maximum(i - 1, 0)] > 0)
    prev_issued = (i > 0) & (tn_ref[jnp.maximum(i - 2, 0)] > 0)

    def gather_copy(tok, r, s):
        return pltpu.make_async_copy(x_hbm.at[pl.ds(tok, 1), :], xg_ref.at[s, pl.ds(r, 1), :], gsem.at[s])

    def scatter_copy(dst, r, s):
        return pltpu.make_async_copy(yb_ref.at[s, pl.ds(r, 1), :], y_hbm.at[pl.ds(dst, 1), :], ssem.at[s])

    def start_all_rows(ids_ref, s, make_copy):
        def body(r8, c):
            for u in range(8):
                r = r8 * 8 + u
                make_copy(ids_ref[0, 0, r], r, s).start()
            return c
        lax.fori_loop(0, tm // 8, body, 0)

    def wait_gather(s):
        pltpu.make_async_copy(x_hbm.at[pl.ds(0, tm), :], xg_ref.at[s], gsem.at[s]).wait()

    def wait_scatter(s):
        pltpu.make_async_copy(yb_ref.at[s], y_hbm.at[pl.ds(0, tm), :], ssem.at[s]).wait()

    @pl.when(j == 0)
    def _():
        @pl.when(i == 0)
        def _():
            start_all_rows(tok_ref, slot, gather_copy)
            yb_ref[1] = jnp.zeros(yb_ref.shape[1:], yb_ref.dtype)
            n_real = y_hbm.shape[0] - 2 * tm
            for half in range(2):
                spare = pltpu.make_async_copy(yb_ref.at[1], y_hbm.at[pl.ds(n_real + half * tm, tm), :], ssem.at[1])
                spare.start()
                spare.wait()

        @pl.when(active | prev_active)
        def _():
            wait_gather(slot)

        @pl.when(prev_issued)
        def _():
            wait_scatter(slot)

        @pl.when(active)
        def _():
            xn_ref[...] = _rms_rows(xg_ref[slot], g_ref[...]).astype(BF16)
            acc_ref[...] = jnp.zeros_like(acc_ref)

        @pl.when(prev_active & jnp.logical_not(active))
        def _():
            start_all_rows(dst_prev_ref, 1 - slot, scatter_copy)

    @pl.when(active)
    def _():
        acc_ref[...] += _swiglu_tile(xn_ref[...], wg_ref[0], wu_ref[0], wd_ref[0])
        for u in range(per_step):
            r = j * per_step + u
            gather_copy(tok_next_ref[0, 0, r], r, 1 - slot).start(priority=1)
            scatter_copy(dst_prev_ref[0, 0, r], r, 1 - slot).start(priority=1)

        @pl.when(j == nf - 1)
        def _():
            yb_ref[slot] = acc_ref[...]

    @pl.when((j == nf - 1) & (i == nt - 1) & (active | prev_active))
    def _():
        wait_scatter(1 - slot)


def moe_experts(x2d, gain, plan, w_gu, w_down, *, tm, tf=512):
    n, d = x2d.shape
    ne, f, _ = w_down.shape
    nf = f // tf
    assert tm % nf == 0 and tm % 16 == 0
    tile_e, tile_n, row_tok, row_dst_prev = plan
    nt = tile_e.shape[0]
    smem_rows = lambda imap: pl.BlockSpec((1, 1, tm), imap, memory_space=pltpu.SMEM)
    live = lambda j, tn, i: j * jnp.minimum(tn[i], 1)
    return pl.pallas_call(
        functools.partial(_moe_kernel, tm=tm, nf=nf),
        grid_spec=pltpu.PrefetchScalarGridSpec(
            num_scalar_prefetch=2,
            grid=(nt, nf),
            in_specs=[smem_rows(lambda i, j, te, tn: (i, 0, 0)),
                      smem_rows(lambda i, j, te, tn: (jnp.minimum(i + 1, nt - 1), 0, 0)),
                      smem_rows(lambda i, j, te, tn: (i, 0, 0)),
                      pl.BlockSpec(memory_space=pl.ANY),
                      pl.BlockSpec((1, d), lambda i, j, te, tn: (0, 0)),
                      pl.BlockSpec((1, d, tf), lambda i, j, te, tn: (te[i], 0, live(j, tn, i))),
                      pl.BlockSpec((1, d, tf), lambda i, j, te, tn: (te[i], 0, live(j, tn, i) + nf)),
                      pl.BlockSpec((1, tf, d), lambda i, j, te, tn: (te[i], live(j, tn, i), 0))],
            out_specs=pl.BlockSpec(memory_space=pl.ANY),
            scratch_shapes=[pltpu.VMEM((2, tm, d), F32), pltpu.VMEM((tm, d), BF16), pltpu.VMEM((tm, d), F32),
                            pltpu.VMEM((2, tm, d), F32),
                            pltpu.SemaphoreType.DMA((2,)), pltpu.SemaphoreType.DMA((2,))]),
        out_shape=jax.ShapeDtypeStruct((2 * n + 2 * tm, d), F32),
        compiler_params=_params(("arbitrary", "arbitrary")),
        name="moe_experts",
    )(tile_e, tile_n, row_tok, row_tok, row_dst_prev, x2d, gain, w_gu, w_gu, w_down)


def moe_plan(route, *, tm):
    n = route.shape[0]
    flat_e = route[:, :2].astype(jnp.int32).reshape(-1)
    nt = (2 * n) // tm + N_EXPERTS + 1
    order = jnp.argsort(flat_e, stable=True).astype(jnp.int32)
    counts = jnp.sum(flat_e[:, None] == jnp.arange(N_EXPERTS)[None, :], axis=0).astype(jnp.int32)
    off = jnp.cumsum(counts) - counts
    tiles = (counts + tm - 1) // tm
    tile_off = jnp.cumsum(tiles) - tiles
    tile_id = jnp.arange(nt, dtype=jnp.int32)
    used = tile_id < jnp.sum(tiles)
    tile_e = jnp.clip(jnp.sum(tile_id[:, None] >= tile_off[None, :], axis=1) - 1, 0, N_EXPERTS - 1)
    tile_e = jnp.where(used, tile_e, tile_e[jnp.maximum(jnp.sum(tiles) - 1, 0)]).astype(jnp.int32)
    first_row = (tile_id - tile_off[tile_e]) * tm
    tile_n = jnp.where(used, jnp.clip(counts[tile_e] - first_row, 0, tm), 0).astype(jnp.int32)
    r = jnp.arange(tm, dtype=jnp.int32)[None, :]
    valid = r < tile_n[:, None]
    a = order[jnp.clip(off[tile_e][:, None] + first_row[:, None] + r, 0, 2 * n - 1)]
    row_tok = jnp.where(valid, a // 2, 0).astype(jnp.int32).reshape(nt, 1, tm)
    spare = 2 * n + (tile_id[:, None] % 2) * tm + r
    row_dst = jnp.where(valid, (a % 2) * n + a // 2, spare).astype(jnp.int32)
    row_dst_prev = jnp.concatenate([2 * n + tm + r, row_dst[:-1]], axis=0).reshape(nt, 1, tm)
    return tile_e, tile_n, row_tok, row_dst_prev


def _combine_kernel(x_ref, route_ref, y0_ref, y1_ref, o_ref):
    o_ref[...] = x_ref[...] + (route_ref[:, 2:3] * y0_ref[...] + route_ref[:, 3:4] * y1_ref[...])


def moe_combine(x2d, route, y, *, tm=1024):
    n, d = x2d.shape
    tm = min(tm, n)
    nb = n // tm
    return pl.pallas_call(
        _combine_kernel,
        grid=(nb,),
        in_specs=[pl.BlockSpec((tm, d), lambda i: (i, 0)),
                  pl.BlockSpec((tm, LANES), lambda i: (i, 0)),
                  pl.BlockSpec((tm, d), lambda i: (i, 0)),
                  pl.BlockSpec((tm, d), lambda i: (i + nb, 0))],
        out_specs=pl.BlockSpec((tm, d), lambda i: (i, 0)),
        out_shape=jax.ShapeDtypeStruct((n, d), F32),
        compiler_params=_params(("parallel",)),
        name="moe_combine",
    )(x2d, route, y, y)


def _pack_w_in(w, d):
    sizes = (256, 256, 256, 256, 64, 64, 256, 64, 4, 512, 128, 128, N_BRANCH * d)
    qa, ka, va, qb, kb, vb, qi, ki, wi, qc, kc, vc, g = jnp.split(w, np.cumsum(sizes)[:-1].tolist(), axis=-1)
    pad = jnp.zeros((w.shape[0], 60), w.dtype)
    mixers = jnp.concatenate([qa * (HEAD_DIM ** -0.5 * LOG2E), ka, va,
                              qb, qi * D_IDX ** -0.5, kb, vb, ki, wi, pad,
                              qc, kc, vc], axis=-1)
    return mixers.astype(BF16), g.astype(BF16)


def kernel(x, mem, rel_bias, norm_mix, w_in, b_gate, qn_dsa, kn_dsa, qn_swa, kn_swa, sinks, w_pa, w_pb, w_pc, w_out, norm_x, norm_mem, w_xq, w_xkv, w_xo, qn_x, kn_x, norm_ffn, w_gu_dense, w_down_dense, w_router, w_gu_moe, w_down_moe):
    b, s, d = x.shape
    depth = w_in.shape[0]
    n = b * s
    row = lambda v: v.reshape(1, -1).astype(F32)
    bias_dsa = dsa_bias_tiles(rel_bias[:, :H_DSA] * LOG2E)
    bias_swa = swa_bias_tiles(rel_bias[:, H_DSA:] * LOG2E)
    qscale = HEAD_DIM ** -0.5 * LOG2E

    x2 = x.reshape(n, d)
    for l in range(depth):
        w_mix, w_gate = _pack_w_in(w_in[l], d)
        proj3 = in_projection(x2, row(norm_mix[l]), w_mix, tn=w_mix.shape[1]).reshape(b, s, -1)
        o_a = sb_attention(proj3, 0)
        o_b = dsa_attention(proj3, 1, row(jnp.tile(qn_dsa[l] * qscale, H_DSA)),
                            row(jnp.concatenate([kn_dsa[l], jnp.zeros_like(kn_dsa[l])])), bias_dsa)
        o_c = swa_attention(proj3, 2, sinks[l].astype(F32) * LOG2E,
                            row(jnp.tile(qn_swa[l] * qscale, H_SW)), row(jnp.tile(kn_swa[l], KV_SW)), bias_swa)
        x2 = merge_project(x2, row(norm_mix[l]), w_gate, o_a.reshape(n, -1), o_b.reshape(n, -1), o_c.reshape(n, -1),
                           b_gate[l].astype(F32), w_pa[l].astype(BF16), w_pb[l].astype(BF16),
                           w_pc[l].astype(BF16), w_out[l].astype(BF16))
        k_mem, v_mem = memory_kv(mem, row(norm_mem[l]), w_xkv[l].astype(BF16), row(jnp.tile(kn_x[l], H_X)))
        x2 = cross_attention(x2.reshape(b, s, d), row(norm_x[l]), w_xq[l].astype(BF16),
                             row(jnp.tile(qn_x[l] * (XHEAD_DIM ** -0.5 * LOG2E), H_X)), k_mem, v_mem,
                             w_xo[l].astype(BF16)).reshape(n, d)
        if l % 2 == 0:
            x2 = dense_ffn(x2, row(norm_ffn[l]), w_gu_dense[l // 2].astype(BF16),
                           w_down_dense[l // 2].astype(BF16), tf=256)
        else:
            wr = jnp.pad(w_router[l // 2].astype(F32), ((0, 0), (0, LANES - N_EXPERTS)))
            wr_hi = wr.astype(BF16)
            wr_lo = (wr - wr_hi.astype(F32)).astype(BF16)
            route = router_gates(x2, row(norm_ffn[l]), wr_hi, wr_lo)
            tm_moe = 672
            y = moe_experts(x2, row(norm_ffn[l]), moe_plan(route, tm=tm_moe), w_gu_moe[l // 2].astype(BF16),
                            w_down_moe[l // 2].astype(BF16), tm=tm_moe)
            x2 = moe_combine(x2, route, y)
    return x2.reshape(b, s, d)
```

```python
import functools
import math

import numpy as np
import jax
import jax.numpy as jnp
from jax import lax
from jax.experimental import pallas as pl
from jax.experimental.pallas import tpu as pltpu

HEAD_DIM = 64
H_SB = 4
H_DSA = 4
H_IDX = 4
D_IDX = 64
TOPK_MAX = 256
H_SW = 8
KV_SW = 2
WINDOW = 128
BLOCK = 128
N_BRANCH = 3
N_BUCKETS = 32
MAX_DISTANCE = 128
H_X = 4
XHEAD_DIM = 128
N_EXPERTS = 8
EPS = 1e-6

LANES = 128
VMEM_LIMIT = 56 * 1024 * 1024
NEG = -1e30
INT_MIN = -(2 ** 31)
LOG2E = math.log2(math.e)

F32 = jnp.float32
BF16 = jnp.bfloat16


def _nt(a, b):
    return lax.dot_general(a, b, (((1,), (1,)), ((), ())), preferred_element_type=F32)


def _mm(a, b):
    return jnp.dot(a, b, preferred_element_type=F32)


def _rms_rows(x, g):
    ms = jnp.mean(x * x, axis=-1, keepdims=True)
    return x * lax.rsqrt(ms + EPS) * g


def _params(sem):
    return pltpu.CompilerParams(dimension_semantics=sem, vmem_limit_bytes=VMEM_LIMIT)


def _sb_kernel(a_ref, o_ref, *, tq):
    i = pl.program_id(1)
    q0 = pl.multiple_of(i * tq, tq)
    hd = HEAD_DIM
    row = lax.broadcasted_iota(jnp.int32, (tq, tq), 0)
    col = lax.broadcasted_iota(jnp.int32, (tq, tq), 1)
    strict = col < row
    u_inc = jnp.where(row >= col, 1.0, 0.0).astype(BF16)
    qs = [a_ref[0, pl.ds(q0, tq), h * hd:(h + 1) * hd] for h in range(H_SB)]

    heads = range(H_SB)

    def block(k0s, accs, carries, diag):
        chains = [(b, h) for b in range(len(k0s)) for h in heads]
        ks = {(b, h): a_ref[0, pl.ds(k0s[b], tq), 256 + h * hd:256 + (h + 1) * hd] for b, h in chains}
        vs = {(b, h): a_ref[0, pl.ds(k0s[b], tq), 512 + h * hd:512 + (h + 1) * hd] for b, h in chains}
        zs = {c: _nt(qs[c[1]], ks[c]) for c in chains}
        lks = {c: -(jnp.maximum(zs[c], 0.0) + jnp.log2(1.0 + jnp.exp2(-jnp.abs(zs[c])))) for c in chains}
        if diag:
            lks = {c: jnp.where(strict, lks[c], 0.0) for c in chains}
        rs = {c: _mm(lks[c].astype(BF16), u_inc) for c in chains}
        carry = {h: (None if diag else carries[h]) for h in heads}
        atts = {}
        for b, h in chains:
            if diag:
                atts[(b, h)] = jnp.where(strict, jnp.exp2(zs[(b, h)] + rs[(b, h)]), 0.0)
                carry[h] = rs[(b, h)][:, 0:1]
            else:
                atts[(b, h)] = jnp.exp2(zs[(b, h)] + rs[(b, h)] + carry[h])
                carry[h] = carry[h] + rs[(b, h)][:, 0:1]
        pvs = {c: _mm(atts[c].astype(BF16), vs[c]) for c in chains}
        new_acc = []
        for h in heads:
            tot = pvs[(0, h)] if diag else accs[h] + pvs[(0, h)]
            for b in range(1, len(k0s)):
                tot = tot + pvs[(b, h)]
            new_acc.append(tot)
        return tuple(new_acc), tuple(carry[h] for h in heads)

    def query_tile(n_before):
        state = block([q0], None, None, True)
        b = n_before - 1
        while b >= 1:
            state = block([b * tq, (b - 1) * tq], state[0], state[1], False)
            b -= 2
        if b == 0:
            state = block([0], state[0], state[1], False)
        o_ref[0] = jnp.concatenate(state[0], axis=-1).astype(o_ref.dtype)
        return 0

    lax.switch(i, [functools.partial(query_tile, n) for n in range(a_ref.shape[1] // tq)])


def sb_attention(proj, col_block, *, tq=256):
    b, s, _ = proj.shape
    return pl.pallas_call(
        functools.partial(_sb_kernel, tq=tq),
        grid=(b, s // tq),
        in_specs=[pl.BlockSpec((1, s, 768), lambda bi, i: (bi, 0, col_block))],
        out_specs=pl.BlockSpec((1, tq, H_SB * HEAD_DIM), lambda bi, i: (bi, i, 0)),
        out_shape=jax.ShapeDtypeStruct((b, s, H_SB * HEAD_DIM), BF16),
        compiler_params=_params(("parallel", "arbitrary")),
        name="sb_attention",
    )(proj)


def _inproj_kernel(x_ref, g_ref, w_ref, o_ref, xn_ref):
    @pl.when(pl.program_id(1) == 0)
    def _():
        xn_ref[...] = _rms_rows(x_ref[...], g_ref[...]).astype(BF16)

    o_ref[...] = _mm(xn_ref[...], w_ref[...]).astype(o_ref.dtype)


def in_projection(x2d, gain, w, *, tm=1024, tn=768):
    n, d = x2d.shape
    tm = min(tm, n)
    c = w.shape[1]
    return pl.pallas_call(
        _inproj_kernel,
        grid=(n // tm, c // tn),
        in_specs=[pl.BlockSpec((tm, d), lambda i, j: (i, 0)),
                  pl.BlockSpec((1, d), lambda i, j: (0, 0)),
                  pl.BlockSpec((d, tn), lambda i, j: (0, j))],
        out_specs=pl.BlockSpec((tm, tn), lambda i, j: (i, j)),
        out_shape=jax.ShapeDtypeStruct((n, c), BF16),
        scratch_shapes=[pltpu.VMEM((tm, d), BF16)],
        compiler_params=_params(("parallel", "arbitrary")),
        name="in_projection",
    )(x2d, gain, w)


def _head_sumsq(x, head_dim):
    r_i = lax.broadcasted_iota(jnp.int32, (LANES, LANES), 0) // head_dim
    c_i = lax.broadcasted_iota(jnp.int32, (LANES, LANES), 1) // head_dim
    bd = jnp.where(r_i == c_i, 1.0, 0.0).astype(F32)
    x2 = x * x
    parts = [_mm(x2[:, b * LANES:(b + 1) * LANES], bd) for b in range(x.shape[1] // LANES)]
    return parts[0] if len(parts) == 1 else jnp.concatenate(parts, axis=-1)


def _head_rms(x, g, head_dim):
    ss = _head_sumsq(x, head_dim)
    return x * lax.rsqrt(ss * (1.0 / head_dim) + EPS) * g


def _eye(n, dtype):
    r = lax.broadcasted_iota(jnp.int32, (n, n), 0)
    c = lax.broadcasted_iota(jnp.int32, (n, n), 1)
    return jnp.where(r == c, 1.0, 0.0).astype(dtype)


def _dsa_kernel(a_ref, gq_ref, gk_ref, bias_ref, o_ref, kbn_ref, vt_ref, keys_ref, tri_ref, *, tq, ck, topk):
    s_len = a_ref.shape[1]
    i = pl.program_id(1)
    q0 = pl.multiple_of(i * tq, tq)
    hd = HEAD_DIM
    sub = ck // tq

    @pl.when(i == 0)
    def _():
        kv = a_ref[0, :, 512:640]
        kvf = kv.astype(F32)
        lane = lax.broadcasted_iota(jnp.int32, (1, LANES), 1)
        ss = _head_sumsq(jnp.where(lane < hd, kvf, 0.0), LANES)
        kn = kvf * lax.rsqrt(ss * (1.0 / hd) + EPS) * gk_ref[...]
        kbn_ref[...] = kn[:, :hd].astype(BF16)
        kvt = _nt(_eye(LANES, BF16), kv)
        for cb in range(s_len // ck):
            vt_ref[cb] = kvt[hd:, cb * ck:(cb + 1) * ck].astype(BF16)
        tri_ref[...] = jnp.where(lax.broadcasted_iota(jnp.int32, (ck, ck), 1)
                                 <= lax.broadcasted_iota(jnp.int32, (ck, ck), 0), 1.0, 0.0).astype(BF16)

    def heads_on_rows(x):
        return jnp.concatenate([x[:, h * hd:(h + 1) * hd] for h in range(x.shape[1] // hd)], axis=0)

    qi_all = heads_on_rows(a_ref[0, pl.ds(q0, tq), 256:512])
    wblk = a_ref[0, pl.ds(q0, tq), 640:768]
    sel_r = lax.broadcasted_iota(jnp.int32, (8, LANES), 0)
    sel_c = lax.broadcasted_iota(jnp.int32, (8, LANES), 1)
    w_t = _nt(jnp.where(sel_c == sel_r + D_IDX, 1.0, 0.0).astype(BF16), wblk)
    qpos = q0 + lax.broadcasted_iota(jnp.int32, (1, tq), 1)
    nck = (q0 + tq + ck - 1) // ck
    row_ck = lax.broadcasted_iota(jnp.int32, (ck, tq), 0)

    qb = a_ref[0, pl.ds(q0, tq), 0:256].astype(F32)
    qn_all = heads_on_rows(_head_rms(qb, gq_ref[...], hd).astype(BF16))

    def index_chunk(c):
        act = jnp.maximum(_nt(a_ref[0, c * ck:(c + 1) * ck, 640:704], qi_all), 0.0)
        sc = w_t[0:1, :] * act[:, 0:tq]
        for h in range(1, H_IDX):
            sc = sc + w_t[h:h + 1, :] * act[:, h * tq:(h + 1) * tq]
        bits = pltpu.bitcast(sc, jnp.int32)
        key = jnp.where(bits < 0, -(bits & 0x7FFFFFFF), bits)
        keys_ref[c * ck:(c + 1) * ck, :] = jnp.where(c * ck + row_ck <= qpos, key, INT_MIN)

    def search(n_chunks):
        def count(pred):
            acc = jnp.zeros((8, tq), jnp.int32)
            for c in range(n_chunks):
                m = jnp.where(pred(keys_ref[c * ck:(c + 1) * ck, :]), 1, 0)
                acc = acc + jnp.sum(m.reshape(ck // 8, 8, tq), axis=0)
            return jnp.sum(acc, axis=0, keepdims=True)

        c0 = count(lambda k: k >= 0)
        t = jnp.where(c0 >= topk, 0, INT_MIN).astype(jnp.int32)

        def vstep(b, t):
            cand = t | lax.shift_left(jnp.int32(1), 30 - b)
            return jnp.where(count(lambda k: k >= cand) >= topk, cand, t)

        t = lax.fori_loop(0, 31, vstep, t)
        need = topk - count(lambda k: k > t)
        return t, jnp.where(t == INT_MIN, 0, need).astype(F32)

    def no_search():
        return jnp.full((1, tq), INT_MIN, jnp.int32), jnp.zeros((1, tq), F32)

    def attend_chunk(c, st, thr, need):
        ms, ls, accs, ties_before = st
        s_all = _nt(kbn_ref[c * ck:(c + 1) * ck, :], qn_all)
        vtc = vt_ref[c]
        key = keys_ref[c * ck:(c + 1) * ck, :]
        tie = key == thr
        tie_rank = ties_before + _mm(tri_ref[...], jnp.where(tie, 1.0, 0.0).astype(BF16))
        sel = (key > thr) | (tie & (tie_rank <= need))
        ties_before = tie_rank[ck - 1:ck, :]
        bidx = [jnp.clip(i - (c * sub + r), 0, 2) for r in range(sub)]
        heads = range(H_DSA)
        s_h = [s_all[:, h * tq:(h + 1) * tq]
               + jnp.concatenate([bias_ref[bidx[r], h] for r in range(sub)], axis=0) for h in heads]
        m_new = [jnp.maximum(ms[h], jnp.max(jnp.where(sel, s_h[h], NEG), axis=0, keepdims=True)) for h in heads]
        p = [jnp.where(sel, jnp.exp2(s_h[h] - m_new[h]), 0.0) for h in heads]
        alpha = [jnp.exp2(ms[h] - m_new[h]) for h in heads]
        pv = [_mm(vtc, p[h].astype(BF16)) for h in heads]
        nl = [ls[h] * alpha[h] + jnp.sum(p[h], axis=0, keepdims=True) for h in heads]
        na = [accs[h] * alpha[h] + pv[h] for h in heads]
        return tuple(m_new), tuple(nl), tuple(na), ties_before

    def query_block(n_chunks):
        for c in range(n_chunks):
            index_chunk(c)
        thr, need = lax.cond(q0 + tq > topk, functools.partial(search, n_chunks), no_search)
        st = (tuple(jnp.full((1, tq), NEG, F32) for _ in range(H_DSA)),
              tuple(jnp.zeros((1, tq), F32) for _ in range(H_DSA)),
              tuple(jnp.zeros((hd, tq), F32) for _ in range(H_DSA)),
              jnp.zeros((1, tq), F32))
        for c in range(n_chunks):
            st = attend_chunk(c, st, thr, need)
        _, ls, accs, _ = st
        o_t = jnp.concatenate([accs[h] / ls[h] for h in range(H_DSA)], axis=0)
        o_ref[0] = _nt(_eye(tq, BF16), o_t.astype(BF16)).astype(o_ref.dtype)
        return 0

    lax.switch(nck - 1, [functools.partial(query_block, n) for n in range(1, s_len // ck + 1)])


def dsa_attention(proj, col_block, gq, gk, bias_t, *, tq=BLOCK, ck=512):
    b, s, _ = proj.shape
    topk = min(TOPK_MAX, s // 4)
    ck = min(ck, s)
    assert tq >= MAX_DISTANCE and s % ck == 0 and ck % tq == 0
    return pl.pallas_call(
        functools.partial(_dsa_kernel, tq=tq, ck=ck, topk=topk),
        grid=(b, s // tq),
        in_specs=[pl.BlockSpec((1, s, 768), lambda bi, i: (bi, 0, col_block)),
                  pl.BlockSpec((1, 256), lambda bi, i: (0, 0)),
                  pl.BlockSpec((1, LANES), lambda bi, i: (0, 0)),
                  pl.BlockSpec((3, H_DSA, tq, tq), lambda bi, i: (0, 0, 0, 0))],
        out_specs=pl.BlockSpec((1, tq, H_DSA * HEAD_DIM), lambda bi, i: (bi, i, 0)),
        out_shape=jax.ShapeDtypeStruct((b, s, H_DSA * HEAD_DIM), BF16),
        scratch_shapes=[pltpu.VMEM((s, HEAD_DIM), BF16),
                        pltpu.VMEM((s // ck, HEAD_DIM, ck), BF16),
                        pltpu.VMEM((s, tq), jnp.int32),
                        pltpu.VMEM((ck, ck), BF16)],
        compiler_params=_params(("parallel", "arbitrary")),
        name="dsa_attention",
    )(proj, gq, gk, bias_t)


def _t5_bucket(rel):
    n = jnp.maximum(rel, 0)
    max_exact = N_BUCKETS // 2
    nf = jnp.maximum(n, 1).astype(F32)
    large = max_exact + (jnp.log(nf / max_exact) / math.log(MAX_DISTANCE / max_exact)
                         * (N_BUCKETS - max_exact)).astype(jnp.int32)
    large = jnp.minimum(large, N_BUCKETS - 1)
    return jnp.where(n < max_exact, n, large)


def _bucket_lookup(tab, rel):
    hit = _t5_bucket(rel)[..., None, None] == jnp.arange(N_BUCKETS)[:, None]
    return jnp.sum(jnp.where(hit, tab.astype(F32), 0.0), axis=-2)


def dsa_bias_tiles(tab, tq=BLOCK):
    ks = jnp.arange(tq)[:, None]
    tl = jnp.arange(tq)[None, :]
    rel = jnp.stack([tl - ks, tq + tl - ks, jnp.full((tq, tq), 2 * tq + MAX_DISTANCE)])
    return _bucket_lookup(tab, rel).transpose(0, 3, 1, 2)


def _swa_kernel(sink_ref, cur_ref, prev_ref, gq_ref, gk_ref, bias_ref, o_ref, *, tq, nb):
    i = pl.program_id(1)
    hd = HEAD_DIM
    g = H_SW // KV_SW
    qn = _head_rms(cur_ref[0, :, 0:512].astype(F32), gq_ref[...], hd).astype(BF16)
    kn = jnp.concatenate([_head_rms(prev_ref[0, :, 512:640].astype(F32), gk_ref[...], hd),
                          _head_rms(cur_ref[0, :, 512:640].astype(F32), gk_ref[...], hd)], axis=0).astype(BF16)
    vv = jnp.concatenate([prev_ref[0, :, 640:768], cur_ref[0, :, 640:768]], axis=0)
    col = lax.broadcasted_iota(jnp.int32, (tq, 2 * tq), 1)
    first = (col >= tq) | (i > 0)
    chains = [(b, h) for b in range(nb) for h in range(H_SW)]
    s = {}
    for b, h in chains:
        kv = h // g
        s_bh = _nt(qn[b * tq:(b + 1) * tq, h * hd:(h + 1) * hd], kn[b * tq:(b + 2) * tq, kv * hd:(kv + 1) * hd])
        s_bh = s_bh + bias_ref[h]
        s[(b, h)] = jnp.where(first, s_bh, NEG) if b == 0 else s_bh
    m = {c: jnp.maximum(jnp.max(s[c], axis=-1, keepdims=True), sink_ref[c[1]]) for c in chains}
    e = {c: jnp.exp2(s[c] - m[c]) for c in chains}
    den = {c: jnp.sum(e[c], axis=-1, keepdims=True) + jnp.exp2(sink_ref[c[1]] - m[c]) for c in chains}
    out = {(b, h): _mm(e[(b, h)].astype(BF16), vv[b * tq:(b + 2) * tq, (h // g) * hd:(h // g + 1) * hd]) / den[(b, h)]
           for b, h in chains}
    for b in range(nb):
        o_ref[0, b * tq:(b + 1) * tq, :] = jnp.concatenate([out[(b, h)] for h in range(H_SW)],
                                                           axis=-1).astype(o_ref.dtype)


def swa_bias_tiles(tab, tq=BLOCK):
    rel = (jnp.arange(tq)[:, None] + tq) - jnp.arange(2 * tq)[None, :]
    in_win = (rel >= 0) & (rel < WINDOW)
    bias = _bucket_lookup(tab, rel).transpose(2, 0, 1)
    return jnp.where(in_win[None], bias, NEG)


def swa_attention(proj, col_block, sinks, gq, gk, bias, *, tq=BLOCK, nb=4):
    b, s, _ = proj.shape
    assert WINDOW <= tq and s % (nb * tq) == 0
    return pl.pallas_call(
        functools.partial(_swa_kernel, tq=tq, nb=nb),
        grid_spec=pltpu.PrefetchScalarGridSpec(
            num_scalar_prefetch=0,
            grid=(b, s // (nb * tq)),
            in_specs=[pl.BlockSpec(memory_space=pltpu.SMEM),
                      pl.BlockSpec((1, nb * tq, 768), lambda bi, i: (bi, i, col_block)),
                      pl.BlockSpec((1, tq, 768), lambda bi, i: (bi, jnp.maximum(nb * i - 1, 0), col_block)),
                      pl.BlockSpec((1, 512), lambda bi, i: (0, 0)),
                      pl.BlockSpec((1, LANES), lambda bi, i: (0, 0)),
                      pl.BlockSpec((H_SW, tq, 2 * tq), lambda bi, i: (0, 0, 0))],
            out_specs=pl.BlockSpec((1, nb * tq, H_SW * HEAD_DIM), lambda bi, i: (bi, i, 0))),
        out_shape=jax.ShapeDtypeStruct((b, s, H_SW * HEAD_DIM), BF16),
        compiler_params=_params(("parallel", "arbitrary")),
        name="swa_attention",
    )(sinks, proj, proj, gq, gk, bias)


def _merge_kernel(x_ref, gn_ref, wg_ref, oa_ref, ob_ref, oc_ref, bg_ref, wa_ref, wb_ref, wc_ref, wo_ref, o_ref):
    d = x_ref.shape[1]
    x = x_ref[...]
    xn = _rms_rows(x, gn_ref[...]).astype(BF16)
    merged = None
    for k, (o_k, w_k) in enumerate(((oa_ref, wa_ref), (ob_ref, wb_ref), (oc_ref, wc_ref))):
        logit = _mm(xn, wg_ref[:, k * d:(k + 1) * d]) + bg_ref[k:k + 1, :]
        gate = 1.0 / (1.0 + jnp.exp(-logit))
        term = gate * _mm(o_k[...], w_k[...])
        merged = term if merged is None else merged + term
    o_ref[...] = x + _mm(merged.astype(BF16), wo_ref[...])


def merge_project(x2d, gain, w_gate, o_a, o_b, o_c, b_gate, w_pa, w_pb, w_pc, w_out, *, tm=512):
    n, d = x2d.shape
    full = lambda a: pl.BlockSpec(a.shape, lambda i: (0,) * a.ndim)
    row = lambda a: pl.BlockSpec((tm, a.shape[1]), lambda i: (i, 0))
    return pl.pallas_call(
        _merge_kernel,
        grid=(n // tm,),
        in_specs=[row(x2d), full(gain), full(w_gate), row(o_a), row(o_b), row(o_c),
                  full(b_gate), full(w_pa), full(w_pb), full(w_pc), full(w_out)],
        out_specs=pl.BlockSpec((tm, d), lambda i: (i, 0)),
        out_shape=jax.ShapeDtypeStruct((n, d), F32),
        compiler_params=_params(("parallel",)),
        name="merge_project",
    )(x2d, gain, w_gate, o_a, o_b, o_c, b_gate, w_pa, w_pb, w_pc, w_out)


def _memkv_kernel(m_ref, g_ref, w_ref, gk_ref, k_ref, v_ref):
    w_x = k_ref.shape[2]
    mn = _rms_rows(m_ref[0], g_ref[...]).astype(BF16)
    kv = _mm(mn, w_ref[...])
    k_ref[0] = _head_rms(kv[:, :w_x], gk_ref[...], XHEAD_DIM).astype(k_ref.dtype)
    v_ref[0] = kv[:, w_x:].astype(v_ref.dtype)


def memory_kv(mem, gain, w_kv, gk):
    b, m, d = mem.shape
    w_x = w_kv.shape[1] // 2
    return pl.pallas_call(
        _memkv_kernel,
        grid=(b,),
        in_specs=[pl.BlockSpec((1, m, d), lambda i: (i, 0, 0)),
                  pl.BlockSpec((1, d), lambda i: (0, 0)),
                  pl.BlockSpec(w_kv.shape, lambda i: (0, 0)),
                  pl.BlockSpec((1, w_x), lambda i: (0, 0))],
        out_specs=[pl.BlockSpec((1, m, w_x), lambda i: (i, 0, 0)),
                   pl.BlockSpec((1, m, w_x), lambda i: (i, 0, 0))],
        out_shape=[jax.ShapeDtypeStruct((b, m, w_x), BF16)] * 2,
        compiler_params=_params(("parallel",)),
        name="memory_kv",
    )(mem, gain, w_kv, gk)


def _xattn_kernel(x_ref, g_ref, wq_ref, gq_ref, k_ref, v_ref, wo_ref, o_ref):
    x = x_ref[0]
    xn = _rms_rows(x, g_ref[...]).astype(BF16)
    q = _head_rms(_mm(xn, wq_ref[...]), gq_ref[...], XHEAD_DIM).astype(BF16)
    sl = [slice(h * XHEAD_DIM, (h + 1) * XHEAD_DIM) for h in range(H_X)]
    s = [_nt(q[:, c], k_ref[0, :, c]) for c in sl]
    e = [jnp.exp2(s_h - jnp.max(s_h, axis=-1, keepdims=True)) for s_h in s]
    den = [jnp.sum(e_h, axis=-1, keepdims=True) for e_h in e]
    outs = [_mm(e[h].astype(BF16), v_ref[0, :, sl[h]]) / den[h] for h in range(H_X)]
    o = jnp.concatenate(outs, axis=-1).astype(BF16)
    o_ref[0] = x + _mm(o, wo_ref[...])


def cross_attention(x, gain, w_q, gq, k_mem, v_mem, w_o, *, tq=512):
    b, s, d = x.shape
    m, w_x = k_mem.shape[1:]
    full = lambda a: pl.BlockSpec(a.shape, lambda bi, i: (0,) * a.ndim)
    return pl.pallas_call(
        _xattn_kernel,
        grid=(b, s // tq),
        in_specs=[pl.BlockSpec((1, tq, d), lambda bi, i: (bi, i, 0)),
                  full(gain), full(w_q), full(gq),
                  pl.BlockSpec((1, m, w_x), lambda bi, i: (bi, 0, 0)),
                  pl.BlockSpec((1, m, w_x), lambda bi, i: (bi, 0, 0)),
                  full(w_o)],
        out_specs=pl.BlockSpec((1, tq, d), lambda bi, i: (bi, i, 0)),
        out_shape=jax.ShapeDtypeStruct((b, s, d), F32),
        compiler_params=_params(("parallel", "parallel")),
        name="cross_attention",
    )(x, gain, w_q, gq, k_mem, v_mem, w_o)


def _router_kernel(x_ref, g_ref, whi_ref, wlo_ref, o_ref):
    hf = _rms_rows(x_ref[...], g_ref[...])
    hi = hf.astype(BF16)
    lo = (hf - hi.astype(F32)).astype(BF16)
    logits = _mm(hi, whi_ref[...]) + (_mm(hi, wlo_ref[...]) + _mm(lo, whi_ref[...]))
    lane = lax.broadcasted_iota(jnp.int32, logits.shape, 1).astype(F32)
    logits = jnp.where(lane < N_EXPERTS, logits, NEG)
    m1 = jnp.max(logits, axis=-1, keepdims=True)
    i1 = jnp.min(jnp.where(logits == m1, lane, float(LANES)), axis=-1, keepdims=True)
    rest = jnp.where(lane == i1, NEG, logits)
    m2 = jnp.max(rest, axis=-1, keepdims=True)
    i2 = jnp.min(jnp.where(rest == m2, lane, float(LANES)), axis=-1, keepdims=True)
    e2 = jnp.exp(m2 - m1)
    den = 1.0 + e2
    o_ref[...] = (jnp.where(lane == 0.0, i1, 0.0) + jnp.where(lane == 1.0, i2, 0.0)
                  + jnp.where(lane == 2.0, 1.0 / den, 0.0) + jnp.where(lane == 3.0, e2 / den, 0.0))


def router_gates(x2d, gain, w_hi, w_lo, *, tm=1024):
    n, d = x2d.shape
    tm = min(tm, n)
    return pl.pallas_call(
        _router_kernel,
        grid=(n // tm,),
        in_specs=[pl.BlockSpec((tm, d), lambda i: (i, 0)),
                  pl.BlockSpec((1, d), lambda i: (0, 0)),
                  pl.BlockSpec((d, LANES), lambda i: (0, 0)),
                  pl.BlockSpec((d, LANES), lambda i: (0, 0))],
        out_specs=pl.BlockSpec((tm, LANES), lambda i: (i, 0)),
        out_shape=jax.ShapeDtypeStruct((n, LANES), F32),
        compiler_params=_params(("parallel",)),
        name="router_gates",
    )(x2d, gain, w_hi, w_lo)


def _swiglu_tile(xn, wg, wu, wd):
    gg = _mm(xn, wg)
    uu = _mm(xn, wu)
    act = gg * (1.0 / (1.0 + jnp.exp(-gg))) * uu
    return _mm(act.astype(BF16), wd)


def _ffn_kernel(x_ref, g_ref, wgu_ref, wd_ref, o_ref, *, tf):
    f = wd_ref.shape[0]
    x = x_ref[...]
    xn = _rms_rows(x, g_ref[...]).astype(BF16)
    acc = None
    for c in range(f // tf):
        part = _swiglu_tile(xn, wgu_ref[:, c * tf:(c + 1) * tf], wgu_ref[:, f + c * tf:f + (c + 1) * tf],
                            wd_ref[c * tf:(c + 1) * tf, :])
        acc = part if acc is None else acc + part
    o_ref[...] = x + acc


def dense_ffn(x2d, gain, w_gu, w_down, *, tm=512, tf=256):
    n, d = x2d.shape
    f = w_down.shape[0]
    tm = min(tm, n)
    assert f % tf == 0
    resident = lambda a: pl.BlockSpec(a.shape, lambda i: (0,) * a.ndim, pipeline_mode=pl.Buffered(1))
    return pl.pallas_call(
        functools.partial(_ffn_kernel, tf=tf),
        grid=(n // tm,),
        in_specs=[pl.BlockSpec((tm, d), lambda i: (i, 0)),
                  pl.BlockSpec((1, d), lambda i: (0, 0)),
                  resident(w_gu), resident(w_down)],
        out_specs=pl.BlockSpec((tm, d), lambda i: (i, 0)),
        out_shape=jax.ShapeDtypeStruct((n, d), F32),
        compiler_params=_params(("parallel",)),
        name="dense_ffn",
    )(x2d, gain, w_gu, w_down)


def _moe_kernel(te_ref, tn_ref, tok_ref, tok_next_ref, dst_prev_ref, x_hbm, g_ref, wgu_ref,
                wd_ref, y_hbm, xg_ref, yb_ref, gsem, ssem, *, tm, tf):
    i = pl.program_id(0)
    nt = pl.num_programs(0)
    slot = i % 2
    f = wd_ref.shape[1]
    nf = f // tf
    per_step = tm // nf
    active = tn_ref[i] > 0
    prev_active = (i > 0) & (tn_ref[jnp.maximum(i - 1, 0)] > 0)
    prev_issued = (i > 0) & (tn_ref[jnp.maximum(i - 2, 0)] > 0)

    def gather_copy(tok, r, s):
        return pltpu.make_async_copy(x_hbm.at[pl.ds(tok, 1), :], xg_ref.at[s, pl.ds(r, 1), :], gsem.at[s])

    def scatter_copy(dst, r, s):
        return pltpu.make_async_copy(yb_ref.at[s, pl.ds(r, 1), :], y_hbm.at[pl.ds(dst, 1), :], ssem.at[s])

    def start_all_rows(ids_ref, s, make_copy):
        def body(r8, c):
            for u in range(8):
                r = r8 * 8 + u
                make_copy(ids_ref[0, 0, r], r, s).start()
            return c
        lax.fori_loop(0, tm // 8, body, 0)

    def wait_gather(s):
        pltpu.make_async_copy(x_hbm.at[pl.ds(0, tm), :], xg_ref.at[s], gsem.at[s]).wait()

    def wait_scatter(s):
        pltpu.make_async_copy(yb_ref.at[s], y_hbm.at[pl.ds(0, tm), :], ssem.at[s]).wait()

    @pl.when(i == 0)
    def _():
        start_all_rows(tok_ref, slot, gather_copy)
        yb_ref[1] = jnp.zeros(yb_ref.shape[1:], yb_ref.dtype)
        n_real = y_hbm.shape[0] - 2 * tm
        for half in range(2):
            spare = pltpu.make_async_copy(yb_ref.at[1], y_hbm.at[pl.ds(n_real + half * tm, tm), :], ssem.at[1])
            spare.start()
            spare.wait()

    @pl.when(active | prev_active)
    def _():
        wait_gather(slot)

    @pl.when(prev_issued)
    def _():
        wait_scatter(slot)

    @pl.when(prev_active & jnp.logical_not(active))
    def _():
        start_all_rows(dst_prev_ref, 1 - slot, scatter_copy)

    @pl.when(active)
    def _():
        xn = _rms_rows(xg_ref[slot], g_ref[...]).astype(BF16)
        acc = None
        for c in range(nf):
            part = _swiglu_tile(xn, wgu_ref[0, :, c * tf:(c + 1) * tf], wgu_ref[0, :, f + c * tf:f + (c + 1) * tf],
                                wd_ref[0, c * tf:(c + 1) * tf, :])
            acc = part if acc is None else acc + part
            for r in range(c * per_step, (c + 1) * per_step):
                gather_copy(tok_next_ref[0, 0, r], r, 1 - slot).start()
                scatter_copy(dst_prev_ref[0, 0, r], r, 1 - slot).start()
        yb_ref[slot] = acc

    @pl.when((i == nt - 1) & (active | prev_active))
    def _():
        wait_scatter(1 - slot)


def moe_experts(x2d, gain, plan, w_gu, w_down, *, tm, tf=512):
    n, d = x2d.shape
    ne, f, _ = w_down.shape
    nf = f // tf
    assert tm % nf == 0 and tm % 16 == 0
    tile_e, tile_n, row_tok, row_dst_prev = plan
    nt = tile_e.shape[0]
    smem_rows = lambda imap: pl.BlockSpec((1, 1, tm), imap, memory_space=pltpu.SMEM)
    expert = lambda shape: pl.BlockSpec(shape, lambda i, te, tn: (te[i], 0, 0), pipeline_mode=pl.Buffered(1))
    return pl.pallas_call(
        functools.partial(_moe_kernel, tm=tm, tf=tf),
        grid_spec=pltpu.PrefetchScalarGridSpec(
            num_scalar_prefetch=2,
            grid=(nt,),
            in_specs=[smem_rows(lambda i, te, tn: (i, 0, 0)),
                      smem_rows(lambda i, te, tn: (jnp.minimum(i + 1, nt - 1), 0, 0)),
                      smem_rows(lambda i, te, tn: (i, 0, 0)),
                      pl.BlockSpec(memory_space=pl.ANY),
                      pl.BlockSpec((1, d), lambda i, te, tn: (0, 0)),
                      expert((1, d, 2 * f)), expert((1, f, d))],
            out_specs=pl.BlockSpec(memory_space=pl.ANY),
            scratch_shapes=[pltpu.VMEM((2, tm, d), F32), pltpu.VMEM((2, tm, d), F32),
                            pltpu.SemaphoreType.DMA((2,)), pltpu.SemaphoreType.DMA((2,))]),
        out_shape=jax.ShapeDtypeStruct((2 * n + 2 * tm, d), F32),
        compiler_params=_params(("arbitrary",)),
        name="moe_experts",
    )(tile_e, tile_n, row_tok, row_tok, row_dst_prev, x2d, gain, w_gu, w_down)


def moe_plan(route, *, tm):
    n = route.shape[0]
    flat_e = route[:, :2].astype(jnp.int32).reshape(-1)
    nt = (2 * n) // tm + N_EXPERTS + 1
    order = jnp.argsort(flat_e, stable=True).astype(jnp.int32)
    counts = jnp.sum(flat_e[:, None] == jnp.arange(N_EXPERTS)[None, :], axis=0).astype(jnp.int32)
    off = jnp.cumsum(counts) - counts
    tiles = (counts + tm - 1) // tm
    tile_off = jnp.cumsum(tiles) - tiles
    tile_id = jnp.arange(nt, dtype=jnp.int32)
    used = tile_id < jnp.sum(tiles)
    tile_e = jnp.clip(jnp.sum(tile_id[:, None] >= tile_off[None, :], axis=1) - 1, 0, N_EXPERTS - 1)
    tile_e = jnp.where(used, tile_e, tile_e[jnp.maximum(jnp.sum(tiles) - 1, 0)]).astype(jnp.int32)
    first_row = (tile_id - tile_off[tile_e]) * tm
    tile_n = jnp.where(used, jnp.clip(counts[tile_e] - first_row, 0, tm), 0).astype(jnp.int32)
    r = jnp.arange(tm, dtype=jnp.int32)[None, :]
    valid = r < tile_n[:, None]
    a = order[jnp.clip(off[tile_e][:, None] + first_row[:, None] + r, 0, 2 * n - 1)]
    row_tok = jnp.where(valid, a // 2, 0).astype(jnp.int32).reshape(nt, 1, tm)
    spare = 2 * n + (tile_id[:, None] % 2) * tm + r
    row_dst = jnp.where(valid, (a % 2) * n + a // 2, spare).astype(jnp.int32)
    row_dst_prev = jnp.concatenate([2 * n + tm + r, row_dst[:-1]], axis=0).reshape(nt, 1, tm)
    return tile_e, tile_n, row_tok, row_dst_prev


def _combine_kernel(x_ref, route_ref, y0_ref, y1_ref, o_ref):
    o_ref[...] = x_ref[...] + (route_ref[:, 2:3] * y0_ref[...] + route_ref[:, 3:4] * y1_ref[...])


def moe_combine(x2d, route, y, *, tm=1024):
    n, d = x2d.shape
    tm = min(tm, n)
    nb = n // tm
    return pl.pallas_call(
        _combine_kernel,
        grid=(nb,),
        in_specs=[pl.BlockSpec((tm, d), lambda i: (i, 0)),
                  pl.BlockSpec((tm, LANES), lambda i: (i, 0)),
                  pl.BlockSpec((tm, d), lambda i: (i, 0)),
                  pl.BlockSpec((tm, d), lambda i: (i + nb, 0))],
        out_specs=pl.BlockSpec((tm, d), lambda i: (i, 0)),
        out_shape=jax.ShapeDtypeStruct((n, d), F32),
        compiler_params=_params(("parallel",)),
        name="moe_combine",
    )(x2d, route, y, y)


def _pack_w_in(w, d):
    sizes = (256, 256, 256, 256, 64, 64, 256, 64, 4, 512, 128, 128, N_BRANCH * d)
    qa, ka, va, qb, kb, vb, qi, ki, wi, qc, kc, vc, g = jnp.split(w, np.cumsum(sizes)[:-1].tolist(), axis=-1)
    pad = jnp.zeros((w.shape[0], 60), w.dtype)
    mixers = jnp.concatenate([qa * (HEAD_DIM ** -0.5 * LOG2E), ka, va,
                              qb, qi * D_IDX ** -0.5, kb, vb, ki, wi, pad,
                              qc, kc, vc], axis=-1)
    return mixers.astype(BF16), g.astype(BF16)


def kernel(x, mem, rel_bias, norm_mix, w_in, b_gate, qn_dsa, kn_dsa, qn_swa, kn_swa, sinks, w_pa, w_pb, w_pc, w_out, norm_x, norm_mem, w_xq, w_xkv, w_xo, qn_x, kn_x, norm_ffn, w_gu_dense, w_down_dense, w_router, w_gu_moe, w_down_moe):
    b, s, d = x.shape
    depth = w_in.shape[0]
    n = b * s
    row = lambda v: v.reshape(1, -1).astype(F32)
    bias_dsa = dsa_bias_tiles(rel_bias[:, :H_DSA] * LOG2E)
    bias_swa = swa_bias_tiles(rel_bias[:, H_DSA:] * LOG2E)
    qscale = HEAD_DIM ** -0.5 * LOG2E

    x2 = x.reshape(n, d)
    for l in range(depth):
        w_mix, w_gate = _pack_w_in(w_in[l], d)
        proj3 = in_projection(x2, row(norm_mix[l]), w_mix, tn=w_mix.shape[1]).reshape(b, s, -1)
        o_a = sb_attention(proj3, 0)
        o_b = dsa_attention(proj3, 1, row(jnp.tile(qn_dsa[l] * qscale, H_DSA)),
                            row(jnp.concatenate([kn_dsa[l], jnp.zeros_like(kn_dsa[l])])), bias_dsa)
        o_c = swa_attention(proj3, 2, sinks[l].astype(F32) * LOG2E,
                            row(jnp.tile(qn_swa[l] * qscale, H_SW)), row(jnp.tile(kn_swa[l], KV_SW)), bias_swa)
        x2 = merge_project(x2, row(norm_mix[l]), w_gate, o_a.reshape(n, -1), o_b.reshape(n, -1), o_c.reshape(n, -1),
                           b_gate[l].astype(F32), w_pa[l].astype(BF16), w_pb[l].astype(BF16),
                           w_pc[l].astype(BF16), w_out[l].astype(BF16))
        k_mem, v_mem = memory_kv(mem, row(norm_mem[l]), w_xkv[l].astype(BF16), row(jnp.tile(kn_x[l], H_X)))
        x2 = cross_attention(x2.reshape(b, s, d), row(norm_x[l]), w_xq[l].astype(BF16),
                             row(jnp.tile(qn_x[l] * (XHEAD_DIM ** -0.5 * LOG2E), H_X)), k_mem, v_mem,
                             w_xo[l].astype(BF16)).reshape(n, d)
        if l % 2 == 0:
            x2 = dense_ffn(x2, row(norm_ffn[l]), w_gu_dense[l // 2].astype(BF16),
                           w_down_dense[l // 2].astype(BF16), tf=256)
        else:
            wr = jnp.pad(w_router[l // 2].astype(F32), ((0, 0), (0, LANES - N_EXPERTS)))
            wr_hi = wr.astype(BF16)
            wr_lo = (wr - wr_hi.astype(F32)).astype(BF16)
            route = router_gates(x2, row(norm_ffn[l]), wr_hi, wr_lo)
            tm_moe = 672
            y = moe_experts(x2, row(norm_ffn[l]), moe_plan(route, tm=tm_moe), w_gu_moe[l // 2].astype(BF16),
                            w_down_moe[l // 2].astype(BF16), tm=tm_moe)
            x2 = moe_combine(x2, route, y)
    return x2.reshape(b, s, d)
```

```python
import functools
import math

import numpy as np
import jax
import jax.numpy as jnp
from jax import lax
from jax.experimental import pallas as pl
from jax.experimental.pallas import tpu as pltpu

HEAD_DIM = 64
H_SB = 4
H_DSA = 4
H_IDX = 4
D_IDX = 64
TOPK_MAX = 256
H_SW = 8
KV_SW = 2
WINDOW = 128
BLOCK = 128
N_BRANCH = 3
N_BUCKETS = 32
MAX_DISTANCE = 128
H_X = 4
XHEAD_DIM = 128
N_EXPERTS = 8
EPS = 1e-6

LANES = 128
VMEM_LIMIT = 56 * 1024 * 1024
NEG = -1e30
INT_MIN = -(2 ** 31)
LOG2E = math.log2(math.e)

F32 = jnp.float32
BF16 = jnp.bfloat16


def _nt(a, b):
    return lax.dot_general(a, b, (((1,), (1,)), ((), ())), preferred_element_type=F32)


def _mm(a, b):
    return jnp.dot(a, b, preferred_element_type=F32)


def _rms_rows(x, g):
    ms = jnp.mean(x * x, axis=-1, keepdims=True)
    return x * lax.rsqrt(ms + EPS) * g


def _params(sem):
    return pltpu.CompilerParams(dimension_semantics=sem, vmem_limit_bytes=VMEM_LIMIT)


def _sb_kernel(a_ref, o_ref, *, tq):
    i = pl.program_id(1)
    q0 = pl.multiple_of(i * tq, tq)
    hd = HEAD_DIM
    row = lax.broadcasted_iota(jnp.int32, (tq, tq), 0)
    col = lax.broadcasted_iota(jnp.int32, (tq, tq), 1)
    strict = col < row
    u_inc = jnp.where(row >= col, 1.0, 0.0).astype(BF16)
    qs = [a_ref[0, pl.ds(q0, tq), h * hd:(h + 1) * hd] for h in range(H_SB)]

    heads = range(H_SB)

    def block(k0s, accs, carries, diag):
        chains = [(b, h) for b in range(len(k0s)) for h in heads]
        ks = {(b, h): a_ref[0, pl.ds(k0s[b], tq), 256 + h * hd:256 + (h + 1) * hd] for b, h in chains}
        vs = {(b, h): a_ref[0, pl.ds(k0s[b], tq), 512 + h * hd:512 + (h + 1) * hd] for b, h in chains}
        zs = {c: _nt(qs[c[1]], ks[c]) for c in chains}
        lks = {c: -(jnp.maximum(zs[c], 0.0) + jnp.log2(1.0 + jnp.exp2(-jnp.abs(zs[c])))) for c in chains}
        if diag:
            lks = {c: jnp.where(strict, lks[c], 0.0) for c in chains}
        rs = {c: _mm(lks[c].astype(BF16), u_inc) for c in chains}
        carry = {h: (None if diag else carries[h]) for h in heads}
        atts = {}
        for b, h in chains:
            if diag:
                atts[(b, h)] = jnp.where(strict, jnp.exp2(zs[(b, h)] + rs[(b, h)]), 0.0)
                carry[h] = rs[(b, h)][:, 0:1]
            else:
                atts[(b, h)] = jnp.exp2(zs[(b, h)] + rs[(b, h)] + carry[h])
                carry[h] = carry[h] + rs[(b, h)][:, 0:1]
        pvs = {c: _mm(atts[c].astype(BF16), vs[c]) for c in chains}
        new_acc = []
        for h in heads:
            tot = pvs[(0, h)] if diag else accs[h] + pvs[(0, h)]
            for b in range(1, len(k0s)):
                tot = tot + pvs[(b, h)]
            new_acc.append(tot)
        return tuple(new_acc), tuple(carry[h] for h in heads)

    def query_tile(n_before):
        state = block([q0], None, None, True)
        b = n_before - 1
        while b >= 1:
            state = block([b * tq, (b - 1) * tq], state[0], state[1], False)
            b -= 2
        if b == 0:
            state = block([0], state[0], state[1], False)
        o_ref[0] = jnp.concatenate(state[0], axis=-1).astype(o_ref.dtype)
        return 0

    lax.switch(i, [functools.partial(query_tile, n) for n in range(a_ref.shape[1] // tq)])


def sb_attention(proj, col_block, *, tq=256):
    b, s, _ = proj.shape
    return pl.pallas_call(
        functools.partial(_sb_kernel, tq=tq),
        grid=(b, s // tq),
        in_specs=[pl.BlockSpec((1, s, 768), lambda bi, i: (bi, 0, col_block))],
        out_specs=pl.BlockSpec((1, tq, H_SB * HEAD_DIM), lambda bi, i: (bi, i, 0)),
        out_shape=jax.ShapeDtypeStruct((b, s, H_SB * HEAD_DIM), BF16),
        compiler_params=_params(("parallel", "arbitrary")),
        name="sb_attention",
    )(proj)


def _inproj_kernel(x_ref, g_ref, w_ref, o_ref, xn_ref):
    @pl.when(pl.program_id(1) == 0)
    def _():
        xn_ref[...] = _rms_rows(x_ref[...], g_ref[...]).astype(BF16)

    o_ref[...] = _mm(xn_ref[...], w_ref[...]).astype(o_ref.dtype)


def in_projection(x2d, gain, w, *, tm=1024, tn=768):
    n, d = x2d.shape
    tm = min(tm, n)
    c = w.shape[1]
    return pl.pallas_call(
        _inproj_kernel,
        grid=(n // tm, c // tn),
        in_specs=[pl.BlockSpec((tm, d), lambda i, j: (i, 0)),
                  pl.BlockSpec((1, d), lambda i, j: (0, 0)),
                  pl.BlockSpec((d, tn), lambda i, j: (0, j))],
        out_specs=pl.BlockSpec((tm, tn), lambda i, j: (i, j)),
        out_shape=jax.ShapeDtypeStruct((n, c), BF16),
        scratch_shapes=[pltpu.VMEM((tm, d), BF16)],
        compiler_params=_params(("parallel", "arbitrary")),
        name="in_projection",
    )(x2d, gain, w)


def _head_sumsq(x, head_dim):
    r_i = lax.broadcasted_iota(jnp.int32, (LANES, LANES), 0) // head_dim
    c_i = lax.broadcasted_iota(jnp.int32, (LANES, LANES), 1) // head_dim
    bd = jnp.where(r_i == c_i, 1.0, 0.0).astype(F32)
    x2 = x * x
    parts = [_mm(x2[:, b * LANES:(b + 1) * LANES], bd) for b in range(x.shape[1] // LANES)]
    return parts[0] if len(parts) == 1 else jnp.concatenate(parts, axis=-1)


def _head_rms(x, g, head_dim):
    ss = _head_sumsq(x, head_dim)
    return x * lax.rsqrt(ss * (1.0 / head_dim) + EPS) * g


def _eye(n, dtype):
    r = lax.broadcasted_iota(jnp.int32, (n, n), 0)
    c = lax.broadcasted_iota(jnp.int32, (n, n), 1)
    return jnp.where(r == c, 1.0, 0.0).astype(dtype)


def _dsa_kernel(a_ref, gq_ref, gk_ref, bias_ref, o_ref, kbn_ref, vt_ref, keys_ref, tri_ref, qn_ref, wt_ref, *, tq, ck, topk):
    s_len = a_ref.shape[1]
    i = pl.program_id(1)
    q0 = pl.multiple_of(i * tq, tq)
    hd = HEAD_DIM
    sub = ck // tq

    @pl.when(i == 0)
    def _():
        kv = a_ref[0, :, 512:640]
        kvf = kv.astype(F32)
        lane = lax.broadcasted_iota(jnp.int32, (1, LANES), 1)
        ss = _head_sumsq(jnp.where(lane < hd, kvf, 0.0), LANES)
        kn = kvf * lax.rsqrt(ss * (1.0 / hd) + EPS) * gk_ref[...]
        kbn_ref[...] = kn[:, :hd].astype(BF16)
        kvt = _nt(_eye(LANES, BF16), kv)
        for cb in range(s_len // ck):
            vt_ref[cb] = kvt[hd:, cb * ck:(cb + 1) * ck].astype(BF16)
        tri_ref[...] = jnp.where(lax.broadcasted_iota(jnp.int32, (ck, ck), 1)
                                 <= lax.broadcasted_iota(jnp.int32, (ck, ck), 0), 1.0, 0.0).astype(BF16)
        qn_ref[...] = _head_rms(a_ref[0, :, 0:256].astype(F32), gq_ref[...], hd).astype(BF16)
        sel_r = lax.broadcasted_iota(jnp.int32, (8, LANES), 0)
        sel_c = lax.broadcasted_iota(jnp.int32, (8, LANES), 1)
        w_all = _nt(jnp.where(sel_c == sel_r + D_IDX, 1.0, 0.0).astype(BF16), a_ref[0, :, 640:768])
        for qb_i in range(s_len // tq):
            wt_ref[qb_i] = w_all[:, qb_i * tq:(qb_i + 1) * tq]

    def heads_on_rows(ref_slice):
        return jnp.concatenate([ref_slice(h * hd) for h in range(H_DSA)], axis=0)

    qi_all = heads_on_rows(lambda c: a_ref[0, pl.ds(q0, tq), 256 + c:256 + c + hd])
    w_t = wt_ref[i]
    qpos = q0 + lax.broadcasted_iota(jnp.int32, (1, tq), 1)
    nck = (q0 + tq + ck - 1) // ck
    row_ck = lax.broadcasted_iota(jnp.int32, (ck, tq), 0)

    qn_all = heads_on_rows(lambda c: qn_ref[pl.ds(q0, tq), c:c + hd])

    def index_chunk(c):
        act = jnp.maximum(_nt(a_ref[0, c * ck:(c + 1) * ck, 640:704], qi_all), 0.0)
        sc = w_t[0:1, :] * act[:, 0:tq]
        for h in range(1, H_IDX):
            sc = sc + w_t[h:h + 1, :] * act[:, h * tq:(h + 1) * tq]
        bits = pltpu.bitcast(sc, jnp.int32)
        key = jnp.where(bits < 0, -(bits & 0x7FFFFFFF), bits)
        keys_ref[c * ck:(c + 1) * ck, :] = jnp.where(c * ck + row_ck <= qpos, key, INT_MIN)

    def search(n_chunks):
        def count(pred):
            acc = jnp.zeros((8, tq), jnp.int32)
            for c in range(n_chunks):
                m = jnp.where(pred(keys_ref[c * ck:(c + 1) * ck, :]), 1, 0)
                acc = acc + jnp.sum(m.reshape(ck // 8, 8, tq), axis=0)
            return jnp.sum(acc, axis=0, keepdims=True)

        c0 = count(lambda k: k >= 0)
        t = jnp.where(c0 >= topk, 0, INT_MIN).astype(jnp.int32)

        def vstep(b, t):
            cand = t | lax.shift_left(jnp.int32(1), 30 - b)
            return jnp.where(count(lambda k: k >= cand) >= topk, cand, t)

        t = lax.fori_loop(0, 31, vstep, t)
        need = topk - count(lambda k: k > t)
        return t, jnp.where(t == INT_MIN, 0, need).astype(F32)

    def no_search():
        return jnp.full((1, tq), INT_MIN, jnp.int32), jnp.zeros((1, tq), F32)

    def attend_chunk(c, st, thr, need):
        ms, ls, accs, ties_before = st
        s_all = _nt(kbn_ref[c * ck:(c + 1) * ck, :], qn_all)
        vtc = vt_ref[c]
        key = keys_ref[c * ck:(c + 1) * ck, :]
        tie = key == thr
        tie_rank = ties_before + _mm(tri_ref[...], jnp.where(tie, 1.0, 0.0).astype(BF16))
        sel = (key > thr) | (tie & (tie_rank <= need))
        ties_before = tie_rank[ck - 1:ck, :]
        bidx = [jnp.clip(i - (c * sub + r), 0, 2) for r in range(sub)]
        heads = range(H_DSA)
        s_h = [s_all[:, h * tq:(h + 1) * tq]
               + jnp.concatenate([bias_ref[bidx[r], h] for r in range(sub)], axis=0) for h in heads]
        m_new = [jnp.maximum(ms[h], jnp.max(jnp.where(sel, s_h[h], NEG), axis=0, keepdims=True)) for h in heads]
        p = [jnp.where(sel, jnp.exp2(s_h[h] - m_new[h]), 0.0) for h in heads]
        alpha = [jnp.exp2(ms[h] - m_new[h]) for h in heads]
        pv = [_mm(vtc, p[h].astype(BF16)) for h in heads]
        nl = [ls[h] * alpha[h] + jnp.sum(p[h], axis=0, keepdims=True) for h in heads]
        na = [accs[h] * alpha[h] + pv[h] for h in heads]
        return tuple(m_new), tuple(nl), tuple(na), ties_before

    def query_block(n_chunks):
        for c in range(n_chunks):
            index_chunk(c)
        thr, need = lax.cond(q0 + tq > topk, functools.partial(search, n_chunks), no_search)
        st = (tuple(jnp.full((1, tq), NEG, F32) for _ in range(H_DSA)),
              tuple(jnp.zeros((1, tq), F32) for _ in range(H_DSA)),
              tuple(jnp.zeros((hd, tq), F32) for _ in range(H_DSA)),
              jnp.zeros((1, tq), F32))
        for c in range(n_chunks):
            st = attend_chunk(c, st, thr, need)
        _, ls, accs, _ = st
        o_t = jnp.concatenate([accs[h] / ls[h] for h in range(H_DSA)], axis=0)
        o_ref[0] = _nt(_eye(tq, BF16), o_t.astype(BF16)).astype(o_ref.dtype)
        return 0

    lax.switch(nck - 1, [functools.partial(query_block, n) for n in range(1, s_len // ck + 1)])


def dsa_attention(proj, col_block, gq, gk, bias_t, *, tq=BLOCK, ck=512):
    b, s, _ = proj.shape
    topk = min(TOPK_MAX, s // 4)
    ck = min(ck, s)
    assert tq >= MAX_DISTANCE and s % ck == 0 and ck % tq == 0
    return pl.pallas_call(
        functools.partial(_dsa_kernel, tq=tq, ck=ck, topk=topk),
        grid=(b, s // tq),
        in_specs=[pl.BlockSpec((1, s, 768), lambda bi, i: (bi, 0, col_block)),
                  pl.BlockSpec((1, 256), lambda bi, i: (0, 0)),
                  pl.BlockSpec((1, LANES), lambda bi, i: (0, 0)),
                  pl.BlockSpec((3, H_DSA, tq, tq), lambda bi, i: (0, 0, 0, 0))],
        out_specs=pl.BlockSpec((1, tq, H_DSA * HEAD_DIM), lambda bi, i: (bi, i, 0)),
        out_shape=jax.ShapeDtypeStruct((b, s, H_DSA * HEAD_DIM), BF16),
        scratch_shapes=[pltpu.VMEM((s, HEAD_DIM), BF16),
                        pltpu.VMEM((s // ck, HEAD_DIM, ck), BF16),
                        pltpu.VMEM((s, tq), jnp.int32),
                        pltpu.VMEM((ck, ck), BF16),
                        pltpu.VMEM((s, H_DSA * HEAD_DIM), BF16),
                        pltpu.VMEM((s // tq, 8, tq), F32)],
        compiler_params=_params(("parallel", "arbitrary")),
        name="dsa_attention",
    )(proj, gq, gk, bias_t)


def _t5_bucket(rel):
    n = jnp.maximum(rel, 0)
    max_exact = N_BUCKETS // 2
    nf = jnp.maximum(n, 1).astype(F32)
    large = max_exact + (jnp.log(nf / max_exact) / math.log(MAX_DISTANCE / max_exact)
                         * (N_BUCKETS - max_exact)).astype(jnp.int32)
    large = jnp.minimum(large, N_BUCKETS - 1)
    return jnp.where(n < max_exact, n, large)


def _bucket_lookup(tab, rel):
    hit = _t5_bucket(rel)[..., None, None] == jnp.arange(N_BUCKETS)[:, None]
    return jnp.sum(jnp.where(hit, tab.astype(F32), 0.0), axis=-2)


def dsa_bias_tiles(tab, tq=BLOCK):
    ks = jnp.arange(tq)[:, None]
    tl = jnp.arange(tq)[None, :]
    rel = jnp.stack([tl - ks, tq + tl - ks, jnp.full((tq, tq), 2 * tq + MAX_DISTANCE)])
    return _bucket_lookup(tab, rel).transpose(0, 3, 1, 2)


def _swa_kernel(sink_ref, cur_ref, prev_ref, gq_ref, gk_ref, bias_ref, o_ref, *, tq, nb):
    i = pl.program_id(1)
    hd = HEAD_DIM
    g = H_SW // KV_SW
    qn = _head_rms(cur_ref[0, :, 0:512].astype(F32), gq_ref[...], hd).astype(BF16)
    kn = jnp.concatenate([_head_rms(prev_ref[0, :, 512:640].astype(F32), gk_ref[...], hd),
                          _head_rms(cur_ref[0, :, 512:640].astype(F32), gk_ref[...], hd)], axis=0).astype(BF16)
    vv = jnp.concatenate([prev_ref[0, :, 640:768], cur_ref[0, :, 640:768]], axis=0)
    col = lax.broadcasted_iota(jnp.int32, (tq, 2 * tq), 1)
    first = (col >= tq) | (i > 0)
    chains = [(b, h) for b in range(nb) for h in range(H_SW)]
    s = {}
    for b, h in chains:
        kv = h // g
        s_bh = _nt(qn[b * tq:(b + 1) * tq, h * hd:(h + 1) * hd], kn[b * tq:(b + 2) * tq, kv * hd:(kv + 1) * hd])
        s_bh = s_bh + bias_ref[h]
        s[(b, h)] = jnp.where(first, s_bh, NEG) if b == 0 else s_bh
    m = {c: jnp.maximum(jnp.max(s[c], axis=-1, keepdims=True), sink_ref[c[1]]) for c in chains}
    e = {c: jnp.exp2(s[c] - m[c]) for c in chains}
    den = {c: jnp.sum(e[c], axis=-1, keepdims=True) + jnp.exp2(sink_ref[c[1]] - m[c]) for c in chains}
    out = {(b, h): _mm(e[(b, h)].astype(BF16), vv[b * tq:(b + 2) * tq, (h // g) * hd:(h // g + 1) * hd]) / den[(b, h)]
           for b, h in chains}
    for b in range(nb):
        o_ref[0, b * tq:(b + 1) * tq, :] = jnp.concatenate([out[(b, h)] for h in range(H_SW)],
                                                           axis=-1).astype(o_ref.dtype)


def swa_bias_tiles(tab, tq=BLOCK):
    rel = (jnp.arange(tq)[:, None] + tq) - jnp.arange(2 * tq)[None, :]
    in_win = (rel >= 0) & (rel < WINDOW)
    bias = _bucket_lookup(tab, rel).transpose(2, 0, 1)
    return jnp.where(in_win[None], bias, NEG)


def swa_attention(proj, col_block, sinks, gq, gk, bias, *, tq=BLOCK, nb=4):
    b, s, _ = proj.shape
    assert WINDOW <= tq and s % (nb * tq) == 0
    return pl.pallas_call(
        functools.partial(_swa_kernel, tq=tq, nb=nb),
        grid_spec=pltpu.PrefetchScalarGridSpec(
            num_scalar_prefetch=0,
            grid=(b, s // (nb * tq)),
            in_specs=[pl.BlockSpec(memory_space=pltpu.SMEM),
                      pl.BlockSpec((1, nb * tq, 768), lambda bi, i: (bi, i, col_block)),
                      pl.BlockSpec((1, tq, 768), lambda bi, i: (bi, jnp.maximum(nb * i - 1, 0), col_block)),
                      pl.BlockSpec((1, 512), lambda bi, i: (0, 0)),
                      pl.BlockSpec((1, LANES), lambda bi, i: (0, 0)),
                      pl.BlockSpec((H_SW, tq, 2 * tq), lambda bi, i: (0, 0, 0))],
            out_specs=pl.BlockSpec((1, nb * tq, H_SW * HEAD_DIM), lambda bi, i: (bi, i, 0))),
        out_shape=jax.ShapeDtypeStruct((b, s, H_SW * HEAD_DIM), BF16),
        compiler_params=_params(("parallel", "arbitrary")),
        name="swa_attention",
    )(sinks, proj, proj, gq, gk, bias)


def _merge_kernel(x_ref, gn_ref, wg_ref, oa_ref, ob_ref, oc_ref, bg_ref, wa_ref, wb_ref, wc_ref, wo_ref, o_ref):
    d = x_ref.shape[1]
    x = x_ref[...]
    xn = _rms_rows(x, gn_ref[...]).astype(BF16)
    merged = None
    for k, (o_k, w_k) in enumerate(((oa_ref, wa_ref), (ob_ref, wb_ref), (oc_ref, wc_ref))):
        logit = _mm(xn, wg_ref[:, k * d:(k + 1) * d]) + bg_ref[k:k + 1, :]
        gate = 1.0 / (1.0 + jnp.exp(-logit))
        term = gate * _mm(o_k[...], w_k[...])
        merged = term if merged is None else merged + term
    o_ref[...] = x + _mm(merged.astype(BF16), wo_ref[...])


def merge_project(x2d, gain, w_gate, o_a, o_b, o_c, b_gate, w_pa, w_pb, w_pc, w_out, *, tm=512):
    n, d = x2d.shape
    full = lambda a: pl.BlockSpec(a.shape, lambda i: (0,) * a.ndim)
    row = lambda a: pl.BlockSpec((tm, a.shape[1]), lambda i: (i, 0))
    return pl.pallas_call(
        _merge_kernel,
        grid=(n // tm,),
        in_specs=[row(x2d), full(gain), full(w_gate), row(o_a), row(o_b), row(o_c),
                  full(b_gate), full(w_pa), full(w_pb), full(w_pc), full(w_out)],
        out_specs=pl.BlockSpec((tm, d), lambda i: (i, 0)),
        out_shape=jax.ShapeDtypeStruct((n, d), F32),
        compiler_params=_params(("parallel",)),
        name="merge_project",
    )(x2d, gain, w_gate, o_a, o_b, o_c, b_gate, w_pa, w_pb, w_pc, w_out)


def _memkv_kernel(m_ref, g_ref, w_ref, gk_ref, k_ref, v_ref):
    w_x = k_ref.shape[2]
    mn = _rms_rows(m_ref[0], g_ref[...]).astype(BF16)
    kv = _mm(mn, w_ref[...])
    k_ref[0] = _head_rms(kv[:, :w_x], gk_ref[...], XHEAD_DIM).astype(k_ref.dtype)
    v_ref[0] = kv[:, w_x:].astype(v_ref.dtype)


def memory_kv(mem, gain, w_kv, gk):
    b, m, d = mem.shape
    w_x = w_kv.shape[1] // 2
    return pl.pallas_call(
        _memkv_kernel,
        grid=(b,),
        in_specs=[pl.BlockSpec((1, m, d), lambda i: (i, 0, 0)),
                  pl.BlockSpec((1, d), lambda i: (0, 0)),
                  pl.BlockSpec(w_kv.shape, lambda i: (0, 0)),
                  pl.BlockSpec((1, w_x), lambda i: (0, 0))],
        out_specs=[pl.BlockSpec((1, m, w_x), lambda i: (i, 0, 0)),
                   pl.BlockSpec((1, m, w_x), lambda i: (i, 0, 0))],
        out_shape=[jax.ShapeDtypeStruct((b, m, w_x), BF16)] * 2,
        compiler_params=_params(("parallel",)),
        name="memory_kv",
    )(mem, gain, w_kv, gk)


def _xattn_kernel(x_ref, g_ref, wq_ref, gq_ref, k_ref, v_ref, wo_ref, o_ref):
    x = x_ref[0]
    xn = _rms_rows(x, g_ref[...]).astype(BF16)
    q = _head_rms(_mm(xn, wq_ref[...]), gq_ref[...], XHEAD_DIM).astype(BF16)
    sl = [slice(h * XHEAD_DIM, (h + 1) * XHEAD_DIM) for h in range(H_X)]
    s = [_nt(q[:, c], k_ref[0, :, c]) for c in sl]
    e = [jnp.exp2(s_h - jnp.max(s_h, axis=-1, keepdims=True)) for s_h in s]
    den = [jnp.sum(e_h, axis=-1, keepdims=True) for e_h in e]
    outs = [_mm(e[h].astype(BF16), v_ref[0, :, sl[h]]) / den[h] for h in range(H_X)]
    o = jnp.concatenate(outs, axis=-1).astype(BF16)
    o_ref[0] = x + _mm(o, wo_ref[...])


def cross_attention(x, gain, w_q, gq, k_mem, v_mem, w_o, *, tq=512):
    b, s, d = x.shape
    m, w_x = k_mem.shape[1:]
    full = lambda a: pl.BlockSpec(a.shape, lambda bi, i: (0,) * a.ndim)
    return pl.pallas_call(
        _xattn_kernel,
        grid=(b, s // tq),
        in_specs=[pl.BlockSpec((1, tq, d), lambda bi, i: (bi, i, 0)),
                  full(gain), full(w_q), full(gq),
                  pl.BlockSpec((1, m, w_x), lambda bi, i: (bi, 0, 0)),
                  pl.BlockSpec((1, m, w_x), lambda bi, i: (bi, 0, 0)),
                  full(w_o)],
        out_specs=pl.BlockSpec((1, tq, d), lambda bi, i: (bi, i, 0)),
        out_shape=jax.ShapeDtypeStruct((b, s, d), F32),
        compiler_params=_params(("parallel", "parallel")),
        name="cross_attention",
    )(x, gain, w_q, gq, k_mem, v_mem, w_o)


def _router_kernel(x_ref, g_ref, whi_ref, wlo_ref, o_ref):
    hf = _rms_rows(x_ref[...], g_ref[...])
    hi = hf.astype(BF16)
    lo = (hf - hi.astype(F32)).astype(BF16)
    logits = _mm(hi, whi_ref[...]) + (_mm(hi, wlo_ref[...]) + _mm(lo, whi_ref[...]))
    lane = lax.broadcasted_iota(jnp.int32, logits.shape, 1).astype(F32)
    logits = jnp.where(lane < N_EXPERTS, logits, NEG)
    m1 = jnp.max(logits, axis=-1, keepdims=True)
    i1 = jnp.min(jnp.where(logits == m1, lane, float(LANES)), axis=-1, keepdims=True)
    rest = jnp.where(lane == i1, NEG, logits)
    m2 = jnp.max(rest, axis=-1, keepdims=True)
    i2 = jnp.min(jnp.where(rest == m2, lane, float(LANES)), axis=-1, keepdims=True)
    e2 = jnp.exp(m2 - m1)
    den = 1.0 + e2
    o_ref[...] = (jnp.where(lane == 0.0, i1, 0.0) + jnp.where(lane == 1.0, i2, 0.0)
                  + jnp.where(lane == 2.0, 1.0 / den, 0.0) + jnp.where(lane == 3.0, e2 / den, 0.0))


def router_gates(x2d, gain, w_hi, w_lo, *, tm=1024):
    n, d = x2d.shape
    tm = min(tm, n)
    return pl.pallas_call(
        _router_kernel,
        grid=(n // tm,),
        in_specs=[pl.BlockSpec((tm, d), lambda i: (i, 0)),
                  pl.BlockSpec((1, d), lambda i: (0, 0)),
                  pl.BlockSpec((d, LANES), lambda i: (0, 0)),
                  pl.BlockSpec((d, LANES), lambda i: (0, 0))],
        out_specs=pl.BlockSpec((tm, LANES), lambda i: (i, 0)),
        out_shape=jax.ShapeDtypeStruct((n, LANES), F32),
        compiler_params=_params(("parallel",)),
        name="router_gates",
    )(x2d, gain, w_hi, w_lo)


def _swiglu_tile(xn, wg, wu, wd):
    gg = _mm(xn, wg)
    uu = _mm(xn, wu)
    act = gg * (1.0 / (1.0 + jnp.exp(-gg))) * uu
    return _mm(act.astype(BF16), wd)


def _ffn_kernel(x_ref, g_ref, wgu_ref, wd_ref, o_ref, *, tf):
    f = wd_ref.shape[0]
    x = x_ref[...]
    xn = _rms_rows(x, g_ref[...]).astype(BF16)
    acc = None
    for c in range(f // tf):
        part = _swiglu_tile(xn, wgu_ref[:, c * tf:(c + 1) * tf], wgu_ref[:, f + c * tf:f + (c + 1) * tf],
                            wd_ref[c * tf:(c + 1) * tf, :])
        acc = part if acc is None else acc + part
    o_ref[...] = x + acc


def dense_ffn(x2d, gain, w_gu, w_down, *, tm=512, tf=256):
    n, d = x2d.shape
    f = w_down.shape[0]
    tm = min(tm, n)
    assert f % tf == 0
    resident = lambda a: pl.BlockSpec(a.shape, lambda i: (0,) * a.ndim, pipeline_mode=pl.Buffered(1))
    return pl.pallas_call(
        functools.partial(_ffn_kernel, tf=tf),
        grid=(n // tm,),
        in_specs=[pl.BlockSpec((tm, d), lambda i: (i, 0)),
                  pl.BlockSpec((1, d), lambda i: (0, 0)),
                  resident(w_gu), resident(w_down)],
        out_specs=pl.BlockSpec((tm, d), lambda i: (i, 0)),
        out_shape=jax.ShapeDtypeStruct((n, d), F32),
        compiler_params=_params(("parallel",)),
        name="dense_ffn",
    )(x2d, gain, w_gu, w_down)


def _moe_kernel(te_ref, tn_ref, tok_ref, tok_next_ref, dst_prev_ref, x_hbm, g_ref, wgu_ref,
                wd_ref, y_hbm, xg_ref, yb_ref, gsem, ssem, *, tm, tf):
    i = pl.program_id(0)
    nt = pl.num_programs(0)
    slot = i % 2
    f = wd_ref.shape[1]
    nf = f // tf
    per_step = tm // nf
    active = tn_ref[i] > 0
    prev_active = (i > 0) & (tn_ref[jnp.maximum(i - 1, 0)] > 0)
    prev_issued = (i > 0) & (tn_ref[jnp.maximum(i - 2, 0)] > 0)

    def gather_copy(tok, r, s):
        return pltpu.make_async_copy(x_hbm.at[pl.ds(tok, 1), :], xg_ref.at[s, pl.ds(r, 1), :], gsem.at[s])

    def scatter_copy(dst, r, s):
        return pltpu.make_async_copy(yb_ref.at[s, pl.ds(r, 1), :], y_hbm.at[pl.ds(dst, 1), :], ssem.at[s])

    def start_all_rows(ids_ref, s, make_copy):
        def body(r8, c):
            for u in range(8):
                r = r8 * 8 + u
                make_copy(ids_ref[0, 0, r], r, s).start()
            return c
        lax.fori_loop(0, tm // 8, body, 0)

    def wait_gather(s):
        pltpu.make_async_copy(x_hbm.at[pl.ds(0, tm), :], xg_ref.at[s], gsem.at[s]).wait()

    def wait_scatter(s):
        pltpu.make_async_copy(yb_ref.at[s], y_hbm.at[pl.ds(0, tm), :], ssem.at[s]).wait()

    @pl.when(i == 0)
    def _():
        start_all_rows(tok_ref, slot, gather_copy)
        yb_ref[1] = jnp.zeros(yb_ref.shape[1:], yb_ref.dtype)
        n_real = y_hbm.shape[0] - 2 * tm
        for half in range(2):
            spare = pltpu.make_async_copy(yb_ref.at[1], y_hbm.at[pl.ds(n_real + half * tm, tm), :], ssem.at[1])
            spare.start()
            spare.wait()

    @pl.when(active | prev_active)
    def _():
        wait_gather(slot)

    @pl.when(prev_issued)
    def _():
        wait_scatter(slot)

    @pl.when(prev_active & jnp.logical_not(active))
    def _():
        start_all_rows(dst_prev_ref, 1 - slot, scatter_copy)

    @pl.when(active)
    def _():
        xn = _rms_rows(xg_ref[slot], g_ref[...]).astype(BF16)
        acc = None
        for c in range(nf):
            part = _swiglu_tile(xn, wgu_ref[0, :, c * tf:(c + 1) * tf], wgu_ref[0, :, f + c * tf:f + (c + 1) * tf],
                                wd_ref[0, c * tf:(c + 1) * tf, :])
            acc = part if acc is None else acc + part
            for r in range(c * per_step, (c + 1) * per_step):
                gather_copy(tok_next_ref[0, 0, r], r, 1 - slot).start()
                scatter_copy(dst_prev_ref[0, 0, r], r, 1 - slot).start()
        yb_ref[slot] = acc

    @pl.when((i == nt - 1) & (active | prev_active))
    def _():
        wait_scatter(1 - slot)


def moe_experts(x2d, gain, plan, w_gu, w_down, *, tm, tf=512):
    n, d = x2d.shape
    ne, f, _ = w_down.shape
    nf = f // tf
    assert tm % nf == 0 and tm % 16 == 0
    tile_e, tile_n, row_tok, row_dst_prev = plan
    nt = tile_e.shape[0]
    smem_rows = lambda imap: pl.BlockSpec((1, 1, tm), imap, memory_space=pltpu.SMEM)
    expert = lambda shape: pl.BlockSpec(shape, lambda i, te, tn: (te[i], 0, 0), pipeline_mode=pl.Buffered(1))
    return pl.pallas_call(
        functools.partial(_moe_kernel, tm=tm, tf=tf),
        grid_spec=pltpu.PrefetchScalarGridSpec(
            num_scalar_prefetch=2,
            grid=(nt,),
            in_specs=[smem_rows(lambda i, te, tn: (i, 0, 0)),
                      smem_rows(lambda i, te, tn: (jnp.minimum(i + 1, nt - 1), 0, 0)),
                      smem_rows(lambda i, te, tn: (i, 0, 0)),
                      pl.BlockSpec(memory_space=pl.ANY),
                      pl.BlockSpec((1, d), lambda i, te, tn: (0, 0)),
                      expert((1, d, 2 * f)), expert((1, f, d))],
            out_specs=pl.BlockSpec(memory_space=pl.ANY),
            scratch_shapes=[pltpu.VMEM((2, tm, d), F32), pltpu.VMEM((2, tm, d), F32),
                            pltpu.SemaphoreType.DMA((2,)), pltpu.SemaphoreType.DMA((2,))]),
        out_shape=jax.ShapeDtypeStruct((2 * n + 2 * tm, d), F32),
        compiler_params=_params(("arbitrary",)),
        name="moe_experts",
    )(tile_e, tile_n, row_tok, row_tok, row_dst_prev, x2d, gain, w_gu, w_down)


def moe_plan(route, *, tm):
    n = route.shape[0]
    flat_e = route[:, :2].astype(jnp.int32).reshape(-1)
    nt = (2 * n) // tm + N_EXPERTS + 1
    order = jnp.argsort(flat_e, stable=True).astype(jnp.int32)
    counts = jnp.sum(flat_e[:, None] == jnp.arange(N_EXPERTS)[None, :], axis=0).astype(jnp.int32)
    off = jnp.cumsum(counts) - counts
    tiles = (counts + tm - 1) // tm
    tile_off = jnp.cumsum(tiles) - tiles
    tile_id = jnp.arange(nt, dtype=jnp.int32)
    used = tile_id < jnp.sum(tiles)
    tile_e = jnp.clip(jnp.sum(tile_id[:, None] >= tile_off[None, :], axis=1) - 1, 0, N_EXPERTS - 1)
    tile_e = jnp.where(used, tile_e, tile_e[jnp.maximum(jnp.sum(tiles) - 1, 0)]).astype(jnp.int32)
    first_row = (tile_id - tile_off[tile_e]) * tm
    tile_n = jnp.where(used, jnp.clip(counts[tile_e] - first_row, 0, tm), 0).astype(jnp.int32)
    r = jnp.arange(tm, dtype=jnp.int32)[None, :]
    valid = r < tile_n[:, None]
    a = order[jnp.clip(off[tile_e][:, None] + first_row[:, None] + r, 0, 2 * n - 1)]
    row_tok = jnp.where(valid, a // 2, 0).astype(jnp.int32).reshape(nt, 1, tm)
    spare = 2 * n + (tile_id[:, None] % 2) * tm + r
    row_dst = jnp.where(valid, (a % 2) * n + a // 2, spare).astype(jnp.int32)
    row_dst_prev = jnp.concatenate([2 * n + tm + r, row_dst[:-1]], axis=0).reshape(nt, 1, tm)
    return tile_e, tile_n, row_tok, row_dst_prev


def _combine_kernel(x_ref, route_ref, y0_ref, y1_ref, o_ref):
    o_ref[...] = x_ref[...] + (route_ref[:, 2:3] * y0_ref[...] + route_ref[:, 3:4] * y1_ref[...])


def moe_combine(x2d, route, y, *, tm=1024):
    n, d = x2d.shape
    tm = min(tm, n)
    nb = n // tm
    return pl.pallas_call(
        _combine_kernel,
        grid=(nb,),
        in_specs=[pl.BlockSpec((tm, d), lambda i: (i, 0)),
                  pl.BlockSpec((tm, LANES), lambda i: (i, 0)),
                  pl.BlockSpec((tm, d), lambda i: (i, 0)),
                  pl.BlockSpec((tm, d), lambda i: (i + nb, 0))],
        out_specs=pl.BlockSpec((tm, d), lambda i: (i, 0)),
        out_shape=jax.ShapeDtypeStruct((n, d), F32),
        compiler_params=_params(("parallel",)),
        name="moe_combine",
    )(x2d, route, y, y)


def _pack_w_in(w, d):
    sizes = (256, 256, 256, 256, 64, 64, 256, 64, 4, 512, 128, 128, N_BRANCH * d)
    qa, ka, va, qb, kb, vb, qi, ki, wi, qc, kc, vc, g = jnp.split(w, np.cumsum(sizes)[:-1].tolist(), axis=-1)
    pad = jnp.zeros((w.shape[0], 60), w.dtype)
    mixers = jnp.concatenate([qa * (HEAD_DIM ** -0.5 * LOG2E), ka, va,
                              qb, qi * D_IDX ** -0.5, kb, vb, ki, wi, pad,
                              qc, kc, vc], axis=-1)
    return mixers.astype(BF16), g.astype(BF16)


def kernel(x, mem, rel_bias, norm_mix, w_in, b_gate, qn_dsa, kn_dsa, qn_swa, kn_swa, sinks, w_pa, w_pb, w_pc, w_out, norm_x, norm_mem, w_xq, w_xkv, w_xo, qn_x, kn_x, norm_ffn, w_gu_dense, w_down_dense, w_router, w_gu_moe, w_down_moe):
    b, s, d = x.shape
    depth = w_in.shape[0]
    n = b * s
    row = lambda v: v.reshape(1, -1).astype(F32)
    bias_dsa = dsa_bias_tiles(rel_bias[:, :H_DSA] * LOG2E)
    bias_swa = swa_bias_tiles(rel_bias[:, H_DSA:] * LOG2E)
    qscale = HEAD_DIM ** -0.5 * LOG2E

    x2 = x.reshape(n, d)
    for l in range(depth):
        w_mix, w_gate = _pack_w_in(w_in[l], d)
        proj3 = in_projection(x2, row(norm_mix[l]), w_mix, tn=w_mix.shape[1]).reshape(b, s, -1)
        o_a = sb_attention(proj3, 0)
        o_b = dsa_attention(proj3, 1, row(jnp.tile(qn_dsa[l] * qscale, H_DSA)),
                            row(jnp.concatenate([kn_dsa[l], jnp.zeros_like(kn_dsa[l])])), bias_dsa)
        o_c = swa_attention(proj3, 2, sinks[l].astype(F32) * LOG2E,
                            row(jnp.tile(qn_swa[l] * qscale, H_SW)), row(jnp.tile(kn_swa[l], KV_SW)), bias_swa)
        x2 = merge_project(x2, row(norm_mix[l]), w_gate, o_a.reshape(n, -1), o_b.reshape(n, -1), o_c.reshape(n, -1),
                           b_gate[l].astype(F32), w_pa[l].astype(BF16), w_pb[l].astype(BF16),
                           w_pc[l].astype(BF16), w_out[l].astype(BF16))
        k_mem, v_mem = memory_kv(mem, row(norm_mem[l]), w_xkv[l].astype(BF16), row(jnp.tile(kn_x[l], H_X)))
        x2 = cross_attention(x2.reshape(b, s, d), row(norm_x[l]), w_xq[l].astype(BF16),
                             row(jnp.tile(qn_x[l] * (XHEAD_DIM ** -0.5 * LOG2E), H_X)), k_mem, v_mem,
                             w_xo[l].astype(BF16)).reshape(n, d)
        if l % 2 == 0:
            x2 = dense_ffn(x2, row(norm_ffn[l]), w_gu_dense[l // 2].astype(BF16),
                           w_down_dense[l // 2].astype(BF16), tf=256)
        else:
            wr = jnp.pad(w_router[l // 2].astype(F32), ((0, 0), (0, LANES - N_EXPERTS)))
            wr_hi = wr.astype(BF16)
            wr_lo = (wr - wr_hi.astype(F32)).astype(BF16)
            route = router_gates(x2, row(norm_ffn[l]), wr_hi, wr_lo)
            tm_moe = 672
            y = moe_experts(x2, row(norm_ffn[l]), moe_plan(route, tm=tm_moe), w_gu_moe[l // 2].astype(BF16),
                            w_down_moe[l // 2].astype(BF16), tm=tm_moe)
            x2 = moe_combine(x2, route, y)
    return x2.reshape(b, s, d)
```
